```python
import math
import jax, jax.numpy as jnp
from jax import lax
import numpy as np

D_MODEL = 1024
BATCH = 8
SEQ = 8192
DEPTH = 4

N_A = DEPTH // 2
N_B = DEPTH - N_A
GDN_HEADS = 8
GDN_HEAD_DIM = D_MODEL // GDN_HEADS
GDN_CONV = 4
GDN_CHUNK = 64
SB_HEADS = 8
SB_HEAD_DIM = D_MODEL // SB_HEADS
SB_BLOCK = 128
D_FF = ((8 * D_MODEL // 3 + 127) // 128) * 128
FFN_CONV = 3
EPS = 1e-6

kernel_name = "yoco_gdn_stickbreaking_convffn"


def rmsnorm(x, g):
    xf = x.astype(jnp.float32)
    xf = xf * lax.rsqrt(jnp.mean(xf * xf, axis=-1, keepdims=True) + EPS)
    return (xf * g.astype(jnp.float32)).astype(x.dtype)


def l2norm(x):
    xf = x.astype(jnp.float32)
    return xf * lax.rsqrt(jnp.sum(xf * xf, axis=-1, keepdims=True) + EPS)


def causal_dwconv(x, w):
    K, C = w.shape
    return lax.conv_general_dilated(
        x, w[:, None, :].astype(x.dtype), window_strides=(1,), padding=[(K - 1, 0)],
        dimension_numbers=("NWC", "WIO", "NWC"), feature_group_count=C)


def gated_delta_rule(q, k, v, g, beta):
    B, T, H, Dk = q.shape
    Dv = v.shape[-1]
    C = GDN_CHUNK
    N = T // C
    f32 = jnp.float32

    def chunks(t):
        t = t.astype(f32).reshape((B, N, C, H) + t.shape[3:])
        return jnp.moveaxis(t, 3, 1)

    q, k, v, beta = chunks(q), chunks(k), chunks(v), chunks(beta)
    g = jnp.cumsum(chunks(g), axis=-1)
    tri = jnp.tril(jnp.ones((C, C), dtype=bool))
    strict = jnp.tril(jnp.ones((C, C), dtype=bool), -1)
    decay = jnp.exp(jnp.where(tri, g[..., :, None] - g[..., None, :], -jnp.inf))
    k_beta = k * beta[..., None]
    L = jnp.where(strict, jnp.einsum("bhnid,bhnjd->bhnij", k_beta, k) * decay, 0.0)
    rhs = jnp.concatenate([v * beta[..., None], k_beta * jnp.exp(g)[..., None]], axis=-1)
    sol = lax.linalg.triangular_solve(L, rhs, left_side=True, lower=True, unit_diagonal=True)
    u, w = sol[..., :Dv], sol[..., Dv:]
    qk = jnp.where(tri, jnp.einsum("bhnid,bhnjd->bhnij", q, k) * decay, 0.0)
    g_last = g[..., -1]
    q_dec = q * jnp.exp(g)[..., None]
    k_dec = k * jnp.exp(g_last[..., None] - g)[..., None]

    def step(S, inp):
        qk_i, q_dec_i, k_dec_i, u_i, w_i, gl_i = inp
        v_new = u_i - jnp.einsum("bhck,bhkv->bhcv", w_i, S)
        o = jnp.einsum("bhck,bhkv->bhcv", q_dec_i, S) + jnp.einsum("bhij,bhjv->bhiv", qk_i, v_new)
        S = S * jnp.exp(gl_i)[..., None, None] + jnp.einsum("bhck,bhcv->bhkv", k_dec_i, v_new)
        return S, o

    xs = tuple(jnp.moveaxis(t, 2, 0) for t in (qk, q_dec, k_dec, u, w, g_last))
    S0 = jnp.zeros((B, H, Dk, Dv), f32)
    _, o = lax.scan(step, S0, xs)
    return jnp.transpose(o, (1, 0, 3, 2, 4)).reshape(B, T, H, Dv)


def gdn_mixer(h, w_in, conv_w, a_log, dt_bias, o_gain, w_out):
    B, T, D = h.shape
    H, Dh = GDN_HEADS, GDN_HEAD_DIM
    proj = h @ w_in
    qkv, z, a, b = jnp.split(proj, [3 * D, 4 * D, 4 * D + H], axis=-1)
    qkv = jax.nn.silu(causal_dwconv(qkv, conv_w))
    q, k, v = jnp.split(qkv, 3, axis=-1)
    q = l2norm(q.reshape(B, T, H, Dh)) * (Dh ** -0.5)
    k = l2norm(k.reshape(B, T, H, Dh))
    v = v.reshape(B, T, H, Dh)
    g = -jnp.exp(a_log.astype(jnp.float32)) * jax.nn.softplus(
        a.astype(jnp.float32) + dt_bias.astype(jnp.float32))
    beta = jax.nn.sigmoid(b.astype(jnp.float32))
    o = gated_delta_rule(q, k, v, g, beta)
    o = rmsnorm(o, o_gain) * jax.nn.silu(z.reshape(B, T, H, Dh).astype(jnp.float32))
    return o.reshape(B, T, D).astype(h.dtype) @ w_out


def stick_breaking_attention(q, k, v):
    B, H, T, D = q.shape
    scale = D ** -0.5
    outs = []
    for blk in range(T // SB_BLOCK):
        lo, hi = blk * SB_BLOCK, (blk + 1) * SB_BLOCK
        z = jnp.einsum("bhtd,bhsd->bhts", q[:, :, lo:hi], k[:, :, :hi],
                       preferred_element_type=jnp.float32) * scale
        t_idx = lo + jnp.arange(SB_BLOCK)[:, None]
        s_idx = jnp.arange(hi)[None, :]
        causal = s_idx < t_idx
        log_fail = jnp.where(causal, jax.nn.log_sigmoid(-z), 0.0)
        after = lax.cumsum(log_fail, axis=3, reverse=True) - log_fail
        a = jnp.where(causal, jnp.exp(jax.nn.log_sigmoid(z) + after), 0.0)
        outs.append(jnp.einsum("bhts,bhsd->bhtd", a, v[:, :, :hi].astype(jnp.float32)))
    return jnp.concatenate(outs, axis=2).astype(q.dtype)


def conv_ffn(h, w_up, conv_w, w_down):
    u = causal_dwconv(h @ w_up, conv_w)
    gate, up = jnp.split(u, 2, axis=-1)
    return (jax.nn.silu(gate) * up) @ w_down


def _fwd_setup_inputs(seed: int = 0) -> dict:
    key = jax.random.key(seed)
    ks = jax.random.split(key, 20)
    D, H = D_MODEL, GDN_HEADS
    f32 = jnp.float32
    out_scale = (2 * DEPTH) ** -0.5

    def nrm(k, shape, fan_in, gain=1.0):
        return jax.random.normal(k, shape, f32) * (gain * fan_in ** -0.5)

    def gvec(k, shape):
        return 1.0 + 0.02 * jax.random.normal(k, shape, f32)

    dt = jnp.exp(jax.random.uniform(ks[4], (N_A, H), f32, math.log(1e-3), math.log(1e-1)))
    return {
        "x": jax.random.normal(ks[0], (BATCH, SEQ, D), f32),
        "a_norm": gvec(ks[1], (N_A, D)),
        "a_w_in": nrm(ks[2], (N_A, D, 4 * D + 2 * H), D),
        "a_conv": nrm(ks[3], (N_A, GDN_CONV, 3 * D), GDN_CONV),
        "a_log": jnp.log(jax.random.uniform(ks[5], (N_A, H), f32, 1.0, 16.0)),
        "a_dt_bias": dt + jnp.log(-jnp.expm1(-dt)),
        "a_out_norm": gvec(ks[6], (N_A, GDN_HEAD_DIM)),
        "a_w_out": nrm(ks[7], (N_A, D, D), D, out_scale),
        "kv_norm": gvec(ks[8], (D,)),
        "w_kv": nrm(ks[9], (D, 2 * D), D),
        "k_norm": gvec(ks[10], (SB_HEAD_DIM,)),
        "b_norm": gvec(ks[11], (N_B, D)),
        "b_w_q": nrm(ks[12], (N_B, D, D), D),
        "q_norm": gvec(ks[13], (N_B, SB_HEAD_DIM)),
        "b_w_out": nrm(ks[14], (N_B, D, D), D, out_scale),
        "ffn_norm": gvec(ks[15], (DEPTH, D)),
        "ffn_w_up": nrm(ks[16], (DEPTH, D, 2 * D_FF), D),
        "ffn_conv": nrm(ks[17], (DEPTH, FFN_CONV, 2 * D_FF), FFN_CONV),
        "ffn_w_down": nrm(ks[18], (DEPTH, D_FF, D), D_FF, out_scale),
    }


def _fwd_reference(x, a_norm, a_w_in, a_conv, a_log, a_dt_bias, a_out_norm, a_w_out,
              kv_norm, w_kv, k_norm, b_norm, b_w_q, q_norm, b_w_out,
              ffn_norm, ffn_w_up, ffn_conv, ffn_w_down):
    B, T, D = x.shape
    H, Dh = SB_HEADS, SB_HEAD_DIM
    for layer in range(DEPTH):
        if layer < N_A:
            x = x + gdn_mixer(rmsnorm(x, a_norm[layer]), a_w_in[layer], a_conv[layer],
                              a_log[layer], a_dt_bias[layer], a_out_norm[layer], a_w_out[layer])
        else:
            j = layer - N_A
            if j == 0:
                kv = rmsnorm(x, kv_norm) @ w_kv
                k_s, v_s = jnp.split(kv, 2, axis=-1)
                k_s = jnp.transpose(rmsnorm(k_s.reshape(B, T, H, Dh), k_norm), (0, 2, 1, 3))
                v_s = jnp.transpose(v_s.reshape(B, T, H, Dh), (0, 2, 1, 3))
            q = rmsnorm((rmsnorm(x, b_norm[j]) @ b_w_q[j]).reshape(B, T, H, Dh), q_norm[j])
            o = stick_breaking_attention(jnp.transpose(q, (0, 2, 1, 3)), k_s, v_s)
            x = x + jnp.transpose(o, (0, 2, 1, 3)).reshape(B, T, D) @ b_w_out[j]
        x = x + conv_ffn(rmsnorm(x, ffn_norm[layer]), ffn_w_up[layer], ffn_conv[layer], ffn_w_down[layer])
    return x


import jax as _jax
import jax.numpy as _jnp

TWIN_FORMAT = 'train_step'
FWD_PARAMS = ['x', 'a_norm', 'a_w_in', 'a_conv', 'a_log', 'a_dt_bias', 'a_out_norm', 'a_w_out', 'kv_norm', 'w_kv', 'k_norm', 'b_norm', 'b_w_q', 'q_norm', 'b_w_out', 'ffn_norm', 'ffn_w_up', 'ffn_conv', 'ffn_w_down']
TWIN_WEIGHTS = ['a_norm', 'a_w_in', 'a_conv', 'a_log', 'a_dt_bias', 'a_out_norm', 'a_w_out', 'kv_norm', 'w_kv', 'k_norm', 'b_norm', 'b_w_q', 'q_norm', 'b_w_out', 'ffn_norm', 'ffn_w_up', 'ffn_conv', 'ffn_w_down']
TWIN_DIFF_INPUT = 'x'
TWIN_INPUTS = ['x', 'a_norm', 'a_w_in', 'a_conv', 'a_log', 'a_dt_bias', 'a_out_norm', 'a_w_out', 'kv_norm', 'w_kv', 'k_norm', 'b_norm', 'b_w_q', 'q_norm', 'b_w_out', 'ffn_norm', 'ffn_w_up', 'ffn_conv', 'ffn_w_down', 'loss_target', 'm_a_norm', 'm_a_w_in', 'm_a_conv', 'm_a_log', 'm_a_dt_bias', 'm_a_out_norm', 'm_a_w_out', 'm_kv_norm', 'm_w_kv', 'm_k_norm', 'm_b_norm', 'm_b_w_q', 'm_q_norm', 'm_b_w_out', 'm_ffn_norm', 'm_ffn_w_up', 'm_ffn_conv', 'm_ffn_w_down', 'v_a_norm', 'v_a_w_in', 'v_a_conv', 'v_a_log', 'v_a_dt_bias', 'v_a_out_norm', 'v_a_w_out', 'v_kv_norm', 'v_w_kv', 'v_k_norm', 'v_b_norm', 'v_b_w_q', 'v_q_norm', 'v_b_w_out', 'v_ffn_norm', 'v_ffn_w_up', 'v_ffn_conv', 'v_ffn_w_down']
TWIN_OUTPUTS = ['loss', 'grad_x', 'grad_a_norm', 'grad_a_w_in', 'grad_a_conv', 'grad_a_log', 'grad_a_dt_bias', 'grad_a_out_norm', 'grad_a_w_out', 'grad_kv_norm', 'grad_w_kv', 'grad_k_norm', 'grad_b_norm', 'grad_b_w_q', 'grad_q_norm', 'grad_b_w_out', 'grad_ffn_norm', 'grad_ffn_w_up', 'grad_ffn_conv', 'grad_ffn_w_down', 'delta_a_norm', 'delta_a_w_in', 'delta_a_conv', 'delta_a_log', 'delta_a_dt_bias', 'delta_a_out_norm', 'delta_a_w_out', 'delta_kv_norm', 'delta_w_kv', 'delta_k_norm', 'delta_b_norm', 'delta_b_w_q', 'delta_q_norm', 'delta_b_w_out', 'delta_ffn_norm', 'delta_ffn_w_up', 'delta_ffn_conv', 'delta_ffn_w_down', 'new_m_a_norm', 'new_m_a_w_in', 'new_m_a_conv', 'new_m_a_log', 'new_m_a_dt_bias', 'new_m_a_out_norm', 'new_m_a_w_out', 'new_m_kv_norm', 'new_m_w_kv', 'new_m_k_norm', 'new_m_b_norm', 'new_m_b_w_q', 'new_m_q_norm', 'new_m_b_w_out', 'new_m_ffn_norm', 'new_m_ffn_w_up', 'new_m_ffn_conv', 'new_m_ffn_w_down', 'new_v_a_norm', 'new_v_a_w_in', 'new_v_a_conv', 'new_v_a_log', 'new_v_a_dt_bias', 'new_v_a_out_norm', 'new_v_a_w_out', 'new_v_kv_norm', 'new_v_w_kv', 'new_v_k_norm', 'new_v_b_norm', 'new_v_b_w_q', 'new_v_q_norm', 'new_v_b_w_out', 'new_v_ffn_norm', 'new_v_ffn_w_up', 'new_v_ffn_conv', 'new_v_ffn_w_down']
TWIN_LEAF_KINDS = {'loss': 'loss', 'grad_x': 'grad_x', 'grad_a_norm': 'grad_w', 'grad_a_w_in': 'grad_w', 'grad_a_conv': 'grad_w', 'grad_a_log': 'grad_w', 'grad_a_dt_bias': 'grad_w', 'grad_a_out_norm': 'grad_w', 'grad_a_w_out': 'grad_w', 'grad_kv_norm': 'grad_w', 'grad_w_kv': 'grad_w', 'grad_k_norm': 'grad_w', 'grad_b_norm': 'grad_w', 'grad_b_w_q': 'grad_w', 'grad_q_norm': 'grad_w', 'grad_b_w_out': 'grad_w', 'grad_ffn_norm': 'grad_w', 'grad_ffn_w_up': 'grad_w', 'grad_ffn_conv': 'grad_w', 'grad_ffn_w_down': 'grad_w', 'delta_a_norm': 'delta_w', 'delta_a_w_in': 'delta_w', 'delta_a_conv': 'delta_w', 'delta_a_log': 'delta_w', 'delta_a_dt_bias': 'delta_w', 'delta_a_out_norm': 'delta_w', 'delta_a_w_out': 'delta_w', 'delta_kv_norm': 'delta_w', 'delta_w_kv': 'delta_w', 'delta_k_norm': 'delta_w', 'delta_b_norm': 'delta_w', 'delta_b_w_q': 'delta_w', 'delta_q_norm': 'delta_w', 'delta_b_w_out': 'delta_w', 'delta_ffn_norm': 'delta_w', 'delta_ffn_w_up': 'delta_w', 'delta_ffn_conv': 'delta_w', 'delta_ffn_w_down': 'delta_w', 'new_m_a_norm': 'new_m', 'new_m_a_w_in': 'new_m', 'new_m_a_conv': 'new_m', 'new_m_a_log': 'new_m', 'new_m_a_dt_bias': 'new_m', 'new_m_a_out_norm': 'new_m', 'new_m_a_w_out': 'new_m', 'new_m_kv_norm': 'new_m', 'new_m_w_kv': 'new_m', 'new_m_k_norm': 'new_m', 'new_m_b_norm': 'new_m', 'new_m_b_w_q': 'new_m', 'new_m_q_norm': 'new_m', 'new_m_b_w_out': 'new_m', 'new_m_ffn_norm': 'new_m', 'new_m_ffn_w_up': 'new_m', 'new_m_ffn_conv': 'new_m', 'new_m_ffn_w_down': 'new_m', 'new_v_a_norm': 'new_v', 'new_v_a_w_in': 'new_v', 'new_v_a_conv': 'new_v', 'new_v_a_log': 'new_v', 'new_v_a_dt_bias': 'new_v', 'new_v_a_out_norm': 'new_v', 'new_v_a_w_out': 'new_v', 'new_v_kv_norm': 'new_v', 'new_v_w_kv': 'new_v', 'new_v_k_norm': 'new_v', 'new_v_b_norm': 'new_v', 'new_v_b_w_q': 'new_v', 'new_v_q_norm': 'new_v', 'new_v_b_w_out': 'new_v', 'new_v_ffn_norm': 'new_v', 'new_v_ffn_w_up': 'new_v', 'new_v_ffn_conv': 'new_v', 'new_v_ffn_w_down': 'new_v'}


def _forward(args):
    return _fwd_reference(*[args[k] for k in FWD_PARAMS])


def _output_shape():
    def fwd():
        inp = _fwd_setup_inputs(0)
        return _fwd_reference(*[inp[k] for k in FWD_PARAMS])
    out = _jax.eval_shape(fwd)
    return out.shape, out.dtype

N_MICROBATCH = 1
ADAM_LR = 0.001
ADAM_B1 = 0.9
ADAM_B2 = 0.999
ADAM_EPS = 1e-08
ADAM_WD = 0.01
ADAM_STEP = 10
PER_EXAMPLE_BATCH_AXIS = {'x': 0, 'loss_target': 0}
SHARED_INPUTS = []
_WEIGHT_DTYPES = {'a_norm': _jnp.float32, 'a_w_in': _jnp.float32, 'a_conv': _jnp.float32, 'a_log': _jnp.float32, 'a_dt_bias': _jnp.float32, 'a_out_norm': _jnp.float32, 'a_w_out': _jnp.float32, 'kv_norm': _jnp.float32, 'w_kv': _jnp.float32, 'k_norm': _jnp.float32, 'b_norm': _jnp.float32, 'b_w_q': _jnp.float32, 'q_norm': _jnp.float32, 'b_w_out': _jnp.float32, 'ffn_norm': _jnp.float32, 'ffn_w_up': _jnp.float32, 'ffn_conv': _jnp.float32, 'ffn_w_down': _jnp.float32}
MOMENT_SCALE = {'a_norm': 3.275728e+00, 'a_w_in': 1.251339e-01, 'a_conv': 1.555152e-01, 'a_log': 9.359528e+00, 'a_dt_bias': 8.925711e+00, 'a_out_norm': 2.312464e+01, 'a_w_out': 8.994306e-01, 'kv_norm': 6.637926e+00, 'w_kv': 1.769989e-01, 'k_norm': 7.813153e+00, 'b_norm': 5.924790e-02, 'b_w_q': 5.727543e-02, 'q_norm': 3.856822e+00, 'b_w_out': 4.519022e-01, 'ffn_norm': 6.363775e+00, 'ffn_w_up': 8.517383e-02, 'ffn_conv': 8.584897e-01, 'ffn_w_down': 3.616087e-01}


def _to_microbatches(a, axis):
    t = _jnp.moveaxis(a, axis, 0)
    t = t.reshape((N_MICROBATCH, t.shape[0] // N_MICROBATCH) + t.shape[1:])
    return _jnp.moveaxis(t, 1, axis + 1)


def setup_inputs(seed: int = 0) -> dict:
    inp = _fwd_setup_inputs(seed)
    key = _jax.random.fold_in(_jax.random.key(seed), 7919)
    shape, _ = _output_shape()
    out = dict(inp)
    out["loss_target"] = _jax.random.normal(_jax.random.fold_in(key, 0), shape, _jnp.float32)
    for i, name in enumerate(TWIN_WEIGHTS):
        w = inp[name].astype(_jnp.float32)
        if MOMENT_SCALE is None:
            s = _jnp.sqrt(_jnp.mean(_jnp.square(w)) + 1e-30)
        else:
            s = MOMENT_SCALE[name]
        km, kv = _jax.random.split(_jax.random.fold_in(key, i + 1))
        out[name] = w
        out["m_" + name] = s * _jax.random.normal(km, w.shape, _jnp.float32)
        out["v_" + name] = (s * s) * _jax.random.uniform(kv, w.shape, _jnp.float32, 0.5, 1.5)
    if N_MICROBATCH > 1:
        for name, axis in PER_EXAMPLE_BATCH_AXIS.items():
            out[name] = _to_microbatches(out[name], axis)
    return {'x': out['x'], 'a_norm': out['a_norm'], 'a_w_in': out['a_w_in'], 'a_conv': out['a_conv'], 'a_log': out['a_log'], 'a_dt_bias': out['a_dt_bias'], 'a_out_norm': out['a_out_norm'], 'a_w_out': out['a_w_out'], 'kv_norm': out['kv_norm'], 'w_kv': out['w_kv'], 'k_norm': out['k_norm'], 'b_norm': out['b_norm'], 'b_w_q': out['b_w_q'], 'q_norm': out['q_norm'], 'b_w_out': out['b_w_out'], 'ffn_norm': out['ffn_norm'], 'ffn_w_up': out['ffn_w_up'], 'ffn_conv': out['ffn_conv'], 'ffn_w_down': out['ffn_w_down'], 'loss_target': out['loss_target'], 'm_a_norm': out['m_a_norm'], 'm_a_w_in': out['m_a_w_in'], 'm_a_conv': out['m_a_conv'], 'm_a_log': out['m_a_log'], 'm_a_dt_bias': out['m_a_dt_bias'], 'm_a_out_norm': out['m_a_out_norm'], 'm_a_w_out': out['m_a_w_out'], 'm_kv_norm': out['m_kv_norm'], 'm_w_kv': out['m_w_kv'], 'm_k_norm': out['m_k_norm'], 'm_b_norm': out['m_b_norm'], 'm_b_w_q': out['m_b_w_q'], 'm_q_norm': out['m_q_norm'], 'm_b_w_out': out['m_b_w_out'], 'm_ffn_norm': out['m_ffn_norm'], 'm_ffn_w_up': out['m_ffn_w_up'], 'm_ffn_conv': out['m_ffn_conv'], 'm_ffn_w_down': out['m_ffn_w_down'], 'v_a_norm': out['v_a_norm'], 'v_a_w_in': out['v_a_w_in'], 'v_a_conv': out['v_a_conv'], 'v_a_log': out['v_a_log'], 'v_a_dt_bias': out['v_a_dt_bias'], 'v_a_out_norm': out['v_a_out_norm'], 'v_a_w_out': out['v_a_w_out'], 'v_kv_norm': out['v_kv_norm'], 'v_w_kv': out['v_w_kv'], 'v_k_norm': out['v_k_norm'], 'v_b_norm': out['v_b_norm'], 'v_b_w_q': out['v_b_w_q'], 'v_q_norm': out['v_q_norm'], 'v_b_w_out': out['v_b_w_out'], 'v_ffn_norm': out['v_ffn_norm'], 'v_ffn_w_up': out['v_ffn_w_up'], 'v_ffn_conv': out['v_ffn_conv'], 'v_ffn_w_down': out['v_ffn_w_down']}


def _loss(weights, diff, rest, loss_target):
    with _jax.named_scope("forward"):
        args = {**rest, TWIN_DIFF_INPUT: diff, **{k: w.astype(_WEIGHT_DTYPES[k]) for k, w in weights.items()}}
        y = _forward(args)
    with _jax.named_scope("loss_head"):
        err = _jnp.square(y.astype(_jnp.float32) - loss_target)
        return 0.5 * _jnp.sum(_jnp.mean(err, axis=-1)) if err.ndim else 0.5 * err


def _adamw(w, g, m, v):
    m = ADAM_B1 * m + (1.0 - ADAM_B1) * g
    v = ADAM_B2 * v + (1.0 - ADAM_B2) * _jnp.square(g)
    m_hat = m / (1.0 - ADAM_B1 ** ADAM_STEP)
    v_hat = v / (1.0 - ADAM_B2 ** ADAM_STEP)
    delta = -ADAM_LR * (m_hat / (_jnp.sqrt(v_hat) + ADAM_EPS) + ADAM_WD * w)
    return delta, m, v


def reference(x, a_norm, a_w_in, a_conv, a_log, a_dt_bias, a_out_norm, a_w_out, kv_norm, w_kv, k_norm, b_norm, b_w_q, q_norm, b_w_out, ffn_norm, ffn_w_up, ffn_conv, ffn_w_down, loss_target, m_a_norm, m_a_w_in, m_a_conv, m_a_log, m_a_dt_bias, m_a_out_norm, m_a_w_out, m_kv_norm, m_w_kv, m_k_norm, m_b_norm, m_b_w_q, m_q_norm, m_b_w_out, m_ffn_norm, m_ffn_w_up, m_ffn_conv, m_ffn_w_down, v_a_norm, v_a_w_in, v_a_conv, v_a_log, v_a_dt_bias, v_a_out_norm, v_a_w_out, v_kv_norm, v_w_kv, v_k_norm, v_b_norm, v_b_w_q, v_q_norm, v_b_w_out, v_ffn_norm, v_ffn_w_up, v_ffn_conv, v_ffn_w_down):
    given = dict(x=x, a_norm=a_norm, a_w_in=a_w_in, a_conv=a_conv, a_log=a_log, a_dt_bias=a_dt_bias, a_out_norm=a_out_norm, a_w_out=a_w_out, kv_norm=kv_norm, w_kv=w_kv, k_norm=k_norm, b_norm=b_norm, b_w_q=b_w_q, q_norm=q_norm, b_w_out=b_w_out, ffn_norm=ffn_norm, ffn_w_up=ffn_w_up, ffn_conv=ffn_conv, ffn_w_down=ffn_w_down, loss_target=loss_target, m_a_norm=m_a_norm, m_a_w_in=m_a_w_in, m_a_conv=m_a_conv, m_a_log=m_a_log, m_a_dt_bias=m_a_dt_bias, m_a_out_norm=m_a_out_norm, m_a_w_out=m_a_w_out, m_kv_norm=m_kv_norm, m_w_kv=m_w_kv, m_k_norm=m_k_norm, m_b_norm=m_b_norm, m_b_w_q=m_b_w_q, m_q_norm=m_q_norm, m_b_w_out=m_b_w_out, m_ffn_norm=m_ffn_norm, m_ffn_w_up=m_ffn_w_up, m_ffn_conv=m_ffn_conv, m_ffn_w_down=m_ffn_w_down, v_a_norm=v_a_norm, v_a_w_in=v_a_w_in, v_a_conv=v_a_conv, v_a_log=v_a_log, v_a_dt_bias=v_a_dt_bias, v_a_out_norm=v_a_out_norm, v_a_w_out=v_a_w_out, v_kv_norm=v_kv_norm, v_w_kv=v_w_kv, v_k_norm=v_k_norm, v_b_norm=v_b_norm, v_b_w_q=v_b_w_q, v_q_norm=v_q_norm, v_b_w_out=v_b_w_out, v_ffn_norm=v_ffn_norm, v_ffn_w_up=v_ffn_w_up, v_ffn_conv=v_ffn_conv, v_ffn_w_down=v_ffn_w_down)
    weights = {n: given[n] for n in TWIN_WEIGHTS}
    shared = {n: given[n] for n in SHARED_INPUTS}
    per_example = {n: given[n] for n in ['x']}
    grad_fn = _jax.value_and_grad(_loss, argnums=(0, 1))

    def one_microbatch(ex, loss_target):
        ex = dict(ex)
        diff = ex.pop(TWIN_DIFF_INPUT)
        return grad_fn(weights, diff, {**shared, **ex}, loss_target)

    if N_MICROBATCH == 1:
        loss, (grad_w, grad_x) = one_microbatch(per_example, given["loss_target"])
    else:
        def body(carry, xs):
            loss_sum, grad_sum = carry
            l_k, (gw_k, gx_k) = one_microbatch(xs[0], xs[1])
            with _jax.named_scope("update"):
                return (loss_sum + l_k, _jax.tree.map(_jnp.add, grad_sum, gw_k)), gx_k

        init = (_jnp.zeros((), _jnp.float32), _jax.tree.map(_jnp.zeros_like, weights))
        (loss, grad_w), grad_x = _jax.lax.scan(body, init, (per_example, given["loss_target"]))
    with _jax.named_scope("update"):
        delta_w, new_m, new_v = {}, {}, {}
        for n in TWIN_WEIGHTS:
            delta_w[n], new_m[n], new_v[n] = _adamw(weights[n], grad_w[n], given["m_" + n], given["v_" + n])
    return (loss, grad_x, *[grad_w[n] for n in TWIN_WEIGHTS], *[delta_w[n] for n in TWIN_WEIGHTS],
            *[new_m[n] for n in TWIN_WEIGHTS], *[new_v[n] for n in TWIN_WEIGHTS])
```

```python
import math

import jax
import jax.numpy as jnp
from jax import lax
from jax.experimental import pallas as pl
from jax.experimental.pallas import tpu as pltpu

F32 = jnp.float32
BF16 = jnp.bfloat16

D_MODEL = 1024
HEADS = 8
HEAD_DIM = 128
GDN_CONV = 4
GDN_CHUNK = 64
D_FF = 2816
FFN_CONV = 3
EPS = 1e-6
N_A = 2
N_B = 2
DEPTH = N_A + N_B
W_IN_COLS = 4 * D_MODEL + 2 * HEADS
W_IN_PAD = 4 * D_MODEL + 128
Z_BLOCK = 3 * D_MODEL // 128
AB_BLOCK = 4 * D_MODEL // 128

ADAM_LR = 0.001
ADAM_B1 = 0.9
ADAM_B2 = 0.999
ADAM_EPS = 1e-08
ADAM_WD = 0.01
ADAM_STEP = 10

LANES = 128
SUBLANES = 8
VMEM_LIMIT = 56 * 1024 * 1024
HALO = SUBLANES
N_CHIPS = 4
N_DEV = 8

HI = lax.Precision.HIGHEST
MESH = pl.DeviceIdType.MESH
ANY = pl.BlockSpec(memory_space=pl.ANY)


def _cp(*sem):
    return pltpu.CompilerParams(dimension_semantics=sem, vmem_limit_bytes=VMEM_LIMIT)


def _tile(n, want, align=SUBLANES):
    t = (min(n, want) // align) * align
    while t > 0 and n % t:
        t -= align
    return t if t > 0 else n


def _dot(a, b, precision=None):
    return jnp.dot(a, b, preferred_element_type=F32, precision=precision)


def _dot_nt(a, b, precision=None):
    return lax.dot_general(a, b, (((1,), (1,)), ((), ())), preferred_element_type=F32, precision=precision)


def _dot_tn(a, b, precision=None):
    return lax.dot_general(a, b, (((0,), (0,)), ((), ())), preferred_element_type=F32, precision=precision)


def _bf(x):
    return x.astype(BF16)


def _sigmoid(x):
    return 1.0 / (1.0 + jnp.exp(-x))


def _softplus(x):
    return jnp.maximum(x, 0.0) + jnp.log(1.0 + jnp.exp(-jnp.abs(x)))


def _silu(x):
    return x * _sigmoid(x)


def _silu_grad(x):
    s = _sigmoid(x)
    return s * (1.0 + x * (1.0 - s))


def _fold_rows(v):
    return jnp.sum(v.reshape(v.shape[0] // SUBLANES, SUBLANES, v.shape[1]), axis=0)


def _accumulate(ref, value, axis):
    @pl.when(pl.program_id(axis) == 0)
    def _():
        ref[...] = jnp.zeros_like(ref)
    ref[...] += value


def norm_matmul(x, gain, w, *, name):
    T, D = x.shape
    N = w.shape[1]
    tm, tn = _tile(T, 512), _tile(N, 512, LANES)

    def body(x_ref, g_ref, w_ref, y_ref, h_ref):
        @pl.when(pl.program_id(1) == 0)
        def _():
            xf = x_ref[...]
            r = lax.rsqrt(jnp.mean(xf * xf, axis=-1, keepdims=True) + EPS)
            h_ref[...] = _bf(xf * r * g_ref[...])
        y_ref[...] = _dot(h_ref[...], w_ref[...])

    return pl.pallas_call(
        body, name=name, grid=(T // tm, N // tn),
        in_specs=[pl.BlockSpec((tm, D), lambda i, j: (i, 0)), pl.BlockSpec((1, D), lambda i, j: (0, 0)),
                  pl.BlockSpec((D, tn), lambda i, j: (0, j))],
        out_specs=[pl.BlockSpec((tm, tn), lambda i, j: (i, j)), pl.BlockSpec((tm, D), lambda i, j: (i, 0))],
        out_shape=[jax.ShapeDtypeStruct((T, N), F32), jax.ShapeDtypeStruct((T, D), BF16)],
        compiler_params=_cp("parallel", "arbitrary"),
    )(x, gain, w)


def matmul_residual(a, w, res, *, name):
    T, K = a.shape
    N = w.shape[1]
    tm, tn = _tile(T, 512), _tile(N, 512, LANES)

    def body(a_ref, w_ref, r_ref, o_ref):
        o_ref[...] = r_ref[...] + _dot(_bf(a_ref[...]), w_ref[...])

    return pl.pallas_call(
        body, name=name, grid=(T // tm, N // tn),
        in_specs=[pl.BlockSpec((tm, K), lambda i, j: (i, 0)), pl.BlockSpec((K, tn), lambda i, j: (0, j)),
                  pl.BlockSpec((tm, tn), lambda i, j: (i, j))],
        out_specs=pl.BlockSpec((tm, tn), lambda i, j: (i, j)),
        out_shape=jax.ShapeDtypeStruct((T, N), F32),
        compiler_params=_cp("parallel", "parallel"),
    )(a, w, res)


def matmul_nt(dy, w, *, name, out_dtype=F32):
    T, N = dy.shape
    K = w.shape[0]
    tm, tk = _tile(T, 256), _tile(K, 512, LANES)

    def body(dy_ref, w_ref, o_ref):
        o_ref[...] = _dot_nt(_bf(dy_ref[...]), w_ref[...]).astype(out_dtype)

    return pl.pallas_call(
        body, name=name, grid=(T // tm, K // tk),
        in_specs=[pl.BlockSpec((tm, N), lambda i, j: (i, 0)), pl.BlockSpec((tk, N), lambda i, j: (j, 0))],
        out_specs=pl.BlockSpec((tm, tk), lambda i, j: (i, j)),
        out_shape=jax.ShapeDtypeStruct((T, K), out_dtype),
        compiler_params=_cp("parallel", "parallel"),
    )(dy, w)


def matmul_nt_normbwd(dy, w, x, gain, dres, *, name):
    T, N = dy.shape
    D = w.shape[0]
    tm = _tile(T, 256)

    def body(dy_ref, w_ref, x_ref, g_ref, dr_ref, dx_ref, dg_ref):
        dh = _dot_nt(_bf(dy_ref[...]), w_ref[...])
        xf = x_ref[...]
        r = lax.rsqrt(jnp.mean(xf * xf, axis=-1, keepdims=True) + EPS)
        xh = xf * r
        dxh = dh * g_ref[...]
        dx_ref[...] = dr_ref[...] + r * (dxh - xh * jnp.mean(dxh * xh, axis=-1, keepdims=True))
        _accumulate(dg_ref, _fold_rows(dh * xh), 0)

    return pl.pallas_call(
        body, name=name, grid=(T // tm,),
        in_specs=[pl.BlockSpec((tm, N), lambda i: (i, 0)), pl.BlockSpec((D, N), lambda i: (0, 0)),
                  pl.BlockSpec((tm, D), lambda i: (i, 0)), pl.BlockSpec((1, D), lambda i: (0, 0)),
                  pl.BlockSpec((tm, D), lambda i: (i, 0))],
        out_specs=[pl.BlockSpec((tm, D), lambda i: (i, 0)), pl.BlockSpec((SUBLANES, D), lambda i: (0, 0))],
        out_shape=[jax.ShapeDtypeStruct((T, D), F32), jax.ShapeDtypeStruct((SUBLANES, D), F32)],
        compiler_params=_cp("arbitrary"),
    )(dy, w, x, gain, dres)


def matmul_tn(a, dy, *, name):
    T, K = a.shape
    N = dy.shape[1]
    tk, tn, tm = _tile(K, 512, LANES), _tile(N, 512, LANES), _tile(T, 1024)

    def body(a_ref, dy_ref, o_ref):
        _accumulate(o_ref, _dot_tn(_bf(a_ref[...]), _bf(dy_ref[...])), 2)

    return pl.pallas_call(
        body, name=name, grid=(K // tk, N // tn, T // tm),
        in_specs=[pl.BlockSpec((tm, tk), lambda i, j, t: (t, i)), pl.BlockSpec((tm, tn), lambda i, j, t: (t, j))],
        out_specs=pl.BlockSpec((tk, tn), lambda i, j, t: (i, j)),
        out_shape=jax.ShapeDtypeStruct((K, N), F32),
        compiler_params=_cp("parallel", "parallel", "arbitrary"),
    )(a, dy)


def _halo_specs(T, tt, tc, col):
    per = tt // HALO
    last = T // HALO - 1
    return [pl.BlockSpec((HALO, tc), lambda j, i: (jnp.maximum(i * per - 1, 0), col(j))),
            pl.BlockSpec((tt, tc), lambda j, i: (i, col(j))),
            pl.BlockSpec((HALO, tc), lambda j, i: (jnp.minimum((i + 1) * per, last), col(j)))]


def _extend(prev_ref, cur_ref, next_ref, nt):
    i = pl.program_id(1)
    p = jnp.where(i > 0, prev_ref[...].astype(F32), 0.0)
    q = jnp.where(i < nt - 1, next_ref[...].astype(F32), 0.0)
    return jnp.concatenate([p, cur_ref[...].astype(F32), q], axis=0)


def _rows_before(e, s):
    return e if s == 0 else pltpu.roll(e, s, 0)


def _rows_after(e, s):
    return e if s == 0 else pltpu.roll(e, e.shape[0] - s, 0)


def _causal_conv(e, w, taps):
    y = w[taps - 1:taps, :] * e
    for s in range(1, taps):
        y = y + w[taps - 1 - s:taps - s, :] * _rows_before(e, s)
    return y


def _causal_conv_bwd(e, dc, w, taps, tt):
    lo, hi = HALO, HALO + tt
    dx = w[taps - 1:taps, :] * dc
    dws = [None] * taps
    dws[taps - 1] = jnp.sum((e * dc)[lo:hi], axis=0, keepdims=True)
    for s in range(1, taps):
        dx = dx + w[taps - 1 - s:taps - s, :] * _rows_after(dc, s)
        dws[taps - 1 - s] = jnp.sum((_rows_before(e, s) * dc)[lo:hi], axis=0, keepdims=True)
    dw = jnp.concatenate(dws + [jnp.zeros((SUBLANES - taps, e.shape[1]), F32)], axis=0)
    return dx[lo:hi], dw


def _qkv_kind(col_block):
    return (col_block >= HEADS).astype(jnp.int32) + (col_block >= 2 * HEADS).astype(jnp.int32)


def _l2norm_scale(kind):
    return jnp.where(kind == 0, HEAD_DIM ** -0.5, 1.0)


def gdn_conv_fwd(proj, conv_w, *, name):
    T = proj.shape[0]
    tt, tc = _tile(T, 512), HEAD_DIM
    nt = T // tt

    def body(p_ref, c_ref, n_ref, w_ref, o_ref):
        kind = _qkv_kind(pl.program_id(0))
        e = _extend(p_ref, c_ref, n_ref, nt)
        s = _silu(_causal_conv(e, w_ref[...], GDN_CONV))[HALO:HALO + tt]
        r = lax.rsqrt(jnp.sum(s * s, axis=-1, keepdims=True) + EPS) * _l2norm_scale(kind)
        o_ref[...] = jnp.where(kind == 2, s, s * r)

    return pl.pallas_call(
        body, name=name, grid=(3 * HEADS, nt),
        in_specs=_halo_specs(T, tt, tc, lambda j: j) + [pl.BlockSpec((SUBLANES, tc), lambda j, i: (0, j))],
        out_specs=pl.BlockSpec((tt, tc), lambda j, i: (i, j)),
        out_shape=jax.ShapeDtypeStruct((T, 3 * D_MODEL), F32),
        compiler_params=_cp("parallel", "parallel"),
    )(proj, proj, proj, conv_w)


def gdn_conv_bwd(proj, conv_w, dq, dk, dv, *, name):
    T = proj.shape[0]
    tt, tc = _tile(T, 512), HEAD_DIM
    nt = T // tt

    def body(p_ref, c_ref, n_ref, w_ref, q0, q1, q2, k0, k1, k2, v0, v1, v2, dx_ref, dw_ref):
        kind = _qkv_kind(pl.program_id(0))
        w = w_ref[...]
        e = _extend(p_ref, c_ref, n_ref, nt)
        c = _causal_conv(e, w, GDN_CONV)
        s = _silu(c)
        dy = jnp.where(kind == 0, _extend(q0, q1, q2, nt),
                       jnp.where(kind == 1, _extend(k0, k1, k2, nt), _extend(v0, v1, v2, nt)))
        r = lax.rsqrt(jnp.sum(s * s, axis=-1, keepdims=True) + EPS)
        y = s * r
        ds_norm = r * _l2norm_scale(kind) * (dy - y * jnp.sum(dy * y, axis=-1, keepdims=True))
        ds = jnp.where(kind == 2, dy, ds_norm)
        dx, dw = _causal_conv_bwd(e, ds * _silu_grad(c), w, GDN_CONV, tt)
        dx_ref[...] = _bf(dx)
        _accumulate(dw_ref, dw, 1)

    grads = []
    for _ in range(3):
        grads += _halo_specs(T, tt, tc, lambda j: j % HEADS)
    return pl.pallas_call(
        body, name=name, grid=(3 * HEADS, nt),
        in_specs=_halo_specs(T, tt, tc, lambda j: j) + [pl.BlockSpec((SUBLANES, tc), lambda j, i: (0, j))] + grads,
        out_specs=[pl.BlockSpec((tt, tc), lambda j, i: (i, j)), pl.BlockSpec((SUBLANES, tc), lambda j, i: (0, j))],
        out_shape=[jax.ShapeDtypeStruct((T, 3 * D_MODEL), BF16), jax.ShapeDtypeStruct((SUBLANES, 3 * D_MODEL), F32)],
        compiler_params=_cp("parallel", "arbitrary"),
    )(proj, proj, proj, conv_w, dq, dq, dq, dk, dk, dk, dv, dv, dv)


FFN_COLS = 256


def ffn_act_fwd(u, conv_w, *, name):
    T = u.shape[0]
    tt, tc = _tile(T, 512), FFN_COLS
    half = D_FF // tc
    nt = T // tt

    def body(gp, gc, gn, up, uc, un, wg_ref, wu_ref, o_ref):
        gate = _causal_conv(_extend(gp, gc, gn, nt), wg_ref[...], FFN_CONV)
        up_ = _causal_conv(_extend(up, uc, un, nt), wu_ref[...], FFN_CONV)
        o_ref[...] = _bf((_silu(gate) * up_)[HALO:HALO + tt])

    return pl.pallas_call(
        body, name=name, grid=(half, nt),
        in_specs=_halo_specs(T, tt, tc, lambda j: j) + _halo_specs(T, tt, tc, lambda j: j + half)
        + [pl.BlockSpec((SUBLANES, tc), lambda j, i: (0, j)), pl.BlockSpec((SUBLANES, tc), lambda j, i: (0, j + half))],
        out_specs=pl.BlockSpec((tt, tc), lambda j, i: (i, j)),
        out_shape=jax.ShapeDtypeStruct((T, D_FF), BF16),
        compiler_params=_cp("parallel", "parallel"),
    )(u, u, u, u, u, u, conv_w, conv_w)


def ffn_act_bwd(u, conv_w, dact, *, name):
    T = u.shape[0]
    tt, tc = _tile(T, 512), FFN_COLS
    half = D_FF // tc
    nt = T // tt

    def body(gp, gc, gn, up, uc, un, wg_ref, wu_ref, dp, dc_, dn, dug_ref, duu_ref, dwg_ref, dwu_ref):
        wg, wu = wg_ref[...], wu_ref[...]
        eg, eu = _extend(gp, gc, gn, nt), _extend(up, uc, un, nt)
        gate, up_ = _causal_conv(eg, wg, FFN_CONV), _causal_conv(eu, wu, FFN_CONV)
        da = _extend(dp, dc_, dn, nt)
        dxg, dwg = _causal_conv_bwd(eg, da * up_ * _silu_grad(gate), wg, FFN_CONV, tt)
        dxu, dwu = _causal_conv_bwd(eu, da * _silu(gate), wu, FFN_CONV, tt)
        dug_ref[...] = _bf(dxg)
        duu_ref[...] = _bf(dxu)
        _accumulate(dwg_ref, dwg, 1)
        _accumulate(dwu_ref, dwu, 1)

    return pl.pallas_call(
        body, name=name, grid=(half, nt),
        in_specs=_halo_specs(T, tt, tc, lambda j: j) + _halo_specs(T, tt, tc, lambda j: j + half)
        + [pl.BlockSpec((SUBLANES, tc), lambda j, i: (0, j)), pl.BlockSpec((SUBLANES, tc), lambda j, i: (0, j + half))]
        + _halo_specs(T, tt, tc, lambda j: j),
        out_specs=[pl.BlockSpec((tt, tc), lambda j, i: (i, j)), pl.BlockSpec((tt, tc), lambda j, i: (i, j)),
                   pl.BlockSpec((SUBLANES, tc), lambda j, i: (0, j)), pl.BlockSpec((SUBLANES, tc), lambda j, i: (0, j))],
        out_shape=[jax.ShapeDtypeStruct((T, D_FF), BF16), jax.ShapeDtypeStruct((T, D_FF), BF16),
                   jax.ShapeDtypeStruct((SUBLANES, D_FF), F32), jax.ShapeDtypeStruct((SUBLANES, D_FF), F32)],
        compiler_params=_cp("parallel", "arbitrary"),
    )(u, u, u, u, u, u, conv_w, conv_w, dact, dact, dact)


def head_norm_fwd(x, gain, z=None, *, x_col=0, z_col=0, name, out_dtype=F32):
    T = x.shape[0]
    tt = _tile(T, 1024)
    gated = z is not None

    def body(*refs):
        x_ref, g_ref = refs[0], refs[1]
        o_ref = refs[-1]
        xf = x_ref[...]
        y = xf * lax.rsqrt(jnp.mean(xf * xf, axis=-1, keepdims=True) + EPS) * g_ref[...]
        if gated:
            y = y * _silu(refs[2][...])
        o_ref[...] = y.astype(out_dtype)

    ins = [pl.BlockSpec((tt, HEAD_DIM), lambda h, i: (i, x_col + h)), pl.BlockSpec((1, HEAD_DIM), lambda h, i: (0, 0))]
    args = [x, gain]
    if gated:
        ins.append(pl.BlockSpec((tt, HEAD_DIM), lambda h, i: (i, z_col + h)))
        args.append(z)
    return pl.pallas_call(
        body, name=name, grid=(HEADS, T // tt), in_specs=ins,
        out_specs=pl.BlockSpec((tt, HEAD_DIM), lambda h, i: (i, h)),
        out_shape=jax.ShapeDtypeStruct((T, D_MODEL), out_dtype),
        compiler_params=_cp("parallel", "parallel"),
    )(*args)


def head_norm_bwd(x, gain, dys, z=None, *, x_col=0, z_col=0, name, dx_dtype=F32):
    T = x.shape[0]
    tt = _tile(T, 1024)
    gated = z is not None
    nd = len(dys)

    def body(*refs):
        x_ref, g_ref = refs[0], refs[1]
        xf = x_ref[...]
        r = lax.rsqrt(jnp.mean(xf * xf, axis=-1, keepdims=True) + EPS)
        xh = xf * r
        dy = refs[2][...].astype(F32)
        for d_ref in refs[3:2 + nd]:
            dy = dy + d_ref[...].astype(F32)
        outs = refs[2 + nd + (1 if gated else 0):]
        if gated:
            zf = refs[2 + nd][...]
            dx_ref, dz_ref, dg_ref = outs
            dz_ref[...] = _bf(dy * xh * g_ref[...] * _silu_grad(zf))
            dn = dy * _silu(zf)
        else:
            dx_ref, dg_ref = outs
            dn = dy
        dxh = dn * g_ref[...]
        dx_ref[...] = (r * (dxh - xh * jnp.mean(dxh * xh, axis=-1, keepdims=True))).astype(dx_dtype)
        _accumulate(dg_ref, _fold_rows(dn * xh), 1)

    tile = pl.BlockSpec((tt, HEAD_DIM), lambda h, i: (i, h))
    ins = [pl.BlockSpec((tt, HEAD_DIM), lambda h, i: (i, x_col + h)), pl.BlockSpec((1, HEAD_DIM), lambda h, i: (0, 0))] + [tile] * nd
    args = [x, gain] + list(dys)
    outs = [tile]
    shapes = [jax.ShapeDtypeStruct((T, D_MODEL), dx_dtype)]
    if gated:
        ins.append(pl.BlockSpec((tt, HEAD_DIM), lambda h, i: (i, z_col + h)))
        args.append(z)
        outs.append(tile)
        shapes.append(jax.ShapeDtypeStruct((T, D_MODEL), BF16))
    outs.append(pl.BlockSpec((SUBLANES, HEAD_DIM), lambda h, i: (0, h)))
    shapes.append(jax.ShapeDtypeStruct((SUBLANES, D_MODEL), F32))
    return pl.pallas_call(
        body, name=name, grid=(HEADS, T // tt), in_specs=ins, out_specs=outs, out_shape=shapes,
        compiler_params=_cp("parallel", "arbitrary"),
    )(*args)


def gates_fwd(proj, a_log, dt_bias, *, name):
    T = proj.shape[0]
    tt = _tile(T, 1024)

    def body(p_ref, al_ref, dt_ref, o_ref):
        p = p_ref[...]
        lane = lax.broadcasted_iota(jnp.int32, p.shape, 1)
        o_ref[...] = jnp.where(lane < HEADS, -jnp.exp(al_ref[...]) * _softplus(p + dt_ref[...]), _sigmoid(p))

    return pl.pallas_call(
        body, name=name, grid=(T // tt,),
        in_specs=[pl.BlockSpec((tt, LANES), lambda i: (i, AB_BLOCK)), pl.BlockSpec((1, LANES), lambda i: (0, 0)),
                  pl.BlockSpec((1, LANES), lambda i: (0, 0))],
        out_specs=pl.BlockSpec((tt, LANES), lambda i: (i, 0)),
        out_shape=jax.ShapeDtypeStruct((T, LANES), F32),
        compiler_params=_cp("parallel"),
    )(proj, a_log, dt_bias)


def gates_bwd(proj, a_log, dt_bias, dgate, *, name):
    T = proj.shape[0]
    tt = _tile(T, 1024)

    def body(p_ref, al_ref, dt_ref, d_ref, dp_ref, dal_ref, ddt_ref):
        p, d = p_ref[...], d_ref[...]
        lane = lax.broadcasted_iota(jnp.int32, p.shape, 1)
        ea = jnp.exp(al_ref[...])
        pa = p + dt_ref[...]
        da = -d * ea * _sigmoid(pa)
        b = _sigmoid(p)
        dp_ref[...] = _bf(jnp.where(lane < HEADS, da, jnp.where(lane < 2 * HEADS, d * b * (1.0 - b), 0.0)))
        _accumulate(dal_ref, _fold_rows(jnp.where(lane < HEADS, -d * ea * _softplus(pa), 0.0)), 0)
        _accumulate(ddt_ref, _fold_rows(jnp.where(lane < HEADS, da, 0.0)), 0)

    acc = pl.BlockSpec((SUBLANES, LANES), lambda i: (0, 0))
    return pl.pallas_call(
        body, name=name, grid=(T // tt,),
        in_specs=[pl.BlockSpec((tt, LANES), lambda i: (i, AB_BLOCK)), pl.BlockSpec((1, LANES), lambda i: (0, 0)),
                  pl.BlockSpec((1, LANES), lambda i: (0, 0)), pl.BlockSpec((tt, LANES), lambda i: (i, 0))],
        out_specs=[pl.BlockSpec((tt, LANES), lambda i: (i, 0)), acc, acc],
        out_shape=[jax.ShapeDtypeStruct((T, LANES), BF16), jax.ShapeDtypeStruct((SUBLANES, LANES), F32),
                   jax.ShapeDtypeStruct((SUBLANES, LANES), F32)],
        compiler_params=_cp("arbitrary"),
    )(proj, a_log, dt_bias, dgate)


def loss_fwd(y, target, *, name):
    T, D = y.shape
    tt = _tile(T, 512)

    def body(y_ref, t_ref, dy_ref, l_ref):
        d = y_ref[...] - t_ref[...]
        dy_ref[...] = d * (1.0 / D)
        sq = d * d
        lanes = sq[:, 0:LANES]
        for c in range(1, D // LANES):
            lanes = lanes + sq[:, c * LANES:(c + 1) * LANES]
        _accumulate(l_ref, _fold_rows(lanes) * (0.5 / D), 0)

    return pl.pallas_call(
        body, name=name, grid=(T // tt,),
        in_specs=[pl.BlockSpec((tt, D), lambda i: (i, 0)), pl.BlockSpec((tt, D), lambda i: (i, 0))],
        out_specs=[pl.BlockSpec((tt, D), lambda i: (i, 0)), pl.BlockSpec((SUBLANES, LANES), lambda i: (0, 0))],
        out_shape=[jax.ShapeDtypeStruct((T, D), F32), jax.ShapeDtypeStruct((SUBLANES, LANES), F32)],
        compiler_params=_cp("arbitrary"),
    )(y, target)


def _unit_lower_inverse(low):
    c = low.shape[0]
    ii = lax.broadcasted_iota(jnp.int32, (c, c), 0)
    jj = lax.broadcasted_iota(jnp.int32, (c, c), 1)
    inv = jnp.where(ii == jj, 1.0, 0.0) - low
    power = low
    for _ in range(int(math.log2(c)) - 1):
        power = _dot(power, power, HI)
        inv = inv + _dot(inv, power, HI)
    return inv


def _gdn_chunk(q, k, v, a_col, a_row, b_col, s0):
    c = q.shape[0]
    ii = lax.broadcasted_iota(jnp.int32, (c, c), 0)
    jj = lax.broadcasted_iota(jnp.int32, (c, c), 1)
    tri, strict = ii >= jj, ii > jj
    g_col = jnp.sum(jnp.where(tri, a_row, 0.0), axis=1, keepdims=True)
    g_row = jnp.sum(jnp.where(ii <= jj, a_col, 0.0), axis=0, keepdims=True)
    gam = jnp.exp(jnp.where(tri, g_col - g_row, -jnp.inf))
    g_last = jnp.sum(a_col, axis=0, keepdims=True)
    gam_col = jnp.exp(g_col)
    del_col = jnp.exp(g_last - g_col)
    kb = k * b_col
    m = _dot_nt(_bf(kb), _bf(k))
    inv = _unit_lower_inverse(jnp.where(strict, m * gam, 0.0))
    ks = _dot(_bf(k), _bf(s0))
    e = v - gam_col * ks
    vn = _dot(inv, b_col * e, HI)
    qk = _dot_nt(_bf(q), _bf(k))
    p = jnp.where(tri, qk * gam, 0.0)
    return dict(tri=tri, strict=strict, ii=ii, jj=jj, gam=gam, g_last=g_last, gam_col=gam_col, del_col=del_col,
                kb=kb, m=m, inv=inv, ks=ks, e=e, vn=vn, qk=qk, p=p)


def gdn_fwd(qkv, a_col, a_row, b_col, *, name):
    T = qkv.shape[0]
    C = GDN_CHUNK
    N = T // C

    def body(q_ref, k_ref, v_ref, ac_ref, ar_ref, bc_ref, o_ref, s_ref, state):
        @pl.when(pl.program_id(1) == 0)
        def _():
            state[...] = jnp.zeros_like(state)
        q, k, v = q_ref[...], k_ref[...], v_ref[...]
        s0 = state[...]
        s_ref[0, 0] = s0
        w = _gdn_chunk(q, k, v, ac_ref[0], ar_ref[0, 0], bc_ref[0], s0)
        o_ref[...] = w["gam_col"] * _dot(_bf(q), _bf(s0)) + _dot(_bf(w["p"]), _bf(w["vn"]))
        state[...] = jnp.exp(w["g_last"]) * s0 + _dot_tn(_bf(w["del_col"] * k), _bf(w["vn"]))

    blk = lambda off: pl.BlockSpec((C, HEAD_DIM), lambda h, n: (n, off + h))
    col = pl.BlockSpec((1, C, 1), lambda h, n: (h, n, 0))
    row = pl.BlockSpec((1, 1, 1, C), lambda h, n: (h, n, 0, 0))
    return pl.pallas_call(
        body, name=name, grid=(HEADS, N),
        in_specs=[blk(0), blk(HEADS), blk(2 * HEADS), col, row, col],
        out_specs=[blk(0), pl.BlockSpec((1, 1, HEAD_DIM, HEAD_DIM), lambda h, n: (h, n, 0, 0))],
        out_shape=[jax.ShapeDtypeStruct((T, D_MODEL), F32), jax.ShapeDtypeStruct((HEADS, N, HEAD_DIM, HEAD_DIM), F32)],
        scratch_shapes=[pltpu.VMEM((HEAD_DIM, HEAD_DIM), F32)],
        compiler_params=_cp("parallel", "arbitrary"),
    )(qkv, qkv, qkv, a_col, a_row, b_col)


def gdn_bwd(qkv, a_col, a_row, b_col, states, do, *, name):
    T = qkv.shape[0]
    C = GDN_CHUNK
    N = T // C
    rev = lambda n: N - 1 - n
    col = pl.BlockSpec((1, C, 1), lambda h, n: (h, rev(n), 0))
    row = pl.BlockSpec((1, 1, 1, C), lambda h, n: (h, rev(n), 0, 0))

    def body(q_ref, k_ref, v_ref, ac_ref, ar_ref, bc_ref, s_ref, do_ref, dq_ref, dk_ref, dv_ref, da_ref, db_ref, dstate):
        @pl.when(pl.program_id(1) == 0)
        def _():
            dstate[...] = jnp.zeros_like(dstate)
        q, k, v = q_ref[...], k_ref[...], v_ref[...]
        b_col_ = bc_ref[0]
        s0, ds1, dout = s_ref[0, 0], dstate[...], do_ref[...]
        w = _gdn_chunk(q, k, v, ac_ref[0], ar_ref[0, 0], b_col_, s0)
        gam, gam_col, del_col, vn = w["gam"], w["gam_col"], w["del_col"], w["vn"]
        s0b, ds1b, doutb, kbf, qbf, vnb = _bf(s0), _bf(ds1), _bf(dout), _bf(k), _bf(q), _bf(vn)
        rows = lambda t: jnp.sum(t, axis=1, keepdims=True)

        dvn = _dot_tn(_bf(w["p"]), doutb) + _dot(_bf(del_col * k), ds1b)
        dqk = jnp.where(w["tri"], _dot_nt(doutb, vnb), 0.0) * gam
        qs = _dot(qbf, s0b)
        dq = gam_col * _dot_nt(doutb, s0b) + _dot(_bf(dqk), kbf)
        dk = _dot_tn(_bf(dqk), qbf)
        dg = gam_col * rows(dout * qs)
        dkd = _dot_nt(vnb, ds1b)
        dk = dk + del_col * dkd
        ddel = del_col * rows(dkd * k)
        dg = dg - ddel
        dg_last = jnp.sum(ddel, axis=0, keepdims=True) + jnp.exp(w["g_last"]) * jnp.sum(rows(ds1 * s0), axis=0, keepdims=True)
        dr = _dot_tn(w["inv"], dvn, HI)
        dm = jnp.where(w["strict"], -_dot_nt(_bf(dr), vnb), 0.0) * gam
        dkb = _dot(_bf(dm), kbf)
        dk = dk + _dot_tn(_bf(dm), _bf(w["kb"]))
        dbeta = rows(dr * w["e"]) + rows(dkb * k)
        de = b_col_ * dr
        dg = dg - gam_col * rows(de * w["ks"])
        dks = -gam_col * de
        dk = dk + _dot_nt(_bf(dks), s0b) + b_col_ * dkb
        dstate[...] = jnp.exp(w["g_last"]) * ds1 + _dot_tn(_bf(gam_col * q), doutb) + _dot_tn(kbf, _bf(dks))
        wg = dqk * w["qk"] + dm * w["m"]
        dg = dg + rows(wg) - jnp.sum(wg, axis=0, keepdims=True).T
        ii_col = lax.broadcasted_iota(jnp.int32, (C, 1), 0)
        dg = dg + jnp.where(ii_col == C - 1, dg_last, 0.0)
        da_ref[0] = rows(jnp.where(w["jj"] >= w["ii"], dg.T, 0.0))
        dq_ref[...] = dq
        dk_ref[...] = dk
        dv_ref[...] = de
        db_ref[0] = dbeta

    blk = lambda off: pl.BlockSpec((C, HEAD_DIM), lambda h, n: (rev(n), off + h))
    return pl.pallas_call(
        body, name=name, grid=(HEADS, N),
        in_specs=[blk(0), blk(HEADS), blk(2 * HEADS), col, row, col,
                  pl.BlockSpec((1, 1, HEAD_DIM, HEAD_DIM), lambda h, n: (h, rev(n), 0, 0)), blk(0)],
        out_specs=[blk(0), blk(0), blk(0), col, col],
        out_shape=[jax.ShapeDtypeStruct((T, D_MODEL), F32)] * 3 + [jax.ShapeDtypeStruct((HEADS, T, 1), F32)] * 2,
        scratch_shapes=[pltpu.VMEM((HEAD_DIM, HEAD_DIM), F32)],
        compiler_params=_cp("parallel", "arbitrary"),
    )(qkv, qkv, qkv, a_col, a_row, b_col, states, do)


SB_BLOCK = 128


def _split_bf16(x):
    hi = _bf(x)
    return hi, _bf(x - hi.astype(F32))


def _sb_tile(qb, k_ref, i, j, blk):
    kb = _bf(k_ref[pl.ds(pl.multiple_of(j * blk, blk), blk), :])
    z = _dot_nt(qb, kb)
    t_idx = i * blk + lax.broadcasted_iota(jnp.int32, (blk, blk), 0)
    s_idx = j * blk + lax.broadcasted_iota(jnp.int32, (blk, blk), 1)
    mask = s_idx < t_idx
    lf = jnp.where(mask, -_softplus(z), 0.0)
    return z, mask, lf


def sb_fwd(q, k, v, *, k_col=0, v_col=0, name):
    T = q.shape[0]
    blk = _tile(T, SB_BLOCK)
    scale = HEAD_DIM ** -0.5

    def body(q_ref, k_ref, v_ref, o_ref, l_ref):
        i = pl.program_id(1)
        qb = _bf(q_ref[...] * scale)
        r_idx = lax.broadcasted_iota(jnp.int32, (blk, blk), 0)
        c_idx = lax.broadcasted_iota(jnp.int32, (blk, blk), 1)
        later = _bf(jnp.where(r_idx > c_idx, 1.0, 0.0))

        def step(jj, carry):
            acc, run = carry
            j = i - jj
            z, mask, lf = _sb_tile(qb, k_ref, i, j, blk)
            hi, lo = _split_bf16(lf)
            after = run + _dot(hi, later) + _dot(lo, later)
            a = jnp.where(mask, jnp.exp(z + lf + after), 0.0)
            vb = _bf(v_ref[pl.ds(pl.multiple_of(j * blk, blk), blk), :])
            return acc + _dot(_bf(a), vb), run + jnp.sum(lf, axis=1, keepdims=True)

        acc, run = lax.fori_loop(0, i + 1, step, (jnp.zeros((blk, HEAD_DIM), F32), jnp.zeros((blk, 1), F32)))
        o_ref[...] = acc
        l_ref[0] = run

    return pl.pallas_call(
        body, name=name, grid=(HEADS, T // blk),
        in_specs=[pl.BlockSpec((blk, HEAD_DIM), lambda h, i: (i, h)), pl.BlockSpec((T, HEAD_DIM), lambda h, i: (0, k_col + h)),
                  pl.BlockSpec((T, HEAD_DIM), lambda h, i: (0, v_col + h))],
        out_specs=[pl.BlockSpec((blk, HEAD_DIM), lambda h, i: (i, h)), pl.BlockSpec((1, blk, 1), lambda h, i: (h, i, 0))],
        out_shape=[jax.ShapeDtypeStruct((T, D_MODEL), F32), jax.ShapeDtypeStruct((HEADS, T, 1), F32)],
        compiler_params=_cp("parallel", "arbitrary"),
    )(q, k, v)


def sb_bwd(q, k, v, ltot, do, *, k_col=0, v_col=0, name):
    T = q.shape[0]
    blk = _tile(T, SB_BLOCK)
    scale = HEAD_DIM ** -0.5

    def body(q_ref, k_ref, v_ref, l_ref, do_ref, dq_ref, dk_ref, dv_ref):
        i = pl.program_id(1)

        @pl.when(i == 0)
        def _():
            dk_ref[...] = jnp.zeros_like(dk_ref)
            dv_ref[...] = jnp.zeros_like(dv_ref)

        qb = _bf(q_ref[...] * scale)
        dob = _bf(do_ref[...])
        ltot_ = l_ref[0]
        r_idx = lax.broadcasted_iota(jnp.int32, (blk, blk), 0)
        c_idx = lax.broadcasted_iota(jnp.int32, (blk, blk), 1)
        upto = _bf(jnp.where(r_idx <= c_idx, 1.0, 0.0))
        before = _bf(jnp.where(r_idx < c_idx, 1.0, 0.0))

        def step(j, carry):
            dq, lpre, cpre = carry
            z, mask, lf = _sb_tile(qb, k_ref, i, j, blk)
            hi, lo = _split_bf16(lf)
            after = ltot_ - (lpre + _dot(hi, upto) + _dot(lo, upto))
            ls = z + lf
            a = jnp.where(mask, jnp.exp(ls + after), 0.0)
            rows = pl.ds(pl.multiple_of(j * blk, blk), blk)
            vb = _bf(v_ref[rows, :])
            p = a * _dot_nt(dob, vb)
            phi, plo = _split_bf16(p)
            left = cpre + _dot(phi, before) + _dot(plo, before)
            dz = jnp.where(mask, p * jnp.exp(lf) - left * jnp.exp(ls), 0.0)
            dzb = _bf(dz)
            dk_ref[rows, :] += _dot_tn(dzb, qb)
            dv_ref[rows, :] += _dot_tn(_bf(a), dob)
            dq = dq + _dot(dzb, _bf(k_ref[rows, :]))
            return dq, lpre + jnp.sum(lf, axis=1, keepdims=True), cpre + jnp.sum(p, axis=1, keepdims=True)

        zero = jnp.zeros((blk, 1), F32)
        dq, _, _ = lax.fori_loop(0, i + 1, step, (jnp.zeros((blk, HEAD_DIM), F32), zero, zero))
        dq_ref[...] = dq * scale

    full = lambda off: pl.BlockSpec((T, HEAD_DIM), lambda h, i: (0, off + h))
    tile = pl.BlockSpec((blk, HEAD_DIM), lambda h, i: (i, h))
    return pl.pallas_call(
        body, name=name, grid=(HEADS, T // blk),
        in_specs=[tile, full(k_col), full(v_col), pl.BlockSpec((1, blk, 1), lambda h, i: (h, i, 0)), tile],
        out_specs=[tile, full(0), full(0)],
        out_shape=[jax.ShapeDtypeStruct((T, D_MODEL), F32)] * 3,
        compiler_params=_cp("parallel", "arbitrary"),
    )(q, k, v, ltot, do)


def sum_slots(slots, *, name):
    n, R, C = slots.shape
    tr = _tile(R, 256)

    def body(s_ref, o_ref):
        acc = s_ref[0].astype(F32)
        for k in range(1, n):
            acc = acc + s_ref[k].astype(F32)
        o_ref[...] = acc

    return pl.pallas_call(
        body, name=name, grid=(R // tr,),
        in_specs=[pl.BlockSpec((n, tr, C), lambda i: (0, i, 0))],
        out_specs=pl.BlockSpec((tr, C), lambda i: (i, 0)),
        out_shape=jax.ShapeDtypeStruct((R, C), F32),
        compiler_params=_cp("parallel"),
    )(slots)


def adamw(w, g_parts, m, v, *, name):
    R, C = w.shape
    tr = _tile(R, 256)
    n = len(g_parts)

    def body(*refs):
        w_ref, m_ref, v_ref = refs[0], refs[1 + n], refs[2 + n]
        g_ref, d_ref, nm_ref, nv_ref = refs[3 + n:]
        g = refs[1][...]
        for r in refs[2:1 + n]:
            g = g + r[...]
        m2 = ADAM_B1 * m_ref[...] + (1.0 - ADAM_B1) * g
        v2 = ADAM_B2 * v_ref[...] + (1.0 - ADAM_B2) * (g * g)
        m_hat = m2 / (1.0 - ADAM_B1 ** ADAM_STEP)
        v_hat = v2 / (1.0 - ADAM_B2 ** ADAM_STEP)
        g_ref[...] = g
        d_ref[...] = -ADAM_LR * (m_hat / (jnp.sqrt(v_hat) + ADAM_EPS) + ADAM_WD * w_ref[...])
        nm_ref[...] = m2
        nv_ref[...] = v2

    spec = pl.BlockSpec((tr, C), lambda i: (i, 0))
    return pl.pallas_call(
        body, name=name, grid=(R // tr,),
        in_specs=[spec] * (3 + n), out_specs=[spec] * 4,
        out_shape=[jax.ShapeDtypeStruct((R, C), F32)] * 4,
        compiler_params=_cp("parallel"),
    )(w, *g_parts, m, v)


CHIP_FLIPS = ((0, 1), (1, 0), (1, 1))


def _place():
    return lax.axis_index("x"), lax.axis_index("y"), lax.axis_index("c")


def _flip(v, f):
    return 1 - v if f else v


def _chip_exchange(arrays, *, scatter, name):
    n = len(arrays)
    if scatter:
        shapes = [jax.ShapeDtypeStruct(a.shape, a.dtype) for a in arrays]
    else:
        shapes = [jax.ShapeDtypeStruct((N_CHIPS,) + a.shape, a.dtype) for a in arrays]

    def body(*refs):
        ins, outs = refs[:n], refs[n:2 * n]
        send_sems, recv_sems, local_sems = refs[2 * n:]
        x, y, c = _place()
        me = 2 * x + y
        started = []
        for k in range(n):
            src_mine = ins[k].at[me] if scatter else ins[k]
            local = pltpu.make_async_copy(src_mine, outs[k].at[me], local_sems.at[k])
            local.start()
            started.append(local)
            for p, (fx, fy) in enumerate(CHIP_FLIPS):
                px, py = _flip(x, fx), _flip(y, fy)
                src = ins[k].at[2 * px + py] if scatter else ins[k]
                pltpu.make_async_remote_copy(src_ref=src, dst_ref=outs[k].at[me], send_sem=send_sems.at[k, p],
                                             recv_sem=recv_sems.at[k, p], device_id=(px, py, c), device_id_type=MESH).start()
        for k in range(n):
            for p, (fx, fy) in enumerate(CHIP_FLIPS):
                px, py = _flip(x, fx), _flip(y, fy)
                src = ins[k].at[2 * px + py] if scatter else ins[k]
                landing = pltpu.make_async_remote_copy(src_ref=src, dst_ref=outs[k].at[2 * px + py], send_sem=send_sems.at[k, p],
                                                       recv_sem=recv_sems.at[k, p], device_id=(px, py, c), device_id_type=MESH)
                landing.wait_send()
                landing.wait_recv()
        for local in started:
            local.wait()

    return pl.pallas_call(
        body, name=name, in_specs=[ANY] * n, out_specs=[ANY] * n, out_shape=shapes,
        scratch_shapes=[pltpu.SemaphoreType.DMA((n, len(CHIP_FLIPS))), pltpu.SemaphoreType.DMA((n, len(CHIP_FLIPS))),
                        pltpu.SemaphoreType.DMA((n,))],
    )(*arrays)


def sibling_swap(arrays, *, name):
    n = len(arrays)

    def body(*refs):
        ins, outs = refs[:n], refs[n:2 * n]
        send_sems, recv_sems = refs[2 * n:]
        x, y, c = _place()
        copies = [pltpu.make_async_remote_copy(src_ref=ins[k], dst_ref=outs[k], send_sem=send_sems.at[k], recv_sem=recv_sems.at[k],
                                               device_id=(x, y, 1 - c), device_id_type=MESH) for k in range(n)]
        for cp in copies:
            cp.start()
        for cp in copies:
            cp.wait_send()
            cp.wait_recv()

    return pl.pallas_call(
        body, name=name, in_specs=[ANY] * n, out_specs=[ANY] * n,
        out_shape=[jax.ShapeDtypeStruct(a.shape, a.dtype) for a in arrays],
        scratch_shapes=[pltpu.SemaphoreType.DMA((n,)), pltpu.SemaphoreType.DMA((n,))],
    )(*arrays)


DEVICE_FLIPS = tuple((fx, fy, fc) for fx in (0, 1) for fy in (0, 1) for fc in (0, 1) if fx or fy or fc)


def all_gather_devices(a, *, name):
    def body(a_ref, o_ref, send_sems, recv_sems, local_sem):
        x, y, c = _place()
        me = 4 * x + 2 * y + c
        local = pltpu.make_async_copy(a_ref, o_ref.at[me], local_sem)
        local.start()
        for p, (fx, fy, fc) in enumerate(DEVICE_FLIPS):
            peer = (_flip(x, fx), _flip(y, fy), _flip(c, fc))
            pltpu.make_async_remote_copy(src_ref=a_ref, dst_ref=o_ref.at[me], send_sem=send_sems.at[p], recv_sem=recv_sems.at[p],
                                         device_id=peer, device_id_type=MESH).start()
        for p, (fx, fy, fc) in enumerate(DEVICE_FLIPS):
            px, py, pc = _flip(x, fx), _flip(y, fy), _flip(c, fc)
            landing = pltpu.make_async_remote_copy(src_ref=a_ref, dst_ref=o_ref.at[4 * px + 2 * py + pc], send_sem=send_sems.at[p],
                                                   recv_sem=recv_sems.at[p], device_id=(px, py, pc), device_id_type=MESH)
            landing.wait_send()
            landing.wait_recv()
        local.wait()

    return pl.pallas_call(
        body, name=name, in_specs=[ANY], out_specs=ANY,
        out_shape=jax.ShapeDtypeStruct((N_DEV,) + a.shape, a.dtype),
        scratch_shapes=[pltpu.SemaphoreType.DMA((len(DEVICE_FLIPS),)), pltpu.SemaphoreType.DMA((len(DEVICE_FLIPS),)),
                        pltpu.SemaphoreType.DMA(())],
    )(a)


def _row(v):
    return v.reshape(1, -1)


def _pad_rows(w):
    return jnp.pad(w, ((0, SUBLANES - w.shape[0]), (0, 0)))


def _pad_lanes(v):
    return jnp.pad(v.reshape(1, -1), ((0, 0), (0, LANES - v.shape[-1])))


def _head_layouts(gates):
    T = gates.shape[0]
    a = gates[:, :HEADS].T
    b = gates[:, HEADS:2 * HEADS].T
    return a[:, :, None], a.reshape(HEADS, T // GDN_CHUNK, 1, GDN_CHUNK), b[:, :, None]


def _ffn_fwd(x, W, l):
    u, h = norm_matmul(x, _row(W["ffn_norm"][l]), W["ffn_w_up"][l], name=f"ffn{l}_up")
    conv = _pad_rows(W["ffn_conv"][l])
    act = ffn_act_fwd(u, conv, name=f"ffn{l}_act")
    y = matmul_residual(act, W["ffn_w_down"][l], x, name=f"ffn{l}_down")
    return y, (x, h, u, conv, act)


def _ffn_bwd(dx, saved, W, l, G):
    x, h, u, conv, act = saved
    dact = matmul_nt(dx, W["ffn_w_down"][l], name=f"ffn{l}_down_dx", out_dtype=BF16)
    G["ffn_w_down"][l] = matmul_tn(act, dx, name=f"ffn{l}_down_dw")
    dug, duu, dwg, dwu = ffn_act_bwd(u, conv, dact, name=f"ffn{l}_act_bwd")
    du = jnp.concatenate([dug, duu], axis=1)
    G["ffn_conv"][l] = jnp.concatenate([dwg, dwu], axis=1)[:FFN_CONV]
    G["ffn_w_up"][l] = matmul_tn(h, du, name=f"ffn{l}_up_dw")
    dx, dgain = matmul_nt_normbwd(du, W["ffn_w_up"][l], x, _row(W["ffn_norm"][l]), dx, name=f"ffn{l}_up_dx")
    G["ffn_norm"][l] = dgain.sum(0)
    return dx


def _gdn_fwd(x, W, l):
    proj, h = norm_matmul(x, _row(W["a_norm"][l]), W["a_w_in"][l], name=f"gdn{l}_in")
    conv = _pad_rows(W["a_conv"][l])
    a_log, dt_bias = _pad_lanes(W["a_log"][l]), _pad_lanes(W["a_dt_bias"][l])
    qkv = gdn_conv_fwd(proj, conv, name=f"gdn{l}_conv")
    heads = _head_layouts(gates_fwd(proj, a_log, dt_bias, name=f"gdn{l}_gates"))
    o, states = gdn_fwd(qkv, *heads, name=f"gdn{l}_rule")
    gain = _row(W["a_out_norm"][l])
    on = head_norm_fwd(o, gain, proj, z_col=Z_BLOCK, name=f"gdn{l}_outnorm", out_dtype=BF16)
    y = matmul_residual(on, W["a_w_out"][l], x, name=f"gdn{l}_out")
    return y, (x, h, proj, conv, a_log, dt_bias, qkv, heads, states, o, gain, on)


def _gdn_bwd(dx, saved, W, l, G):
    x, h, proj, conv, a_log, dt_bias, qkv, heads, states, o, gain, on = saved
    T = x.shape[0]
    don = matmul_nt(dx, W["a_w_out"][l], name=f"gdn{l}_out_dx")
    G["a_w_out"][l] = matmul_tn(on, dx, name=f"gdn{l}_out_dw")
    do, dz, dgain = head_norm_bwd(o, gain, [don], proj, z_col=Z_BLOCK, name=f"gdn{l}_outnorm_bwd")
    G["a_out_norm"][l] = dgain.reshape(SUBLANES, HEADS, HEAD_DIM).sum((0, 1))
    dq, dk, dv, da, db = gdn_bwd(qkv, *heads, states, do, name=f"gdn{l}_rule_bwd")
    dqkv, dconv = gdn_conv_bwd(proj, conv, dq, dk, dv, name=f"gdn{l}_conv_bwd")
    G["a_conv"][l] = dconv[:GDN_CONV]
    dgate = jnp.concatenate([da[:, :, 0].T, db[:, :, 0].T, jnp.zeros((T, LANES - 2 * HEADS), F32)], axis=1)
    dab, dal, ddt = gates_bwd(proj, a_log, dt_bias, dgate, name=f"gdn{l}_gates_bwd")
    G["a_log"][l] = dal.sum(0)[:HEADS]
    G["a_dt_bias"][l] = ddt.sum(0)[:HEADS]
    dproj = jnp.concatenate([dqkv, dz, dab], axis=1)
    G["a_w_in"][l] = matmul_tn(h, dproj, name=f"gdn{l}_in_dw")
    dx, dgain = matmul_nt_normbwd(dproj, W["a_w_in"][l], x, _row(W["a_norm"][l]), dx, name=f"gdn{l}_in_dx")
    G["a_norm"][l] = dgain.sum(0)
    return dx


def _sb_fwd(x, kn, kv, W, j):
    qp, h = norm_matmul(x, _row(W["b_norm"][j]), W["b_w_q"][j], name=f"sb{j}_q")
    gain = _row(W["q_norm"][j])
    q = head_norm_fwd(qp, gain, name=f"sb{j}_qnorm")
    o, ltot = sb_fwd(q, kn, kv, v_col=HEADS, name=f"sb{j}_attn")
    y = matmul_residual(o, W["b_w_out"][j], x, name=f"sb{j}_out")
    return y, (x, h, qp, gain, q, o, ltot)


def _sb_bwd(dx, saved, kn, kv, W, j, G):
    x, h, qp, gain, q, o, ltot = saved
    do = matmul_nt(dx, W["b_w_out"][j], name=f"sb{j}_out_dx")
    G["b_w_out"][j] = matmul_tn(o, dx, name=f"sb{j}_out_dw")
    dq, dk, dv = sb_bwd(q, kn, kv, ltot, do, v_col=HEADS, name=f"sb{j}_attn_bwd")
    dqp, dgain = head_norm_bwd(qp, gain, [dq], name=f"sb{j}_qnorm_bwd", dx_dtype=BF16)
    G["q_norm"][j] = dgain.reshape(SUBLANES, HEADS, HEAD_DIM).sum((0, 1))
    G["b_w_q"][j] = matmul_tn(h, dqp, name=f"sb{j}_q_dw")
    dx, dgain = matmul_nt_normbwd(dqp, W["b_w_q"][j], x, _row(W["b_norm"][j]), dx, name=f"sb{j}_q_dx")
    G["b_norm"][j] = dgain.sum(0)
    return dx, dk, dv


def local_step(x, target, W):
    G = {k: [None] * (N_A if k.startswith("a_") else N_B if k in ("b_norm", "b_w_q", "q_norm", "b_w_out") else DEPTH)
         for k in ("a_norm", "a_w_in", "a_conv", "a_log", "a_dt_bias", "a_out_norm", "a_w_out", "b_norm", "b_w_q", "q_norm",
                   "b_w_out", "ffn_norm", "ffn_w_up", "ffn_conv", "ffn_w_down")}
    tape = []
    for l in range(N_A):
        x, s_mix = _gdn_fwd(x, W, l)
        x, s_ffn = _ffn_fwd(x, W, l)
        tape.append((s_mix, s_ffn))
    x_kv = x
    kv, h_kv = norm_matmul(x, _row(W["kv_norm"]), W["w_kv"], name="kv_proj")
    k_gain = _row(W["k_norm"])
    kn = head_norm_fwd(kv, k_gain, name="k_norm")
    for j in range(N_B):
        x, s_mix = _sb_fwd(x, kn, kv, W, j)
        x, s_ffn = _ffn_fwd(x, W, N_A + j)
        tape.append((s_mix, s_ffn))
    dx, loss = loss_fwd(x, target, name="loss")

    dks, dvs = [], []
    for j in reversed(range(N_B)):
        s_mix, s_ffn = tape[N_A + j]
        dx = _ffn_bwd(dx, s_ffn, W, N_A + j, G)
        dx, dk, dv = _sb_bwd(dx, s_mix, kn, kv, W, j, G)
        dks.append(dk)
        dvs.append(dv)
    dkp, dgain = head_norm_bwd(kv, k_gain, dks, name="k_norm_bwd", dx_dtype=BF16)
    G["k_norm"] = dgain.reshape(SUBLANES, HEADS, HEAD_DIM).sum((0, 1))
    dkv = jnp.concatenate([dkp, (dvs[0] + dvs[1]).astype(BF16)], axis=1)
    G["w_kv"] = matmul_tn(h_kv, dkv, name="kv_proj_dw")
    dx, dgain = matmul_nt_normbwd(dkv, W["w_kv"], x_kv, _row(W["kv_norm"]), dx, name="kv_proj_dx")
    G["kv_norm"] = dgain.sum(0)
    for l in reversed(range(N_A)):
        s_mix, s_ffn = tape[l]
        dx = _ffn_bwd(dx, s_ffn, W, l, G)
        dx = _gdn_bwd(dx, s_mix, W, l, G)
    return loss, dx, G


MATRICES = {"a_w_in": 2, "a_w_out": 1, "w_kv": 1, "b_w_q": 1, "b_w_out": 1, "ffn_w_up": 2, "ffn_w_down": 1}
SMALL_SHARDED = {"a_norm": 1, "a_conv": 2, "ffn_conv": 2}
SMALL_REPLICATED = ("a_log", "a_dt_bias", "a_out_norm", "kv_norm", "k_norm", "b_norm", "q_norm", "ffn_norm")
WEIGHT_ORDER = ("a_norm", "a_w_in", "a_conv", "a_log", "a_dt_bias", "a_out_norm", "a_w_out", "kv_norm", "w_kv", "k_norm",
                "b_norm", "b_w_q", "q_norm", "b_w_out", "ffn_norm", "ffn_w_up", "ffn_conv", "ffn_w_down")
SMALL_ORDER = tuple(n for n in WEIGHT_ORDER if n not in MATRICES)
PACK_QUANTUM = SUBLANES * LANES


def _unshard(g, axis):
    g = jnp.moveaxis(g, 0, axis)
    return g.reshape(g.shape[:axis] + (g.shape[axis] * g.shape[axis + 1],) + g.shape[axis + 2:])


def _shards(full, axis):
    n = full.shape[axis] // N_CHIPS
    return jnp.moveaxis(full.reshape(full.shape[:axis] + (N_CHIPS, n) + full.shape[axis + 1:]), axis, 0)


def _pack(arrays):
    parts = []
    for a in arrays:
        flat = a.reshape(-1)
        parts.append(jnp.pad(flat, (0, -flat.shape[0] % PACK_QUANTUM)).reshape(-1, LANES))
    return jnp.concatenate(parts, axis=0)


def _unpack(buf, shapes):
    out, row = [], 0
    for s in shapes:
        size = math.prod(s)
        rows = -(-size // PACK_QUANTUM) * SUBLANES
        out.append(buf[row:row + rows].reshape(-1)[:size].reshape(s))
        row += rows
    return out


def _stack(per_layer):
    return jnp.stack(per_layer) if isinstance(per_layer, list) else per_layer


def _as_2d(a):
    return a.reshape(-1, a.shape[-1])


def kernel(x, a_norm, a_w_in, a_conv, a_log, a_dt_bias, a_out_norm, a_w_out, kv_norm, w_kv, k_norm, b_norm, b_w_q, q_norm, b_w_out, ffn_norm, ffn_w_up, ffn_conv, ffn_w_down, loss_target, m_a_norm, m_a_w_in, m_a_conv, m_a_log, m_a_dt_bias, m_a_out_norm, m_a_w_out, m_kv_norm, m_w_kv, m_k_norm, m_b_norm, m_b_w_q, m_q_norm, m_b_w_out, m_ffn_norm, m_ffn_w_up, m_ffn_conv, m_ffn_w_down, v_a_norm, v_a_w_in, v_a_conv, v_a_log, v_a_dt_bias, v_a_out_norm, v_a_w_out, v_kv_norm, v_w_kv, v_k_norm, v_b_norm, v_b_w_q, v_q_norm, v_b_w_out, v_ffn_norm, v_ffn_w_up, v_ffn_conv, v_ffn_w_down):
    local = dict(a_norm=a_norm, a_w_in=a_w_in, a_conv=a_conv, a_log=a_log, a_dt_bias=a_dt_bias, a_out_norm=a_out_norm,
                 a_w_out=a_w_out, kv_norm=kv_norm, w_kv=w_kv, k_norm=k_norm, b_norm=b_norm, b_w_q=b_w_q, q_norm=q_norm,
                 b_w_out=b_w_out, ffn_norm=ffn_norm, ffn_w_up=ffn_w_up, ffn_conv=ffn_conv, ffn_w_down=ffn_w_down)
    mom = dict(a_norm=m_a_norm, a_w_in=m_a_w_in, a_conv=m_a_conv, a_log=m_a_log, a_dt_bias=m_a_dt_bias, a_out_norm=m_a_out_norm,
               a_w_out=m_a_w_out, kv_norm=m_kv_norm, w_kv=m_w_kv, k_norm=m_k_norm, b_norm=m_b_norm, b_w_q=m_b_w_q, q_norm=m_q_norm,
               b_w_out=m_b_w_out, ffn_norm=m_ffn_norm, ffn_w_up=m_ffn_w_up, ffn_conv=m_ffn_conv, ffn_w_down=m_ffn_w_down)
    var = dict(a_norm=v_a_norm, a_w_in=v_a_w_in, a_conv=v_a_conv, a_log=v_a_log, a_dt_bias=v_a_dt_bias, a_out_norm=v_a_out_norm,
               a_w_out=v_a_w_out, kv_norm=v_kv_norm, w_kv=v_w_kv, k_norm=v_k_norm, b_norm=v_b_norm, b_w_q=v_b_w_q, q_norm=v_q_norm,
               b_w_out=v_b_w_out, ffn_norm=v_ffn_norm, ffn_w_up=v_ffn_w_up, ffn_conv=v_ffn_conv, ffn_w_down=v_ffn_w_down)
    chip = 2 * lax.axis_index("x") + lax.axis_index("y")

    mats = list(MATRICES)
    small_sharded = list(SMALL_SHARDED)
    gathered = _chip_exchange([local[n].astype(BF16) for n in mats] + [_pack([local[n] for n in small_sharded])],
                              scatter=False, name="gather_weights")
    W = {n: local[n] for n in SMALL_REPLICATED}
    for n, g in zip(mats, gathered):
        W[n] = _unshard(g, MATRICES[n])
    W["a_w_in"] = jnp.pad(W["a_w_in"], ((0, 0), (0, 0), (0, W_IN_PAD - W_IN_COLS)))
    shard_shapes = [local[n].shape for n in small_sharded]
    per_chip = [_unpack(gathered[-1][j], shard_shapes) for j in range(N_CHIPS)]
    for i, n in enumerate(small_sharded):
        W[n] = _unshard(jnp.stack([per_chip[j][i] for j in range(N_CHIPS)]), SMALL_SHARDED[n])

    T = x.shape[1]
    loss_part, dx, G = local_step(x.reshape(T, D_MODEL), loss_target.reshape(T, D_MODEL), W)
    G = {n: _stack(g) for n, g in G.items()}
    G["a_w_in"] = G["a_w_in"][:, :, :W_IN_COLS]

    contrib = [_shards(G[n], MATRICES[n]).astype(BF16) for n in mats]
    contrib = [c.reshape(N_CHIPS, -1, c.shape[-1]) for c in contrib]
    received = _chip_exchange(contrib, scatter=True, name="scatter_grads")
    mine = [sum_slots(r, name=f"sum_{n}") for n, r in zip(mats, received)]
    theirs = sibling_swap(mine, name="swap_grads")

    small_full = {n: G[n] for n in SMALL_ORDER}
    packed = _pack([small_full[n] for n in SMALL_ORDER] + [loss_part])
    total = sum_slots(all_gather_devices(packed, name="gather_small"), name="sum_small")
    small_shapes = [small_full[n].shape for n in SMALL_ORDER] + [loss_part.shape]
    summed = dict(zip(SMALL_ORDER + ("loss",), _unpack(total, small_shapes)))
    loss = jnp.sum(summed.pop("loss"))
    for n, axis in SMALL_SHARDED.items():
        size = local[n].shape[axis]
        summed[n] = lax.dynamic_slice_in_dim(summed[n], chip * size, size, axis)

    grads, deltas, new_m, new_v = {}, {}, {}, {}
    for n, p_mine, p_theirs in zip(mats, mine, theirs):
        outs = adamw(_as_2d(local[n]), [p_mine, p_theirs], _as_2d(mom[n]), _as_2d(var[n]), name=f"adamw_{n}")
        grads[n], deltas[n], new_m[n], new_v[n] = [o.reshape(local[n].shape) for o in outs]
    small_local_shapes = [local[n].shape for n in SMALL_ORDER]
    outs = adamw(_pack([local[n] for n in SMALL_ORDER]), [_pack([summed[n] for n in SMALL_ORDER])],
                 _pack([mom[n] for n in SMALL_ORDER]), _pack([var[n] for n in SMALL_ORDER]), name="adamw_small")
    for d, o in zip((grads, deltas, new_m, new_v), outs):
        d.update(zip(SMALL_ORDER, _unpack(o, small_local_shapes)))

    return (loss, dx.reshape(x.shape), *[grads[n] for n in WEIGHT_ORDER], *[deltas[n] for n in WEIGHT_ORDER],
            *[new_m[n] for n in WEIGHT_ORDER], *[new_v[n] for n in WEIGHT_ORDER])
```

```python
import math

import jax
import jax.numpy as jnp
from jax import lax
from jax.experimental import pallas as pl
from jax.experimental.pallas import tpu as pltpu

F32 = jnp.float32
BF16 = jnp.bfloat16

D_MODEL = 1024
HEADS = 8
HEAD_DIM = 128
GDN_CONV = 4
GDN_CHUNK = 64
D_FF = 2816
FFN_CONV = 3
EPS = 1e-6
N_A = 2
N_B = 2
DEPTH = N_A + N_B
W_IN_COLS = 4 * D_MODEL + 2 * HEADS
W_IN_PAD = 4 * D_MODEL + 128
Z_BLOCK = 3 * D_MODEL // 128
AB_BLOCK = 4 * D_MODEL // 128

ADAM_LR = 0.001
ADAM_B1 = 0.9
ADAM_B2 = 0.999
ADAM_EPS = 1e-08
ADAM_WD = 0.01
ADAM_STEP = 10

LANES = 128
SUBLANES = 8
VMEM_LIMIT = 56 * 1024 * 1024
HALO = SUBLANES
N_CHIPS = 4
N_DEV = 8

HI = lax.Precision.HIGHEST
MESH = pl.DeviceIdType.MESH
ANY = pl.BlockSpec(memory_space=pl.ANY)


def _cp(*sem):
    return pltpu.CompilerParams(dimension_semantics=sem, vmem_limit_bytes=VMEM_LIMIT)


def _tile(n, want, align=SUBLANES):
    t = (min(n, want) // align) * align
    while t > 0 and n % t:
        t -= align
    return t if t > 0 else n


def _dot(a, b, precision=None):
    return jnp.dot(a, b, preferred_element_type=F32, precision=precision)


def _dot_nt(a, b, precision=None):
    return lax.dot_general(a, b, (((1,), (1,)), ((), ())), preferred_element_type=F32, precision=precision)


def _dot_tn(a, b, precision=None):
    return lax.dot_general(a, b, (((0,), (0,)), ((), ())), preferred_element_type=F32, precision=precision)


def _bf(x):
    return x.astype(BF16)


def _sigmoid(x):
    return 1.0 / (1.0 + jnp.exp(-x))


def _softplus(x):
    return jnp.maximum(x, 0.0) + jnp.log(1.0 + jnp.exp(-jnp.abs(x)))


def _silu(x):
    return x * _sigmoid(x)


def _silu_grad(x):
    s = _sigmoid(x)
    return s * (1.0 + x * (1.0 - s))


def _fold_rows(v):
    return jnp.sum(v.reshape(v.shape[0] // SUBLANES, SUBLANES, v.shape[1]), axis=0)


def _accumulate(ref, value, axis):
    @pl.when(pl.program_id(axis) == 0)
    def _():
        ref[...] = jnp.zeros_like(ref)
    ref[...] += value


TILE_BUDGET = 44 * 1024 * 1024


def _rows_that_fit(T, fixed_bytes, row_bytes, want=1024):
    tm = _tile(T, want)
    while tm > SUBLANES and 2 * (fixed_bytes + tm * row_bytes) > TILE_BUDGET:
        tm //= 2
    return tm


def norm_matmul(x, gain, w, *, name):
    T, D = x.shape
    N = w.shape[1]
    tn = N
    tm = _rows_that_fit(T, D * N * 2, D * 4 + D * 2 + N * 4)

    def body(x_ref, g_ref, w_ref, y_ref, h_ref):
        @pl.when(pl.program_id(1) == 0)
        def _():
            xf = x_ref[...]
            r = lax.rsqrt(jnp.mean(xf * xf, axis=-1, keepdims=True) + EPS)
            h_ref[...] = _bf(xf * r * g_ref[...])
        y_ref[...] = _dot(h_ref[...], w_ref[...])

    return pl.pallas_call(
        body, name=name, grid=(T // tm, N // tn),
        in_specs=[pl.BlockSpec((tm, D), lambda i, j: (i, 0)), pl.BlockSpec((1, D), lambda i, j: (0, 0)),
                  pl.BlockSpec((D, tn), lambda i, j: (0, j))],
        out_specs=[pl.BlockSpec((tm, tn), lambda i, j: (i, j)), pl.BlockSpec((tm, D), lambda i, j: (i, 0))],
        out_shape=[jax.ShapeDtypeStruct((T, N), F32), jax.ShapeDtypeStruct((T, D), BF16)],
        compiler_params=_cp("parallel", "arbitrary"),
    )(x, gain, w)


def matmul_residual(a, w, res, *, name):
    T, K = a.shape
    N = w.shape[1]
    tn = N
    tm = _rows_that_fit(T, K * N * 2, K * a.dtype.itemsize + 2 * N * 4)

    def body(a_ref, w_ref, r_ref, o_ref):
        o_ref[...] = r_ref[...] + _dot(_bf(a_ref[...]), w_ref[...])

    return pl.pallas_call(
        body, name=name, grid=(T // tm, N // tn),
        in_specs=[pl.BlockSpec((tm, K), lambda i, j: (i, 0)), pl.BlockSpec((K, tn), lambda i, j: (0, j)),
                  pl.BlockSpec((tm, tn), lambda i, j: (i, j))],
        out_specs=pl.BlockSpec((tm, tn), lambda i, j: (i, j)),
        out_shape=jax.ShapeDtypeStruct((T, N), F32),
        compiler_params=_cp("parallel", "parallel"),
    )(a, w, res)


def matmul_nt(dy, w, *, name, out_dtype=F32):
    T, N = dy.shape
    K = w.shape[0]
    tk = K
    tm = _rows_that_fit(T, K * N * 2, N * dy.dtype.itemsize + K * jnp.dtype(out_dtype).itemsize)

    def body(dy_ref, w_ref, o_ref):
        o_ref[...] = _dot_nt(_bf(dy_ref[...]), w_ref[...]).astype(out_dtype)

    return pl.pallas_call(
        body, name=name, grid=(T // tm, K // tk),
        in_specs=[pl.BlockSpec((tm, N), lambda i, j: (i, 0)), pl.BlockSpec((tk, N), lambda i, j: (j, 0))],
        out_specs=pl.BlockSpec((tm, tk), lambda i, j: (i, j)),
        out_shape=jax.ShapeDtypeStruct((T, K), out_dtype),
        compiler_params=_cp("parallel", "parallel"),
    )(dy, w)


def matmul_nt_normbwd(dy, w, x, gain, dres, *, name):
    T, N = dy.shape
    D = w.shape[0]
    tm = _tile(T, 256)

    def body(dy_ref, w_ref, x_ref, g_ref, dr_ref, dx_ref, dg_ref):
        dh = _dot_nt(_bf(dy_ref[...]), w_ref[...])
        xf = x_ref[...]
        r = lax.rsqrt(jnp.mean(xf * xf, axis=-1, keepdims=True) + EPS)
        xh = xf * r
        dxh = dh * g_ref[...]
        dx_ref[...] = dr_ref[...] + r * (dxh - xh * jnp.mean(dxh * xh, axis=-1, keepdims=True))
        _accumulate(dg_ref, _fold_rows(dh * xh), 0)

    return pl.pallas_call(
        body, name=name, grid=(T // tm,),
        in_specs=[pl.BlockSpec((tm, N), lambda i: (i, 0)), pl.BlockSpec((D, N), lambda i: (0, 0)),
                  pl.BlockSpec((tm, D), lambda i: (i, 0)), pl.BlockSpec((1, D), lambda i: (0, 0)),
                  pl.BlockSpec((tm, D), lambda i: (i, 0))],
        out_specs=[pl.BlockSpec((tm, D), lambda i: (i, 0)), pl.BlockSpec((SUBLANES, D), lambda i: (0, 0))],
        out_shape=[jax.ShapeDtypeStruct((T, D), F32), jax.ShapeDtypeStruct((SUBLANES, D), F32)],
        compiler_params=_cp("arbitrary"),
    )(dy, w, x, gain, dres)


def matmul_tn(a, dy, *, name):
    T, K = a.shape
    N = dy.shape[1]
    tk = _tile(K, 1408, LANES)
    tn = _tile(N, 1024 if N <= 2048 else 512, LANES)
    tm = _tile(T, 4096 if tn <= 512 else 2048)

    def body(a_ref, dy_ref, o_ref):
        _accumulate(o_ref, _dot_tn(_bf(a_ref[...]), _bf(dy_ref[...])), 2)

    return pl.pallas_call(
        body, name=name, grid=(K // tk, N // tn, T // tm),
        in_specs=[pl.BlockSpec((tm, tk), lambda i, j, t: (t, i)), pl.BlockSpec((tm, tn), lambda i, j, t: (t, j))],
        out_specs=pl.BlockSpec((tk, tn), lambda i, j, t: (i, j)),
        out_shape=jax.ShapeDtypeStruct((K, N), F32),
        compiler_params=_cp("parallel", "parallel", "arbitrary"),
    )(a, dy)


def _halo_specs(T, tt, tc, col):
    per = tt // HALO
    last = T // HALO - 1
    return [pl.BlockSpec((HALO, tc), lambda j, i: (jnp.maximum(i * per - 1, 0), col(j))),
            pl.BlockSpec((tt, tc), lambda j, i: (i, col(j))),
            pl.BlockSpec((HALO, tc), lambda j, i: (jnp.minimum((i + 1) * per, last), col(j)))]


def _extend(prev_ref, cur_ref, next_ref, nt):
    i = pl.program_id(1)
    p = jnp.where(i > 0, prev_ref[...].astype(F32), 0.0)
    q = jnp.where(i < nt - 1, next_ref[...].astype(F32), 0.0)
    return jnp.concatenate([p, cur_ref[...].astype(F32), q], axis=0)


def _rows_before(e, s):
    return e if s == 0 else pltpu.roll(e, s, 0)


def _rows_after(e, s):
    return e if s == 0 else pltpu.roll(e, e.shape[0] - s, 0)


def _causal_conv(e, w, taps):
    y = w[taps - 1:taps, :] * e
    for s in range(1, taps):
        y = y + w[taps - 1 - s:taps - s, :] * _rows_before(e, s)
    return y


def _causal_conv_bwd(e, dc, w, taps, tt):
    lo, hi = HALO, HALO + tt
    dx = w[taps - 1:taps, :] * dc
    dws = [None] * taps
    dws[taps - 1] = jnp.sum((e * dc)[lo:hi], axis=0, keepdims=True)
    for s in range(1, taps):
        dx = dx + w[taps - 1 - s:taps - s, :] * _rows_after(dc, s)
        dws[taps - 1 - s] = jnp.sum((_rows_before(e, s) * dc)[lo:hi], axis=0, keepdims=True)
    dw = jnp.concatenate(dws + [jnp.zeros((SUBLANES - taps, e.shape[1]), F32)], axis=0)
    return dx[lo:hi], dw


def _qkv_kind(col_block):
    return (col_block >= HEADS).astype(jnp.int32) + (col_block >= 2 * HEADS).astype(jnp.int32)


def _l2norm_scale(kind):
    return jnp.where(kind == 0, HEAD_DIM ** -0.5, 1.0)


def gdn_conv_fwd(proj, conv_w, *, name):
    T = proj.shape[0]
    tt, tc = _tile(T, 512), HEAD_DIM
    nt = T // tt

    def body(p_ref, c_ref, n_ref, w_ref, o_ref):
        kind = _qkv_kind(pl.program_id(0))
        e = _extend(p_ref, c_ref, n_ref, nt)
        s = _silu(_causal_conv(e, w_ref[...], GDN_CONV))[HALO:HALO + tt]
        r = lax.rsqrt(jnp.sum(s * s, axis=-1, keepdims=True) + EPS) * _l2norm_scale(kind)
        o_ref[...] = jnp.where(kind == 2, s, s * r)

    return pl.pallas_call(
        body, name=name, grid=(3 * HEADS, nt),
        in_specs=_halo_specs(T, tt, tc, lambda j: j) + [pl.BlockSpec((SUBLANES, tc), lambda j, i: (0, j))],
        out_specs=pl.BlockSpec((tt, tc), lambda j, i: (i, j)),
        out_shape=jax.ShapeDtypeStruct((T, 3 * D_MODEL), F32),
        compiler_params=_cp("parallel", "parallel"),
    )(proj, proj, proj, conv_w)


def gdn_conv_bwd(proj, conv_w, dq, dk, dv, *, name):
    T = proj.shape[0]
    tt, tc = _tile(T, 512), HEAD_DIM
    nt = T // tt

    def body(p_ref, c_ref, n_ref, w_ref, q0, q1, q2, k0, k1, k2, v0, v1, v2, dx_ref, dw_ref):
        kind = _qkv_kind(pl.program_id(0))
        w = w_ref[...]
        e = _extend(p_ref, c_ref, n_ref, nt)
        c = _causal_conv(e, w, GDN_CONV)
        s = _silu(c)
        dy = jnp.where(kind == 0, _extend(q0, q1, q2, nt),
                       jnp.where(kind == 1, _extend(k0, k1, k2, nt), _extend(v0, v1, v2, nt)))
        r = lax.rsqrt(jnp.sum(s * s, axis=-1, keepdims=True) + EPS)
        y = s * r
        ds_norm = r * _l2norm_scale(kind) * (dy - y * jnp.sum(dy * y, axis=-1, keepdims=True))
        ds = jnp.where(kind == 2, dy, ds_norm)
        dx, dw = _causal_conv_bwd(e, ds * _silu_grad(c), w, GDN_CONV, tt)
        dx_ref[...] = _bf(dx)
        _accumulate(dw_ref, dw, 1)

    grads = []
    for _ in range(3):
        grads += _halo_specs(T, tt, tc, lambda j: j % HEADS)
    return pl.pallas_call(
        body, name=name, grid=(3 * HEADS, nt),
        in_specs=_halo_specs(T, tt, tc, lambda j: j) + [pl.BlockSpec((SUBLANES, tc), lambda j, i: (0, j))] + grads,
        out_specs=[pl.BlockSpec((tt, tc), lambda j, i: (i, j)), pl.BlockSpec((SUBLANES, tc), lambda j, i: (0, j))],
        out_shape=[jax.ShapeDtypeStruct((T, 3 * D_MODEL), BF16), jax.ShapeDtypeStruct((SUBLANES, 3 * D_MODEL), F32)],
        compiler_params=_cp("parallel", "arbitrary"),
    )(proj, proj, proj, conv_w, dq, dq, dq, dk, dk, dk, dv, dv, dv)


FFN_COLS = 256


def ffn_act_fwd(u, conv_w, *, name):
    T = u.shape[0]
    tt, tc = _tile(T, 512), FFN_COLS
    half = D_FF // tc
    nt = T // tt

    def body(gp, gc, gn, up, uc, un, wg_ref, wu_ref, o_ref):
        gate = _causal_conv(_extend(gp, gc, gn, nt), wg_ref[...], FFN_CONV)
        up_ = _causal_conv(_extend(up, uc, un, nt), wu_ref[...], FFN_CONV)
        o_ref[...] = _bf((_silu(gate) * up_)[HALO:HALO + tt])

    return pl.pallas_call(
        body, name=name, grid=(half, nt),
        in_specs=_halo_specs(T, tt, tc, lambda j: j) + _halo_specs(T, tt, tc, lambda j: j + half)
        + [pl.BlockSpec((SUBLANES, tc), lambda j, i: (0, j)), pl.BlockSpec((SUBLANES, tc), lambda j, i: (0, j + half))],
        out_specs=pl.BlockSpec((tt, tc), lambda j, i: (i, j)),
        out_shape=jax.ShapeDtypeStruct((T, D_FF), BF16),
        compiler_params=_cp("parallel", "parallel"),
    )(u, u, u, u, u, u, conv_w, conv_w)


def ffn_act_bwd(u, conv_w, dact, *, name):
    T = u.shape[0]
    tt, tc = _tile(T, 512), FFN_COLS
    half = D_FF // tc
    nt = T // tt

    def body(gp, gc, gn, up, uc, un, wg_ref, wu_ref, dp, dc_, dn, dug_ref, duu_ref, dwg_ref, dwu_ref):
        wg, wu = wg_ref[...], wu_ref[...]
        eg, eu = _extend(gp, gc, gn, nt), _extend(up, uc, un, nt)
        gate, up_ = _causal_conv(eg, wg, FFN_CONV), _causal_conv(eu, wu, FFN_CONV)
        da = _extend(dp, dc_, dn, nt)
        dxg, dwg = _causal_conv_bwd(eg, da * up_ * _silu_grad(gate), wg, FFN_CONV, tt)
        dxu, dwu = _causal_conv_bwd(eu, da * _silu(gate), wu, FFN_CONV, tt)
        dug_ref[...] = _bf(dxg)
        duu_ref[...] = _bf(dxu)
        _accumulate(dwg_ref, dwg, 1)
        _accumulate(dwu_ref, dwu, 1)

    return pl.pallas_call(
        body, name=name, grid=(half, nt),
        in_specs=_halo_specs(T, tt, tc, lambda j: j) + _halo_specs(T, tt, tc, lambda j: j + half)
        + [pl.BlockSpec((SUBLANES, tc), lambda j, i: (0, j)), pl.BlockSpec((SUBLANES, tc), lambda j, i: (0, j + half))]
        + _halo_specs(T, tt, tc, lambda j: j),
        out_specs=[pl.BlockSpec((tt, tc), lambda j, i: (i, j)), pl.BlockSpec((tt, tc), lambda j, i: (i, j)),
                   pl.BlockSpec((SUBLANES, tc), lambda j, i: (0, j)), pl.BlockSpec((SUBLANES, tc), lambda j, i: (0, j))],
        out_shape=[jax.ShapeDtypeStruct((T, D_FF), BF16), jax.ShapeDtypeStruct((T, D_FF), BF16),
                   jax.ShapeDtypeStruct((SUBLANES, D_FF), F32), jax.ShapeDtypeStruct((SUBLANES, D_FF), F32)],
        compiler_params=_cp("parallel", "arbitrary"),
    )(u, u, u, u, u, u, conv_w, conv_w, dact, dact, dact)


def head_norm_fwd(x, gain, z=None, *, x_col=0, z_col=0, name, out_dtype=F32):
    T = x.shape[0]
    tt = _tile(T, 1024)
    gated = z is not None

    def body(*refs):
        x_ref, g_ref = refs[0], refs[1]
        o_ref = refs[-1]
        xf = x_ref[...]
        y = xf * lax.rsqrt(jnp.mean(xf * xf, axis=-1, keepdims=True) + EPS) * g_ref[...]
        if gated:
            y = y * _silu(refs[2][...])
        o_ref[...] = y.astype(out_dtype)

    ins = [pl.BlockSpec((tt, HEAD_DIM), lambda h, i: (i, x_col + h)), pl.BlockSpec((1, HEAD_DIM), lambda h, i: (0, 0))]
    args = [x, gain]
    if gated:
        ins.append(pl.BlockSpec((tt, HEAD_DIM), lambda h, i: (i, z_col + h)))
        args.append(z)
    return pl.pallas_call(
        body, name=name, grid=(HEADS, T // tt), in_specs=ins,
        out_specs=pl.BlockSpec((tt, HEAD_DIM), lambda h, i: (i, h)),
        out_shape=jax.ShapeDtypeStruct((T, D_MODEL), out_dtype),
        compiler_params=_cp("parallel", "parallel"),
    )(*args)


def head_norm_bwd(x, gain, dys, z=None, *, x_col=0, z_col=0, name, dx_dtype=F32):
    T = x.shape[0]
    tt = _tile(T, 1024)
    gated = z is not None
    nd = len(dys)

    def body(*refs):
        x_ref, g_ref = refs[0], refs[1]
        xf = x_ref[...]
        r = lax.rsqrt(jnp.mean(xf * xf, axis=-1, keepdims=True) + EPS)
        xh = xf * r
        dy = refs[2][...].astype(F32)
        for d_ref in refs[3:2 + nd]:
            dy = dy + d_ref[...].astype(F32)
        outs = refs[2 + nd + (1 if gated else 0):]
        if gated:
            zf = refs[2 + nd][...]
            dx_ref, dz_ref, dg_ref = outs
            dz_ref[...] = _bf(dy * xh * g_ref[...] * _silu_grad(zf))
            dn = dy * _silu(zf)
        else:
            dx_ref, dg_ref = outs
            dn = dy
        dxh = dn * g_ref[...]
        dx_ref[...] = (r * (dxh - xh * jnp.mean(dxh * xh, axis=-1, keepdims=True))).astype(dx_dtype)
        _accumulate(dg_ref, _fold_rows(dn * xh), 1)

    tile = pl.BlockSpec((tt, HEAD_DIM), lambda h, i: (i, h))
    ins = [pl.BlockSpec((tt, HEAD_DIM), lambda h, i: (i, x_col + h)), pl.BlockSpec((1, HEAD_DIM), lambda h, i: (0, 0))] + [tile] * nd
    args = [x, gain] + list(dys)
    outs = [tile]
    shapes = [jax.ShapeDtypeStruct((T, D_MODEL), dx_dtype)]
    if gated:
        ins.append(pl.BlockSpec((tt, HEAD_DIM), lambda h, i: (i, z_col + h)))
        args.append(z)
        outs.append(tile)
        shapes.append(jax.ShapeDtypeStruct((T, D_MODEL), BF16))
    outs.append(pl.BlockSpec((SUBLANES, HEAD_DIM), lambda h, i: (0, h)))
    shapes.append(jax.ShapeDtypeStruct((SUBLANES, D_MODEL), F32))
    return pl.pallas_call(
        body, name=name, grid=(HEADS, T // tt), in_specs=ins, out_specs=outs, out_shape=shapes,
        compiler_params=_cp("parallel", "arbitrary"),
    )(*args)


def gates_fwd(proj, a_log, dt_bias, *, name):
    T = proj.shape[0]
    tt = _tile(T, 1024)

    def body(p_ref, al_ref, dt_ref, o_ref):
        p = p_ref[...]
        lane = lax.broadcasted_iota(jnp.int32, p.shape, 1)
        o_ref[...] = jnp.where(lane < HEADS, -jnp.exp(al_ref[...]) * _softplus(p + dt_ref[...]), _sigmoid(p))

    return pl.pallas_call(
        body, name=name, grid=(T // tt,),
        in_specs=[pl.BlockSpec((tt, LANES), lambda i: (i, AB_BLOCK)), pl.BlockSpec((1, LANES), lambda i: (0, 0)),
                  pl.BlockSpec((1, LANES), lambda i: (0, 0))],
        out_specs=pl.BlockSpec((tt, LANES), lambda i: (i, 0)),
        out_shape=jax.ShapeDtypeStruct((T, LANES), F32),
        compiler_params=_cp("parallel"),
    )(proj, a_log, dt_bias)


def gates_bwd(proj, a_log, dt_bias, dgate, *, name):
    T = proj.shape[0]
    tt = _tile(T, 1024)

    def body(p_ref, al_ref, dt_ref, d_ref, dp_ref, dal_ref, ddt_ref):
        p, d = p_ref[...], d_ref[...]
        lane = lax.broadcasted_iota(jnp.int32, p.shape, 1)
        ea = jnp.exp(al_ref[...])
        pa = p + dt_ref[...]
        da = -d * ea * _sigmoid(pa)
        b = _sigmoid(p)
        dp_ref[...] = _bf(jnp.where(lane < HEADS, da, jnp.where(lane < 2 * HEADS, d * b * (1.0 - b), 0.0)))
        _accumulate(dal_ref, _fold_rows(jnp.where(lane < HEADS, -d * ea * _softplus(pa), 0.0)), 0)
        _accumulate(ddt_ref, _fold_rows(jnp.where(lane < HEADS, da, 0.0)), 0)

    acc = pl.BlockSpec((SUBLANES, LANES), lambda i: (0, 0))
    return pl.pallas_call(
        body, name=name, grid=(T // tt,),
        in_specs=[pl.BlockSpec((tt, LANES), lambda i: (i, AB_BLOCK)), pl.BlockSpec((1, LANES), lambda i: (0, 0)),
                  pl.BlockSpec((1, LANES), lambda i: (0, 0)), pl.BlockSpec((tt, LANES), lambda i: (i, 0))],
        out_specs=[pl.BlockSpec((tt, LANES), lambda i: (i, 0)), acc, acc],
        out_shape=[jax.ShapeDtypeStruct((T, LANES), BF16), jax.ShapeDtypeStruct((SUBLANES, LANES), F32),
                   jax.ShapeDtypeStruct((SUBLANES, LANES), F32)],
        compiler_params=_cp("arbitrary"),
    )(proj, a_log, dt_bias, dgate)


def loss_fwd(y, target, *, name):
    T, D = y.shape
    tt = _tile(T, 512)

    def body(y_ref, t_ref, dy_ref, l_ref):
        d = y_ref[...] - t_ref[...]
        dy_ref[...] = d * (1.0 / D)
        sq = d * d
        lanes = sq[:, 0:LANES]
        for c in range(1, D // LANES):
            lanes = lanes + sq[:, c * LANES:(c + 1) * LANES]
        _accumulate(l_ref, _fold_rows(lanes) * (0.5 / D), 0)

    return pl.pallas_call(
        body, name=name, grid=(T // tt,),
        in_specs=[pl.BlockSpec((tt, D), lambda i: (i, 0)), pl.BlockSpec((tt, D), lambda i: (i, 0))],
        out_specs=[pl.BlockSpec((tt, D), lambda i: (i, 0)), pl.BlockSpec((SUBLANES, LANES), lambda i: (0, 0))],
        out_shape=[jax.ShapeDtypeStruct((T, D), F32), jax.ShapeDtypeStruct((SUBLANES, LANES), F32)],
        compiler_params=_cp("arbitrary"),
    )(y, target)


def _split_bf16(x):
    hi = _bf(x)
    return hi, _bf(x - hi.astype(F32))


def _dot3(a, b, dot=_dot):
    return dot(a[0], b[0]) + dot(a[0], b[1]) + dot(a[1], b[0])


def _each(fn, *lists):
    return [fn(*args) for args in zip(*lists)]


def _unit_lower_inverses(lows):
    c = lows[0].shape[0]
    ii = lax.broadcasted_iota(jnp.int32, (c, c), 0)
    jj = lax.broadcasted_iota(jnp.int32, (c, c), 1)
    eye = jnp.where(ii == jj, 1.0, 0.0)
    invs = _each(lambda low: eye - low, lows)
    powers = _each(_split_bf16, lows)
    for _ in range(int(math.log2(c)) - 1):
        powers = _each(lambda p: _split_bf16(_dot3(p, p)), powers)
        invs = _each(lambda inv, p: inv + _dot3(_split_bf16(inv), p), invs, powers)
    return invs


def _gdn_chunks(heads):
    q, k, v, a_col, a_row, b_col, s0 = (list(t) for t in zip(*heads))
    c = q[0].shape[0]
    ii = lax.broadcasted_iota(jnp.int32, (c, c), 0)
    jj = lax.broadcasted_iota(jnp.int32, (c, c), 1)
    tri, strict = ii >= jj, ii > jj
    g_col = _each(lambda ar: jnp.sum(jnp.where(tri, ar, 0.0), axis=1, keepdims=True), a_row)
    g_row = _each(lambda ac: jnp.sum(jnp.where(ii <= jj, ac, 0.0), axis=0, keepdims=True), a_col)
    gam = _each(lambda gc, gr: jnp.exp(jnp.where(tri, gc - gr, -jnp.inf)), g_col, g_row)
    g_last = _each(lambda ac: jnp.sum(ac, axis=0, keepdims=True), a_col)
    gam_col = _each(jnp.exp, g_col)
    del_col = _each(lambda gl, gc: jnp.exp(gl - gc), g_last, g_col)
    kb = _each(lambda k_, b: k_ * b, k, b_col)
    m = _each(lambda kb_, k_: _dot_nt(_bf(kb_), _bf(k_)), kb, k)
    ks = _each(lambda k_, s: _dot(_bf(k_), _bf(s)), k, s0)
    qk = _each(lambda q_, k_: _dot_nt(_bf(q_), _bf(k_)), q, k)
    inv = _unit_lower_inverses(_each(lambda m_, g: jnp.where(strict, m_ * g, 0.0), m, gam))
    e = _each(lambda v_, gc, ks_: v_ - gc * ks_, v, gam_col, ks)
    inv = _each(_split_bf16, inv)
    vn = _each(lambda inv_, b, e_: _dot3(inv_, _split_bf16(b * e_)), inv, b_col, e)
    p = _each(lambda qk_, g: jnp.where(tri, qk_ * g, 0.0), qk, gam)
    return [dict(tri=tri, strict=strict, ii=ii, jj=jj, gam=gam[g], g_last=g_last[g], gam_col=gam_col[g], del_col=del_col[g],
                 kb=kb[g], m=m[g], inv=inv[g], ks=ks[g], e=e[g], vn=vn[g], qk=qk[g], p=p[g]) for g in range(len(heads))]


GDN_HEADS_PER_STEP = HEADS


def _head_cols(ref, g):
    return ref[:, g * HEAD_DIM:(g + 1) * HEAD_DIM]


def _load_heads(q_ref, k_ref, v_ref, ar_ref, br_ref, states):
    return [(_head_cols(q_ref, g), _head_cols(k_ref, g), _head_cols(v_ref, g), ar_ref[g, 0].T, ar_ref[g, 0], br_ref[g, 0].T, states(g))
            for g in range(GDN_HEADS_PER_STEP)]


def gdn_fwd(qkv, a_row, b_row, *, name):
    T = qkv.shape[0]
    C = GDN_CHUNK
    N = T // C
    G = GDN_HEADS_PER_STEP

    def body(q_ref, k_ref, v_ref, ar_ref, br_ref, o_ref, s_ref, state):
        @pl.when(pl.program_id(1) == 0)
        def _():
            state[...] = jnp.zeros_like(state)
        loaded = _load_heads(q_ref, k_ref, v_ref, ar_ref, br_ref, lambda g: state[g])
        ws = _gdn_chunks(loaded)
        qs = _each(lambda h: _dot(_bf(h[0]), _bf(h[6])), loaded)
        pv = _each(lambda w: _dot(_bf(w["p"]), _bf(w["vn"])), ws)
        kv = _each(lambda h, w: _dot_tn(_bf(w["del_col"] * h[1]), _bf(w["vn"])), loaded, ws)
        for g, w in enumerate(ws):
            s0 = loaded[g][6]
            s_ref[g, 0] = s0
            o_ref[:, g * HEAD_DIM:(g + 1) * HEAD_DIM] = w["gam_col"] * qs[g] + pv[g]
            state[g] = jnp.exp(w["g_last"]) * s0 + kv[g]

    per = HEADS // G
    blk = lambda off: pl.BlockSpec((C, G * HEAD_DIM), lambda h, n: (n, off * per + h))
    row = pl.BlockSpec((G, 1, 1, C), lambda h, n: (h, n, 0, 0))
    return pl.pallas_call(
        body, name=name, grid=(per, N),
        in_specs=[blk(0), blk(1), blk(2), row, row],
        out_specs=[blk(0), pl.BlockSpec((G, 1, HEAD_DIM, HEAD_DIM), lambda h, n: (h, n, 0, 0))],
        out_shape=[jax.ShapeDtypeStruct((T, D_MODEL), F32), jax.ShapeDtypeStruct((HEADS, N, HEAD_DIM, HEAD_DIM), F32)],
        scratch_shapes=[pltpu.VMEM((G, HEAD_DIM, HEAD_DIM), F32)],
        compiler_params=_cp("parallel", "arbitrary"),
    )(qkv, qkv, qkv, a_row, b_row)


def gdn_bwd(qkv, a_row, b_row, states, do, *, name):
    T = qkv.shape[0]
    C = GDN_CHUNK
    N = T // C
    G = GDN_HEADS_PER_STEP
    per = HEADS // G
    rev = lambda n: N - 1 - n
    row = pl.BlockSpec((G, 1, 1, C), lambda h, n: (h, rev(n), 0, 0))

    def body(q_ref, k_ref, v_ref, ar_ref, br_ref, s_ref, do_ref, dq_ref, dk_ref, dv_ref, da_ref, db_ref, dstate):
        @pl.when(pl.program_id(1) == 0)
        def _():
            dstate[...] = jnp.zeros_like(dstate)
        loaded = _load_heads(q_ref, k_ref, v_ref, ar_ref, br_ref, lambda g: s_ref[g, 0])
        hs = _gdn_chunks(loaded)
        for g, d in enumerate(hs):
            q, k, _, _, _, b, s0 = loaded[g]
            d.update(q=q, k=k, b=b, s0=s0, ds1=dstate[g], dout=_head_cols(do_ref, g))
        rows = lambda t: jnp.sum(t, axis=1, keepdims=True)
        ii_col = lax.broadcasted_iota(jnp.int32, (C, 1), 0)

        def stage(**fns):
            for key, fn in fns.items():
                for d in hs:
                    d[key] = fn(d)

        stage(s0b=lambda d: _bf(d["s0"]), ds1b=lambda d: _bf(d["ds1"]), doutb=lambda d: _bf(d["dout"]),
              kbf=lambda d: _bf(d["k"]), qbf=lambda d: _bf(d["q"]), vnb=lambda d: _bf(d["vn"]))
        stage(dvn=lambda d: _dot_tn(_bf(d["p"]), d["doutb"]) + _dot(_bf(d["del_col"] * d["k"]), d["ds1b"]),
              dqk=lambda d: jnp.where(d["tri"], _dot_nt(d["doutb"], d["vnb"]), 0.0) * d["gam"],
              qs=lambda d: _dot(d["qbf"], d["s0b"]),
              dkd=lambda d: _dot_nt(d["vnb"], d["ds1b"]))
        stage(dr=lambda d: _dot3(d["inv"], _split_bf16(d["dvn"]), _dot_tn),
              dq=lambda d: d["gam_col"] * _dot_nt(d["doutb"], d["s0b"]) + _dot(_bf(d["dqk"]), d["kbf"]),
              dk=lambda d: _dot_tn(_bf(d["dqk"]), d["qbf"]) + d["del_col"] * d["dkd"],
              ddel=lambda d: d["del_col"] * rows(d["dkd"] * d["k"]))
        stage(dg=lambda d: d["gam_col"] * rows(d["dout"] * d["qs"]) - d["ddel"],
              dg_last=lambda d: jnp.sum(d["ddel"], axis=0, keepdims=True)
              + jnp.exp(d["g_last"]) * jnp.sum(rows(d["ds1"] * d["s0"]), axis=0, keepdims=True),
              dm=lambda d: jnp.where(d["strict"], -_dot_nt(_bf(d["dr"]), d["vnb"]), 0.0) * d["gam"],
              de=lambda d: d["b"] * d["dr"])
        stage(dkb=lambda d: _dot(_bf(d["dm"]), d["kbf"]),
              dks=lambda d: -d["gam_col"] * d["de"])
        stage(dk=lambda d: d["dk"] + _dot_tn(_bf(d["dm"]), _bf(d["kb"])) + _dot_nt(_bf(d["dks"]), d["s0b"]) + d["b"] * d["dkb"],
              dbeta=lambda d: rows(d["dr"] * d["e"]) + rows(d["dkb"] * d["k"]),
              ds0=lambda d: jnp.exp(d["g_last"]) * d["ds1"] + _dot_tn(_bf(d["gam_col"] * d["q"]), d["doutb"])
              + _dot_tn(d["kbf"], _bf(d["dks"])),
              wg=lambda d: d["dqk"] * d["qk"] + d["dm"] * d["m"])
        stage(dg=lambda d: d["dg"] - d["gam_col"] * rows(d["de"] * d["ks"]) + rows(d["wg"])
              - jnp.sum(d["wg"], axis=0, keepdims=True).T + jnp.where(ii_col == C - 1, d["dg_last"], 0.0))
        stage(da=lambda d: jnp.sum(jnp.where(d["ii"] >= d["jj"], d["dg"], 0.0), axis=0, keepdims=True),
              db=lambda d: d["dbeta"].T)
        for g, d in enumerate(hs):
            cols = slice(g * HEAD_DIM, (g + 1) * HEAD_DIM)
            dstate[g] = d["ds0"]
            da_ref[g, 0] = d["da"]
            dq_ref[:, cols] = d["dq"]
            dk_ref[:, cols] = d["dk"]
            dv_ref[:, cols] = d["de"]
            db_ref[g, 0] = d["db"]

    blk = lambda off: pl.BlockSpec((C, G * HEAD_DIM), lambda h, n: (rev(n), off * per + h))
    return pl.pallas_call(
        body, name=name, grid=(per, N),
        in_specs=[blk(0), blk(1), blk(2), row, row,
                  pl.BlockSpec((G, 1, HEAD_DIM, HEAD_DIM), lambda h, n: (h, rev(n), 0, 0)), blk(0)],
        out_specs=[blk(0), blk(0), blk(0), row, row],
        out_shape=[jax.ShapeDtypeStruct((T, D_MODEL), F32)] * 3 + [jax.ShapeDtypeStruct((HEADS, N, 1, C), F32)] * 2,
        scratch_shapes=[pltpu.VMEM((G, HEAD_DIM, HEAD_DIM), F32)],
        compiler_params=_cp("parallel", "arbitrary"),
    )(qkv, qkv, qkv, a_row, b_row, states, do)


SB_BLOCK = 128


def _sb_tile(qb, k_ref, i, j, blk):
    kb = _bf(k_ref[pl.ds(pl.multiple_of(j * blk, blk), blk), :])
    z = _dot_nt(qb, kb)
    t_idx = i * blk + lax.broadcasted_iota(jnp.int32, (blk, blk), 0)
    s_idx = j * blk + lax.broadcasted_iota(jnp.int32, (blk, blk), 1)
    mask = s_idx < t_idx
    lf = jnp.where(mask, -_softplus(z), 0.0)
    return z, mask, lf


SB_DEAD = 105.0


def sb_fwd(q, k, v, *, k_col=0, v_col=0, name):
    T = q.shape[0]
    blk = _tile(T, SB_BLOCK)
    scale = HEAD_DIM ** -0.5

    def body(q_ref, k_ref, v_ref, o_ref, l_ref, n_ref):
        i = pl.program_id(1)
        qb = _bf(q_ref[...] * scale)
        r_idx = lax.broadcasted_iota(jnp.int32, (blk, blk), 0)
        c_idx = lax.broadcasted_iota(jnp.int32, (blk, blk), 1)
        later = _bf(jnp.where(r_idx > c_idx, 1.0, 0.0))

        def alive(carry):
            jj, _, run = carry
            return jnp.logical_and(jj <= i, jnp.max(run) > -SB_DEAD)

        def step(carry):
            jj, acc, run = carry
            j = i - jj
            z, mask, lf = _sb_tile(qb, k_ref, i, j, blk)
            hi, lo = _split_bf16(lf)
            after = run + _dot(hi, later) + _dot(lo, later)
            a = jnp.where(mask, jnp.exp(z + lf + after), 0.0)
            vb = _bf(v_ref[pl.ds(pl.multiple_of(j * blk, blk), blk), :])
            return jj + 1, acc + _dot(_bf(a), vb), run + jnp.sum(lf, axis=1, keepdims=True)

        visited, acc, run = lax.while_loop(
            alive, step, (jnp.int32(0), jnp.zeros((blk, HEAD_DIM), F32), jnp.zeros((blk, 1), F32)))
        o_ref[...] = acc
        l_ref[0, 0] = run.T
        n_ref[0, 0] = jnp.full((SUBLANES, LANES), visited.astype(F32))

    return pl.pallas_call(
        body, name=name, grid=(HEADS, T // blk),
        in_specs=[pl.BlockSpec((blk, HEAD_DIM), lambda h, i: (i, h)), pl.BlockSpec((T, HEAD_DIM), lambda h, i: (0, k_col + h)),
                  pl.BlockSpec((T, HEAD_DIM), lambda h, i: (0, v_col + h))],
        out_specs=[pl.BlockSpec((blk, HEAD_DIM), lambda h, i: (i, h)), pl.BlockSpec((1, 1, 1, blk), lambda h, i: (h, i, 0, 0)),
                   pl.BlockSpec((1, 1, SUBLANES, LANES), lambda h, i: (h, i, 0, 0))],
        out_shape=[jax.ShapeDtypeStruct((T, D_MODEL), F32), jax.ShapeDtypeStruct((HEADS, T // blk, 1, blk), F32),
                   jax.ShapeDtypeStruct((HEADS, T // blk, SUBLANES, LANES), F32)],
        compiler_params=_cp("parallel", "arbitrary"),
    )(q, k, v)


def sb_bwd(q, k, v, ltot, visited, do, *, k_col=0, v_col=0, name):
    T = q.shape[0]
    blk = _tile(T, SB_BLOCK)
    scale = HEAD_DIM ** -0.5

    def body(q_ref, k_ref, v_ref, l_ref, n_ref, do_ref, dq_ref, dk_ref, dv_ref):
        i = pl.program_id(1)
        first = i + 1 - jnp.max(n_ref[0, 0]).astype(jnp.int32)

        @pl.when(i == 0)
        def _():
            dk_ref[...] = jnp.zeros_like(dk_ref)
            dv_ref[...] = jnp.zeros_like(dv_ref)

        qb = _bf(q_ref[...] * scale)
        dob = _bf(do_ref[...])
        ltot_ = l_ref[0, 0].T
        r_idx = lax.broadcasted_iota(jnp.int32, (blk, blk), 0)
        c_idx = lax.broadcasted_iota(jnp.int32, (blk, blk), 1)
        upto = _bf(jnp.where(r_idx <= c_idx, 1.0, 0.0))
        before = _bf(jnp.where(r_idx < c_idx, 1.0, 0.0))

        def step(j, carry):
            dq, lpre, cpre = carry
            z, mask, lf = _sb_tile(qb, k_ref, i, j, blk)
            hi, lo = _split_bf16(lf)
            after = ltot_ - (lpre + _dot(hi, upto) + _dot(lo, upto))
            ls = z + lf
            a = jnp.where(mask, jnp.exp(ls + after), 0.0)
            rows = pl.ds(pl.multiple_of(j * blk, blk), blk)
            vb = _bf(v_ref[rows, :])
            p = a * _dot_nt(dob, vb)
            phi, plo = _split_bf16(p)
            left = cpre + _dot(phi, before) + _dot(plo, before)
            dz = jnp.where(mask, p * jnp.exp(lf) - left * jnp.exp(ls), 0.0)
            dzb = _bf(dz)
            dk_ref[rows, :] += _dot_tn(dzb, qb)
            dv_ref[rows, :] += _dot_tn(_bf(a), dob)
            dq = dq + _dot(dzb, _bf(k_ref[rows, :]))
            return dq, lpre + jnp.sum(lf, axis=1, keepdims=True), cpre + jnp.sum(p, axis=1, keepdims=True)

        zero = jnp.zeros((blk, 1), F32)
        dq, _, _ = lax.fori_loop(first, i + 1, step, (jnp.zeros((blk, HEAD_DIM), F32), zero, zero))
        dq_ref[...] = dq * scale

    full = lambda off: pl.BlockSpec((T, HEAD_DIM), lambda h, i: (0, off + h))
    tile = pl.BlockSpec((blk, HEAD_DIM), lambda h, i: (i, h))
    return pl.pallas_call(
        body, name=name, grid=(HEADS, T // blk),
        in_specs=[tile, full(k_col), full(v_col), pl.BlockSpec((1, 1, 1, blk), lambda h, i: (h, i, 0, 0)),
                  pl.BlockSpec((1, 1, SUBLANES, LANES), lambda h, i: (h, i, 0, 0)), tile],
        out_specs=[tile, full(0), full(0)],
        out_shape=[jax.ShapeDtypeStruct((T, D_MODEL), F32)] * 3,
        compiler_params=_cp("parallel", "arbitrary"),
    )(q, k, v, ltot, visited, do)


def sum_slots(slots, *, name):
    n, R, C = slots.shape
    tr = _tile(R, 256)

    def body(s_ref, o_ref):
        acc = s_ref[0].astype(F32)
        for k in range(1, n):
            acc = acc + s_ref[k].astype(F32)
        o_ref[...] = acc

    return pl.pallas_call(
        body, name=name, grid=(R // tr,),
        in_specs=[pl.BlockSpec((n, tr, C), lambda i: (0, i, 0))],
        out_specs=pl.BlockSpec((tr, C), lambda i: (i, 0)),
        out_shape=jax.ShapeDtypeStruct((R, C), F32),
        compiler_params=_cp("parallel"),
    )(slots)


def adamw(w, g_parts, m, v, *, name):
    R, C = w.shape
    tr = _tile(R, 256)
    n = len(g_parts)

    def body(*refs):
        w_ref, m_ref, v_ref = refs[0], refs[1 + n], refs[2 + n]
        g_ref, d_ref, nm_ref, nv_ref = refs[3 + n:]
        g = refs[1][...]
        for r in refs[2:1 + n]:
            g = g + r[...]
        m2 = ADAM_B1 * m_ref[...] + (1.0 - ADAM_B1) * g
        v2 = ADAM_B2 * v_ref[...] + (1.0 - ADAM_B2) * (g * g)
        m_hat = m2 / (1.0 - ADAM_B1 ** ADAM_STEP)
        v_hat = v2 / (1.0 - ADAM_B2 ** ADAM_STEP)
        g_ref[...] = g
        d_ref[...] = -ADAM_LR * (m_hat / (jnp.sqrt(v_hat) + ADAM_EPS) + ADAM_WD * w_ref[...])
        nm_ref[...] = m2
        nv_ref[...] = v2

    spec = pl.BlockSpec((tr, C), lambda i: (i, 0))
    return pl.pallas_call(
        body, name=name, grid=(R // tr,),
        in_specs=[spec] * (3 + n), out_specs=[spec] * 4,
        out_shape=[jax.ShapeDtypeStruct((R, C), F32)] * 4,
        compiler_params=_cp("parallel"),
    )(w, *g_parts, m, v)


CHIP_FLIPS = ((0, 1), (1, 0), (1, 1))


def _place():
    return lax.axis_index("x"), lax.axis_index("y"), lax.axis_index("c")


def _flip(v, f):
    return 1 - v if f else v


def _chip_exchange(arrays, *, scatter, name):
    n = len(arrays)
    if scatter:
        shapes = [jax.ShapeDtypeStruct(a.shape, a.dtype) for a in arrays]
    else:
        shapes = [jax.ShapeDtypeStruct((N_CHIPS,) + a.shape, a.dtype) for a in arrays]

    def body(*refs):
        ins, outs = refs[:n], refs[n:2 * n]
        send_sems, recv_sems, local_sems = refs[2 * n:]
        x, y, c = _place()
        me = 2 * x + y
        started = []
        for k in range(n):
            src_mine = ins[k].at[me] if scatter else ins[k]
            local = pltpu.make_async_copy(src_mine, outs[k].at[me], local_sems.at[k])
            local.start()
            started.append(local)
            for p, (fx, fy) in enumerate(CHIP_FLIPS):
                px, py = _flip(x, fx), _flip(y, fy)
                src = ins[k].at[2 * px + py] if scatter else ins[k]
                pltpu.make_async_remote_copy(src_ref=src, dst_ref=outs[k].at[me], send_sem=send_sems.at[k, p],
                                             recv_sem=recv_sems.at[k, p], device_id=(px, py, c), device_id_type=MESH).start()
        for k in range(n):
            for p, (fx, fy) in enumerate(CHIP_FLIPS):
                px, py = _flip(x, fx), _flip(y, fy)
                src = ins[k].at[2 * px + py] if scatter else ins[k]
                landing = pltpu.make_async_remote_copy(src_ref=src, dst_ref=outs[k].at[2 * px + py], send_sem=send_sems.at[k, p],
                                                       recv_sem=recv_sems.at[k, p], device_id=(px, py, c), device_id_type=MESH)
                landing.wait_send()
                landing.wait_recv()
        for local in started:
            local.wait()

    return pl.pallas_call(
        body, name=name, in_specs=[ANY] * n, out_specs=[ANY] * n, out_shape=shapes,
        scratch_shapes=[pltpu.SemaphoreType.DMA((n, len(CHIP_FLIPS))), pltpu.SemaphoreType.DMA((n, len(CHIP_FLIPS))),
                        pltpu.SemaphoreType.DMA((n,))],
    )(*arrays)


def sibling_swap(arrays, *, name):
    n = len(arrays)

    def body(*refs):
        ins, outs = refs[:n], refs[n:2 * n]
        send_sems, recv_sems = refs[2 * n:]
        x, y, c = _place()
        copies = [pltpu.make_async_remote_copy(src_ref=ins[k], dst_ref=outs[k], send_sem=send_sems.at[k], recv_sem=recv_sems.at[k],
                                               device_id=(x, y, 1 - c), device_id_type=MESH) for k in range(n)]
        for cp in copies:
            cp.start()
        for cp in copies:
            cp.wait_send()
            cp.wait_recv()

    return pl.pallas_call(
        body, name=name, in_specs=[ANY] * n, out_specs=[ANY] * n,
        out_shape=[jax.ShapeDtypeStruct(a.shape, a.dtype) for a in arrays],
        scratch_shapes=[pltpu.SemaphoreType.DMA((n,)), pltpu.SemaphoreType.DMA((n,))],
    )(*arrays)


DEVICE_FLIPS = tuple((fx, fy, fc) for fx in (0, 1) for fy in (0, 1) for fc in (0, 1) if fx or fy or fc)


def all_gather_devices(a, *, name):
    def body(a_ref, o_ref, send_sems, recv_sems, local_sem):
        x, y, c = _place()
        me = 4 * x + 2 * y + c
        local = pltpu.make_async_copy(a_ref, o_ref.at[me], local_sem)
        local.start()
        for p, (fx, fy, fc) in enumerate(DEVICE_FLIPS):
            peer = (_flip(x, fx), _flip(y, fy), _flip(c, fc))
            pltpu.make_async_remote_copy(src_ref=a_ref, dst_ref=o_ref.at[me], send_sem=send_sems.at[p], recv_sem=recv_sems.at[p],
                                         device_id=peer, device_id_type=MESH).start()
        for p, (fx, fy, fc) in enumerate(DEVICE_FLIPS):
            px, py, pc = _flip(x, fx), _flip(y, fy), _flip(c, fc)
            landing = pltpu.make_async_remote_copy(src_ref=a_ref, dst_ref=o_ref.at[4 * px + 2 * py + pc], send_sem=send_sems.at[p],
                                                   recv_sem=recv_sems.at[p], device_id=(px, py, pc), device_id_type=MESH)
            landing.wait_send()
            landing.wait_recv()
        local.wait()

    return pl.pallas_call(
        body, name=name, in_specs=[ANY], out_specs=ANY,
        out_shape=jax.ShapeDtypeStruct((N_DEV,) + a.shape, a.dtype),
        scratch_shapes=[pltpu.SemaphoreType.DMA((len(DEVICE_FLIPS),)), pltpu.SemaphoreType.DMA((len(DEVICE_FLIPS),)),
                        pltpu.SemaphoreType.DMA(())],
    )(a)


def _row(v):
    return v.reshape(1, -1)


def _pad_rows(w):
    return jnp.pad(w, ((0, SUBLANES - w.shape[0]), (0, 0)))


def _pad_lanes(v):
    return jnp.pad(v.reshape(1, -1), ((0, 0), (0, LANES - v.shape[-1])))


def _head_layouts(gates):
    T = gates.shape[0]
    rows = lambda cols: cols.T.reshape(HEADS, T // GDN_CHUNK, 1, GDN_CHUNK)
    return rows(gates[:, :HEADS]), rows(gates[:, HEADS:2 * HEADS])


def _ffn_fwd(x, W, l):
    u, h = norm_matmul(x, _row(W["ffn_norm"][l]), W["ffn_w_up"][l], name=f"ffn{l}_up")
    conv = _pad_rows(W["ffn_conv"][l])
    act = ffn_act_fwd(u, conv, name=f"ffn{l}_act")
    y = matmul_residual(act, W["ffn_w_down"][l], x, name=f"ffn{l}_down")
    return y, (x, h, u, conv, act)


def _ffn_bwd(dx, saved, W, l, G):
    x, h, u, conv, act = saved
    dact = matmul_nt(dx, W["ffn_w_down"][l], name=f"ffn{l}_down_dx", out_dtype=BF16)
    G["ffn_w_down"][l] = matmul_tn(act, dx, name=f"ffn{l}_down_dw")
    dug, duu, dwg, dwu = ffn_act_bwd(u, conv, dact, name=f"ffn{l}_act_bwd")
    du = jnp.concatenate([dug, duu], axis=1)
    G["ffn_conv"][l] = jnp.concatenate([dwg, dwu], axis=1)[:FFN_CONV]
    G["ffn_w_up"][l] = matmul_tn(h, du, name=f"ffn{l}_up_dw")
    dx, dgain = matmul_nt_normbwd(du, W["ffn_w_up"][l], x, _row(W["ffn_norm"][l]), dx, name=f"ffn{l}_up_dx")
    G["ffn_norm"][l] = dgain.sum(0)
    return dx


def _gdn_fwd(x, W, l):
    proj, h = norm_matmul(x, _row(W["a_norm"][l]), W["a_w_in"][l], name=f"gdn{l}_in")
    conv = _pad_rows(W["a_conv"][l])
    a_log, dt_bias = _pad_lanes(W["a_log"][l]), _pad_lanes(W["a_dt_bias"][l])
    qkv = gdn_conv_fwd(proj, conv, name=f"gdn{l}_conv")
    heads = _head_layouts(gates_fwd(proj, a_log, dt_bias, name=f"gdn{l}_gates"))
    o, states = gdn_fwd(qkv, *heads, name=f"gdn{l}_rule")
    gain = _row(W["a_out_norm"][l])
    on = head_norm_fwd(o, gain, proj, z_col=Z_BLOCK, name=f"gdn{l}_outnorm", out_dtype=BF16)
    y = matmul_residual(on, W["a_w_out"][l], x, name=f"gdn{l}_out")
    return y, (x, h, proj, conv, a_log, dt_bias, qkv, heads, states, o, gain, on)


def _gdn_bwd(dx, saved, W, l, G):
    x, h, proj, conv, a_log, dt_bias, qkv, heads, states, o, gain, on = saved
    T = x.shape[0]
    don = matmul_nt(dx, W["a_w_out"][l], name=f"gdn{l}_out_dx")
    G["a_w_out"][l] = matmul_tn(on, dx, name=f"gdn{l}_out_dw")
    do, dz, dgain = head_norm_bwd(o, gain, [don], proj, z_col=Z_BLOCK, name=f"gdn{l}_outnorm_bwd")
    G["a_out_norm"][l] = dgain.reshape(SUBLANES, HEADS, HEAD_DIM).sum((0, 1))
    dq, dk, dv, da, db = gdn_bwd(qkv, *heads, states, do, name=f"gdn{l}_rule_bwd")
    dqkv, dconv = gdn_conv_bwd(proj, conv, dq, dk, dv, name=f"gdn{l}_conv_bwd")
    G["a_conv"][l] = dconv[:GDN_CONV]
    dgate = jnp.concatenate([da.reshape(HEADS, T).T, db.reshape(HEADS, T).T, jnp.zeros((T, LANES - 2 * HEADS), F32)], axis=1)
    dab, dal, ddt = gates_bwd(proj, a_log, dt_bias, dgate, name=f"gdn{l}_gates_bwd")
    G["a_log"][l] = dal.sum(0)[:HEADS]
    G["a_dt_bias"][l] = ddt.sum(0)[:HEADS]
    dproj = jnp.concatenate([dqkv, dz, dab], axis=1)
    G["a_w_in"][l] = matmul_tn(h, dproj, name=f"gdn{l}_in_dw")
    dx, dgain = matmul_nt_normbwd(dproj, W["a_w_in"][l], x, _row(W["a_norm"][l]), dx, name=f"gdn{l}_in_dx")
    G["a_norm"][l] = dgain.sum(0)
    return dx


def _sb_fwd(x, kn, kv, W, j):
    qp, h = norm_matmul(x, _row(W["b_norm"][j]), W["b_w_q"][j], name=f"sb{j}_q")
    gain = _row(W["q_norm"][j])
    q = head_norm_fwd(qp, gain, name=f"sb{j}_qnorm")
    o, ltot, visited = sb_fwd(q, kn, kv, v_col=HEADS, name=f"sb{j}_attn")
    y = matmul_residual(o, W["b_w_out"][j], x, name=f"sb{j}_out")
    return y, (x, h, qp, gain, q, o, ltot, visited)


def _sb_bwd(dx, saved, kn, kv, W, j, G):
    x, h, qp, gain, q, o, ltot, visited = saved
    do = matmul_nt(dx, W["b_w_out"][j], name=f"sb{j}_out_dx")
    G["b_w_out"][j] = matmul_tn(o, dx, name=f"sb{j}_out_dw")
    dq, dk, dv = sb_bwd(q, kn, kv, ltot, visited, do, v_col=HEADS, name=f"sb{j}_attn_bwd")
    dqp, dgain = head_norm_bwd(qp, gain, [dq], name=f"sb{j}_qnorm_bwd", dx_dtype=BF16)
    G["q_norm"][j] = dgain.reshape(SUBLANES, HEADS, HEAD_DIM).sum((0, 1))
    G["b_w_q"][j] = matmul_tn(h, dqp, name=f"sb{j}_q_dw")
    dx, dgain = matmul_nt_normbwd(dqp, W["b_w_q"][j], x, _row(W["b_norm"][j]), dx, name=f"sb{j}_q_dx")
    G["b_norm"][j] = dgain.sum(0)
    return dx, dk, dv


def local_step(x, target, W):
    G = {k: [None] * (N_A if k.startswith("a_") else N_B if k in ("b_norm", "b_w_q", "q_norm", "b_w_out") else DEPTH)
         for k in ("a_norm", "a_w_in", "a_conv", "a_log", "a_dt_bias", "a_out_norm", "a_w_out", "b_norm", "b_w_q", "q_norm",
                   "b_w_out", "ffn_norm", "ffn_w_up", "ffn_conv", "ffn_w_down")}
    tape = []
    for l in range(N_A):
        x, s_mix = _gdn_fwd(x, W, l)
        x, s_ffn = _ffn_fwd(x, W, l)
        tape.append((s_mix, s_ffn))
    x_kv = x
    kv, h_kv = norm_matmul(x, _row(W["kv_norm"]), W["w_kv"], name="kv_proj")
    k_gain = _row(W["k_norm"])
    kn = head_norm_fwd(kv, k_gain, name="k_norm")
    for j in range(N_B):
        x, s_mix = _sb_fwd(x, kn, kv, W, j)
        x, s_ffn = _ffn_fwd(x, W, N_A + j)
        tape.append((s_mix, s_ffn))
    dx, loss = loss_fwd(x, target, name="loss")

    dks, dvs = [], []
    for j in reversed(range(N_B)):
        s_mix, s_ffn = tape[N_A + j]
        dx = _ffn_bwd(dx, s_ffn, W, N_A + j, G)
        dx, dk, dv = _sb_bwd(dx, s_mix, kn, kv, W, j, G)
        dks.append(dk)
        dvs.append(dv)
    dkp, dgain = head_norm_bwd(kv, k_gain, dks, name="k_norm_bwd", dx_dtype=BF16)
    G["k_norm"] = dgain.reshape(SUBLANES, HEADS, HEAD_DIM).sum((0, 1))
    dkv = jnp.concatenate([dkp, (dvs[0] + dvs[1]).astype(BF16)], axis=1)
    G["w_kv"] = matmul_tn(h_kv, dkv, name="kv_proj_dw")
    dx, dgain = matmul_nt_normbwd(dkv, W["w_kv"], x_kv, _row(W["kv_norm"]), dx, name="kv_proj_dx")
    G["kv_norm"] = dgain.sum(0)
    for l in reversed(range(N_A)):
        s_mix, s_ffn = tape[l]
        dx = _ffn_bwd(dx, s_ffn, W, l, G)
        dx = _gdn_bwd(dx, s_mix, W, l, G)
    return loss, dx, G


MATRICES = {"a_w_in": 2, "a_w_out": 1, "w_kv": 1, "b_w_q": 1, "b_w_out": 1, "ffn_w_up": 2, "ffn_w_down": 1}
SMALL_SHARDED = {"a_norm": 1, "a_conv": 2, "ffn_conv": 2}
SMALL_REPLICATED = ("a_log", "a_dt_bias", "a_out_norm", "kv_norm", "k_norm", "b_norm", "q_norm", "ffn_norm")
WEIGHT_ORDER = ("a_norm", "a_w_in", "a_conv", "a_log", "a_dt_bias", "a_out_norm", "a_w_out", "kv_norm", "w_kv", "k_norm",
                "b_norm", "b_w_q", "q_norm", "b_w_out", "ffn_norm", "ffn_w_up", "ffn_conv", "ffn_w_down")
SMALL_ORDER = tuple(n for n in WEIGHT_ORDER if n not in MATRICES)
PACK_QUANTUM = SUBLANES * LANES


def _unshard(g, axis):
    g = jnp.moveaxis(g, 0, axis)
    return g.reshape(g.shape[:axis] + (g.shape[axis] * g.shape[axis + 1],) + g.shape[axis + 2:])


def _shards(full, axis):
    n = full.shape[axis] // N_CHIPS
    return jnp.moveaxis(full.reshape(full.shape[:axis] + (N_CHIPS, n) + full.shape[axis + 1:]), axis, 0)


def _pack(arrays):
    parts = []
    for a in arrays:
        flat = a.reshape(-1)
        parts.append(jnp.pad(flat, (0, -flat.shape[0] % PACK_QUANTUM)).reshape(-1, LANES))
    return jnp.concatenate(parts, axis=0)


def _unpack(buf, shapes):
    out, row = [], 0
    for s in shapes:
        size = math.prod(s)
        rows = -(-size // PACK_QUANTUM) * SUBLANES
        out.append(buf[row:row + rows].reshape(-1)[:size].reshape(s))
        row += rows
    return out


def _stack(per_layer):
    return jnp.stack(per_layer) if isinstance(per_layer, list) else per_layer


def _as_2d(a):
    return a.reshape(-1, a.shape[-1])


def kernel(x, a_norm, a_w_in, a_conv, a_log, a_dt_bias, a_out_norm, a_w_out, kv_norm, w_kv, k_norm, b_norm, b_w_q, q_norm, b_w_out, ffn_norm, ffn_w_up, ffn_conv, ffn_w_down, loss_target, m_a_norm, m_a_w_in, m_a_conv, m_a_log, m_a_dt_bias, m_a_out_norm, m_a_w_out, m_kv_norm, m_w_kv, m_k_norm, m_b_norm, m_b_w_q, m_q_norm, m_b_w_out, m_ffn_norm, m_ffn_w_up, m_ffn_conv, m_ffn_w_down, v_a_norm, v_a_w_in, v_a_conv, v_a_log, v_a_dt_bias, v_a_out_norm, v_a_w_out, v_kv_norm, v_w_kv, v_k_norm, v_b_norm, v_b_w_q, v_q_norm, v_b_w_out, v_ffn_norm, v_ffn_w_up, v_ffn_conv, v_ffn_w_down):
    local = dict(a_norm=a_norm, a_w_in=a_w_in, a_conv=a_conv, a_log=a_log, a_dt_bias=a_dt_bias, a_out_norm=a_out_norm,
                 a_w_out=a_w_out, kv_norm=kv_norm, w_kv=w_kv, k_norm=k_norm, b_norm=b_norm, b_w_q=b_w_q, q_norm=q_norm,
                 b_w_out=b_w_out, ffn_norm=ffn_norm, ffn_w_up=ffn_w_up, ffn_conv=ffn_conv, ffn_w_down=ffn_w_down)
    mom = dict(a_norm=m_a_norm, a_w_in=m_a_w_in, a_conv=m_a_conv, a_log=m_a_log, a_dt_bias=m_a_dt_bias, a_out_norm=m_a_out_norm,
               a_w_out=m_a_w_out, kv_norm=m_kv_norm, w_kv=m_w_kv, k_norm=m_k_norm, b_norm=m_b_norm, b_w_q=m_b_w_q, q_norm=m_q_norm,
               b_w_out=m_b_w_out, ffn_norm=m_ffn_norm, ffn_w_up=m_ffn_w_up, ffn_conv=m_ffn_conv, ffn_w_down=m_ffn_w_down)
    var = dict(a_norm=v_a_norm, a_w_in=v_a_w_in, a_conv=v_a_conv, a_log=v_a_log, a_dt_bias=v_a_dt_bias, a_out_norm=v_a_out_norm,
               a_w_out=v_a_w_out, kv_norm=v_kv_norm, w_kv=v_w_kv, k_norm=v_k_norm, b_norm=v_b_norm, b_w_q=v_b_w_q, q_norm=v_q_norm,
               b_w_out=v_b_w_out, ffn_norm=v_ffn_norm, ffn_w_up=v_ffn_w_up, ffn_conv=v_ffn_conv, ffn_w_down=v_ffn_w_down)
    chip = 2 * lax.axis_index("x") + lax.axis_index("y")

    mats = list(MATRICES)
    small_sharded = list(SMALL_SHARDED)
    gathered = _chip_exchange([local[n].astype(BF16) for n in mats] + [_pack([local[n] for n in small_sharded])],
                              scatter=False, name="gather_weights")
    W = {n: local[n] for n in SMALL_REPLICATED}
    for n, g in zip(mats, gathered):
        W[n] = _unshard(g, MATRICES[n])
    W["a_w_in"] = jnp.pad(W["a_w_in"], ((0, 0), (0, 0), (0, W_IN_PAD - W_IN_COLS)))
    shard_shapes = [local[n].shape for n in small_sharded]
    per_chip = [_unpack(gathered[-1][j], shard_shapes) for j in range(N_CHIPS)]
    for i, n in enumerate(small_sharded):
        W[n] = _unshard(jnp.stack([per_chip[j][i] for j in range(N_CHIPS)]), SMALL_SHARDED[n])

    T = x.shape[1]
    loss_part, dx, G = local_step(x.reshape(T, D_MODEL), loss_target.reshape(T, D_MODEL), W)
    G = {n: _stack(g) for n, g in G.items()}
    G["a_w_in"] = G["a_w_in"][:, :, :W_IN_COLS]

    contrib = [_shards(G[n], MATRICES[n]).astype(BF16) for n in mats]
    contrib = [c.reshape(N_CHIPS, -1, c.shape[-1]) for c in contrib]
    received = _chip_exchange(contrib, scatter=True, name="scatter_grads")
    mine = [sum_slots(r, name=f"sum_{n}") for n, r in zip(mats, received)]
    theirs = sibling_swap(mine, name="swap_grads")

    small_full = {n: G[n] for n in SMALL_ORDER}
    packed = _pack([small_full[n] for n in SMALL_ORDER] + [loss_part])
    total = sum_slots(all_gather_devices(packed, name="gather_small"), name="sum_small")
    small_shapes = [small_full[n].shape for n in SMALL_ORDER] + [loss_part.shape]
    summed = dict(zip(SMALL_ORDER + ("loss",), _unpack(total, small_shapes)))
    loss = jnp.sum(summed.pop("loss"))
    for n, axis in SMALL_SHARDED.items():
        size = local[n].shape[axis]
        summed[n] = lax.dynamic_slice_in_dim(summed[n], chip * size, size, axis)

    grads, deltas, new_m, new_v = {}, {}, {}, {}
    for n, p_mine, p_theirs in zip(mats, mine, theirs):
        outs = adamw(_as_2d(local[n]), [p_mine, p_theirs], _as_2d(mom[n]), _as_2d(var[n]), name=f"adamw_{n}")
        grads[n], deltas[n], new_m[n], new_v[n] = [o.reshape(local[n].shape) for o in outs]
    small_local_shapes = [local[n].shape for n in SMALL_ORDER]
    outs = adamw(_pack([local[n] for n in SMALL_ORDER]), [_pack([summed[n] for n in SMALL_ORDER])],
                 _pack([mom[n] for n in SMALL_ORDER]), _pack([var[n] for n in SMALL_ORDER]), name="adamw_small")
    for d, o in zip((grads, deltas, new_m, new_v), outs):
        d.update(zip(SMALL_ORDER, _unpack(o, small_local_shapes)))

    return (loss, dx.reshape(x.shape), *[grads[n] for n in WEIGHT_ORDER], *[deltas[n] for n in WEIGHT_ORDER],
            *[new_m[n] for n in WEIGHT_ORDER], *[new_v[n] for n in WEIGHT_ORDER])
```

```python
import math

import jax
import jax.numpy as jnp
from jax import lax
from jax.experimental import pallas as pl
from jax.experimental.pallas import tpu as pltpu

F32 = jnp.float32
BF16 = jnp.bfloat16

D_MODEL = 1024
HEADS = 8
HEAD_DIM = 128
GDN_CONV = 4
GDN_CHUNK = 64
D_FF = 2816
FFN_CONV = 3
EPS = 1e-6
N_A = 2
N_B = 2
DEPTH = N_A + N_B
W_IN_COLS = 4 * D_MODEL + 2 * HEADS
W_IN_PAD = 4 * D_MODEL + 128
Z_BLOCK = 3 * D_MODEL // 128
AB_BLOCK = 4 * D_MODEL // 128

ADAM_LR = 0.001
ADAM_B1 = 0.9
ADAM_B2 = 0.999
ADAM_EPS = 1e-08
ADAM_WD = 0.01
ADAM_STEP = 10

LANES = 128
SUBLANES = 8
VMEM_LIMIT = 56 * 1024 * 1024
HALO = SUBLANES
N_CHIPS = 4
N_DEV = 8

HI = lax.Precision.HIGHEST
MESH = pl.DeviceIdType.MESH
ANY = pl.BlockSpec(memory_space=pl.ANY)


def _cp(*sem):
    return pltpu.CompilerParams(dimension_semantics=sem, vmem_limit_bytes=VMEM_LIMIT)


def _tile(n, want, align=SUBLANES):
    t = (min(n, want) // align) * align
    while t > 0 and n % t:
        t -= align
    return t if t > 0 else n


def _dot(a, b, precision=None):
    return jnp.dot(a, b, preferred_element_type=F32, precision=precision)


def _dot_nt(a, b, precision=None):
    return lax.dot_general(a, b, (((1,), (1,)), ((), ())), preferred_element_type=F32, precision=precision)


def _dot_tn(a, b, precision=None):
    return lax.dot_general(a, b, (((0,), (0,)), ((), ())), preferred_element_type=F32, precision=precision)


def _bf(x):
    return x.astype(BF16)


def _sigmoid(x):
    return 0.5 * jnp.tanh(0.5 * x) + 0.5


def _softplus(x):
    return jnp.maximum(x, 0.0) + jnp.log(1.0 + jnp.exp(-jnp.abs(x)))


def _silu(x):
    return x * _sigmoid(x)


def _silu_and_grad(x):
    s = _sigmoid(x)
    return x * s, s * (1.0 + x * (1.0 - s))


def _fold_rows(v):
    return jnp.sum(v.reshape(v.shape[0] // SUBLANES, SUBLANES, v.shape[1]), axis=0)


def _accumulate(ref, value, axis):
    @pl.when(pl.program_id(axis) == 0)
    def _():
        ref[...] = jnp.zeros_like(ref)
    ref[...] += value


TILE_BUDGET = 44 * 1024 * 1024


def _rows_that_fit(T, fixed_bytes, row_bytes, want=1024):
    tm = _tile(T, want)
    while tm > SUBLANES and 2 * (fixed_bytes + tm * row_bytes) > TILE_BUDGET:
        tm //= 2
    return tm


def norm_matmul(x, gain, w, *, name):
    T, D = x.shape
    N = w.shape[1]
    tn = N
    tm = _rows_that_fit(T, D * N * 2, D * 4 + D * 2 + N * 4)

    def body(x_ref, g_ref, w_ref, y_ref, h_ref):
        @pl.when(pl.program_id(1) == 0)
        def _():
            xf = x_ref[...]
            r = lax.rsqrt(jnp.mean(xf * xf, axis=-1, keepdims=True) + EPS)
            h_ref[...] = _bf(xf * r * g_ref[...])
        y_ref[...] = _dot(h_ref[...], w_ref[...])

    return pl.pallas_call(
        body, name=name, grid=(T // tm, N // tn),
        in_specs=[pl.BlockSpec((tm, D), lambda i, j: (i, 0)), pl.BlockSpec((1, D), lambda i, j: (0, 0)),
                  pl.BlockSpec((D, tn), lambda i, j: (0, j))],
        out_specs=[pl.BlockSpec((tm, tn), lambda i, j: (i, j)), pl.BlockSpec((tm, D), lambda i, j: (i, 0))],
        out_shape=[jax.ShapeDtypeStruct((T, N), F32), jax.ShapeDtypeStruct((T, D), BF16)],
        compiler_params=_cp("parallel", "arbitrary"),
    )(x, gain, w)


def matmul_residual(a, w, res, *, name):
    T, K = a.shape
    N = w.shape[1]
    tn = N
    tm = _rows_that_fit(T, K * N * 2, K * a.dtype.itemsize + 2 * N * 4)

    def body(a_ref, w_ref, r_ref, o_ref):
        o_ref[...] = r_ref[...] + _dot(_bf(a_ref[...]), w_ref[...])

    return pl.pallas_call(
        body, name=name, grid=(T // tm, N // tn),
        in_specs=[pl.BlockSpec((tm, K), lambda i, j: (i, 0)), pl.BlockSpec((K, tn), lambda i, j: (0, j)),
                  pl.BlockSpec((tm, tn), lambda i, j: (i, j))],
        out_specs=pl.BlockSpec((tm, tn), lambda i, j: (i, j)),
        out_shape=jax.ShapeDtypeStruct((T, N), F32),
        compiler_params=_cp("parallel", "parallel"),
    )(a, w, res)


def matmul_nt(dy, w, *, name, out_dtype=F32):
    T, N = dy.shape
    K = w.shape[0]
    tk = K
    tm = _rows_that_fit(T, K * N * 2, N * dy.dtype.itemsize + K * jnp.dtype(out_dtype).itemsize)

    def body(dy_ref, w_ref, o_ref):
        o_ref[...] = _dot_nt(_bf(dy_ref[...]), w_ref[...]).astype(out_dtype)

    return pl.pallas_call(
        body, name=name, grid=(T // tm, K // tk),
        in_specs=[pl.BlockSpec((tm, N), lambda i, j: (i, 0)), pl.BlockSpec((tk, N), lambda i, j: (j, 0))],
        out_specs=pl.BlockSpec((tm, tk), lambda i, j: (i, j)),
        out_shape=jax.ShapeDtypeStruct((T, K), out_dtype),
        compiler_params=_cp("parallel", "parallel"),
    )(dy, w)


def matmul_nt_normbwd(parts, w, x, gain, dres, *, name):
    T = x.shape[0]
    D = w.shape[0]
    tm = _tile(T, 256)
    n = len(parts)

    def body(*refs):
        x_ref, g_ref, dr_ref, dx_ref, dg_ref = refs[2 * n:]
        dh = _dot_nt(_bf(refs[0][...]), refs[n][...])
        for i in range(1, n):
            dh = dh + _dot_nt(_bf(refs[i][...]), refs[n + i][...])
        xf = x_ref[...]
        r = lax.rsqrt(jnp.mean(xf * xf, axis=-1, keepdims=True) + EPS)
        xh = xf * r
        dxh = dh * g_ref[...]
        dx_ref[...] = dr_ref[...] + r * (dxh - xh * jnp.mean(dxh * xh, axis=-1, keepdims=True))
        _accumulate(dg_ref, _fold_rows(dh * xh), 0)

    dy_specs = [pl.BlockSpec((tm, dy.shape[1]), lambda i: (i, 0)) for dy, _ in parts]
    w_specs = [pl.BlockSpec((D, dy.shape[1]), lambda i, b=b: (0, b)) for dy, b in parts]
    return pl.pallas_call(
        body, name=name, grid=(T // tm,),
        in_specs=dy_specs + w_specs + [pl.BlockSpec((tm, D), lambda i: (i, 0)), pl.BlockSpec((1, D), lambda i: (0, 0)),
                                       pl.BlockSpec((tm, D), lambda i: (i, 0))],
        out_specs=[pl.BlockSpec((tm, D), lambda i: (i, 0)), pl.BlockSpec((SUBLANES, D), lambda i: (0, 0))],
        out_shape=[jax.ShapeDtypeStruct((T, D), F32), jax.ShapeDtypeStruct((SUBLANES, D), F32)],
        compiler_params=_cp("arbitrary"),
    )(*[dy for dy, _ in parts], *([w] * n), x, gain, dres)


def matmul_tn(a, dy, *, name):
    T, K = a.shape
    N = dy.shape[1]
    tk = _tile(K, 1408, LANES)
    tn = _tile(N, 1024 if N <= 2048 else 512, LANES)
    tm = _tile(T, 4096 if tn <= 512 else 2048)

    def body(a_ref, dy_ref, o_ref):
        _accumulate(o_ref, _dot_tn(_bf(a_ref[...]), _bf(dy_ref[...])), 2)

    return pl.pallas_call(
        body, name=name, grid=(K // tk, N // tn, T // tm),
        in_specs=[pl.BlockSpec((tm, tk), lambda i, j, t: (t, i)), pl.BlockSpec((tm, tn), lambda i, j, t: (t, j))],
        out_specs=pl.BlockSpec((tk, tn), lambda i, j, t: (i, j)),
        out_shape=jax.ShapeDtypeStruct((K, N), F32),
        compiler_params=_cp("parallel", "parallel", "arbitrary"),
    )(a, dy)


def _halo_specs(T, tt, tc, col):
    per = tt // HALO
    last = T // HALO - 1
    return [pl.BlockSpec((HALO, tc), lambda j, i: (jnp.maximum(i * per - 1, 0), col(j))),
            pl.BlockSpec((tt, tc), lambda j, i: (i, col(j))),
            pl.BlockSpec((HALO, tc), lambda j, i: (jnp.minimum((i + 1) * per, last), col(j)))]


def _extend(prev_ref, cur_ref, next_ref, nt):
    i = pl.program_id(1)
    p = jnp.where(i > 0, prev_ref[...].astype(F32), 0.0)
    q = jnp.where(i < nt - 1, next_ref[...].astype(F32), 0.0)
    return jnp.concatenate([p, cur_ref[...].astype(F32), q], axis=0)


def _rows_before(e, s):
    return e if s == 0 else pltpu.roll(e, s, 0)


def _rows_after(e, s):
    return e if s == 0 else pltpu.roll(e, e.shape[0] - s, 0)


def _causal_conv(e, w, taps):
    y = w[taps - 1:taps, :] * e
    for s in range(1, taps):
        y = y + w[taps - 1 - s:taps - s, :] * _rows_before(e, s)
    return y


def _causal_conv_bwd(e, dc, w, taps, tt):
    lo, hi = HALO, HALO + tt
    dx = w[taps - 1:taps, :] * dc
    dws = [None] * taps
    dws[taps - 1] = jnp.sum((e * dc)[lo:hi], axis=0, keepdims=True)
    for s in range(1, taps):
        dx = dx + w[taps - 1 - s:taps - s, :] * _rows_after(dc, s)
        dws[taps - 1 - s] = jnp.sum((_rows_before(e, s) * dc)[lo:hi], axis=0, keepdims=True)
    dw = jnp.concatenate(dws + [jnp.zeros((SUBLANES - taps, e.shape[1]), F32)], axis=0)
    return dx[lo:hi], dw


def _qkv_kind(col_block):
    return (col_block >= HEADS).astype(jnp.int32) + (col_block >= 2 * HEADS).astype(jnp.int32)


def _l2norm_scale(kind):
    return jnp.where(kind == 0, HEAD_DIM ** -0.5, 1.0)


def gdn_conv_fwd(proj, conv_w, *, name):
    T = proj.shape[0]
    tt, tc = _tile(T, 512), HEAD_DIM
    nt = T // tt

    def body(p_ref, c_ref, n_ref, w_ref, o_ref):
        kind = _qkv_kind(pl.program_id(0))
        e = _extend(p_ref, c_ref, n_ref, nt)
        s = _silu(_causal_conv(e, w_ref[...], GDN_CONV))[HALO:HALO + tt]
        r = lax.rsqrt(jnp.sum(s * s, axis=-1, keepdims=True) + EPS) * _l2norm_scale(kind)
        o_ref[...] = jnp.where(kind == 2, s, s * r)

    return pl.pallas_call(
        body, name=name, grid=(3 * HEADS, nt),
        in_specs=_halo_specs(T, tt, tc, lambda j: j) + [pl.BlockSpec((SUBLANES, tc), lambda j, i: (0, j))],
        out_specs=pl.BlockSpec((tt, tc), lambda j, i: (i, j)),
        out_shape=jax.ShapeDtypeStruct((T, 3 * D_MODEL), F32),
        compiler_params=_cp("parallel", "parallel"),
    )(proj, proj, proj, conv_w)


def gdn_conv_bwd(proj, conv_w, dqkv, *, name):
    T = proj.shape[0]
    tt, tc = _tile(T, 512), HEAD_DIM
    nt = T // tt

    def body(p_ref, c_ref, n_ref, w_ref, dp_ref, dc_ref, dn_ref, dx_ref, dw_ref):
        kind = _qkv_kind(pl.program_id(0))
        w = w_ref[...]
        e = _extend(p_ref, c_ref, n_ref, nt)
        c = _causal_conv(e, w, GDN_CONV)
        s, s_grad = _silu_and_grad(c)
        dy = _extend(dp_ref, dc_ref, dn_ref, nt)
        r = lax.rsqrt(jnp.sum(s * s, axis=-1, keepdims=True) + EPS)
        y = s * r
        ds_norm = r * _l2norm_scale(kind) * (dy - y * jnp.sum(dy * y, axis=-1, keepdims=True))
        ds = jnp.where(kind == 2, dy, ds_norm)
        dx, dw = _causal_conv_bwd(e, ds * s_grad, w, GDN_CONV, tt)
        dx_ref[...] = _bf(dx)
        _accumulate(dw_ref, dw, 1)

    return pl.pallas_call(
        body, name=name, grid=(3 * HEADS, nt),
        in_specs=_halo_specs(T, tt, tc, lambda j: j) + [pl.BlockSpec((SUBLANES, tc), lambda j, i: (0, j))]
        + _halo_specs(T, tt, tc, lambda j: j),
        out_specs=[pl.BlockSpec((tt, tc), lambda j, i: (i, j)), pl.BlockSpec((SUBLANES, tc), lambda j, i: (0, j))],
        out_shape=[jax.ShapeDtypeStruct((T, 3 * D_MODEL), BF16), jax.ShapeDtypeStruct((SUBLANES, 3 * D_MODEL), F32)],
        compiler_params=_cp("parallel", "arbitrary"),
    )(proj, proj, proj, conv_w, dqkv, dqkv, dqkv)


FFN_COLS = 256


def ffn_act_fwd(u, conv_w, *, name):
    T = u.shape[0]
    tt, tc = _tile(T, 512), FFN_COLS
    half = D_FF // tc
    nt = T // tt

    def body(gp, gc, gn, up, uc, un, wg_ref, wu_ref, o_ref):
        gate = _causal_conv(_extend(gp, gc, gn, nt), wg_ref[...], FFN_CONV)
        up_ = _causal_conv(_extend(up, uc, un, nt), wu_ref[...], FFN_CONV)
        o_ref[...] = _bf((_silu(gate) * up_)[HALO:HALO + tt])

    return pl.pallas_call(
        body, name=name, grid=(half, nt),
        in_specs=_halo_specs(T, tt, tc, lambda j: j) + _halo_specs(T, tt, tc, lambda j: j + half)
        + [pl.BlockSpec((SUBLANES, tc), lambda j, i: (0, j)), pl.BlockSpec((SUBLANES, tc), lambda j, i: (0, j + half))],
        out_specs=pl.BlockSpec((tt, tc), lambda j, i: (i, j)),
        out_shape=jax.ShapeDtypeStruct((T, D_FF), BF16),
        compiler_params=_cp("parallel", "parallel"),
    )(u, u, u, u, u, u, conv_w, conv_w)


def ffn_act_bwd(u, conv_w, dact, *, name):
    T = u.shape[0]
    tt, tc = _tile(T, 512), FFN_COLS
    half = D_FF // tc
    nt = T // tt

    def body(gp, gc, gn, up, uc, un, wg_ref, wu_ref, dp, dc_, dn, dug_ref, duu_ref, dwg_ref, dwu_ref):
        wg, wu = wg_ref[...], wu_ref[...]
        eg, eu = _extend(gp, gc, gn, nt), _extend(up, uc, un, nt)
        gate, up_ = _causal_conv(eg, wg, FFN_CONV), _causal_conv(eu, wu, FFN_CONV)
        da = _extend(dp, dc_, dn, nt)
        act, act_grad = _silu_and_grad(gate)
        dxg, dwg = _causal_conv_bwd(eg, da * up_ * act_grad, wg, FFN_CONV, tt)
        dxu, dwu = _causal_conv_bwd(eu, da * act, wu, FFN_CONV, tt)
        dug_ref[...] = _bf(dxg)
        duu_ref[...] = _bf(dxu)
        _accumulate(dwg_ref, dwg, 1)
        _accumulate(dwu_ref, dwu, 1)

    return pl.pallas_call(
        body, name=name, grid=(half, nt),
        in_specs=_halo_specs(T, tt, tc, lambda j: j) + _halo_specs(T, tt, tc, lambda j: j + half)
        + [pl.BlockSpec((SUBLANES, tc), lambda j, i: (0, j)), pl.BlockSpec((SUBLANES, tc), lambda j, i: (0, j + half))]
        + _halo_specs(T, tt, tc, lambda j: j),
        out_specs=[pl.BlockSpec((tt, tc), lambda j, i: (i, j)), pl.BlockSpec((tt, tc), lambda j, i: (i, j)),
                   pl.BlockSpec((SUBLANES, tc), lambda j, i: (0, j)), pl.BlockSpec((SUBLANES, tc), lambda j, i: (0, j))],
        out_shape=[jax.ShapeDtypeStruct((T, D_FF), BF16), jax.ShapeDtypeStruct((T, D_FF), BF16),
                   jax.ShapeDtypeStruct((SUBLANES, D_FF), F32), jax.ShapeDtypeStruct((SUBLANES, D_FF), F32)],
        compiler_params=_cp("parallel", "arbitrary"),
    )(u, u, u, u, u, u, conv_w, conv_w, dact, dact, dact)


def head_norm_fwd(x, gain, z=None, *, x_col=0, z_col=0, name, out_dtype=F32):
    T = x.shape[0]
    tt = _tile(T, 1024)
    gated = z is not None

    def body(*refs):
        x_ref, g_ref = refs[0], refs[1]
        o_ref = refs[-1]
        xf = x_ref[...]
        y = xf * lax.rsqrt(jnp.mean(xf * xf, axis=-1, keepdims=True) + EPS) * g_ref[...]
        if gated:
            y = y * _silu(refs[2][...])
        o_ref[...] = y.astype(out_dtype)

    ins = [pl.BlockSpec((tt, HEAD_DIM), lambda h, i: (i, x_col + h)), pl.BlockSpec((1, HEAD_DIM), lambda h, i: (0, 0))]
    args = [x, gain]
    if gated:
        ins.append(pl.BlockSpec((tt, HEAD_DIM), lambda h, i: (i, z_col + h)))
        args.append(z)
    return pl.pallas_call(
        body, name=name, grid=(HEADS, T // tt), in_specs=ins,
        out_specs=pl.BlockSpec((tt, HEAD_DIM), lambda h, i: (i, h)),
        out_shape=jax.ShapeDtypeStruct((T, D_MODEL), out_dtype),
        compiler_params=_cp("parallel", "parallel"),
    )(*args)


def head_norm_bwd(x, gain, dys, z=None, *, x_col=0, z_col=0, name, dx_dtype=F32):
    T = x.shape[0]
    tt = _tile(T, 1024)
    gated = z is not None
    nd = len(dys)

    def body(*refs):
        x_ref, g_ref = refs[0], refs[1]
        xf = x_ref[...]
        r = lax.rsqrt(jnp.mean(xf * xf, axis=-1, keepdims=True) + EPS)
        xh = xf * r
        dy = refs[2][...].astype(F32)
        for d_ref in refs[3:2 + nd]:
            dy = dy + d_ref[...].astype(F32)
        outs = refs[2 + nd + (1 if gated else 0):]
        if gated:
            zf = refs[2 + nd][...]
            dx_ref, dz_ref, dg_ref = outs
            gate, gate_grad = _silu_and_grad(zf)
            dz_ref[...] = _bf(dy * xh * g_ref[...] * gate_grad)
            dn = dy * gate
        else:
            dx_ref, dg_ref = outs
            dn = dy
        dxh = dn * g_ref[...]
        dx_ref[...] = (r * (dxh - xh * jnp.mean(dxh * xh, axis=-1, keepdims=True))).astype(dx_dtype)
        _accumulate(dg_ref, _fold_rows(dn * xh), 1)

    tile = pl.BlockSpec((tt, HEAD_DIM), lambda h, i: (i, h))
    ins = [pl.BlockSpec((tt, HEAD_DIM), lambda h, i: (i, x_col + h)), pl.BlockSpec((1, HEAD_DIM), lambda h, i: (0, 0))] + [tile] * nd
    args = [x, gain] + list(dys)
    outs = [tile]
    shapes = [jax.ShapeDtypeStruct((T, D_MODEL), dx_dtype)]
    if gated:
        ins.append(pl.BlockSpec((tt, HEAD_DIM), lambda h, i: (i, z_col + h)))
        args.append(z)
        outs.append(tile)
        shapes.append(jax.ShapeDtypeStruct((T, D_MODEL), BF16))
    outs.append(pl.BlockSpec((SUBLANES, HEAD_DIM), lambda h, i: (0, h)))
    shapes.append(jax.ShapeDtypeStruct((SUBLANES, D_MODEL), F32))
    return pl.pallas_call(
        body, name=name, grid=(HEADS, T // tt), in_specs=ins, out_specs=outs, out_shape=shapes,
        compiler_params=_cp("parallel", "arbitrary"),
    )(*args)


def gates_fwd(proj, a_log, dt_bias, *, name):
    T = proj.shape[0]
    tt = _tile(T, 1024)

    def body(p_ref, al_ref, dt_ref, o_ref):
        p = p_ref[...]
        lane = lax.broadcasted_iota(jnp.int32, p.shape, 1)
        o_ref[...] = jnp.where(lane < HEADS, -jnp.exp(al_ref[...]) * _softplus(p + dt_ref[...]), _sigmoid(p))

    return pl.pallas_call(
        body, name=name, grid=(T // tt,),
        in_specs=[pl.BlockSpec((tt, LANES), lambda i: (i, AB_BLOCK)), pl.BlockSpec((1, LANES), lambda i: (0, 0)),
                  pl.BlockSpec((1, LANES), lambda i: (0, 0))],
        out_specs=pl.BlockSpec((tt, LANES), lambda i: (i, 0)),
        out_shape=jax.ShapeDtypeStruct((T, LANES), F32),
        compiler_params=_cp("parallel"),
    )(proj, a_log, dt_bias)


def gates_bwd(proj, a_log, dt_bias, dgate, *, name):
    T = proj.shape[0]
    tt = _tile(T, 1024)

    def body(p_ref, al_ref, dt_ref, d_ref, dp_ref, dal_ref, ddt_ref):
        p, d = p_ref[...], d_ref[...]
        lane = lax.broadcasted_iota(jnp.int32, p.shape, 1)
        ea = jnp.exp(al_ref[...])
        pa = p + dt_ref[...]
        da = -d * ea * _sigmoid(pa)
        b = _sigmoid(p)
        dp_ref[...] = _bf(jnp.where(lane < HEADS, da, jnp.where(lane < 2 * HEADS, d * b * (1.0 - b), 0.0)))
        _accumulate(dal_ref, _fold_rows(jnp.where(lane < HEADS, -d * ea * _softplus(pa), 0.0)), 0)
        _accumulate(ddt_ref, _fold_rows(jnp.where(lane < HEADS, da, 0.0)), 0)

    acc = pl.BlockSpec((SUBLANES, LANES), lambda i: (0, 0))
    return pl.pallas_call(
        body, name=name, grid=(T // tt,),
        in_specs=[pl.BlockSpec((tt, LANES), lambda i: (i, AB_BLOCK)), pl.BlockSpec((1, LANES), lambda i: (0, 0)),
                  pl.BlockSpec((1, LANES), lambda i: (0, 0)), pl.BlockSpec((tt, LANES), lambda i: (i, 0))],
        out_specs=[pl.BlockSpec((tt, LANES), lambda i: (i, 0)), acc, acc],
        out_shape=[jax.ShapeDtypeStruct((T, LANES), BF16), jax.ShapeDtypeStruct((SUBLANES, LANES), F32),
                   jax.ShapeDtypeStruct((SUBLANES, LANES), F32)],
        compiler_params=_cp("arbitrary"),
    )(proj, a_log, dt_bias, dgate)


def loss_fwd(y, target, *, name):
    T, D = y.shape
    tt = _tile(T, 512)

    def body(y_ref, t_ref, dy_ref, l_ref):
        d = y_ref[...] - t_ref[...]
        dy_ref[...] = d * (1.0 / D)
        sq = d * d
        lanes = sq[:, 0:LANES]
        for c in range(1, D // LANES):
            lanes = lanes + sq[:, c * LANES:(c + 1) * LANES]
        _accumulate(l_ref, _fold_rows(lanes) * (0.5 / D), 0)

    return pl.pallas_call(
        body, name=name, grid=(T // tt,),
        in_specs=[pl.BlockSpec((tt, D), lambda i: (i, 0)), pl.BlockSpec((tt, D), lambda i: (i, 0))],
        out_specs=[pl.BlockSpec((tt, D), lambda i: (i, 0)), pl.BlockSpec((SUBLANES, LANES), lambda i: (0, 0))],
        out_shape=[jax.ShapeDtypeStruct((T, D), F32), jax.ShapeDtypeStruct((SUBLANES, LANES), F32)],
        compiler_params=_cp("arbitrary"),
    )(y, target)


def _split_bf16(x):
    hi = _bf(x)
    return hi, _bf(x - hi.astype(F32))


def _dot3(a, b, dot=_dot):
    return dot(a[0], b[0]) + dot(a[0], b[1]) + dot(a[1], b[0])


def _each(fn, *lists):
    return [fn(*args) for args in zip(*lists)]


def _unit_lower_inverses(lows):
    c = lows[0].shape[0]
    ii = lax.broadcasted_iota(jnp.int32, (c, c), 0)
    jj = lax.broadcasted_iota(jnp.int32, (c, c), 1)
    eye = jnp.where(ii == jj, 1.0, 0.0)
    invs = _each(lambda low: eye - low, lows)
    powers = _each(_split_bf16, lows)
    for _ in range(int(math.log2(c)) - 1):
        powers = _each(lambda p: _split_bf16(_dot3(p, p)), powers)
        invs = _each(lambda inv, p: inv + _dot3(_split_bf16(inv), p), invs, powers)
    return invs


def _gdn_chunks(heads):
    q, k, v, a_col, a_row, b_col, s0 = (list(t) for t in zip(*heads))
    c = q[0].shape[0]
    ii = lax.broadcasted_iota(jnp.int32, (c, c), 0)
    jj = lax.broadcasted_iota(jnp.int32, (c, c), 1)
    tri, strict = ii >= jj, ii > jj
    g_col = _each(lambda ar: jnp.sum(jnp.where(tri, ar, 0.0), axis=1, keepdims=True), a_row)
    g_row = _each(lambda ac: jnp.sum(jnp.where(ii <= jj, ac, 0.0), axis=0, keepdims=True), a_col)
    gam = _each(lambda gc, gr: jnp.exp(jnp.where(tri, gc - gr, -jnp.inf)), g_col, g_row)
    g_last = _each(lambda ac: jnp.sum(ac, axis=0, keepdims=True), a_col)
    gam_col = _each(jnp.exp, g_col)
    del_col = _each(lambda gl, gc: jnp.exp(gl - gc), g_last, g_col)
    kb = _each(lambda k_, b: k_ * b, k, b_col)
    m = _each(lambda kb_, k_: _dot_nt(_bf(kb_), _bf(k_)), kb, k)
    ks = _each(lambda k_, s: _dot(_bf(k_), _bf(s)), k, s0)
    qk = _each(lambda q_, k_: _dot_nt(_bf(q_), _bf(k_)), q, k)
    inv = _unit_lower_inverses(_each(lambda m_, g: jnp.where(strict, m_ * g, 0.0), m, gam))
    e = _each(lambda v_, gc, ks_: v_ - gc * ks_, v, gam_col, ks)
    inv = _each(_split_bf16, inv)
    vn = _each(lambda inv_, b, e_: _dot3(inv_, _split_bf16(b * e_)), inv, b_col, e)
    p = _each(lambda qk_, g: jnp.where(tri, qk_ * g, 0.0), qk, gam)
    return [dict(tri=tri, strict=strict, ii=ii, jj=jj, gam=gam[g], g_last=g_last[g], gam_col=gam_col[g], del_col=del_col[g],
                 kb=kb[g], m=m[g], inv=inv[g], ks=ks[g], e=e[g], vn=vn[g], qk=qk[g], p=p[g]) for g in range(len(heads))]


GDN_HEADS_PER_STEP = HEADS


def _head_cols(ref, g):
    return ref[:, g * HEAD_DIM:(g + 1) * HEAD_DIM]


def _load_heads(q_ref, k_ref, v_ref, ar_ref, br_ref, states):
    return [(_head_cols(q_ref, g), _head_cols(k_ref, g), _head_cols(v_ref, g), ar_ref[g, 0].T, ar_ref[g, 0], br_ref[g, 0].T, states(g))
            for g in range(GDN_HEADS_PER_STEP)]


def gdn_fwd(qkv, a_row, b_row, *, name):
    T = qkv.shape[0]
    C = GDN_CHUNK
    N = T // C
    G = GDN_HEADS_PER_STEP

    def body(q_ref, k_ref, v_ref, ar_ref, br_ref, o_ref, s_ref, state):
        @pl.when(pl.program_id(1) == 0)
        def _():
            state[...] = jnp.zeros_like(state)
        loaded = _load_heads(q_ref, k_ref, v_ref, ar_ref, br_ref, lambda g: state[g])
        ws = _gdn_chunks(loaded)
        qs = _each(lambda h: _dot(_bf(h[0]), _bf(h[6])), loaded)
        pv = _each(lambda w: _dot(_bf(w["p"]), _bf(w["vn"])), ws)
        kv = _each(lambda h, w: _dot_tn(_bf(w["del_col"] * h[1]), _bf(w["vn"])), loaded, ws)
        for g, w in enumerate(ws):
            s0 = loaded[g][6]
            s_ref[g, 0] = s0
            o_ref[:, g * HEAD_DIM:(g + 1) * HEAD_DIM] = w["gam_col"] * qs[g] + pv[g]
            state[g] = jnp.exp(w["g_last"]) * s0 + kv[g]

    per = HEADS // G
    blk = lambda off: pl.BlockSpec((C, G * HEAD_DIM), lambda h, n: (n, off * per + h))
    row = pl.BlockSpec((G, 1, 1, C), lambda h, n: (h, n, 0, 0))
    return pl.pallas_call(
        body, name=name, grid=(per, N),
        in_specs=[blk(0), blk(1), blk(2), row, row],
        out_specs=[blk(0), pl.BlockSpec((G, 1, HEAD_DIM, HEAD_DIM), lambda h, n: (h, n, 0, 0))],
        out_shape=[jax.ShapeDtypeStruct((T, D_MODEL), F32), jax.ShapeDtypeStruct((HEADS, N, HEAD_DIM, HEAD_DIM), F32)],
        scratch_shapes=[pltpu.VMEM((G, HEAD_DIM, HEAD_DIM), F32)],
        compiler_params=_cp("parallel", "arbitrary"),
    )(qkv, qkv, qkv, a_row, b_row)


def gdn_bwd(qkv, a_row, b_row, states, do, *, name):
    T = qkv.shape[0]
    C = GDN_CHUNK
    N = T // C
    G = GDN_HEADS_PER_STEP
    per = HEADS // G
    rev = lambda n: N - 1 - n
    row = pl.BlockSpec((G, 1, 1, C), lambda h, n: (h, rev(n), 0, 0))

    def body(q_ref, k_ref, v_ref, ar_ref, br_ref, s_ref, do_ref, dqkv_ref, da_ref, db_ref, dstate):
        @pl.when(pl.program_id(1) == 0)
        def _():
            dstate[...] = jnp.zeros_like(dstate)
        loaded = _load_heads(q_ref, k_ref, v_ref, ar_ref, br_ref, lambda g: s_ref[g, 0])
        hs = _gdn_chunks(loaded)
        for g, d in enumerate(hs):
            q, k, _, _, _, b, s0 = loaded[g]
            d.update(q=q, k=k, b=b, s0=s0, ds1=dstate[g], dout=_head_cols(do_ref, g))
        rows = lambda t: jnp.sum(t, axis=1, keepdims=True)
        ii_col = lax.broadcasted_iota(jnp.int32, (C, 1), 0)

        def stage(**fns):
            for key, fn in fns.items():
                for d in hs:
                    d[key] = fn(d)

        stage(s0b=lambda d: _bf(d["s0"]), ds1b=lambda d: _bf(d["ds1"]), doutb=lambda d: _bf(d["dout"]),
              kbf=lambda d: _bf(d["k"]), qbf=lambda d: _bf(d["q"]), vnb=lambda d: _bf(d["vn"]))
        stage(dvn=lambda d: _dot_tn(_bf(d["p"]), d["doutb"]) + _dot(_bf(d["del_col"] * d["k"]), d["ds1b"]),
              dqk=lambda d: jnp.where(d["tri"], _dot_nt(d["doutb"], d["vnb"]), 0.0) * d["gam"],
              qs=lambda d: _dot(d["qbf"], d["s0b"]),
              dkd=lambda d: _dot_nt(d["vnb"], d["ds1b"]))
        stage(dr=lambda d: _dot3(d["inv"], _split_bf16(d["dvn"]), _dot_tn),
              dq=lambda d: d["gam_col"] * _dot_nt(d["doutb"], d["s0b"]) + _dot(_bf(d["dqk"]), d["kbf"]),
              dk=lambda d: _dot_tn(_bf(d["dqk"]), d["qbf"]) + d["del_col"] * d["dkd"],
              ddel=lambda d: d["del_col"] * rows(d["dkd"] * d["k"]))
        stage(dg=lambda d: d["gam_col"] * rows(d["dout"] * d["qs"]) - d["ddel"],
              dg_last=lambda d: jnp.sum(d["ddel"], axis=0, keepdims=True)
              + jnp.exp(d["g_last"]) * jnp.sum(rows(d["ds1"] * d["s0"]), axis=0, keepdims=True),
              dm=lambda d: jnp.where(d["strict"], -_dot_nt(_bf(d["dr"]), d["vnb"]), 0.0) * d["gam"],
              de=lambda d: d["b"] * d["dr"])
        stage(dkb=lambda d: _dot(_bf(d["dm"]), d["kbf"]),
              dks=lambda d: -d["gam_col"] * d["de"])
        stage(dk=lambda d: d["dk"] + _dot_tn(_bf(d["dm"]), _bf(d["kb"])) + _dot_nt(_bf(d["dks"]), d["s0b"]) + d["b"] * d["dkb"],
              dbeta=lambda d: rows(d["dr"] * d["e"]) + rows(d["dkb"] * d["k"]),
              ds0=lambda d: jnp.exp(d["g_last"]) * d["ds1"] + _dot_tn(_bf(d["gam_col"] * d["q"]), d["doutb"])
              + _dot_tn(d["kbf"], _bf(d["dks"])),
              wg=lambda d: d["dqk"] * d["qk"] + d["dm"] * d["m"])
        stage(dg=lambda d: d["dg"] - d["gam_col"] * rows(d["de"] * d["ks"]) + rows(d["wg"])
              - jnp.sum(d["wg"], axis=0, keepdims=True).T + jnp.where(ii_col == C - 1, d["dg_last"], 0.0))
        stage(da=lambda d: jnp.sum(jnp.where(d["ii"] >= d["jj"], d["dg"], 0.0), axis=0, keepdims=True),
              db=lambda d: d["dbeta"].T)
        for g, d in enumerate(hs):
            dstate[g] = d["ds0"]
            da_ref[g, 0] = d["da"]
            db_ref[g, 0] = d["db"]
            for part, key in enumerate(("dq", "dk", "de")):
                start = part * D_MODEL + g * HEAD_DIM
                dqkv_ref[:, start:start + HEAD_DIM] = d[key]

    assert per == 1
    blk = lambda off: pl.BlockSpec((C, G * HEAD_DIM), lambda h, n: (rev(n), off * per + h))
    return pl.pallas_call(
        body, name=name, grid=(per, N),
        in_specs=[blk(0), blk(1), blk(2), row, row,
                  pl.BlockSpec((G, 1, HEAD_DIM, HEAD_DIM), lambda h, n: (h, rev(n), 0, 0)), blk(0)],
        out_specs=[pl.BlockSpec((C, 3 * D_MODEL), lambda h, n: (rev(n), 0)), row, row],
        out_shape=[jax.ShapeDtypeStruct((T, 3 * D_MODEL), F32)] + [jax.ShapeDtypeStruct((HEADS, N, 1, C), F32)] * 2,
        scratch_shapes=[pltpu.VMEM((G, HEAD_DIM, HEAD_DIM), F32)],
        compiler_params=_cp("parallel", "arbitrary"),
    )(qkv, qkv, qkv, a_row, b_row, states, do)


SB_BLOCK = 128


SB_QBLOCKS = 4


def _sb_rows(j, blk):
    return pl.ds(pl.multiple_of(j * blk, blk), blk)


def _sb_tile(qb, k_ref, i, j, blk, live):
    z = _dot_nt(qb, _bf(k_ref[_sb_rows(j, blk), :]))
    t_idx = i * blk + lax.broadcasted_iota(jnp.int32, (blk, blk), 0)
    s_idx = j * blk + lax.broadcasted_iota(jnp.int32, (blk, blk), 1)
    mask = jnp.logical_and(s_idx < t_idx, live)
    lf = jnp.where(mask, -_softplus(z), 0.0)
    return z, mask, lf


SB_DEAD = 105.0


def sb_fwd(q, k, v, *, k_col=0, v_col=0, name):
    T = q.shape[0]
    blk = _tile(T, SB_BLOCK)
    P = SB_QBLOCKS
    scale = HEAD_DIM ** -0.5

    def body(q_ref, k_ref, v_ref, o_ref, l_ref, n_ref):
        iq = [P * pl.program_id(1) + p for p in range(P)]
        qb = [_bf(q_ref[p * blk:(p + 1) * blk, :] * scale) for p in range(P)]
        r_idx = lax.broadcasted_iota(jnp.int32, (blk, blk), 0)
        c_idx = lax.broadcasted_iota(jnp.int32, (blk, blk), 1)
        later = _bf(jnp.where(r_idx > c_idx, 1.0, 0.0))

        def live_blocks(jj, runs):
            return [jnp.logical_and(jj <= i, jnp.max(run) > -SB_DEAD) for i, run in zip(iq, runs)]

        def alive(carry):
            jj, _, runs, _ = carry
            some = False
            for f in live_blocks(jj, runs):
                some = jnp.logical_or(some, f)
            return some

        def step(carry):
            jj, accs, runs, visited = carry
            live = live_blocks(jj, runs)
            js = [jnp.maximum(i - jj, 0) for i in iq]
            tiles = _each(lambda q_, i, j, f: _sb_tile(q_, k_ref, i, j, blk, f), qb, iq, js, live)
            parts = _each(lambda t: _split_bf16(t[2]), tiles)
            after = _each(lambda run, s: run + _dot(s[0], later) + _dot(s[1], later), runs, parts)
            a = _each(lambda t, af: jnp.where(t[1], jnp.exp(t[0] + t[2] + af), 0.0), tiles, after)
            vb = _each(lambda j: _bf(v_ref[_sb_rows(j, blk), :]), js)
            accs = _each(lambda acc, a_, v_: acc + _dot(_bf(a_), v_), accs, a, vb)
            runs = _each(lambda run, t: run + jnp.sum(t[2], axis=1, keepdims=True), runs, tiles)
            visited = _each(lambda n, f: n + f.astype(jnp.int32), visited, live)
            return jj + 1, accs, runs, visited

        start = (jnp.int32(0), [jnp.zeros((blk, HEAD_DIM), F32)] * P, [jnp.zeros((blk, 1), F32)] * P, [jnp.int32(0)] * P)
        _, accs, runs, visited = lax.while_loop(alive, step, start)
        for p in range(P):
            o_ref[p * blk:(p + 1) * blk, :] = accs[p]
            l_ref[0, p] = runs[p].T
            n_ref[0, p] = jnp.full((SUBLANES, LANES), visited[p].astype(F32))

    return pl.pallas_call(
        body, name=name, grid=(HEADS, T // (P * blk)),
        in_specs=[pl.BlockSpec((P * blk, HEAD_DIM), lambda h, i: (i, h)), pl.BlockSpec((T, HEAD_DIM), lambda h, i: (0, k_col + h)),
                  pl.BlockSpec((T, HEAD_DIM), lambda h, i: (0, v_col + h))],
        out_specs=[pl.BlockSpec((P * blk, HEAD_DIM), lambda h, i: (i, h)), pl.BlockSpec((1, P, 1, blk), lambda h, i: (h, i, 0, 0)),
                   pl.BlockSpec((1, P, SUBLANES, LANES), lambda h, i: (h, i, 0, 0))],
        out_shape=[jax.ShapeDtypeStruct((T, D_MODEL), F32), jax.ShapeDtypeStruct((HEADS, T // blk, 1, blk), F32),
                   jax.ShapeDtypeStruct((HEADS, T // blk, SUBLANES, LANES), F32)],
        compiler_params=_cp("parallel", "arbitrary"),
    )(q, k, v)


def sb_bwd(q, k, v, ltot, visited, do, *, k_col=0, v_col=0, name):
    T = q.shape[0]
    blk = _tile(T, SB_BLOCK)
    P = SB_QBLOCKS
    scale = HEAD_DIM ** -0.5

    def body(q_ref, k_ref, v_ref, l_ref, n_ref, do_ref, dq_ref, dk_ref, dv_ref):
        iq = [P * pl.program_id(1) + p for p in range(P)]
        count = [jnp.max(n_ref[0, p]).astype(jnp.int32) for p in range(P)]
        first = [i + 1 - n for i, n in zip(iq, count)]
        trips = count[0]
        for n in count[1:]:
            trips = jnp.maximum(trips, n)

        @pl.when(pl.program_id(1) == 0)
        def _():
            dk_ref[...] = jnp.zeros_like(dk_ref)
            dv_ref[...] = jnp.zeros_like(dv_ref)

        qb = [_bf(q_ref[p * blk:(p + 1) * blk, :] * scale) for p in range(P)]
        dob = [_bf(do_ref[p * blk:(p + 1) * blk, :]) for p in range(P)]
        ltot_ = [l_ref[0, p].T for p in range(P)]
        r_idx = lax.broadcasted_iota(jnp.int32, (blk, blk), 0)
        c_idx = lax.broadcasted_iota(jnp.int32, (blk, blk), 1)
        upto = _bf(jnp.where(r_idx <= c_idx, 1.0, 0.0))
        before = _bf(jnp.where(r_idx < c_idx, 1.0, 0.0))

        def step(t, carry):
            dqs, lpre, cpre = carry
            live = [f + t <= i for f, i in zip(first, iq)]
            js = [jnp.minimum(f + t, i) for f, i in zip(first, iq)]
            tiles = _each(lambda q_, i, j, f: _sb_tile(q_, k_ref, i, j, blk, f), qb, iq, js, live)
            parts = _each(lambda tl: _split_bf16(tl[2]), tiles)
            after = _each(lambda lt, lp, s: lt - (lp + _dot(s[0], upto) + _dot(s[1], upto)), ltot_, lpre, parts)
            ls = _each(lambda tl: tl[0] + tl[2], tiles)
            a = _each(lambda tl, ls_, af: jnp.where(tl[1], jnp.exp(ls_ + af), 0.0), tiles, ls, after)
            vb = _each(lambda j: _bf(v_ref[_sb_rows(j, blk), :]), js)
            p = _each(lambda a_, do_, v_: a_ * _dot_nt(do_, v_), a, dob, vb)
            pparts = _each(_split_bf16, p)
            left = _each(lambda cp, s: cp + _dot(s[0], before) + _dot(s[1], before), cpre, pparts)
            dzb = _each(lambda tl, p_, lf_, ls_: _bf(jnp.where(tl[1], p_ * jnp.exp(tl[2]) - lf_ * jnp.exp(ls_), 0.0)),
                        tiles, p, left, ls)
            dks = _each(lambda dz, q_: _dot_tn(dz, q_), dzb, qb)
            dvs = _each(lambda a_, do_: _dot_tn(_bf(a_), do_), a, dob)
            dqs = _each(lambda dq, dz, j: dq + _dot(dz, _bf(k_ref[_sb_rows(j, blk), :])), dqs, dzb, js)
            for j, dk, dv in zip(js, dks, dvs):
                dk_ref[_sb_rows(j, blk), :] += dk
                dv_ref[_sb_rows(j, blk), :] += dv
            lpre = _each(lambda lp, tl: lp + jnp.sum(tl[2], axis=1, keepdims=True), lpre, tiles)
            cpre = _each(lambda cp, p_: cp + jnp.sum(p_, axis=1, keepdims=True), cpre, p)
            return dqs, lpre, cpre

        zero = [jnp.zeros((blk, 1), F32)] * P
        dqs, _, _ = lax.fori_loop(0, trips, step, ([jnp.zeros((blk, HEAD_DIM), F32)] * P, zero, zero))
        for p in range(P):
            dq_ref[p * blk:(p + 1) * blk, :] = dqs[p] * scale

    full = lambda off: pl.BlockSpec((T, HEAD_DIM), lambda h, i: (0, off + h))
    tile = pl.BlockSpec((P * blk, HEAD_DIM), lambda h, i: (i, h))
    return pl.pallas_call(
        body, name=name, grid=(HEADS, T // (P * blk)),
        in_specs=[tile, full(k_col), full(v_col), pl.BlockSpec((1, P, 1, blk), lambda h, i: (h, i, 0, 0)),
                  pl.BlockSpec((1, P, SUBLANES, LANES), lambda h, i: (h, i, 0, 0)), tile],
        out_specs=[tile, full(0), full(0)],
        out_shape=[jax.ShapeDtypeStruct((T, D_MODEL), F32)] * 3,
        compiler_params=_cp("parallel", "arbitrary"),
    )(q, k, v, ltot, visited, do)


def sum_slots(slots, *, name):
    n, R, C = slots.shape
    tr = _tile(R, 256)

    def body(s_ref, o_ref):
        acc = s_ref[0].astype(F32)
        for k in range(1, n):
            acc = acc + s_ref[k].astype(F32)
        o_ref[...] = acc

    return pl.pallas_call(
        body, name=name, grid=(R // tr,),
        in_specs=[pl.BlockSpec((n, tr, C), lambda i: (0, i, 0))],
        out_specs=pl.BlockSpec((tr, C), lambda i: (i, 0)),
        out_shape=jax.ShapeDtypeStruct((R, C), F32),
        compiler_params=_cp("parallel"),
    )(slots)


def adamw(w, g_parts, m, v, *, name):
    R, C = w.shape
    tr = _tile(R, 256)
    n = len(g_parts)

    def body(*refs):
        w_ref, m_ref, v_ref = refs[0], refs[1 + n], refs[2 + n]
        g_ref, d_ref, nm_ref, nv_ref = refs[3 + n:]
        g = refs[1][...]
        for r in refs[2:1 + n]:
            g = g + r[...]
        m2 = ADAM_B1 * m_ref[...] + (1.0 - ADAM_B1) * g
        v2 = ADAM_B2 * v_ref[...] + (1.0 - ADAM_B2) * (g * g)
        m_hat = m2 / (1.0 - ADAM_B1 ** ADAM_STEP)
        v_hat = v2 / (1.0 - ADAM_B2 ** ADAM_STEP)
        g_ref[...] = g
        d_ref[...] = -ADAM_LR * (m_hat / (jnp.sqrt(v_hat) + ADAM_EPS) + ADAM_WD * w_ref[...])
        nm_ref[...] = m2
        nv_ref[...] = v2

    spec = pl.BlockSpec((tr, C), lambda i: (i, 0))
    return pl.pallas_call(
        body, name=name, grid=(R // tr,),
        in_specs=[spec] * (3 + n), out_specs=[spec] * 4,
        out_shape=[jax.ShapeDtypeStruct((R, C), F32)] * 4,
        compiler_params=_cp("parallel"),
    )(w, *g_parts, m, v)


CHIP_FLIPS = ((0, 1), (1, 0), (1, 1))


def _place():
    return lax.axis_index("x"), lax.axis_index("y"), lax.axis_index("c")


def _flip(v, f):
    return 1 - v if f else v


def _chip_exchange(arrays, *, scatter, name):
    n = len(arrays)
    if scatter:
        shapes = [jax.ShapeDtypeStruct(a.shape, a.dtype) for a in arrays]
    else:
        shapes = [jax.ShapeDtypeStruct((N_CHIPS,) + a.shape, a.dtype) for a in arrays]

    def body(*refs):
        ins, outs = refs[:n], refs[n:2 * n]
        send_sems, recv_sems, local_sems = refs[2 * n:]
        x, y, c = _place()
        me = 2 * x + y
        started = []
        for k in range(n):
            src_mine = ins[k].at[me] if scatter else ins[k]
            local = pltpu.make_async_copy(src_mine, outs[k].at[me], local_sems.at[k])
            local.start()
            started.append(local)
            for p, (fx, fy) in enumerate(CHIP_FLIPS):
                px, py = _flip(x, fx), _flip(y, fy)
                src = ins[k].at[2 * px + py] if scatter else ins[k]
                pltpu.make_async_remote_copy(src_ref=src, dst_ref=outs[k].at[me], send_sem=send_sems.at[k, p],
                                             recv_sem=recv_sems.at[k, p], device_id=(px, py, c), device_id_type=MESH).start()
        for k in range(n):
            for p, (fx, fy) in enumerate(CHIP_FLIPS):
                px, py = _flip(x, fx), _flip(y, fy)
                src = ins[k].at[2 * px + py] if scatter else ins[k]
                landing = pltpu.make_async_remote_copy(src_ref=src, dst_ref=outs[k].at[2 * px + py], send_sem=send_sems.at[k, p],
                                                       recv_sem=recv_sems.at[k, p], device_id=(px, py, c), device_id_type=MESH)
                landing.wait_send()
                landing.wait_recv()
        for local in started:
            local.wait()

    return pl.pallas_call(
        body, name=name, in_specs=[ANY] * n, out_specs=[ANY] * n, out_shape=shapes,
        scratch_shapes=[pltpu.SemaphoreType.DMA((n, len(CHIP_FLIPS))), pltpu.SemaphoreType.DMA((n, len(CHIP_FLIPS))),
                        pltpu.SemaphoreType.DMA((n,))],
    )(*arrays)


def sibling_swap(arrays, *, name):
    n = len(arrays)

    def body(*refs):
        ins, outs = refs[:n], refs[n:2 * n]
        send_sems, recv_sems = refs[2 * n:]
        x, y, c = _place()
        copies = [pltpu.make_async_remote_copy(src_ref=ins[k], dst_ref=outs[k], send_sem=send_sems.at[k], recv_sem=recv_sems.at[k],
                                               device_id=(x, y, 1 - c), device_id_type=MESH) for k in range(n)]
        for cp in copies:
            cp.start()
        for cp in copies:
            cp.wait_send()
            cp.wait_recv()

    return pl.pallas_call(
        body, name=name, in_specs=[ANY] * n, out_specs=[ANY] * n,
        out_shape=[jax.ShapeDtypeStruct(a.shape, a.dtype) for a in arrays],
        scratch_shapes=[pltpu.SemaphoreType.DMA((n,)), pltpu.SemaphoreType.DMA((n,))],
    )(*arrays)


DEVICE_FLIPS = tuple((fx, fy, fc) for fx in (0, 1) for fy in (0, 1) for fc in (0, 1) if fx or fy or fc)


def all_gather_devices(a, *, name):
    def body(a_ref, o_ref, send_sems, recv_sems, local_sem):
        x, y, c = _place()
        me = 4 * x + 2 * y + c
        local = pltpu.make_async_copy(a_ref, o_ref.at[me], local_sem)
        local.start()
        for p, (fx, fy, fc) in enumerate(DEVICE_FLIPS):
            peer = (_flip(x, fx), _flip(y, fy), _flip(c, fc))
            pltpu.make_async_remote_copy(src_ref=a_ref, dst_ref=o_ref.at[me], send_sem=send_sems.at[p], recv_sem=recv_sems.at[p],
                                         device_id=peer, device_id_type=MESH).start()
        for p, (fx, fy, fc) in enumerate(DEVICE_FLIPS):
            px, py, pc = _flip(x, fx), _flip(y, fy), _flip(c, fc)
            landing = pltpu.make_async_remote_copy(src_ref=a_ref, dst_ref=o_ref.at[4 * px + 2 * py + pc], send_sem=send_sems.at[p],
                                                   recv_sem=recv_sems.at[p], device_id=(px, py, pc), device_id_type=MESH)
            landing.wait_send()
            landing.wait_recv()
        local.wait()

    return pl.pallas_call(
        body, name=name, in_specs=[ANY], out_specs=ANY,
        out_shape=jax.ShapeDtypeStruct((N_DEV,) + a.shape, a.dtype),
        scratch_shapes=[pltpu.SemaphoreType.DMA((len(DEVICE_FLIPS),)), pltpu.SemaphoreType.DMA((len(DEVICE_FLIPS),)),
                        pltpu.SemaphoreType.DMA(())],
    )(a)


def _row(v):
    return v.reshape(1, -1)


def _pad_rows(w):
    return jnp.pad(w, ((0, SUBLANES - w.shape[0]), (0, 0)))


def _pad_lanes(v):
    return jnp.pad(v.reshape(1, -1), ((0, 0), (0, LANES - v.shape[-1])))


def _head_layouts(gates):
    T = gates.shape[0]
    rows = lambda cols: cols.T.reshape(HEADS, T // GDN_CHUNK, 1, GDN_CHUNK)
    return rows(gates[:, :HEADS]), rows(gates[:, HEADS:2 * HEADS])


def _ffn_fwd(x, W, l):
    u, h = norm_matmul(x, _row(W["ffn_norm"][l]), W["ffn_w_up"][l], name=f"ffn{l}_up")
    conv = _pad_rows(W["ffn_conv"][l])
    act = ffn_act_fwd(u, conv, name=f"ffn{l}_act")
    y = matmul_residual(act, W["ffn_w_down"][l], x, name=f"ffn{l}_down")
    return y, (x, h, u, conv, act)


def _ffn_bwd(dx, saved, W, l, G):
    x, h, u, conv, act = saved
    dact = matmul_nt(dx, W["ffn_w_down"][l], name=f"ffn{l}_down_dx", out_dtype=BF16)
    G["ffn_w_down"][l] = matmul_tn(act, dx, name=f"ffn{l}_down_dw")
    dug, duu, dwg, dwu = ffn_act_bwd(u, conv, dact, name=f"ffn{l}_act_bwd")
    G["ffn_conv"][l] = jnp.concatenate([dwg, dwu], axis=1)[:FFN_CONV]
    G["ffn_w_up"][l] = jnp.concatenate([matmul_tn(h, dug, name=f"ffn{l}_gate_dw"), matmul_tn(h, duu, name=f"ffn{l}_up_dw")], axis=1)
    dx, dgain = matmul_nt_normbwd([(dug, 0), (duu, 1)], W["ffn_w_up"][l], x, _row(W["ffn_norm"][l]), dx, name=f"ffn{l}_up_dx")
    G["ffn_norm"][l] = dgain.sum(0)
    return dx


def _gdn_fwd(x, W, l):
    proj, h = norm_matmul(x, _row(W["a_norm"][l]), W["a_w_in"][l], name=f"gdn{l}_in")
    conv = _pad_rows(W["a_conv"][l])
    a_log, dt_bias = _pad_lanes(W["a_log"][l]), _pad_lanes(W["a_dt_bias"][l])
    qkv = gdn_conv_fwd(proj, conv, name=f"gdn{l}_conv")
    heads = _head_layouts(gates_fwd(proj, a_log, dt_bias, name=f"gdn{l}_gates"))
    o, states = gdn_fwd(qkv, *heads, name=f"gdn{l}_rule")
    gain = _row(W["a_out_norm"][l])
    on = head_norm_fwd(o, gain, proj, z_col=Z_BLOCK, name=f"gdn{l}_outnorm", out_dtype=BF16)
    y = matmul_residual(on, W["a_w_out"][l], x, name=f"gdn{l}_out")
    return y, (x, h, proj, conv, a_log, dt_bias, qkv, heads, states, o, gain, on)


def _gdn_bwd(dx, saved, W, l, G):
    x, h, proj, conv, a_log, dt_bias, qkv, heads, states, o, gain, on = saved
    T = x.shape[0]
    don = matmul_nt(dx, W["a_w_out"][l], name=f"gdn{l}_out_dx")
    G["a_w_out"][l] = matmul_tn(on, dx, name=f"gdn{l}_out_dw")
    do, dz, dgain = head_norm_bwd(o, gain, [don], proj, z_col=Z_BLOCK, name=f"gdn{l}_outnorm_bwd")
    G["a_out_norm"][l] = dgain.reshape(SUBLANES, HEADS, HEAD_DIM).sum((0, 1))
    dqkv, da, db = gdn_bwd(qkv, *heads, states, do, name=f"gdn{l}_rule_bwd")
    dqkv, dconv = gdn_conv_bwd(proj, conv, dqkv, name=f"gdn{l}_conv_bwd")
    G["a_conv"][l] = dconv[:GDN_CONV]
    dgate = jnp.concatenate([da.reshape(HEADS, T).T, db.reshape(HEADS, T).T, jnp.zeros((T, LANES - 2 * HEADS), F32)], axis=1)
    dab, dal, ddt = gates_bwd(proj, a_log, dt_bias, dgate, name=f"gdn{l}_gates_bwd")
    G["a_log"][l] = dal.sum(0)[:HEADS]
    G["a_dt_bias"][l] = ddt.sum(0)[:HEADS]
    parts = [(dqkv, 0), (dz, Z_BLOCK * LANES // D_MODEL), (dab, AB_BLOCK)]
    G["a_w_in"][l] = jnp.concatenate([matmul_tn(h, d, name=f"gdn{l}_in_dw{i}") for i, (d, _) in enumerate(parts)], axis=1)
    dx, dgain = matmul_nt_normbwd(parts, W["a_w_in"][l], x, _row(W["a_norm"][l]), dx, name=f"gdn{l}_in_dx")
    G["a_norm"][l] = dgain.sum(0)
    return dx


def _sb_fwd(x, kn, kv, W, j):
    qp, h = norm_matmul(x, _row(W["b_norm"][j]), W["b_w_q"][j], name=f"sb{j}_q")
    gain = _row(W["q_norm"][j])
    q = head_norm_fwd(qp, gain, name=f"sb{j}_qnorm")
    o, ltot, visited = sb_fwd(q, kn, kv, v_col=HEADS, name=f"sb{j}_attn")
    y = matmul_residual(o, W["b_w_out"][j], x, name=f"sb{j}_out")
    return y, (x, h, qp, gain, q, o, ltot, visited)


def _sb_bwd(dx, saved, kn, kv, W, j, G):
    x, h, qp, gain, q, o, ltot, visited = saved
    do = matmul_nt(dx, W["b_w_out"][j], name=f"sb{j}_out_dx")
    G["b_w_out"][j] = matmul_tn(o, dx, name=f"sb{j}_out_dw")
    dq, dk, dv = sb_bwd(q, kn, kv, ltot, visited, do, v_col=HEADS, name=f"sb{j}_attn_bwd")
    dqp, dgain = head_norm_bwd(qp, gain, [dq], name=f"sb{j}_qnorm_bwd", dx_dtype=BF16)
    G["q_norm"][j] = dgain.reshape(SUBLANES, HEADS, HEAD_DIM).sum((0, 1))
    G["b_w_q"][j] = matmul_tn(h, dqp, name=f"sb{j}_q_dw")
    dx, dgain = matmul_nt_normbwd([(dqp, 0)], W["b_w_q"][j], x, _row(W["b_norm"][j]), dx, name=f"sb{j}_q_dx")
    G["b_norm"][j] = dgain.sum(0)
    return dx, dk, dv


def local_step(x, target, W):
    G = {k: [None] * (N_A if k.startswith("a_") else N_B if k in ("b_norm", "b_w_q", "q_norm", "b_w_out") else DEPTH)
         for k in ("a_norm", "a_w_in", "a_conv", "a_log", "a_dt_bias", "a_out_norm", "a_w_out", "b_norm", "b_w_q", "q_norm",
                   "b_w_out", "ffn_norm", "ffn_w_up", "ffn_conv", "ffn_w_down")}
    tape = []
    for l in range(N_A):
        x, s_mix = _gdn_fwd(x, W, l)
        x, s_ffn = _ffn_fwd(x, W, l)
        tape.append((s_mix, s_ffn))
    x_kv = x
    kv, h_kv = norm_matmul(x, _row(W["kv_norm"]), W["w_kv"], name="kv_proj")
    k_gain = _row(W["k_norm"])
    kn = head_norm_fwd(kv, k_gain, name="k_norm")
    for j in range(N_B):
        x, s_mix = _sb_fwd(x, kn, kv, W, j)
        x, s_ffn = _ffn_fwd(x, W, N_A + j)
        tape.append((s_mix, s_ffn))
    dx, loss = loss_fwd(x, target, name="loss")

    dks, dvs = [], []
    for j in reversed(range(N_B)):
        s_mix, s_ffn = tape[N_A + j]
        dx = _ffn_bwd(dx, s_ffn, W, N_A + j, G)
        dx, dk, dv = _sb_bwd(dx, s_mix, kn, kv, W, j, G)
        dks.append(dk)
        dvs.append(dv)
    dkp, dgain = head_norm_bwd(kv, k_gain, dks, name="k_norm_bwd", dx_dtype=BF16)
    G["k_norm"] = dgain.reshape(SUBLANES, HEADS, HEAD_DIM).sum((0, 1))
    dv = dvs[0] + dvs[1]
    G["w_kv"] = jnp.concatenate([matmul_tn(h_kv, dkp, name="k_proj_dw"), matmul_tn(h_kv, dv, name="v_proj_dw")], axis=1)
    dx, dgain = matmul_nt_normbwd([(dkp, 0), (dv, 1)], W["w_kv"], x_kv, _row(W["kv_norm"]), dx, name="kv_proj_dx")
    G["kv_norm"] = dgain.sum(0)
    for l in reversed(range(N_A)):
        s_mix, s_ffn = tape[l]
        dx = _ffn_bwd(dx, s_ffn, W, l, G)
        dx = _gdn_bwd(dx, s_mix, W, l, G)
    return loss, dx, G


MATRICES = {"a_w_in": 2, "a_w_out": 1, "w_kv": 1, "b_w_q": 1, "b_w_out": 1, "ffn_w_up": 2, "ffn_w_down": 1}
SMALL_SHARDED = {"a_norm": 1, "a_conv": 2, "ffn_conv": 2}
SMALL_REPLICATED = ("a_log", "a_dt_bias", "a_out_norm", "kv_norm", "k_norm", "b_norm", "q_norm", "ffn_norm")
WEIGHT_ORDER = ("a_norm", "a_w_in", "a_conv", "a_log", "a_dt_bias", "a_out_norm", "a_w_out", "kv_norm", "w_kv", "k_norm",
                "b_norm", "b_w_q", "q_norm", "b_w_out", "ffn_norm", "ffn_w_up", "ffn_conv", "ffn_w_down")
SMALL_ORDER = tuple(n for n in WEIGHT_ORDER if n not in MATRICES)
PACK_QUANTUM = SUBLANES * LANES


def _unshard(g, axis):
    g = jnp.moveaxis(g, 0, axis)
    return g.reshape(g.shape[:axis] + (g.shape[axis] * g.shape[axis + 1],) + g.shape[axis + 2:])


def _shards(full, axis):
    n = full.shape[axis] // N_CHIPS
    return jnp.moveaxis(full.reshape(full.shape[:axis] + (N_CHIPS, n) + full.shape[axis + 1:]), axis, 0)


def _pack(arrays):
    parts = []
    for a in arrays:
        flat = a.reshape(-1)
        parts.append(jnp.pad(flat, (0, -flat.shape[0] % PACK_QUANTUM)).reshape(-1, LANES))
    return jnp.concatenate(parts, axis=0)


def _unpack(buf, shapes):
    out, row = [], 0
    for s in shapes:
        size = math.prod(s)
        rows = -(-size // PACK_QUANTUM) * SUBLANES
        out.append(buf[row:row + rows].reshape(-1)[:size].reshape(s))
        row += rows
    return out


def _stack(per_layer):
    return jnp.stack(per_layer) if isinstance(per_layer, list) else per_layer


def _as_2d(a):
    return a.reshape(-1, a.shape[-1])


def kernel(x, a_norm, a_w_in, a_conv, a_log, a_dt_bias, a_out_norm, a_w_out, kv_norm, w_kv, k_norm, b_norm, b_w_q, q_norm, b_w_out, ffn_norm, ffn_w_up, ffn_conv, ffn_w_down, loss_target, m_a_norm, m_a_w_in, m_a_conv, m_a_log, m_a_dt_bias, m_a_out_norm, m_a_w_out, m_kv_norm, m_w_kv, m_k_norm, m_b_norm, m_b_w_q, m_q_norm, m_b_w_out, m_ffn_norm, m_ffn_w_up, m_ffn_conv, m_ffn_w_down, v_a_norm, v_a_w_in, v_a_conv, v_a_log, v_a_dt_bias, v_a_out_norm, v_a_w_out, v_kv_norm, v_w_kv, v_k_norm, v_b_norm, v_b_w_q, v_q_norm, v_b_w_out, v_ffn_norm, v_ffn_w_up, v_ffn_conv, v_ffn_w_down):
    local = dict(a_norm=a_norm, a_w_in=a_w_in, a_conv=a_conv, a_log=a_log, a_dt_bias=a_dt_bias, a_out_norm=a_out_norm,
                 a_w_out=a_w_out, kv_norm=kv_norm, w_kv=w_kv, k_norm=k_norm, b_norm=b_norm, b_w_q=b_w_q, q_norm=q_norm,
                 b_w_out=b_w_out, ffn_norm=ffn_norm, ffn_w_up=ffn_w_up, ffn_conv=ffn_conv, ffn_w_down=ffn_w_down)
    mom = dict(a_norm=m_a_norm, a_w_in=m_a_w_in, a_conv=m_a_conv, a_log=m_a_log, a_dt_bias=m_a_dt_bias, a_out_norm=m_a_out_norm,
               a_w_out=m_a_w_out, kv_norm=m_kv_norm, w_kv=m_w_kv, k_norm=m_k_norm, b_norm=m_b_norm, b_w_q=m_b_w_q, q_norm=m_q_norm,
               b_w_out=m_b_w_out, ffn_norm=m_ffn_norm, ffn_w_up=m_ffn_w_up, ffn_conv=m_ffn_conv, ffn_w_down=m_ffn_w_down)
    var = dict(a_norm=v_a_norm, a_w_in=v_a_w_in, a_conv=v_a_conv, a_log=v_a_log, a_dt_bias=v_a_dt_bias, a_out_norm=v_a_out_norm,
               a_w_out=v_a_w_out, kv_norm=v_kv_norm, w_kv=v_w_kv, k_norm=v_k_norm, b_norm=v_b_norm, b_w_q=v_b_w_q, q_norm=v_q_norm,
               b_w_out=v_b_w_out, ffn_norm=v_ffn_norm, ffn_w_up=v_ffn_w_up, ffn_conv=v_ffn_conv, ffn_w_down=v_ffn_w_down)
    chip = 2 * lax.axis_index("x") + lax.axis_index("y")

    mats = list(MATRICES)
    small_sharded = list(SMALL_SHARDED)
    gathered = _chip_exchange([local[n].astype(BF16) for n in mats] + [_pack([local[n] for n in small_sharded])],
                              scatter=False, name="gather_weights")
    W = {n: local[n] for n in SMALL_REPLICATED}
    for n, g in zip(mats, gathered):
        W[n] = _unshard(g, MATRICES[n])
    W["a_w_in"] = jnp.pad(W["a_w_in"], ((0, 0), (0, 0), (0, W_IN_PAD - W_IN_COLS)))
    shard_shapes = [local[n].shape for n in small_sharded]
    per_chip = [_unpack(gathered[-1][j], shard_shapes) for j in range(N_CHIPS)]
    for i, n in enumerate(small_sharded):
        W[n] = _unshard(jnp.stack([per_chip[j][i] for j in range(N_CHIPS)]), SMALL_SHARDED[n])

    T = x.shape[1]
    loss_part, dx, G = local_step(x.reshape(T, D_MODEL), loss_target.reshape(T, D_MODEL), W)
    G = {n: _stack(g) for n, g in G.items()}
    G["a_w_in"] = G["a_w_in"][:, :, :W_IN_COLS]

    contrib = [_shards(G[n], MATRICES[n]).astype(BF16) for n in mats]
    contrib = [c.reshape(N_CHIPS, -1, c.shape[-1]) for c in contrib]
    received = _chip_exchange(contrib, scatter=True, name="scatter_grads")
    mine = [sum_slots(r, name=f"sum_{n}") for n, r in zip(mats, received)]
    theirs = sibling_swap(mine, name="swap_grads")

    small_full = {n: G[n] for n in SMALL_ORDER}
    packed = _pack([small_full[n] for n in SMALL_ORDER] + [loss_part])
    total = sum_slots(all_gather_devices(packed, name="gather_small"), name="sum_small")
    small_shapes = [small_full[n].shape for n in SMALL_ORDER] + [loss_part.shape]
    summed = dict(zip(SMALL_ORDER + ("loss",), _unpack(total, small_shapes)))
    loss = jnp.sum(summed.pop("loss"))
    for n, axis in SMALL_SHARDED.items():
        size = local[n].shape[axis]
        summed[n] = lax.dynamic_slice_in_dim(summed[n], chip * size, size, axis)

    grads, deltas, new_m, new_v = {}, {}, {}, {}
    for n, p_mine, p_theirs in zip(mats, mine, theirs):
        outs = adamw(_as_2d(local[n]), [p_mine, p_theirs], _as_2d(mom[n]), _as_2d(var[n]), name=f"adamw_{n}")
        grads[n], deltas[n], new_m[n], new_v[n] = [o.reshape(local[n].shape) for o in outs]
    small_local_shapes = [local[n].shape for n in SMALL_ORDER]
    outs = adamw(_pack([local[n] for n in SMALL_ORDER]), [_pack([summed[n] for n in SMALL_ORDER])],
                 _pack([mom[n] for n in SMALL_ORDER]), _pack([var[n] for n in SMALL_ORDER]), name="adamw_small")
    for d, o in zip((grads, deltas, new_m, new_v), outs):
        d.update(zip(SMALL_ORDER, _unpack(o, small_local_shapes)))

    return (loss, dx.reshape(x.shape), *[grads[n] for n in WEIGHT_ORDER], *[deltas[n] for n in WEIGHT_ORDER],
            *[new_m[n] for n in WEIGHT_ORDER], *[new_v[n] for n in WEIGHT_ORDER])
```

```python
import math

import jax
import jax.numpy as jnp
from jax import lax
from jax.experimental import pallas as pl
from jax.experimental.pallas import tpu as pltpu

F32 = jnp.float32
BF16 = jnp.bfloat16

D_MODEL = 1024
HEADS = 8
HEAD_DIM = 128
GDN_CONV = 4
GDN_CHUNK = 64
D_FF = 2816
FFN_CONV = 3
EPS = 1e-6
N_A = 2
N_B = 2
DEPTH = N_A + N_B
W_IN_COLS = 4 * D_MODEL + 2 * HEADS
W_IN_PAD = 4 * D_MODEL + 128
Z_BLOCK = 3 * D_MODEL // 128
AB_BLOCK = 4 * D_MODEL // 128

ADAM_LR = 0.001
ADAM_B1 = 0.9
ADAM_B2 = 0.999
ADAM_EPS = 1e-08
ADAM_WD = 0.01
ADAM_STEP = 10

LANES = 128
SUBLANES = 8
VMEM_LIMIT = 56 * 1024 * 1024
HALO = SUBLANES
N_CHIPS = 4
N_DEV = 8

HI = lax.Precision.HIGHEST
MESH = pl.DeviceIdType.MESH
ANY = pl.BlockSpec(memory_space=pl.ANY)


def _cp(*sem):
    return pltpu.CompilerParams(dimension_semantics=sem, vmem_limit_bytes=VMEM_LIMIT)


def _tile(n, want, align=SUBLANES):
    t = (min(n, want) // align) * align
    while t > 0 and n % t:
        t -= align
    return t if t > 0 else n


def _dot(a, b, precision=None):
    return jnp.dot(a, b, preferred_element_type=F32, precision=precision)


def _dot_nt(a, b, precision=None):
    return lax.dot_general(a, b, (((1,), (1,)), ((), ())), preferred_element_type=F32, precision=precision)


def _dot_tn(a, b, precision=None):
    return lax.dot_general(a, b, (((0,), (0,)), ((), ())), preferred_element_type=F32, precision=precision)


def _bf(x):
    return x.astype(BF16)


def _sigmoid(x):
    return 0.5 * jnp.tanh(0.5 * x) + 0.5


def _softplus(x):
    return jnp.maximum(x, 0.0) + jnp.log(1.0 + jnp.exp(-jnp.abs(x)))


def _silu(x):
    return x * _sigmoid(x)


def _silu_and_grad(x):
    s = _sigmoid(x)
    return x * s, s * (1.0 + x * (1.0 - s))


def _fold_rows(v):
    return jnp.sum(v.reshape(v.shape[0] // SUBLANES, SUBLANES, v.shape[1]), axis=0)


def _accumulate(ref, value, axis):
    @pl.when(pl.program_id(axis) == 0)
    def _():
        ref[...] = jnp.zeros_like(ref)
    ref[...] += value


TILE_BUDGET = 44 * 1024 * 1024


def _rows_that_fit(T, fixed_bytes, row_bytes, want=1024):
    tm = _tile(T, want)
    while tm > SUBLANES and 2 * (fixed_bytes + tm * row_bytes) > TILE_BUDGET:
        tm //= 2
    return tm


def norm_matmul(x, gain, w, *, name):
    T, D = x.shape
    N = w.shape[1]
    tn = N
    tm = _rows_that_fit(T, D * N * 2, D * 4 + D * 2 + N * 4)

    def body(x_ref, g_ref, w_ref, y_ref, h_ref):
        @pl.when(pl.program_id(1) == 0)
        def _():
            xf = x_ref[...]
            r = lax.rsqrt(jnp.mean(xf * xf, axis=-1, keepdims=True) + EPS)
            h_ref[...] = _bf(xf * r * g_ref[...])
        y_ref[...] = _dot(h_ref[...], w_ref[...])

    return pl.pallas_call(
        body, name=name, grid=(T // tm, N // tn),
        in_specs=[pl.BlockSpec((tm, D), lambda i, j: (i, 0)), pl.BlockSpec((1, D), lambda i, j: (0, 0)),
                  pl.BlockSpec((D, tn), lambda i, j: (0, j))],
        out_specs=[pl.BlockSpec((tm, tn), lambda i, j: (i, j)), pl.BlockSpec((tm, D), lambda i, j: (i, 0))],
        out_shape=[jax.ShapeDtypeStruct((T, N), F32), jax.ShapeDtypeStruct((T, D), BF16)],
        compiler_params=_cp("parallel", "arbitrary"),
    )(x, gain, w)


def matmul_residual(a, w, res, *, name):
    T, K = a.shape
    N = w.shape[1]
    tn = N
    tm = _rows_that_fit(T, K * N * 2, K * a.dtype.itemsize + 2 * N * 4)

    def body(a_ref, w_ref, r_ref, o_ref):
        o_ref[...] = r_ref[...] + _dot(_bf(a_ref[...]), w_ref[...])

    return pl.pallas_call(
        body, name=name, grid=(T // tm, N // tn),
        in_specs=[pl.BlockSpec((tm, K), lambda i, j: (i, 0)), pl.BlockSpec((K, tn), lambda i, j: (0, j)),
                  pl.BlockSpec((tm, tn), lambda i, j: (i, j))],
        out_specs=pl.BlockSpec((tm, tn), lambda i, j: (i, j)),
        out_shape=jax.ShapeDtypeStruct((T, N), F32),
        compiler_params=_cp("parallel", "parallel"),
    )(a, w, res)


def matmul_nt(dy, w, *, name, out_dtype=F32):
    T, N = dy.shape
    K = w.shape[0]
    tk = K
    tm = _rows_that_fit(T, K * N * 2, N * dy.dtype.itemsize + K * jnp.dtype(out_dtype).itemsize)

    def body(dy_ref, w_ref, o_ref):
        o_ref[...] = _dot_nt(_bf(dy_ref[...]), w_ref[...]).astype(out_dtype)

    return pl.pallas_call(
        body, name=name, grid=(T // tm, K // tk),
        in_specs=[pl.BlockSpec((tm, N), lambda i, j: (i, 0)), pl.BlockSpec((tk, N), lambda i, j: (j, 0))],
        out_specs=pl.BlockSpec((tm, tk), lambda i, j: (i, j)),
        out_shape=jax.ShapeDtypeStruct((T, K), out_dtype),
        compiler_params=_cp("parallel", "parallel"),
    )(dy, w)


def matmul_nt_normbwd(parts, w, x, gain, dres, *, name):
    T = x.shape[0]
    D = w.shape[0]
    tm = _tile(T, 256)
    n = len(parts)

    def body(*refs):
        x_ref, g_ref, dr_ref, dx_ref, dg_ref = refs[2 * n:]
        dh = _dot_nt(_bf(refs[0][...]), refs[n][...])
        for i in range(1, n):
            dh = dh + _dot_nt(_bf(refs[i][...]), refs[n + i][...])
        xf = x_ref[...]
        r = lax.rsqrt(jnp.mean(xf * xf, axis=-1, keepdims=True) + EPS)
        xh = xf * r
        dxh = dh * g_ref[...]
        dx_ref[...] = dr_ref[...] + r * (dxh - xh * jnp.mean(dxh * xh, axis=-1, keepdims=True))
        _accumulate(dg_ref, _fold_rows(dh * xh), 0)

    dy_specs = [pl.BlockSpec((tm, dy.shape[1]), lambda i: (i, 0)) for dy, _ in parts]
    w_specs = [pl.BlockSpec((D, dy.shape[1]), lambda i, b=b: (0, b)) for dy, b in parts]
    return pl.pallas_call(
        body, name=name, grid=(T // tm,),
        in_specs=dy_specs + w_specs + [pl.BlockSpec((tm, D), lambda i: (i, 0)), pl.BlockSpec((1, D), lambda i: (0, 0)),
                                       pl.BlockSpec((tm, D), lambda i: (i, 0))],
        out_specs=[pl.BlockSpec((tm, D), lambda i: (i, 0)), pl.BlockSpec((SUBLANES, D), lambda i: (0, 0))],
        out_shape=[jax.ShapeDtypeStruct((T, D), F32), jax.ShapeDtypeStruct((SUBLANES, D), F32)],
        compiler_params=_cp("arbitrary"),
    )(*[dy for dy, _ in parts], *([w] * n), x, gain, dres)


def matmul_tn(a, dy, *, name):
    T, K = a.shape
    N = dy.shape[1]
    tk = _tile(K, 1408, LANES)
    tn = _tile(N, 1408 if N <= 2816 else 512, LANES)
    tm = _tile(T, 4096 if tn <= 512 else 2048)

    def body(a_ref, dy_ref, o_ref):
        _accumulate(o_ref, _dot_tn(_bf(a_ref[...]), _bf(dy_ref[...])), 2)

    return pl.pallas_call(
        body, name=name, grid=(K // tk, N // tn, T // tm),
        in_specs=[pl.BlockSpec((tm, tk), lambda i, j, t: (t, i)), pl.BlockSpec((tm, tn), lambda i, j, t: (t, j))],
        out_specs=pl.BlockSpec((tk, tn), lambda i, j, t: (i, j)),
        out_shape=jax.ShapeDtypeStruct((K, N), F32),
        compiler_params=_cp("parallel", "parallel", "arbitrary"),
    )(a, dy)


def _halo_specs(T, tt, tc, col):
    per = tt // HALO
    last = T // HALO - 1
    return [pl.BlockSpec((HALO, tc), lambda j, i: (jnp.maximum(i * per - 1, 0), col(j))),
            pl.BlockSpec((tt, tc), lambda j, i: (i, col(j))),
            pl.BlockSpec((HALO, tc), lambda j, i: (jnp.minimum((i + 1) * per, last), col(j)))]


def _extend(prev_ref, cur_ref, next_ref, nt):
    i = pl.program_id(1)
    p = jnp.where(i > 0, prev_ref[...].astype(F32), 0.0)
    q = jnp.where(i < nt - 1, next_ref[...].astype(F32), 0.0)
    return jnp.concatenate([p, cur_ref[...].astype(F32), q], axis=0)


def _rows_before(e, s):
    return e if s == 0 else pltpu.roll(e, s, 0)


def _rows_after(e, s):
    return e if s == 0 else pltpu.roll(e, e.shape[0] - s, 0)


def _causal_conv(e, w, taps):
    y = w[taps - 1:taps, :] * e
    for s in range(1, taps):
        y = y + w[taps - 1 - s:taps - s, :] * _rows_before(e, s)
    return y


def _causal_conv_bwd(e, dc, w, taps, tt):
    lo, hi = HALO, HALO + tt
    dx = w[taps - 1:taps, :] * dc
    dws = [None] * taps
    dws[taps - 1] = jnp.sum((e * dc)[lo:hi], axis=0, keepdims=True)
    for s in range(1, taps):
        dx = dx + w[taps - 1 - s:taps - s, :] * _rows_after(dc, s)
        dws[taps - 1 - s] = jnp.sum((_rows_before(e, s) * dc)[lo:hi], axis=0, keepdims=True)
    dw = jnp.concatenate(dws + [jnp.zeros((SUBLANES - taps, e.shape[1]), F32)], axis=0)
    return dx[lo:hi], dw


def _qkv_kind(col_block):
    return (col_block >= HEADS).astype(jnp.int32) + (col_block >= 2 * HEADS).astype(jnp.int32)


def _l2norm_scale(kind):
    return jnp.where(kind == 0, HEAD_DIM ** -0.5, 1.0)


def gdn_conv_fwd(proj, conv_w, *, name, exchange=None):
    T = proj.shape[0]
    tt, tc = _tile(T, 512), HEAD_DIM
    nt = T // tt

    def body(p_ref, c_ref, n_ref, w_ref, o_ref):
        kind = _qkv_kind(pl.program_id(0))
        e = _extend(p_ref, c_ref, n_ref, nt)
        s = _silu(_causal_conv(e, w_ref[...], GDN_CONV))[HALO:HALO + tt]
        r = lax.rsqrt(jnp.sum(s * s, axis=-1, keepdims=True) + EPS) * _l2norm_scale(kind)
        o_ref[...] = jnp.where(kind == 2, s, s * r)

    (qkv,), brought = hosted_call(
        body, name=name, grid=(3 * HEADS, nt),
        in_specs=_halo_specs(T, tt, tc, lambda j: j) + [pl.BlockSpec((SUBLANES, tc), lambda j, i: (0, j))],
        out_specs=[pl.BlockSpec((tt, tc), lambda j, i: (i, j))],
        out_shape=[jax.ShapeDtypeStruct((T, 3 * D_MODEL), F32)],
        scratch_shapes=[], args=(proj, proj, proj, conv_w), exchange=exchange)
    return qkv, brought


def gdn_conv_bwd(proj, conv_w, dqkv, *, name):
    T = proj.shape[0]
    tt, tc = _tile(T, 512), HEAD_DIM
    nt = T // tt

    def body(p_ref, c_ref, n_ref, w_ref, dp_ref, dc_ref, dn_ref, dx_ref, dw_ref):
        kind = _qkv_kind(pl.program_id(0))
        w = w_ref[...]
        e = _extend(p_ref, c_ref, n_ref, nt)
        c = _causal_conv(e, w, GDN_CONV)
        s, s_grad = _silu_and_grad(c)
        dy = _extend(dp_ref, dc_ref, dn_ref, nt)
        r = lax.rsqrt(jnp.sum(s * s, axis=-1, keepdims=True) + EPS)
        y = s * r
        ds_norm = r * _l2norm_scale(kind) * (dy - y * jnp.sum(dy * y, axis=-1, keepdims=True))
        ds = jnp.where(kind == 2, dy, ds_norm)
        dx, dw = _causal_conv_bwd(e, ds * s_grad, w, GDN_CONV, tt)
        dx_ref[...] = _bf(dx)
        _accumulate(dw_ref, dw, 1)

    return pl.pallas_call(
        body, name=name, grid=(3 * HEADS, nt),
        in_specs=_halo_specs(T, tt, tc, lambda j: j) + [pl.BlockSpec((SUBLANES, tc), lambda j, i: (0, j))]
        + _halo_specs(T, tt, tc, lambda j: j),
        out_specs=[pl.BlockSpec((tt, tc), lambda j, i: (i, j)), pl.BlockSpec((SUBLANES, tc), lambda j, i: (0, j))],
        out_shape=[jax.ShapeDtypeStruct((T, 3 * D_MODEL), BF16), jax.ShapeDtypeStruct((SUBLANES, 3 * D_MODEL), F32)],
        compiler_params=_cp("parallel", "arbitrary"),
    )(proj, proj, proj, conv_w, dqkv, dqkv, dqkv)


FFN_COLS = 256


def ffn_act_fwd(u, conv_w, *, name):
    T = u.shape[0]
    tt, tc = _tile(T, 512), FFN_COLS
    half = D_FF // tc
    nt = T // tt

    def body(gp, gc, gn, up, uc, un, wg_ref, wu_ref, o_ref):
        gate = _causal_conv(_extend(gp, gc, gn, nt), wg_ref[...], FFN_CONV)
        up_ = _causal_conv(_extend(up, uc, un, nt), wu_ref[...], FFN_CONV)
        o_ref[...] = _bf((_silu(gate) * up_)[HALO:HALO + tt])

    return pl.pallas_call(
        body, name=name, grid=(half, nt),
        in_specs=_halo_specs(T, tt, tc, lambda j: j) + _halo_specs(T, tt, tc, lambda j: j + half)
        + [pl.BlockSpec((SUBLANES, tc), lambda j, i: (0, j)), pl.BlockSpec((SUBLANES, tc), lambda j, i: (0, j + half))],
        out_specs=pl.BlockSpec((tt, tc), lambda j, i: (i, j)),
        out_shape=jax.ShapeDtypeStruct((T, D_FF), BF16),
        compiler_params=_cp("parallel", "parallel"),
    )(u, u, u, u, u, u, conv_w, conv_w)


def ffn_act_bwd(u, conv_w, dact, *, name):
    T = u.shape[0]
    tt, tc = _tile(T, 512), FFN_COLS
    half = D_FF // tc
    nt = T // tt

    def body(gp, gc, gn, up, uc, un, wg_ref, wu_ref, dp, dc_, dn, dug_ref, duu_ref, dwg_ref, dwu_ref):
        wg, wu = wg_ref[...], wu_ref[...]
        eg, eu = _extend(gp, gc, gn, nt), _extend(up, uc, un, nt)
        gate, up_ = _causal_conv(eg, wg, FFN_CONV), _causal_conv(eu, wu, FFN_CONV)
        da = _extend(dp, dc_, dn, nt)
        act, act_grad = _silu_and_grad(gate)
        dxg, dwg = _causal_conv_bwd(eg, da * up_ * act_grad, wg, FFN_CONV, tt)
        dxu, dwu = _causal_conv_bwd(eu, da * act, wu, FFN_CONV, tt)
        dug_ref[...] = _bf(dxg)
        duu_ref[...] = _bf(dxu)
        _accumulate(dwg_ref, dwg, 1)
        _accumulate(dwu_ref, dwu, 1)

    return pl.pallas_call(
        body, name=name, grid=(half, nt),
        in_specs=_halo_specs(T, tt, tc, lambda j: j) + _halo_specs(T, tt, tc, lambda j: j + half)
        + [pl.BlockSpec((SUBLANES, tc), lambda j, i: (0, j)), pl.BlockSpec((SUBLANES, tc), lambda j, i: (0, j + half))]
        + _halo_specs(T, tt, tc, lambda j: j),
        out_specs=[pl.BlockSpec((tt, tc), lambda j, i: (i, j)), pl.BlockSpec((tt, tc), lambda j, i: (i, j)),
                   pl.BlockSpec((SUBLANES, tc), lambda j, i: (0, j)), pl.BlockSpec((SUBLANES, tc), lambda j, i: (0, j))],
        out_shape=[jax.ShapeDtypeStruct((T, D_FF), BF16), jax.ShapeDtypeStruct((T, D_FF), BF16),
                   jax.ShapeDtypeStruct((SUBLANES, D_FF), F32), jax.ShapeDtypeStruct((SUBLANES, D_FF), F32)],
        compiler_params=_cp("parallel", "arbitrary"),
    )(u, u, u, u, u, u, conv_w, conv_w, dact, dact, dact)


def head_norm_fwd(x, gain, z=None, *, x_col=0, z_col=0, name, out_dtype=F32):
    T = x.shape[0]
    tt = _tile(T, 1024)
    gated = z is not None

    def body(*refs):
        x_ref, g_ref = refs[0], refs[1]
        o_ref = refs[-1]
        xf = x_ref[...]
        y = xf * lax.rsqrt(jnp.mean(xf * xf, axis=-1, keepdims=True) + EPS) * g_ref[...]
        if gated:
            y = y * _silu(refs[2][...])
        o_ref[...] = y.astype(out_dtype)

    ins = [pl.BlockSpec((tt, HEAD_DIM), lambda h, i: (i, x_col + h)), pl.BlockSpec((1, HEAD_DIM), lambda h, i: (0, 0))]
    args = [x, gain]
    if gated:
        ins.append(pl.BlockSpec((tt, HEAD_DIM), lambda h, i: (i, z_col + h)))
        args.append(z)
    return pl.pallas_call(
        body, name=name, grid=(HEADS, T // tt), in_specs=ins,
        out_specs=pl.BlockSpec((tt, HEAD_DIM), lambda h, i: (i, h)),
        out_shape=jax.ShapeDtypeStruct((T, D_MODEL), out_dtype),
        compiler_params=_cp("parallel", "parallel"),
    )(*args)


def head_norm_bwd(x, gain, dys, z=None, *, x_col=0, z_col=0, name, dx_dtype=F32):
    T = x.shape[0]
    tt = _tile(T, 1024)
    gated = z is not None
    nd = len(dys)

    def body(*refs):
        x_ref, g_ref = refs[0], refs[1]
        xf = x_ref[...]
        r = lax.rsqrt(jnp.mean(xf * xf, axis=-1, keepdims=True) + EPS)
        xh = xf * r
        dy = refs[2][...].astype(F32)
        for d_ref in refs[3:2 + nd]:
            dy = dy + d_ref[...].astype(F32)
        outs = refs[2 + nd + (1 if gated else 0):]
        if gated:
            zf = refs[2 + nd][...]
            dx_ref, dz_ref, dg_ref = outs
            gate, gate_grad = _silu_and_grad(zf)
            dz_ref[...] = _bf(dy * xh * g_ref[...] * gate_grad)
            dn = dy * gate
        else:
            dx_ref, dg_ref = outs
            dn = dy
        dxh = dn * g_ref[...]
        dx_ref[...] = (r * (dxh - xh * jnp.mean(dxh * xh, axis=-1, keepdims=True))).astype(dx_dtype)
        _accumulate(dg_ref, _fold_rows(dn * xh), 1)

    tile = pl.BlockSpec((tt, HEAD_DIM), lambda h, i: (i, h))
    ins = [pl.BlockSpec((tt, HEAD_DIM), lambda h, i: (i, x_col + h)), pl.BlockSpec((1, HEAD_DIM), lambda h, i: (0, 0))] + [tile] * nd
    args = [x, gain] + list(dys)
    outs = [tile]
    shapes = [jax.ShapeDtypeStruct((T, D_MODEL), dx_dtype)]
    if gated:
        ins.append(pl.BlockSpec((tt, HEAD_DIM), lambda h, i: (i, z_col + h)))
        args.append(z)
        outs.append(tile)
        shapes.append(jax.ShapeDtypeStruct((T, D_MODEL), BF16))
    outs.append(pl.BlockSpec((SUBLANES, HEAD_DIM), lambda h, i: (0, h)))
    shapes.append(jax.ShapeDtypeStruct((SUBLANES, D_MODEL), F32))
    return pl.pallas_call(
        body, name=name, grid=(HEADS, T // tt), in_specs=ins, out_specs=outs, out_shape=shapes,
        compiler_params=_cp("parallel", "arbitrary"),
    )(*args)


def gates_fwd(proj, a_log, dt_bias, *, name):
    T = proj.shape[0]
    tt = _tile(T, 1024)

    def body(p_ref, al_ref, dt_ref, o_ref):
        p = p_ref[...]
        lane = lax.broadcasted_iota(jnp.int32, p.shape, 1)
        o_ref[...] = jnp.where(lane < HEADS, -jnp.exp(al_ref[...]) * _softplus(p + dt_ref[...]), _sigmoid(p))

    return pl.pallas_call(
        body, name=name, grid=(T // tt,),
        in_specs=[pl.BlockSpec((tt, LANES), lambda i: (i, AB_BLOCK)), pl.BlockSpec((1, LANES), lambda i: (0, 0)),
                  pl.BlockSpec((1, LANES), lambda i: (0, 0))],
        out_specs=pl.BlockSpec((tt, LANES), lambda i: (i, 0)),
        out_shape=jax.ShapeDtypeStruct((T, LANES), F32),
        compiler_params=_cp("parallel"),
    )(proj, a_log, dt_bias)


def gates_bwd(proj, a_log, dt_bias, dgate, *, name):
    T = proj.shape[0]
    tt = _tile(T, 1024)

    def body(p_ref, al_ref, dt_ref, d_ref, dp_ref, dal_ref, ddt_ref):
        p, d = p_ref[...], d_ref[...]
        lane = lax.broadcasted_iota(jnp.int32, p.shape, 1)
        ea = jnp.exp(al_ref[...])
        pa = p + dt_ref[...]
        da = -d * ea * _sigmoid(pa)
        b = _sigmoid(p)
        dp_ref[...] = _bf(jnp.where(lane < HEADS, da, jnp.where(lane < 2 * HEADS, d * b * (1.0 - b), 0.0)))
        _accumulate(dal_ref, _fold_rows(jnp.where(lane < HEADS, -d * ea * _softplus(pa), 0.0)), 0)
        _accumulate(ddt_ref, _fold_rows(jnp.where(lane < HEADS, da, 0.0)), 0)

    acc = pl.BlockSpec((SUBLANES, LANES), lambda i: (0, 0))
    return pl.pallas_call(
        body, name=name, grid=(T // tt,),
        in_specs=[pl.BlockSpec((tt, LANES), lambda i: (i, AB_BLOCK)), pl.BlockSpec((1, LANES), lambda i: (0, 0)),
                  pl.BlockSpec((1, LANES), lambda i: (0, 0)), pl.BlockSpec((tt, LANES), lambda i: (i, 0))],
        out_specs=[pl.BlockSpec((tt, LANES), lambda i: (i, 0)), acc, acc],
        out_shape=[jax.ShapeDtypeStruct((T, LANES), BF16), jax.ShapeDtypeStruct((SUBLANES, LANES), F32),
                   jax.ShapeDtypeStruct((SUBLANES, LANES), F32)],
        compiler_params=_cp("arbitrary"),
    )(proj, a_log, dt_bias, dgate)


def loss_fwd(y, target, *, name):
    T, D = y.shape
    tt = _tile(T, 512)

    def body(y_ref, t_ref, dy_ref, l_ref):
        d = y_ref[...] - t_ref[...]
        dy_ref[...] = d * (1.0 / D)
        sq = d * d
        lanes = sq[:, 0:LANES]
        for c in range(1, D // LANES):
            lanes = lanes + sq[:, c * LANES:(c + 1) * LANES]
        _accumulate(l_ref, _fold_rows(lanes) * (0.5 / D), 0)

    return pl.pallas_call(
        body, name=name, grid=(T // tt,),
        in_specs=[pl.BlockSpec((tt, D), lambda i: (i, 0)), pl.BlockSpec((tt, D), lambda i: (i, 0))],
        out_specs=[pl.BlockSpec((tt, D), lambda i: (i, 0)), pl.BlockSpec((SUBLANES, LANES), lambda i: (0, 0))],
        out_shape=[jax.ShapeDtypeStruct((T, D), F32), jax.ShapeDtypeStruct((SUBLANES, LANES), F32)],
        compiler_params=_cp("arbitrary"),
    )(y, target)


def _split_bf16(x):
    hi = _bf(x)
    return hi, _bf(x - hi.astype(F32))


def _dot3(a, b, dot=_dot):
    return dot(a[0], b[0]) + dot(a[0], b[1]) + dot(a[1], b[0])


def _each(fn, *lists):
    return [fn(*args) for args in zip(*lists)]


def _unit_lower_inverses(lows):
    c = lows[0].shape[0]
    ii = lax.broadcasted_iota(jnp.int32, (c, c), 0)
    jj = lax.broadcasted_iota(jnp.int32, (c, c), 1)
    eye = jnp.where(ii == jj, 1.0, 0.0)
    invs = _each(lambda low: eye - low, lows)
    powers = _each(_split_bf16, lows)
    for _ in range(int(math.log2(c)) - 1):
        powers = _each(lambda p: _split_bf16(_dot3(p, p)), powers)
        invs = _each(lambda inv, p: inv + _dot3(_split_bf16(inv), p), invs, powers)
    return invs


def _gdn_chunks(heads):
    q, k, v, a_col, a_row, b_col, s0 = (list(t) for t in zip(*heads))
    c = q[0].shape[0]
    ii = lax.broadcasted_iota(jnp.int32, (c, c), 0)
    jj = lax.broadcasted_iota(jnp.int32, (c, c), 1)
    tri, strict = ii >= jj, ii > jj
    g_col = _each(lambda ar: jnp.sum(jnp.where(tri, ar, 0.0), axis=1, keepdims=True), a_row)
    g_row = _each(lambda ac: jnp.sum(jnp.where(ii <= jj, ac, 0.0), axis=0, keepdims=True), a_col)
    gam = _each(lambda gc, gr: jnp.exp(jnp.where(tri, gc - gr, -jnp.inf)), g_col, g_row)
    g_last = _each(lambda ac: jnp.sum(ac, axis=0, keepdims=True), a_col)
    gam_col = _each(jnp.exp, g_col)
    del_col = _each(lambda gl, gc: jnp.exp(gl - gc), g_last, g_col)
    kb = _each(lambda k_, b: k_ * b, k, b_col)
    m = _each(lambda kb_, k_: _dot_nt(_bf(kb_), _bf(k_)), kb, k)
    ks = _each(lambda k_, s: _dot(_bf(k_), _bf(s)), k, s0)
    qk = _each(lambda q_, k_: _dot_nt(_bf(q_), _bf(k_)), q, k)
    inv = _unit_lower_inverses(_each(lambda m_, g: jnp.where(strict, m_ * g, 0.0), m, gam))
    e = _each(lambda v_, gc, ks_: v_ - gc * ks_, v, gam_col, ks)
    inv = _each(_split_bf16, inv)
    vn = _each(lambda inv_, b, e_: _dot3(inv_, _split_bf16(b * e_)), inv, b_col, e)
    p = _each(lambda qk_, g: jnp.where(tri, qk_ * g, 0.0), qk, gam)
    return [dict(tri=tri, strict=strict, ii=ii, jj=jj, gam=gam[g], g_last=g_last[g], gam_col=gam_col[g], del_col=del_col[g],
                 kb=kb[g], m=m[g], inv=inv[g], ks=ks[g], e=e[g], vn=vn[g], qk=qk[g], p=p[g]) for g in range(len(heads))]


GDN_HEADS_PER_STEP = HEADS


def _head_cols(ref, g):
    return ref[:, g * HEAD_DIM:(g + 1) * HEAD_DIM]


def _load_heads(q_ref, k_ref, v_ref, ar_ref, br_ref, states):
    return [(_head_cols(q_ref, g), _head_cols(k_ref, g), _head_cols(v_ref, g), ar_ref[g, 0].T, ar_ref[g, 0], br_ref[g, 0].T, states(g))
            for g in range(GDN_HEADS_PER_STEP)]


def gdn_fwd(qkv, a_row, b_row, *, name, exchange=None):
    T = qkv.shape[0]
    C = GDN_CHUNK
    N = T // C
    G = GDN_HEADS_PER_STEP

    def body(q_ref, k_ref, v_ref, ar_ref, br_ref, o_ref, s_ref, state):
        @pl.when(pl.program_id(1) == 0)
        def _():
            state[...] = jnp.zeros_like(state)
        loaded = _load_heads(q_ref, k_ref, v_ref, ar_ref, br_ref, lambda g: state[g])
        ws = _gdn_chunks(loaded)
        qs = _each(lambda h: _dot(_bf(h[0]), _bf(h[6])), loaded)
        pv = _each(lambda w: _dot(_bf(w["p"]), _bf(w["vn"])), ws)
        kv = _each(lambda h, w: _dot_tn(_bf(w["del_col"] * h[1]), _bf(w["vn"])), loaded, ws)
        for g, w in enumerate(ws):
            s0 = loaded[g][6]
            s_ref[g, 0] = s0
            o_ref[:, g * HEAD_DIM:(g + 1) * HEAD_DIM] = w["gam_col"] * qs[g] + pv[g]
            state[g] = jnp.exp(w["g_last"]) * s0 + kv[g]

    per = HEADS // G
    blk = lambda off: pl.BlockSpec((C, G * HEAD_DIM), lambda h, n: (n, off * per + h))
    row = pl.BlockSpec((G, 1, 1, C), lambda h, n: (h, n, 0, 0))
    (o, states), brought = hosted_call(
        body, name=name, grid=(per, N),
        in_specs=[blk(0), blk(1), blk(2), row, row],
        out_specs=[blk(0), pl.BlockSpec((G, 1, HEAD_DIM, HEAD_DIM), lambda h, n: (h, n, 0, 0))],
        out_shape=[jax.ShapeDtypeStruct((T, D_MODEL), F32), jax.ShapeDtypeStruct((HEADS, N, HEAD_DIM, HEAD_DIM), F32)],
        scratch_shapes=[pltpu.VMEM((G, HEAD_DIM, HEAD_DIM), F32)],
        args=(qkv, qkv, qkv, a_row, b_row), exchange=exchange)
    return o, states, brought


def gdn_bwd(qkv, a_row, b_row, states, do, *, name, exchange=None):
    T = qkv.shape[0]
    C = GDN_CHUNK
    N = T // C
    G = GDN_HEADS_PER_STEP
    per = HEADS // G
    rev = lambda n: N - 1 - n
    row = pl.BlockSpec((G, 1, 1, C), lambda h, n: (h, rev(n), 0, 0))

    def body(q_ref, k_ref, v_ref, ar_ref, br_ref, s_ref, do_ref, dqkv_ref, da_ref, db_ref, dstate):
        @pl.when(pl.program_id(1) == 0)
        def _():
            dstate[...] = jnp.zeros_like(dstate)
        loaded = _load_heads(q_ref, k_ref, v_ref, ar_ref, br_ref, lambda g: s_ref[g, 0])
        hs = _gdn_chunks(loaded)
        for g, d in enumerate(hs):
            q, k, _, _, _, b, s0 = loaded[g]
            d.update(q=q, k=k, b=b, s0=s0, ds1=dstate[g], dout=_head_cols(do_ref, g))
        rows = lambda t: jnp.sum(t, axis=1, keepdims=True)
        ii_col = lax.broadcasted_iota(jnp.int32, (C, 1), 0)

        def stage(**fns):
            for key, fn in fns.items():
                for d in hs:
                    d[key] = fn(d)

        stage(s0b=lambda d: _bf(d["s0"]), ds1b=lambda d: _bf(d["ds1"]), doutb=lambda d: _bf(d["dout"]),
              kbf=lambda d: _bf(d["k"]), qbf=lambda d: _bf(d["q"]), vnb=lambda d: _bf(d["vn"]))
        stage(dvn=lambda d: _dot_tn(_bf(d["p"]), d["doutb"]) + _dot(_bf(d["del_col"] * d["k"]), d["ds1b"]),
              dqk=lambda d: jnp.where(d["tri"], _dot_nt(d["doutb"], d["vnb"]), 0.0) * d["gam"],
              qs=lambda d: _dot(d["qbf"], d["s0b"]),
              dkd=lambda d: _dot_nt(d["vnb"], d["ds1b"]))
        stage(dr=lambda d: _dot3(d["inv"], _split_bf16(d["dvn"]), _dot_tn),
              dq=lambda d: d["gam_col"] * _dot_nt(d["doutb"], d["s0b"]) + _dot(_bf(d["dqk"]), d["kbf"]),
              dk=lambda d: _dot_tn(_bf(d["dqk"]), d["qbf"]) + d["del_col"] * d["dkd"],
              ddel=lambda d: d["del_col"] * rows(d["dkd"] * d["k"]))
        stage(dg=lambda d: d["gam_col"] * rows(d["dout"] * d["qs"]) - d["ddel"],
              dg_last=lambda d: jnp.sum(d["ddel"], axis=0, keepdims=True)
              + jnp.exp(d["g_last"]) * jnp.sum(rows(d["ds1"] * d["s0"]), axis=0, keepdims=True),
              dm=lambda d: jnp.where(d["strict"], -_dot_nt(_bf(d["dr"]), d["vnb"]), 0.0) * d["gam"],
              de=lambda d: d["b"] * d["dr"])
        stage(dkb=lambda d: _dot(_bf(d["dm"]), d["kbf"]),
              dks=lambda d: -d["gam_col"] * d["de"])
        stage(dk=lambda d: d["dk"] + _dot_tn(_bf(d["dm"]), _bf(d["kb"])) + _dot_nt(_bf(d["dks"]), d["s0b"]) + d["b"] * d["dkb"],
              dbeta=lambda d: rows(d["dr"] * d["e"]) + rows(d["dkb"] * d["k"]),
              ds0=lambda d: jnp.exp(d["g_last"]) * d["ds1"] + _dot_tn(_bf(d["gam_col"] * d["q"]), d["doutb"])
              + _dot_tn(d["kbf"], _bf(d["dks"])),
              wg=lambda d: d["dqk"] * d["qk"] + d["dm"] * d["m"])
        stage(dg=lambda d: d["dg"] - d["gam_col"] * rows(d["de"] * d["ks"]) + rows(d["wg"])
              - jnp.sum(d["wg"], axis=0, keepdims=True).T + jnp.where(ii_col == C - 1, d["dg_last"], 0.0))
        stage(da=lambda d: jnp.sum(jnp.where(d["ii"] >= d["jj"], d["dg"], 0.0), axis=0, keepdims=True),
              db=lambda d: d["dbeta"].T)
        for g, d in enumerate(hs):
            dstate[g] = d["ds0"]
            da_ref[g, 0] = d["da"]
            db_ref[g, 0] = d["db"]
            for part, key in enumerate(("dq", "dk", "de")):
                start = part * D_MODEL + g * HEAD_DIM
                dqkv_ref[:, start:start + HEAD_DIM] = d[key]

    assert per == 1
    blk = lambda off: pl.BlockSpec((C, G * HEAD_DIM), lambda h, n: (rev(n), off * per + h))
    (dqkv, da, db), brought = hosted_call(
        body, name=name, grid=(per, N),
        in_specs=[blk(0), blk(1), blk(2), row, row,
                  pl.BlockSpec((G, 1, HEAD_DIM, HEAD_DIM), lambda h, n: (h, rev(n), 0, 0)), blk(0)],
        out_specs=[pl.BlockSpec((C, 3 * D_MODEL), lambda h, n: (rev(n), 0)), row, row],
        out_shape=[jax.ShapeDtypeStruct((T, 3 * D_MODEL), F32)] + [jax.ShapeDtypeStruct((HEADS, N, 1, C), F32)] * 2,
        scratch_shapes=[pltpu.VMEM((G, HEAD_DIM, HEAD_DIM), F32)],
        args=(qkv, qkv, qkv, a_row, b_row, states, do), exchange=exchange)
    return dqkv, da, db, brought


SB_BLOCK = 128


SB_QBLOCKS = 4


def _sb_rows(j, blk):
    return pl.ds(pl.multiple_of(j * blk, blk), blk)


def _sb_tile(qb, k_ref, i, j, blk, live):
    z = _dot_nt(qb, _bf(k_ref[_sb_rows(j, blk), :]))
    t_idx = i * blk + lax.broadcasted_iota(jnp.int32, (blk, blk), 0)
    s_idx = j * blk + lax.broadcasted_iota(jnp.int32, (blk, blk), 1)
    mask = jnp.logical_and(s_idx < t_idx, live)
    lf = jnp.where(mask, -_softplus(z), 0.0)
    return z, mask, lf


SB_DEAD = 105.0


def sb_fwd(q, k, v, *, k_col=0, v_col=0, name):
    T = q.shape[0]
    blk = _tile(T, SB_BLOCK)
    P = SB_QBLOCKS
    scale = HEAD_DIM ** -0.5

    def body(q_ref, k_ref, v_ref, o_ref, l_ref, n_ref):
        iq = [P * pl.program_id(1) + p for p in range(P)]
        qb = [_bf(q_ref[p * blk:(p + 1) * blk, :] * scale) for p in range(P)]
        r_idx = lax.broadcasted_iota(jnp.int32, (blk, blk), 0)
        c_idx = lax.broadcasted_iota(jnp.int32, (blk, blk), 1)
        later = _bf(jnp.where(r_idx > c_idx, 1.0, 0.0))

        def live_blocks(jj, runs):
            return [jnp.logical_and(jj <= i, jnp.max(run) > -SB_DEAD) for i, run in zip(iq, runs)]

        def alive(carry):
            jj, _, runs, _ = carry
            some = False
            for f in live_blocks(jj, runs):
                some = jnp.logical_or(some, f)
            return some

        def step(carry):
            jj, accs, runs, visited = carry
            live = live_blocks(jj, runs)
            js = [jnp.maximum(i - jj, 0) for i in iq]
            tiles = _each(lambda q_, i, j, f: _sb_tile(q_, k_ref, i, j, blk, f), qb, iq, js, live)
            parts = _each(lambda t: _split_bf16(t[2]), tiles)
            after = _each(lambda run, s: run + _dot(s[0], later) + _dot(s[1], later), runs, parts)
            a = _each(lambda t, af: jnp.where(t[1], jnp.exp(t[0] + t[2] + af), 0.0), tiles, after)
            vb = _each(lambda j: _bf(v_ref[_sb_rows(j, blk), :]), js)
            accs = _each(lambda acc, a_, v_: acc + _dot(_bf(a_), v_), accs, a, vb)
            runs = _each(lambda run, t: run + jnp.sum(t[2], axis=1, keepdims=True), runs, tiles)
            visited = _each(lambda n, f: n + f.astype(jnp.int32), visited, live)
            return jj + 1, accs, runs, visited

        start = (jnp.int32(0), [jnp.zeros((blk, HEAD_DIM), F32)] * P, [jnp.zeros((blk, 1), F32)] * P, [jnp.int32(0)] * P)
        _, accs, runs, visited = lax.while_loop(alive, step, start)
        for p in range(P):
            o_ref[p * blk:(p + 1) * blk, :] = accs[p]
            l_ref[0, p] = runs[p].T
            n_ref[0, p] = jnp.full((SUBLANES, LANES), visited[p].astype(F32))

    return pl.pallas_call(
        body, name=name, grid=(HEADS, T // (P * blk)),
        in_specs=[pl.BlockSpec((P * blk, HEAD_DIM), lambda h, i: (i, h)), pl.BlockSpec((T, HEAD_DIM), lambda h, i: (0, k_col + h)),
                  pl.BlockSpec((T, HEAD_DIM), lambda h, i: (0, v_col + h))],
        out_specs=[pl.BlockSpec((P * blk, HEAD_DIM), lambda h, i: (i, h)), pl.BlockSpec((1, P, 1, blk), lambda h, i: (h, i, 0, 0)),
                   pl.BlockSpec((1, P, SUBLANES, LANES), lambda h, i: (h, i, 0, 0))],
        out_shape=[jax.ShapeDtypeStruct((T, D_MODEL), F32), jax.ShapeDtypeStruct((HEADS, T // blk, 1, blk), F32),
                   jax.ShapeDtypeStruct((HEADS, T // blk, SUBLANES, LANES), F32)],
        compiler_params=_cp("parallel", "arbitrary"),
    )(q, k, v)


def sb_bwd(q, k, v, ltot, visited, do, *, k_col=0, v_col=0, name, exchange=None):
    T = q.shape[0]
    blk = _tile(T, SB_BLOCK)
    P = SB_QBLOCKS
    scale = HEAD_DIM ** -0.5

    def body(q_ref, k_ref, v_ref, l_ref, n_ref, do_ref, dq_ref, dk_ref, dv_ref):
        iq = [P * pl.program_id(1) + p for p in range(P)]
        count = [jnp.max(n_ref[0, p]).astype(jnp.int32) for p in range(P)]
        first = [i + 1 - n for i, n in zip(iq, count)]
        trips = count[0]
        for n in count[1:]:
            trips = jnp.maximum(trips, n)

        @pl.when(pl.program_id(1) == 0)
        def _():
            dk_ref[...] = jnp.zeros_like(dk_ref)
            dv_ref[...] = jnp.zeros_like(dv_ref)

        qb = [_bf(q_ref[p * blk:(p + 1) * blk, :] * scale) for p in range(P)]
        dob = [_bf(do_ref[p * blk:(p + 1) * blk, :]) for p in range(P)]
        ltot_ = [l_ref[0, p].T for p in range(P)]
        r_idx = lax.broadcasted_iota(jnp.int32, (blk, blk), 0)
        c_idx = lax.broadcasted_iota(jnp.int32, (blk, blk), 1)
        upto = _bf(jnp.where(r_idx <= c_idx, 1.0, 0.0))
        before = _bf(jnp.where(r_idx < c_idx, 1.0, 0.0))

        def step(t, carry):
            dqs, lpre, cpre = carry
            live = [f + t <= i for f, i in zip(first, iq)]
            js = [jnp.minimum(f + t, i) for f, i in zip(first, iq)]
            tiles = _each(lambda q_, i, j, f: _sb_tile(q_, k_ref, i, j, blk, f), qb, iq, js, live)
            parts = _each(lambda tl: _split_bf16(tl[2]), tiles)
            after = _each(lambda lt, lp, s: lt - (lp + _dot(s[0], upto) + _dot(s[1], upto)), ltot_, lpre, parts)
            ls = _each(lambda tl: tl[0] + tl[2], tiles)
            a = _each(lambda tl, ls_, af: jnp.where(tl[1], jnp.exp(ls_ + af), 0.0), tiles, ls, after)
            vb = _each(lambda j: _bf(v_ref[_sb_rows(j, blk), :]), js)
            p = _each(lambda a_, do_, v_: a_ * _dot_nt(do_, v_), a, dob, vb)
            pparts = _each(_split_bf16, p)
            left = _each(lambda cp, s: cp + _dot(s[0], before) + _dot(s[1], before), cpre, pparts)
            dzb = _each(lambda tl, p_, lf_, ls_: _bf(jnp.where(tl[1], p_ * jnp.exp(tl[2]) - lf_ * jnp.exp(ls_), 0.0)),
                        tiles, p, left, ls)
            dks = _each(lambda dz, q_: _dot_tn(dz, q_), dzb, qb)
            dvs = _each(lambda a_, do_: _dot_tn(_bf(a_), do_), a, dob)
            dqs = _each(lambda dq, dz, j: dq + _dot(dz, _bf(k_ref[_sb_rows(j, blk), :])), dqs, dzb, js)
            for j, dk, dv in zip(js, dks, dvs):
                dk_ref[_sb_rows(j, blk), :] += dk
                dv_ref[_sb_rows(j, blk), :] += dv
            lpre = _each(lambda lp, tl: lp + jnp.sum(tl[2], axis=1, keepdims=True), lpre, tiles)
            cpre = _each(lambda cp, p_: cp + jnp.sum(p_, axis=1, keepdims=True), cpre, p)
            return dqs, lpre, cpre

        zero = [jnp.zeros((blk, 1), F32)] * P
        dqs, _, _ = lax.fori_loop(0, trips, step, ([jnp.zeros((blk, HEAD_DIM), F32)] * P, zero, zero))
        for p in range(P):
            dq_ref[p * blk:(p + 1) * blk, :] = dqs[p] * scale

    full = lambda off: pl.BlockSpec((T, HEAD_DIM), lambda h, i: (0, off + h))
    tile = pl.BlockSpec((P * blk, HEAD_DIM), lambda h, i: (i, h))
    (dq, dk, dv), brought = hosted_call(
        body, name=name, grid=(HEADS, T // (P * blk)),
        in_specs=[tile, full(k_col), full(v_col), pl.BlockSpec((1, P, 1, blk), lambda h, i: (h, i, 0, 0)),
                  pl.BlockSpec((1, P, SUBLANES, LANES), lambda h, i: (h, i, 0, 0)), tile],
        out_specs=[tile, full(0), full(0)],
        out_shape=[jax.ShapeDtypeStruct((T, D_MODEL), F32)] * 3,
        scratch_shapes=[], args=(q, k, v, ltot, visited, do), exchange=exchange)
    return dq, dk, dv, brought


def sum_slots(slots, *, name):
    n, R, C = slots.shape
    tr = _tile(R, 256)

    def body(s_ref, o_ref):
        acc = s_ref[0].astype(F32)
        for k in range(1, n):
            acc = acc + s_ref[k].astype(F32)
        o_ref[...] = acc

    return pl.pallas_call(
        body, name=name, grid=(R // tr,),
        in_specs=[pl.BlockSpec((n, tr, C), lambda i: (0, i, 0))],
        out_specs=pl.BlockSpec((tr, C), lambda i: (i, 0)),
        out_shape=jax.ShapeDtypeStruct((R, C), F32),
        compiler_params=_cp("parallel"),
    )(slots)


def adamw(w, g_parts, m, v, *, name):
    R, C = w.shape
    tr = _tile(R, 256)
    n = len(g_parts)

    def body(*refs):
        w_ref, m_ref, v_ref = refs[0], refs[1 + n], refs[2 + n]
        g_ref, d_ref, nm_ref, nv_ref = refs[3 + n:]
        g = refs[1][...]
        for r in refs[2:1 + n]:
            g = g + r[...]
        m2 = ADAM_B1 * m_ref[...] + (1.0 - ADAM_B1) * g
        v2 = ADAM_B2 * v_ref[...] + (1.0 - ADAM_B2) * (g * g)
        m_hat = m2 / (1.0 - ADAM_B1 ** ADAM_STEP)
        v_hat = v2 / (1.0 - ADAM_B2 ** ADAM_STEP)
        g_ref[...] = g
        d_ref[...] = -ADAM_LR * (m_hat / (jnp.sqrt(v_hat) + ADAM_EPS) + ADAM_WD * w_ref[...])
        nm_ref[...] = m2
        nv_ref[...] = v2

    spec = pl.BlockSpec((tr, C), lambda i: (i, 0))
    return pl.pallas_call(
        body, name=name, grid=(R // tr,),
        in_specs=[spec] * (3 + n), out_specs=[spec] * 4,
        out_shape=[jax.ShapeDtypeStruct((R, C), F32)] * 4,
        compiler_params=_cp("parallel"),
    )(w, *g_parts, m, v)


CHIP_FLIPS = ((0, 1), (1, 0), (1, 1))


def _place():
    return lax.axis_index("x"), lax.axis_index("y"), lax.axis_index("c")


def _flip(v, f):
    return 1 - v if f else v


class ChipExchange:
    def __init__(self, arrays, scatter):
        self.arrays, self.scatter, self.n = list(arrays), scatter, len(arrays)
        lead = () if scatter else (N_CHIPS,)
        self.out_shape = [jax.ShapeDtypeStruct(lead + a.shape, a.dtype) for a in arrays]
        self.scratch = [pltpu.SemaphoreType.DMA((self.n, len(CHIP_FLIPS))), pltpu.SemaphoreType.DMA((self.n, len(CHIP_FLIPS))),
                        pltpu.SemaphoreType.DMA((self.n,))]

    def _copies(self, ins, outs, sems):
        send_sems, recv_sems, local_sems = sems
        x, y, c = _place()
        me = 2 * x + y
        local, sent, landing = [], [], []
        for k in range(self.n):
            local.append(pltpu.make_async_copy(ins[k].at[me] if self.scatter else ins[k], outs[k].at[me], local_sems.at[k]))
            for p, (fx, fy) in enumerate(CHIP_FLIPS):
                px, py = _flip(x, fx), _flip(y, fy)
                src = ins[k].at[2 * px + py] if self.scatter else ins[k]
                for dst, group in ((me, sent), (2 * px + py, landing)):
                    group.append(pltpu.make_async_remote_copy(src_ref=src, dst_ref=outs[k].at[dst], send_sem=send_sems.at[k, p],
                                                              recv_sem=recv_sems.at[k, p], device_id=(px, py, c), device_id_type=MESH))
        return local, sent, landing

    def start(self, ins, outs, sems):
        local, sent, _ = self._copies(ins, outs, sems)
        for cp in local + sent:
            cp.start()

    def finish(self, ins, outs, sems):
        local, _, landing = self._copies(ins, outs, sems)
        for cp in landing:
            cp.wait_send()
            cp.wait_recv()
        for cp in local:
            cp.wait()


def chip_exchange(arrays, *, scatter, name):
    ex = ChipExchange(arrays, scatter)
    n = ex.n

    def body(*refs):
        ins, outs, sems = refs[:n], refs[n:2 * n], refs[2 * n:]
        ex.start(ins, outs, sems)
        ex.finish(ins, outs, sems)

    return pl.pallas_call(body, name=name, in_specs=[ANY] * n, out_specs=[ANY] * n, out_shape=ex.out_shape,
                          scratch_shapes=ex.scratch)(*arrays)


def hosted_call(body, *, name, grid, in_specs, out_specs, out_shape, scratch_shapes, args, exchange=None):
    in_specs, out_specs, out_shape = list(in_specs), list(out_specs), list(out_shape)
    params = _cp(*["arbitrary"] * len(grid))
    if exchange is None:
        outs = pl.pallas_call(body, name=name, grid=grid, in_specs=in_specs, out_specs=out_specs, out_shape=out_shape,
                              scratch_shapes=list(scratch_shapes), compiler_params=params)(*args)
        return list(outs), []
    n, n_in, n_out, n_scr = exchange.n, len(in_specs), len(out_specs), len(scratch_shapes)

    def both(*refs):
        ins, t_ins = refs[:n_in], refs[n_in:n_in + n]
        outs, t_outs = refs[n_in + n:n_in + n + n_out], refs[n_in + n + n_out:n_in + 2 * n + n_out]
        scratch, sems = refs[n_in + 2 * n + n_out:n_in + 2 * n + n_out + n_scr], refs[n_in + 2 * n + n_out + n_scr:]
        first, last = True, True
        for axis, size in enumerate(grid):
            first = jnp.logical_and(first, pl.program_id(axis) == 0)
            last = jnp.logical_and(last, pl.program_id(axis) == size - 1)

        @pl.when(first)
        def _():
            exchange.start(t_ins, t_outs, sems)

        body(*ins, *outs, *scratch)

        @pl.when(last)
        def _():
            exchange.finish(t_ins, t_outs, sems)

    outs = pl.pallas_call(both, name=name, grid=grid, in_specs=in_specs + [ANY] * n, out_specs=out_specs + [ANY] * n,
                          out_shape=out_shape + exchange.out_shape, scratch_shapes=list(scratch_shapes) + exchange.scratch,
                          compiler_params=params)(*args, *exchange.arrays)
    return list(outs[:n_out]), list(outs[n_out:])


def sibling_swap(arrays, *, name):
    n = len(arrays)

    def body(*refs):
        ins, outs = refs[:n], refs[n:2 * n]
        send_sems, recv_sems = refs[2 * n:]
        x, y, c = _place()
        copies = [pltpu.make_async_remote_copy(src_ref=ins[k], dst_ref=outs[k], send_sem=send_sems.at[k], recv_sem=recv_sems.at[k],
                                               device_id=(x, y, 1 - c), device_id_type=MESH) for k in range(n)]
        for cp in copies:
            cp.start()
        for cp in copies:
            cp.wait_send()
            cp.wait_recv()

    return pl.pallas_call(
        body, name=name, in_specs=[ANY] * n, out_specs=[ANY] * n,
        out_shape=[jax.ShapeDtypeStruct(a.shape, a.dtype) for a in arrays],
        scratch_shapes=[pltpu.SemaphoreType.DMA((n,)), pltpu.SemaphoreType.DMA((n,))],
    )(*arrays)


DEVICE_FLIPS = tuple((fx, fy, fc) for fx in (0, 1) for fy in (0, 1) for fc in (0, 1) if fx or fy or fc)


def all_gather_devices(a, *, name):
    def body(a_ref, o_ref, send_sems, recv_sems, local_sem):
        x, y, c = _place()
        me = 4 * x + 2 * y + c
        local = pltpu.make_async_copy(a_ref, o_ref.at[me], local_sem)
        local.start()
        for p, (fx, fy, fc) in enumerate(DEVICE_FLIPS):
            peer = (_flip(x, fx), _flip(y, fy), _flip(c, fc))
            pltpu.make_async_remote_copy(src_ref=a_ref, dst_ref=o_ref.at[me], send_sem=send_sems.at[p], recv_sem=recv_sems.at[p],
                                         device_id=peer, device_id_type=MESH).start()
        for p, (fx, fy, fc) in enumerate(DEVICE_FLIPS):
            px, py, pc = _flip(x, fx), _flip(y, fy), _flip(c, fc)
            landing = pltpu.make_async_remote_copy(src_ref=a_ref, dst_ref=o_ref.at[4 * px + 2 * py + pc], send_sem=send_sems.at[p],
                                                   recv_sem=recv_sems.at[p], device_id=(px, py, pc), device_id_type=MESH)
            landing.wait_send()
            landing.wait_recv()
        local.wait()

    return pl.pallas_call(
        body, name=name, in_specs=[ANY], out_specs=ANY,
        out_shape=jax.ShapeDtypeStruct((N_DEV,) + a.shape, a.dtype),
        scratch_shapes=[pltpu.SemaphoreType.DMA((len(DEVICE_FLIPS),)), pltpu.SemaphoreType.DMA((len(DEVICE_FLIPS),)),
                        pltpu.SemaphoreType.DMA(())],
    )(a)


def _row(v):
    return v.reshape(1, -1)


def _pad_rows(w):
    return jnp.pad(w, ((0, SUBLANES - w.shape[0]), (0, 0)))


def _pad_lanes(v):
    return jnp.pad(v.reshape(1, -1), ((0, 0), (0, LANES - v.shape[-1])))


def _head_layouts(gates):
    T = gates.shape[0]
    rows = lambda cols: cols.T.reshape(HEADS, T // GDN_CHUNK, 1, GDN_CHUNK)
    return rows(gates[:, :HEADS]), rows(gates[:, HEADS:2 * HEADS])


def _ffn_fwd(x, W, l):
    u, h = norm_matmul(x, _row(W["ffn_norm"][l]), W["ffn_w_up"][l], name=f"ffn{l}_up")
    conv = _pad_rows(W["ffn_conv"][l])
    act = ffn_act_fwd(u, conv, name=f"ffn{l}_act")
    y = matmul_residual(act, W["ffn_w_down"][l], x, name=f"ffn{l}_down")
    return y, (x, h, u, conv, act)


def _ffn_bwd(dx, saved, W, l, G):
    x, h, u, conv, act = saved
    dact = matmul_nt(dx, W["ffn_w_down"][l], name=f"ffn{l}_down_dx", out_dtype=BF16)
    G["ffn_w_down"][l] = matmul_tn(act, dx, name=f"ffn{l}_down_dw")
    dug, duu, dwg, dwu = ffn_act_bwd(u, conv, dact, name=f"ffn{l}_act_bwd")
    G["ffn_conv"][l] = jnp.concatenate([dwg, dwu], axis=1)[:FFN_CONV]
    G["ffn_w_up"][l] = jnp.concatenate([matmul_tn(h, dug, name=f"ffn{l}_gate_dw"), matmul_tn(h, duu, name=f"ffn{l}_up_dw")], axis=1)
    dx, dgain = matmul_nt_normbwd([(dug, 0), (duu, 1)], W["ffn_w_up"][l], x, _row(W["ffn_norm"][l]), dx, name=f"ffn{l}_up_dx")
    G["ffn_norm"][l] = dgain.sum(0)
    return dx


class NoTraffic:
    def fetch(self, host):
        return None

    def arrived(self, host, brought, W):
        pass

    def flush(self, G):
        return None

    def landed(self, brought):
        pass


def _gdn_fwd(x, W, l, plan):
    proj, h = norm_matmul(x, _row(W["a_norm"][l]), W["a_w_in"][l], name=f"gdn{l}_in")
    conv = _pad_rows(W["a_conv"][l])
    a_log, dt_bias = _pad_lanes(W["a_log"][l]), _pad_lanes(W["a_dt_bias"][l])
    qkv, brought = gdn_conv_fwd(proj, conv, name=f"gdn{l}_conv", exchange=plan.fetch(f"gdn{l}_conv"))
    plan.arrived(f"gdn{l}_conv", brought, W)
    heads = _head_layouts(gates_fwd(proj, a_log, dt_bias, name=f"gdn{l}_gates"))
    o, states, brought = gdn_fwd(qkv, *heads, name=f"gdn{l}_rule", exchange=plan.fetch(f"gdn{l}_rule"))
    plan.arrived(f"gdn{l}_rule", brought, W)
    gain = _row(W["a_out_norm"][l])
    on = head_norm_fwd(o, gain, proj, z_col=Z_BLOCK, name=f"gdn{l}_outnorm", out_dtype=BF16)
    y = matmul_residual(on, W["a_w_out"][l], x, name=f"gdn{l}_out")
    return y, (x, h, proj, conv, a_log, dt_bias, qkv, heads, states, o, gain, on)


def _gdn_bwd(dx, saved, W, l, G, plan):
    x, h, proj, conv, a_log, dt_bias, qkv, heads, states, o, gain, on = saved
    T = x.shape[0]
    don = matmul_nt(dx, W["a_w_out"][l], name=f"gdn{l}_out_dx")
    G["a_w_out"][l] = matmul_tn(on, dx, name=f"gdn{l}_out_dw")
    do, dz, dgain = head_norm_bwd(o, gain, [don], proj, z_col=Z_BLOCK, name=f"gdn{l}_outnorm_bwd")
    G["a_out_norm"][l] = dgain.reshape(SUBLANES, HEADS, HEAD_DIM).sum((0, 1))
    dqkv, da, db, brought = gdn_bwd(qkv, *heads, states, do, name=f"gdn{l}_rule_bwd", exchange=plan.flush(G))
    plan.landed(brought)
    dqkv, dconv = gdn_conv_bwd(proj, conv, dqkv, name=f"gdn{l}_conv_bwd")
    G["a_conv"][l] = dconv[:GDN_CONV]
    dgate = jnp.concatenate([da.reshape(HEADS, T).T, db.reshape(HEADS, T).T, jnp.zeros((T, LANES - 2 * HEADS), F32)], axis=1)
    dab, dal, ddt = gates_bwd(proj, a_log, dt_bias, dgate, name=f"gdn{l}_gates_bwd")
    G["a_log"][l] = dal.sum(0)[:HEADS]
    G["a_dt_bias"][l] = ddt.sum(0)[:HEADS]
    parts = [(dqkv, 0), (dz, Z_BLOCK * LANES // D_MODEL), (dab, AB_BLOCK)]
    G["a_w_in"][l] = jnp.concatenate([matmul_tn(h, d, name=f"gdn{l}_in_dw{i}") for i, (d, _) in enumerate(parts)], axis=1)
    dx, dgain = matmul_nt_normbwd(parts, W["a_w_in"][l], x, _row(W["a_norm"][l]), dx, name=f"gdn{l}_in_dx")
    G["a_norm"][l] = dgain.sum(0)
    return dx


def _sb_fwd(x, kn, kv, W, j):
    qp, h = norm_matmul(x, _row(W["b_norm"][j]), W["b_w_q"][j], name=f"sb{j}_q")
    gain = _row(W["q_norm"][j])
    q = head_norm_fwd(qp, gain, name=f"sb{j}_qnorm")
    o, ltot, visited = sb_fwd(q, kn, kv, v_col=HEADS, name=f"sb{j}_attn")
    y = matmul_residual(o, W["b_w_out"][j], x, name=f"sb{j}_out")
    return y, (x, h, qp, gain, q, o, ltot, visited)


def _sb_bwd(dx, saved, kn, kv, W, j, G, plan):
    x, h, qp, gain, q, o, ltot, visited = saved
    do = matmul_nt(dx, W["b_w_out"][j], name=f"sb{j}_out_dx")
    G["b_w_out"][j] = matmul_tn(o, dx, name=f"sb{j}_out_dw")
    dq, dk, dv, brought = sb_bwd(q, kn, kv, ltot, visited, do, v_col=HEADS, name=f"sb{j}_attn_bwd", exchange=plan.flush(G))
    plan.landed(brought)
    dqp, dgain = head_norm_bwd(qp, gain, [dq], name=f"sb{j}_qnorm_bwd", dx_dtype=BF16)
    G["q_norm"][j] = dgain.reshape(SUBLANES, HEADS, HEAD_DIM).sum((0, 1))
    G["b_w_q"][j] = matmul_tn(h, dqp, name=f"sb{j}_q_dw")
    dx, dgain = matmul_nt_normbwd([(dqp, 0)], W["b_w_q"][j], x, _row(W["b_norm"][j]), dx, name=f"sb{j}_q_dx")
    G["b_norm"][j] = dgain.sum(0)
    return dx, dk, dv


def local_step(x, target, W, plan=None):
    plan = plan or NoTraffic()
    G = {k: [None] * (N_A if k.startswith("a_") else N_B if k in ("b_norm", "b_w_q", "q_norm", "b_w_out") else DEPTH)
         for k in ("a_norm", "a_w_in", "a_conv", "a_log", "a_dt_bias", "a_out_norm", "a_w_out", "b_norm", "b_w_q", "q_norm",
                   "b_w_out", "ffn_norm", "ffn_w_up", "ffn_conv", "ffn_w_down")}
    G["w_kv"] = [None]
    tape = []
    for l in range(N_A):
        x, s_mix = _gdn_fwd(x, W, l, plan)
        x, s_ffn = _ffn_fwd(x, W, l)
        tape.append((s_mix, s_ffn))
    x_kv = x
    kv, h_kv = norm_matmul(x, _row(W["kv_norm"]), W["w_kv"], name="kv_proj")
    k_gain = _row(W["k_norm"])
    kn = head_norm_fwd(kv, k_gain, name="k_norm")
    for j in range(N_B):
        x, s_mix = _sb_fwd(x, kn, kv, W, j)
        x, s_ffn = _ffn_fwd(x, W, N_A + j)
        tape.append((s_mix, s_ffn))
    dx, loss = loss_fwd(x, target, name="loss")

    dks, dvs = [], []
    for j in reversed(range(N_B)):
        s_mix, s_ffn = tape[N_A + j]
        dx = _ffn_bwd(dx, s_ffn, W, N_A + j, G)
        dx, dk, dv = _sb_bwd(dx, s_mix, kn, kv, W, j, G, plan)
        dks.append(dk)
        dvs.append(dv)
    dkp, dgain = head_norm_bwd(kv, k_gain, dks, name="k_norm_bwd", dx_dtype=BF16)
    G["k_norm"] = dgain.reshape(SUBLANES, HEADS, HEAD_DIM).sum((0, 1))
    dv = dvs[0] + dvs[1]
    G["w_kv"][0] = jnp.concatenate([matmul_tn(h_kv, dkp, name="k_proj_dw"), matmul_tn(h_kv, dv, name="v_proj_dw")], axis=1)
    dx, dgain = matmul_nt_normbwd([(dkp, 0), (dv, 1)], W["w_kv"], x_kv, _row(W["kv_norm"]), dx, name="kv_proj_dx")
    G["kv_norm"] = dgain.sum(0)
    for l in reversed(range(N_A)):
        s_mix, s_ffn = tape[l]
        dx = _ffn_bwd(dx, s_ffn, W, l, G)
        dx = _gdn_bwd(dx, s_mix, W, l, G, plan)
    return loss, dx, G


MATRICES = {"a_w_in": 2, "a_w_out": 1, "w_kv": 1, "b_w_q": 1, "b_w_out": 1, "ffn_w_up": 2, "ffn_w_down": 1}
SMALL_SHARDED = {"a_norm": 1, "a_conv": 2, "ffn_conv": 2}
SMALL_REPLICATED = ("a_log", "a_dt_bias", "a_out_norm", "kv_norm", "k_norm", "b_norm", "q_norm", "ffn_norm")
WEIGHT_ORDER = ("a_norm", "a_w_in", "a_conv", "a_log", "a_dt_bias", "a_out_norm", "a_w_out", "kv_norm", "w_kv", "k_norm",
                "b_norm", "b_w_q", "q_norm", "b_w_out", "ffn_norm", "ffn_w_up", "ffn_conv", "ffn_w_down")
SMALL_ORDER = tuple(n for n in WEIGHT_ORDER if n not in MATRICES)
PACK_QUANTUM = SUBLANES * LANES


def _unshard(g, axis):
    g = jnp.moveaxis(g, 0, axis)
    return g.reshape(g.shape[:axis] + (g.shape[axis] * g.shape[axis + 1],) + g.shape[axis + 2:])


def _shards(full, axis):
    n = full.shape[axis] // N_CHIPS
    return jnp.moveaxis(full.reshape(full.shape[:axis] + (N_CHIPS, n) + full.shape[axis + 1:]), axis, 0)


def _pack(arrays):
    parts = []
    for a in arrays:
        flat = a.reshape(-1)
        parts.append(jnp.pad(flat, (0, -flat.shape[0] % PACK_QUANTUM)).reshape(-1, LANES))
    return jnp.concatenate(parts, axis=0)


def _unpack(buf, shapes):
    out, row = [], 0
    for s in shapes:
        size = math.prod(s)
        rows = -(-size // PACK_QUANTUM) * SUBLANES
        out.append(buf[row:row + rows].reshape(-1)[:size].reshape(s))
        row += rows
    return out


def _stack(per_layer):
    return jnp.stack(per_layer) if isinstance(per_layer, list) else per_layer


FETCH_BESIDE = {
    "gdn0_conv": (("ffn_w_up", 0, 2),),
    "gdn0_rule": (("ffn_w_down", 0, 2), ("w_kv", 0, None), ("b_w_q", 0, 2), ("b_w_out", 0, 2)),
    "gdn1_conv": (("ffn_w_up", 2, 2),),
    "gdn1_rule": (("ffn_w_down", 2, 2),),
}
FETCH_FIRST = (("a_w_in", 0, 2), ("a_w_out", 0, 2))


class Traffic:
    def __init__(self, local):
        self.local = local
        self.shipped, self.received = [], {}

    @staticmethod
    def _assemble(W, name, first, layers, gathered):
        whole = _unshard(gathered, MATRICES[name])
        if name == "a_w_in":
            whole = jnp.pad(whole, ((0, 0), (0, 0), (0, W_IN_PAD - W_IN_COLS)))
        if layers is None:
            W[name] = whole
        else:
            W.setdefault(name, {}).update({first + i: whole[i] for i in range(layers)})

    def _shards_of(self, wanted):
        return [(self.local[name] if layers is None else self.local[name][first:first + layers]).astype(BF16)
                for name, first, layers in wanted]

    def fetch_first(self, extra):
        brought = chip_exchange(self._shards_of(FETCH_FIRST) + list(extra), scatter=False, name="gather_first")
        W = {}
        for (name, first, layers), g in zip(FETCH_FIRST, brought):
            self._assemble(W, name, first, layers, g)
        return W, brought[len(FETCH_FIRST):]

    def fetch(self, host):
        return ChipExchange(self._shards_of(FETCH_BESIDE[host]), scatter=False) if host in FETCH_BESIDE else None

    def arrived(self, host, brought, W):
        for (name, first, layers), g in zip(FETCH_BESIDE.get(host, ()), brought):
            self._assemble(W, name, first, layers, g)

    def _ready(self, G):
        out = []
        for name, axis in MATRICES.items():
            for layer, g in enumerate(G[name]):
                if g is None or (name, layer) in self.shipped or (name, layer) in [r[:2] for r in out]:
                    continue
                g = g[:, :W_IN_COLS] if name == "a_w_in" else g
                piece = _shards(g, axis - (0 if name == "w_kv" else 1)).astype(BF16)
                out.append((name, layer, piece.reshape(N_CHIPS, -1, piece.shape[-1])))
        return out

    def flush(self, G):
        ready = self._ready(G)
        self.in_flight = [r[:2] for r in ready]
        self.shipped += self.in_flight
        return ChipExchange([r[2] for r in ready], scatter=True) if ready else None

    def landed(self, brought):
        self.received.update(zip(self.in_flight, brought))
        self.in_flight = []

    def flush_last(self, G):
        ready = self._ready(G)
        self.shipped += [r[:2] for r in ready]
        brought = chip_exchange([r[2] for r in ready], scatter=True, name="scatter_last")
        self.received.update(zip([r[:2] for r in ready], brought))

    def my_sums(self):
        sums = {}
        for name in MATRICES:
            layers = sorted(l for n, l in self.received if n == name)
            parts = [sum_slots(self.received[(name, l)], name=f"sum_{name}{l}") for l in layers]
            sums[name] = parts[0] if len(parts) == 1 else jnp.concatenate(parts, axis=0)
        return sums


def _as_2d(a):
    return a.reshape(-1, a.shape[-1])


def kernel(x, a_norm, a_w_in, a_conv, a_log, a_dt_bias, a_out_norm, a_w_out, kv_norm, w_kv, k_norm, b_norm, b_w_q, q_norm, b_w_out, ffn_norm, ffn_w_up, ffn_conv, ffn_w_down, loss_target, m_a_norm, m_a_w_in, m_a_conv, m_a_log, m_a_dt_bias, m_a_out_norm, m_a_w_out, m_kv_norm, m_w_kv, m_k_norm, m_b_norm, m_b_w_q, m_q_norm, m_b_w_out, m_ffn_norm, m_ffn_w_up, m_ffn_conv, m_ffn_w_down, v_a_norm, v_a_w_in, v_a_conv, v_a_log, v_a_dt_bias, v_a_out_norm, v_a_w_out, v_kv_norm, v_w_kv, v_k_norm, v_b_norm, v_b_w_q, v_q_norm, v_b_w_out, v_ffn_norm, v_ffn_w_up, v_ffn_conv, v_ffn_w_down):
    local = dict(a_norm=a_norm, a_w_in=a_w_in, a_conv=a_conv, a_log=a_log, a_dt_bias=a_dt_bias, a_out_norm=a_out_norm,
                 a_w_out=a_w_out, kv_norm=kv_norm, w_kv=w_kv, k_norm=k_norm, b_norm=b_norm, b_w_q=b_w_q, q_norm=q_norm,
                 b_w_out=b_w_out, ffn_norm=ffn_norm, ffn_w_up=ffn_w_up, ffn_conv=ffn_conv, ffn_w_down=ffn_w_down)
    mom = dict(a_norm=m_a_norm, a_w_in=m_a_w_in, a_conv=m_a_conv, a_log=m_a_log, a_dt_bias=m_a_dt_bias, a_out_norm=m_a_out_norm,
               a_w_out=m_a_w_out, kv_norm=m_kv_norm, w_kv=m_w_kv, k_norm=m_k_norm, b_norm=m_b_norm, b_w_q=m_b_w_q, q_norm=m_q_norm,
               b_w_out=m_b_w_out, ffn_norm=m_ffn_norm, ffn_w_up=m_ffn_w_up, ffn_conv=m_ffn_conv, ffn_w_down=m_ffn_w_down)
    var = dict(a_norm=v_a_norm, a_w_in=v_a_w_in, a_conv=v_a_conv, a_log=v_a_log, a_dt_bias=v_a_dt_bias, a_out_norm=v_a_out_norm,
               a_w_out=v_a_w_out, kv_norm=v_kv_norm, w_kv=v_w_kv, k_norm=v_k_norm, b_norm=v_b_norm, b_w_q=v_b_w_q, q_norm=v_q_norm,
               b_w_out=v_b_w_out, ffn_norm=v_ffn_norm, ffn_w_up=v_ffn_w_up, ffn_conv=v_ffn_conv, ffn_w_down=v_ffn_w_down)
    chip = 2 * lax.axis_index("x") + lax.axis_index("y")

    mats = list(MATRICES)
    small_sharded = list(SMALL_SHARDED)
    traffic = Traffic(local)
    W, (vectors,) = traffic.fetch_first([_pack([local[n] for n in small_sharded])])
    W.update({n: local[n] for n in SMALL_REPLICATED})
    shard_shapes = [local[n].shape for n in small_sharded]
    per_chip = [_unpack(vectors[j], shard_shapes) for j in range(N_CHIPS)]
    for i, n in enumerate(small_sharded):
        W[n] = _unshard(jnp.stack([per_chip[j][i] for j in range(N_CHIPS)]), SMALL_SHARDED[n])

    T = x.shape[1]
    loss_part, dx, G = local_step(x.reshape(T, D_MODEL), loss_target.reshape(T, D_MODEL), W, traffic)

    traffic.flush_last(G)
    sums = traffic.my_sums()
    mine = [sums[n] for n in mats]
    theirs = sibling_swap(mine, name="swap_grads")

    small_full = {n: _stack(G[n]) for n in SMALL_ORDER}
    packed = _pack([small_full[n] for n in SMALL_ORDER] + [loss_part])
    total = sum_slots(all_gather_devices(packed, name="gather_small"), name="sum_small")
    small_shapes = [small_full[n].shape for n in SMALL_ORDER] + [loss_part.shape]
    summed = dict(zip(SMALL_ORDER + ("loss",), _unpack(total, small_shapes)))
    loss = jnp.sum(summed.pop("loss"))
    for n, axis in SMALL_SHARDED.items():
        size = local[n].shape[axis]
        summed[n] = lax.dynamic_slice_in_dim(summed[n], chip * size, size, axis)

    grads, deltas, new_m, new_v = {}, {}, {}, {}
    for n, p_mine, p_theirs in zip(mats, mine, theirs):
        outs = adamw(_as_2d(local[n]), [p_mine, p_theirs], _as_2d(mom[n]), _as_2d(var[n]), name=f"adamw_{n}")
        grads[n], deltas[n], new_m[n], new_v[n] = [o.reshape(local[n].shape) for o in outs]
    small_local_shapes = [local[n].shape for n in SMALL_ORDER]
    outs = adamw(_pack([local[n] for n in SMALL_ORDER]), [_pack([summed[n] for n in SMALL_ORDER])],
                 _pack([mom[n] for n in SMALL_ORDER]), _pack([var[n] for n in SMALL_ORDER]), name="adamw_small")
    for d, o in zip((grads, deltas, new_m, new_v), outs):
        d.update(zip(SMALL_ORDER, _unpack(o, small_local_shapes)))

    return (loss, dx.reshape(x.shape), *[grads[n] for n in WEIGHT_ORDER], *[deltas[n] for n in WEIGHT_ORDER],
            *[new_m[n] for n in WEIGHT_ORDER], *[new_v[n] for n in WEIGHT_ORDER])
```

```python
import math

import jax
import jax.numpy as jnp
from jax import lax
from jax.experimental import pallas as pl
from jax.experimental.pallas import tpu as pltpu

F32 = jnp.float32
BF16 = jnp.bfloat16

D_MODEL = 1024
HEADS = 8
HEAD_DIM = 128
GDN_CONV = 4
GDN_CHUNK = 64
D_FF = 2816
FFN_CONV = 3
EPS = 1e-6
N_A = 2
N_B = 2
DEPTH = N_A + N_B
W_IN_COLS = 4 * D_MODEL + 2 * HEADS
W_IN_PAD = 4 * D_MODEL + 128
Z_BLOCK = 3 * D_MODEL // 128
AB_BLOCK = 4 * D_MODEL // 128

ADAM_LR = 0.001
ADAM_B1 = 0.9
ADAM_B2 = 0.999
ADAM_EPS = 1e-08
ADAM_WD = 0.01
ADAM_STEP = 10

LANES = 128
SUBLANES = 8
VMEM_LIMIT = 56 * 1024 * 1024
HALO = SUBLANES
N_CHIPS = 4
N_DEV = 8

HI = lax.Precision.HIGHEST
MESH = pl.DeviceIdType.MESH
ANY = pl.BlockSpec(memory_space=pl.ANY)


def _cp(*sem):
    return pltpu.CompilerParams(dimension_semantics=sem, vmem_limit_bytes=VMEM_LIMIT)


def _tile(n, want, align=SUBLANES):
    t = (min(n, want) // align) * align
    while t > 0 and n % t:
        t -= align
    return t if t > 0 else n


def _dot(a, b, precision=None):
    return jnp.dot(a, b, preferred_element_type=F32, precision=precision)


def _dot_nt(a, b, precision=None):
    return lax.dot_general(a, b, (((1,), (1,)), ((), ())), preferred_element_type=F32, precision=precision)


def _dot_tn(a, b, precision=None):
    return lax.dot_general(a, b, (((0,), (0,)), ((), ())), preferred_element_type=F32, precision=precision)


def _bf(x):
    return x.astype(BF16)


def _sigmoid(x):
    return 0.5 * jnp.tanh(0.5 * x) + 0.5


def _softplus(x):
    return jnp.maximum(x, 0.0) + jnp.log(1.0 + jnp.exp(-jnp.abs(x)))


def _silu(x):
    return x * _sigmoid(x)


def _silu_and_grad(x):
    s = _sigmoid(x)
    return x * s, s * (1.0 + x * (1.0 - s))


def _fold_rows(v):
    return jnp.sum(v.reshape(v.shape[0] // SUBLANES, SUBLANES, v.shape[1]), axis=0)


def _accumulate(ref, value, axis):
    @pl.when(pl.program_id(axis) == 0)
    def _():
        ref[...] = jnp.zeros_like(ref)
    ref[...] += value


TILE_BUDGET = 44 * 1024 * 1024


def _rows_that_fit(T, fixed_bytes, row_bytes, want=1024):
    tm = _tile(T, want)
    while tm > SUBLANES and 2 * (fixed_bytes + tm * row_bytes) > TILE_BUDGET:
        tm //= 2
    return tm


def norm_matmul(x, gain, w, *, name):
    T, D = x.shape
    N = w.shape[1]
    tn = N
    tm = _rows_that_fit(T, D * N * 2, D * 4 + D * 2 + N * 4)

    def body(x_ref, g_ref, w_ref, y_ref, h_ref):
        @pl.when(pl.program_id(1) == 0)
        def _():
            xf = x_ref[...]
            r = lax.rsqrt(jnp.mean(xf * xf, axis=-1, keepdims=True) + EPS)
            h_ref[...] = _bf(xf * r * g_ref[...])
        y_ref[...] = _dot(h_ref[...], w_ref[...])

    return pl.pallas_call(
        body, name=name, grid=(T // tm, N // tn),
        in_specs=[pl.BlockSpec((tm, D), lambda i, j: (i, 0)), pl.BlockSpec((1, D), lambda i, j: (0, 0)),
                  pl.BlockSpec((D, tn), lambda i, j: (0, j))],
        out_specs=[pl.BlockSpec((tm, tn), lambda i, j: (i, j)), pl.BlockSpec((tm, D), lambda i, j: (i, 0))],
        out_shape=[jax.ShapeDtypeStruct((T, N), F32), jax.ShapeDtypeStruct((T, D), BF16)],
        compiler_params=_cp("parallel", "arbitrary"),
    )(x, gain, w)


def matmul_residual(a, w, res, *, name):
    T, K = a.shape
    N = w.shape[1]
    tn = N
    tm = _rows_that_fit(T, K * N * 2, K * a.dtype.itemsize + 2 * N * 4)

    def body(a_ref, w_ref, r_ref, o_ref):
        o_ref[...] = r_ref[...] + _dot(_bf(a_ref[...]), w_ref[...])

    return pl.pallas_call(
        body, name=name, grid=(T // tm, N // tn),
        in_specs=[pl.BlockSpec((tm, K), lambda i, j: (i, 0)), pl.BlockSpec((K, tn), lambda i, j: (0, j)),
                  pl.BlockSpec((tm, tn), lambda i, j: (i, j))],
        out_specs=pl.BlockSpec((tm, tn), lambda i, j: (i, j)),
        out_shape=jax.ShapeDtypeStruct((T, N), F32),
        compiler_params=_cp("parallel", "parallel"),
    )(a, w, res)


def matmul_nt(dy, w, *, name, out_dtype=F32):
    T, N = dy.shape
    K = w.shape[0]
    tk = K
    tm = _rows_that_fit(T, K * N * 2, N * dy.dtype.itemsize + K * jnp.dtype(out_dtype).itemsize)

    def body(dy_ref, w_ref, o_ref):
        o_ref[...] = _dot_nt(_bf(dy_ref[...]), w_ref[...]).astype(out_dtype)

    return pl.pallas_call(
        body, name=name, grid=(T // tm, K // tk),
        in_specs=[pl.BlockSpec((tm, N), lambda i, j: (i, 0)), pl.BlockSpec((tk, N), lambda i, j: (j, 0))],
        out_specs=pl.BlockSpec((tm, tk), lambda i, j: (i, j)),
        out_shape=jax.ShapeDtypeStruct((T, K), out_dtype),
        compiler_params=_cp("parallel", "parallel"),
    )(dy, w)


def matmul_nt_normbwd(parts, w, x, gain, dres, *, name, exchange=None):
    T = x.shape[0]
    D = w.shape[0]
    tm = _tile(T, 256)
    n = len(parts)

    def body(*refs):
        x_ref, g_ref, dr_ref, dx_ref, dg_ref = refs[2 * n:]
        dh = _dot_nt(_bf(refs[0][...]), refs[n][...])
        for i in range(1, n):
            dh = dh + _dot_nt(_bf(refs[i][...]), refs[n + i][...])
        xf = x_ref[...]
        r = lax.rsqrt(jnp.mean(xf * xf, axis=-1, keepdims=True) + EPS)
        xh = xf * r
        dxh = dh * g_ref[...]
        dx_ref[...] = dr_ref[...] + r * (dxh - xh * jnp.mean(dxh * xh, axis=-1, keepdims=True))
        _accumulate(dg_ref, _fold_rows(dh * xh), 0)

    dy_specs = [pl.BlockSpec((tm, dy.shape[1]), lambda i: (i, 0)) for dy, _ in parts]
    w_specs = [pl.BlockSpec((D, dy.shape[1]), lambda i, b=b: (0, b)) for dy, b in parts]
    (dx, dgain), brought = hosted_call(
        body, name=name, grid=(T // tm,),
        in_specs=dy_specs + w_specs + [pl.BlockSpec((tm, D), lambda i: (i, 0)), pl.BlockSpec((1, D), lambda i: (0, 0)),
                                       pl.BlockSpec((tm, D), lambda i: (i, 0))],
        out_specs=[pl.BlockSpec((tm, D), lambda i: (i, 0)), pl.BlockSpec((SUBLANES, D), lambda i: (0, 0))],
        out_shape=[jax.ShapeDtypeStruct((T, D), F32), jax.ShapeDtypeStruct((SUBLANES, D), F32)],
        scratch_shapes=[], args=(*[dy for dy, _ in parts], *([w] * n), x, gain, dres), exchange=exchange)
    return (dx, dgain, brought) if exchange is not None else (dx, dgain)


def matmul_tn(a, dy, *, name):
    T, K = a.shape
    N = dy.shape[1]
    tk = _tile(K, 1408, LANES)
    tn = _tile(N, 1408 if N <= 2816 else 512, LANES)
    tm = _tile(T, 4096 if tn <= 512 else 2048)

    def body(a_ref, dy_ref, o_ref):
        _accumulate(o_ref, _dot_tn(_bf(a_ref[...]), _bf(dy_ref[...])), 2)

    return pl.pallas_call(
        body, name=name, grid=(K // tk, N // tn, T // tm),
        in_specs=[pl.BlockSpec((tm, tk), lambda i, j, t: (t, i)), pl.BlockSpec((tm, tn), lambda i, j, t: (t, j))],
        out_specs=pl.BlockSpec((tk, tn), lambda i, j, t: (i, j)),
        out_shape=jax.ShapeDtypeStruct((K, N), F32),
        compiler_params=_cp("parallel", "parallel", "arbitrary"),
    )(a, dy)


def _halo_specs(T, tt, tc, col):
    per = tt // HALO
    last = T // HALO - 1
    return [pl.BlockSpec((HALO, tc), lambda j, i: (jnp.maximum(i * per - 1, 0), col(j))),
            pl.BlockSpec((tt, tc), lambda j, i: (i, col(j))),
            pl.BlockSpec((HALO, tc), lambda j, i: (jnp.minimum((i + 1) * per, last), col(j)))]


def _extend(prev_ref, cur_ref, next_ref, nt):
    i = pl.program_id(1)
    p = jnp.where(i > 0, prev_ref[...].astype(F32), 0.0)
    q = jnp.where(i < nt - 1, next_ref[...].astype(F32), 0.0)
    return jnp.concatenate([p, cur_ref[...].astype(F32), q], axis=0)


def _rows_before(e, s):
    return e if s == 0 else pltpu.roll(e, s, 0)


def _rows_after(e, s):
    return e if s == 0 else pltpu.roll(e, e.shape[0] - s, 0)


def _causal_conv(e, w, taps):
    y = w[taps - 1:taps, :] * e
    for s in range(1, taps):
        y = y + w[taps - 1 - s:taps - s, :] * _rows_before(e, s)
    return y


def _causal_conv_bwd(e, dc, w, taps, tt):
    lo, hi = HALO, HALO + tt
    dx = w[taps - 1:taps, :] * dc
    dws = [None] * taps
    dws[taps - 1] = jnp.sum((e * dc)[lo:hi], axis=0, keepdims=True)
    for s in range(1, taps):
        dx = dx + w[taps - 1 - s:taps - s, :] * _rows_after(dc, s)
        dws[taps - 1 - s] = jnp.sum((_rows_before(e, s) * dc)[lo:hi], axis=0, keepdims=True)
    dw = jnp.concatenate(dws + [jnp.zeros((SUBLANES - taps, e.shape[1]), F32)], axis=0)
    return dx[lo:hi], dw


def _qkv_kind(col_block):
    return (col_block >= HEADS).astype(jnp.int32) + (col_block >= 2 * HEADS).astype(jnp.int32)


def _l2norm_scale(kind):
    return jnp.where(kind == 0, HEAD_DIM ** -0.5, 1.0)


def gdn_conv_fwd(proj, conv_w, *, name, exchange=None):
    T = proj.shape[0]
    tt, tc = _tile(T, 512), HEAD_DIM
    nt = T // tt

    def body(p_ref, c_ref, n_ref, w_ref, o_ref):
        kind = _qkv_kind(pl.program_id(0))
        e = _extend(p_ref, c_ref, n_ref, nt)
        s = _silu(_causal_conv(e, w_ref[...], GDN_CONV))[HALO:HALO + tt]
        r = lax.rsqrt(jnp.sum(s * s, axis=-1, keepdims=True) + EPS) * _l2norm_scale(kind)
        o_ref[...] = jnp.where(kind == 2, s, s * r)

    (qkv,), brought = hosted_call(
        body, name=name, grid=(3 * HEADS, nt),
        in_specs=_halo_specs(T, tt, tc, lambda j: j) + [pl.BlockSpec((SUBLANES, tc), lambda j, i: (0, j))],
        out_specs=[pl.BlockSpec((tt, tc), lambda j, i: (i, j))],
        out_shape=[jax.ShapeDtypeStruct((T, 3 * D_MODEL), F32)],
        scratch_shapes=[], args=(proj, proj, proj, conv_w), exchange=exchange)
    return qkv, brought


def gdn_conv_bwd(proj, conv_w, dqkv, *, name):
    T = proj.shape[0]
    tt, tc = _tile(T, 512), HEAD_DIM
    nt = T // tt

    def body(p_ref, c_ref, n_ref, w_ref, dp_ref, dc_ref, dn_ref, dx_ref, dw_ref):
        kind = _qkv_kind(pl.program_id(0))
        w = w_ref[...]
        e = _extend(p_ref, c_ref, n_ref, nt)
        c = _causal_conv(e, w, GDN_CONV)
        s, s_grad = _silu_and_grad(c)
        dy = _extend(dp_ref, dc_ref, dn_ref, nt)
        r = lax.rsqrt(jnp.sum(s * s, axis=-1, keepdims=True) + EPS)
        y = s * r
        ds_norm = r * _l2norm_scale(kind) * (dy - y * jnp.sum(dy * y, axis=-1, keepdims=True))
        ds = jnp.where(kind == 2, dy, ds_norm)
        dx, dw = _causal_conv_bwd(e, ds * s_grad, w, GDN_CONV, tt)
        dx_ref[...] = _bf(dx)
        _accumulate(dw_ref, dw, 1)

    return pl.pallas_call(
        body, name=name, grid=(3 * HEADS, nt),
        in_specs=_halo_specs(T, tt, tc, lambda j: j) + [pl.BlockSpec((SUBLANES, tc), lambda j, i: (0, j))]
        + _halo_specs(T, tt, tc, lambda j: j),
        out_specs=[pl.BlockSpec((tt, tc), lambda j, i: (i, j)), pl.BlockSpec((SUBLANES, tc), lambda j, i: (0, j))],
        out_shape=[jax.ShapeDtypeStruct((T, 3 * D_MODEL), BF16), jax.ShapeDtypeStruct((SUBLANES, 3 * D_MODEL), F32)],
        compiler_params=_cp("parallel", "arbitrary"),
    )(proj, proj, proj, conv_w, dqkv, dqkv, dqkv)


FFN_COLS = 256


def ffn_act_fwd(u, conv_w, *, name, exchange=None):
    T = u.shape[0]
    tt, tc = _tile(T, 512), FFN_COLS
    half = D_FF // tc
    nt = T // tt

    def body(gp, gc, gn, up, uc, un, wg_ref, wu_ref, o_ref):
        gate = _causal_conv(_extend(gp, gc, gn, nt), wg_ref[...], FFN_CONV)
        up_ = _causal_conv(_extend(up, uc, un, nt), wu_ref[...], FFN_CONV)
        o_ref[...] = _bf((_silu(gate) * up_)[HALO:HALO + tt])

    (act,), brought = hosted_call(
        body, name=name, grid=(half, nt),
        in_specs=_halo_specs(T, tt, tc, lambda j: j) + _halo_specs(T, tt, tc, lambda j: j + half)
        + [pl.BlockSpec((SUBLANES, tc), lambda j, i: (0, j)), pl.BlockSpec((SUBLANES, tc), lambda j, i: (0, j + half))],
        out_specs=[pl.BlockSpec((tt, tc), lambda j, i: (i, j))],
        out_shape=[jax.ShapeDtypeStruct((T, D_FF), BF16)],
        scratch_shapes=[], args=(u, u, u, u, u, u, conv_w, conv_w), exchange=exchange)
    return act, brought


def ffn_act_bwd(u, conv_w, dact, *, name):
    T = u.shape[0]
    tt, tc = _tile(T, 512), FFN_COLS
    half = D_FF // tc
    nt = T // tt

    def body(gp, gc, gn, up, uc, un, wg_ref, wu_ref, dp, dc_, dn, dug_ref, duu_ref, dwg_ref, dwu_ref):
        wg, wu = wg_ref[...], wu_ref[...]
        eg, eu = _extend(gp, gc, gn, nt), _extend(up, uc, un, nt)
        gate, up_ = _causal_conv(eg, wg, FFN_CONV), _causal_conv(eu, wu, FFN_CONV)
        da = _extend(dp, dc_, dn, nt)
        act, act_grad = _silu_and_grad(gate)
        dxg, dwg = _causal_conv_bwd(eg, da * up_ * act_grad, wg, FFN_CONV, tt)
        dxu, dwu = _causal_conv_bwd(eu, da * act, wu, FFN_CONV, tt)
        dug_ref[...] = _bf(dxg)
        duu_ref[...] = _bf(dxu)
        _accumulate(dwg_ref, dwg, 1)
        _accumulate(dwu_ref, dwu, 1)

    return pl.pallas_call(
        body, name=name, grid=(half, nt),
        in_specs=_halo_specs(T, tt, tc, lambda j: j) + _halo_specs(T, tt, tc, lambda j: j + half)
        + [pl.BlockSpec((SUBLANES, tc), lambda j, i: (0, j)), pl.BlockSpec((SUBLANES, tc), lambda j, i: (0, j + half))]
        + _halo_specs(T, tt, tc, lambda j: j),
        out_specs=[pl.BlockSpec((tt, tc), lambda j, i: (i, j)), pl.BlockSpec((tt, tc), lambda j, i: (i, j)),
                   pl.BlockSpec((SUBLANES, tc), lambda j, i: (0, j)), pl.BlockSpec((SUBLANES, tc), lambda j, i: (0, j))],
        out_shape=[jax.ShapeDtypeStruct((T, D_FF), BF16), jax.ShapeDtypeStruct((T, D_FF), BF16),
                   jax.ShapeDtypeStruct((SUBLANES, D_FF), F32), jax.ShapeDtypeStruct((SUBLANES, D_FF), F32)],
        compiler_params=_cp("parallel", "arbitrary"),
    )(u, u, u, u, u, u, conv_w, conv_w, dact, dact, dact)


def head_norm_fwd(x, gain, z=None, *, x_col=0, z_col=0, name, out_dtype=F32):
    T = x.shape[0]
    tt = _tile(T, 512)
    gated = z is not None

    def body(*refs):
        x_ref, g_ref = refs[0], refs[1]
        o_ref = refs[-1]
        for h in range(HEADS):
            cols = slice(h * HEAD_DIM, (h + 1) * HEAD_DIM)
            xf = x_ref[:, cols]
            y = xf * lax.rsqrt(jnp.mean(xf * xf, axis=-1, keepdims=True) + EPS) * g_ref[...]
            if gated:
                y = y * _silu(refs[2][:, cols])
            o_ref[:, cols] = y.astype(out_dtype)

    wide = lambda col: pl.BlockSpec((tt, D_MODEL), lambda i: (i, col // HEADS))
    ins = [wide(x_col), pl.BlockSpec((1, HEAD_DIM), lambda i: (0, 0))]
    args = [x, gain]
    if gated:
        ins.append(wide(z_col))
        args.append(z)
    return pl.pallas_call(
        body, name=name, grid=(T // tt,), in_specs=ins, out_specs=wide(0),
        out_shape=jax.ShapeDtypeStruct((T, D_MODEL), out_dtype),
        compiler_params=_cp("parallel"),
    )(*args)


def head_norm_bwd(x, gain, dys, z=None, *, x_col=0, z_col=0, name, dx_dtype=F32):
    T = x.shape[0]
    tt = _tile(T, 512)
    gated = z is not None
    nd = len(dys)

    def body(*refs):
        x_ref, g_ref = refs[0], refs[1]
        outs = refs[2 + nd + (1 if gated else 0):]
        dx_ref, dg_ref = outs[0], outs[-1]
        folds = []
        for h in range(HEADS):
            cols = slice(h * HEAD_DIM, (h + 1) * HEAD_DIM)
            xf = x_ref[:, cols]
            r = lax.rsqrt(jnp.mean(xf * xf, axis=-1, keepdims=True) + EPS)
            xh = xf * r
            dy = refs[2][:, cols].astype(F32)
            for d_ref in refs[3:2 + nd]:
                dy = dy + d_ref[:, cols].astype(F32)
            if gated:
                gate, gate_grad = _silu_and_grad(refs[2 + nd][:, cols])
                outs[1][:, cols] = _bf(dy * xh * g_ref[...] * gate_grad)
                dn = dy * gate
            else:
                dn = dy
            dxh = dn * g_ref[...]
            dx_ref[:, cols] = (r * (dxh - xh * jnp.mean(dxh * xh, axis=-1, keepdims=True))).astype(dx_dtype)
            folds.append(_fold_rows(dn * xh))
        _accumulate(dg_ref, jnp.concatenate(folds, axis=1), 0)

    wide = lambda col: pl.BlockSpec((tt, D_MODEL), lambda i: (i, col // HEADS))
    ins = [wide(x_col), pl.BlockSpec((1, HEAD_DIM), lambda i: (0, 0))] + [wide(0)] * nd
    args = [x, gain] + list(dys)
    outs = [wide(0)]
    shapes = [jax.ShapeDtypeStruct((T, D_MODEL), dx_dtype)]
    if gated:
        ins.append(wide(z_col))
        args.append(z)
        outs.append(wide(0))
        shapes.append(jax.ShapeDtypeStruct((T, D_MODEL), BF16))
    outs.append(pl.BlockSpec((SUBLANES, D_MODEL), lambda i: (0, 0)))
    shapes.append(jax.ShapeDtypeStruct((SUBLANES, D_MODEL), F32))
    return pl.pallas_call(
        body, name=name, grid=(T // tt,), in_specs=ins, out_specs=outs, out_shape=shapes,
        compiler_params=_cp("arbitrary"),
    )(*args)


def gates_fwd(proj, a_log, dt_bias, *, name):
    T = proj.shape[0]
    tt = _tile(T, 1024)

    def body(p_ref, al_ref, dt_ref, o_ref):
        p = p_ref[...]
        lane = lax.broadcasted_iota(jnp.int32, p.shape, 1)
        o_ref[...] = jnp.where(lane < HEADS, -jnp.exp(al_ref[...]) * _softplus(p + dt_ref[...]), _sigmoid(p))

    return pl.pallas_call(
        body, name=name, grid=(T // tt,),
        in_specs=[pl.BlockSpec((tt, LANES), lambda i: (i, AB_BLOCK)), pl.BlockSpec((1, LANES), lambda i: (0, 0)),
                  pl.BlockSpec((1, LANES), lambda i: (0, 0))],
        out_specs=pl.BlockSpec((tt, LANES), lambda i: (i, 0)),
        out_shape=jax.ShapeDtypeStruct((T, LANES), F32),
        compiler_params=_cp("parallel"),
    )(proj, a_log, dt_bias)


def gates_bwd(proj, a_log, dt_bias, dgate, *, name):
    T = proj.shape[0]
    tt = _tile(T, 1024)

    def body(p_ref, al_ref, dt_ref, d_ref, dp_ref, dal_ref, ddt_ref):
        p, d = p_ref[...], d_ref[...]
        lane = lax.broadcasted_iota(jnp.int32, p.shape, 1)
        ea = jnp.exp(al_ref[...])
        pa = p + dt_ref[...]
        da = -d * ea * _sigmoid(pa)
        b = _sigmoid(p)
        dp_ref[...] = _bf(jnp.where(lane < HEADS, da, jnp.where(lane < 2 * HEADS, d * b * (1.0 - b), 0.0)))
        _accumulate(dal_ref, _fold_rows(jnp.where(lane < HEADS, -d * ea * _softplus(pa), 0.0)), 0)
        _accumulate(ddt_ref, _fold_rows(jnp.where(lane < HEADS, da, 0.0)), 0)

    acc = pl.BlockSpec((SUBLANES, LANES), lambda i: (0, 0))
    return pl.pallas_call(
        body, name=name, grid=(T // tt,),
        in_specs=[pl.BlockSpec((tt, LANES), lambda i: (i, AB_BLOCK)), pl.BlockSpec((1, LANES), lambda i: (0, 0)),
                  pl.BlockSpec((1, LANES), lambda i: (0, 0)), pl.BlockSpec((tt, LANES), lambda i: (i, 0))],
        out_specs=[pl.BlockSpec((tt, LANES), lambda i: (i, 0)), acc, acc],
        out_shape=[jax.ShapeDtypeStruct((T, LANES), BF16), jax.ShapeDtypeStruct((SUBLANES, LANES), F32),
                   jax.ShapeDtypeStruct((SUBLANES, LANES), F32)],
        compiler_params=_cp("arbitrary"),
    )(proj, a_log, dt_bias, dgate)


def loss_fwd(y, target, *, name):
    T, D = y.shape
    tt = _tile(T, 512)

    def body(y_ref, t_ref, dy_ref, l_ref):
        d = y_ref[...] - t_ref[...]
        dy_ref[...] = d * (1.0 / D)
        sq = d * d
        lanes = sq[:, 0:LANES]
        for c in range(1, D // LANES):
            lanes = lanes + sq[:, c * LANES:(c + 1) * LANES]
        _accumulate(l_ref, _fold_rows(lanes) * (0.5 / D), 0)

    return pl.pallas_call(
        body, name=name, grid=(T // tt,),
        in_specs=[pl.BlockSpec((tt, D), lambda i: (i, 0)), pl.BlockSpec((tt, D), lambda i: (i, 0))],
        out_specs=[pl.BlockSpec((tt, D), lambda i: (i, 0)), pl.BlockSpec((SUBLANES, LANES), lambda i: (0, 0))],
        out_shape=[jax.ShapeDtypeStruct((T, D), F32), jax.ShapeDtypeStruct((SUBLANES, LANES), F32)],
        compiler_params=_cp("arbitrary"),
    )(y, target)


def _split_bf16(x):
    hi = _bf(x)
    return hi, _bf(x - hi.astype(F32))


def _dot3(a, b, dot=_dot):
    return dot(a[0], b[0]) + dot(a[0], b[1]) + dot(a[1], b[0])


def _each(fn, *lists):
    return [fn(*args) for args in zip(*lists)]


def _unit_lower_inverses(lows):
    c = lows[0].shape[0]
    ii = lax.broadcasted_iota(jnp.int32, (c, c), 0)
    jj = lax.broadcasted_iota(jnp.int32, (c, c), 1)
    eye = jnp.where(ii == jj, 1.0, 0.0)
    invs = _each(lambda low: eye - low, lows)
    powers = _each(_split_bf16, lows)
    for _ in range(int(math.log2(c)) - 1):
        powers = _each(lambda p: _split_bf16(_dot3(p, p)), powers)
        invs = _each(lambda inv, p: inv + _dot3(_split_bf16(inv), p), invs, powers)
    return invs


def _gdn_chunks(heads):
    q, k, v, a_col, a_row, b_col, s0 = (list(t) for t in zip(*heads))
    c = q[0].shape[0]
    ii = lax.broadcasted_iota(jnp.int32, (c, c), 0)
    jj = lax.broadcasted_iota(jnp.int32, (c, c), 1)
    tri, strict = ii >= jj, ii > jj
    g_col = _each(lambda ar: jnp.sum(jnp.where(tri, ar, 0.0), axis=1, keepdims=True), a_row)
    g_row = _each(lambda ac: jnp.sum(jnp.where(ii <= jj, ac, 0.0), axis=0, keepdims=True), a_col)
    gam = _each(lambda gc, gr: jnp.exp(jnp.where(tri, gc - gr, -jnp.inf)), g_col, g_row)
    g_last = _each(lambda ac: jnp.sum(ac, axis=0, keepdims=True), a_col)
    gam_col = _each(jnp.exp, g_col)
    del_col = _each(lambda gl, gc: jnp.exp(gl - gc), g_last, g_col)
    kb = _each(lambda k_, b: k_ * b, k, b_col)
    m = _each(lambda kb_, k_: _dot_nt(_bf(kb_), _bf(k_)), kb, k)
    ks = _each(lambda k_, s: _dot(_bf(k_), _bf(s)), k, s0)
    qk = _each(lambda q_, k_: _dot_nt(_bf(q_), _bf(k_)), q, k)
    inv = _unit_lower_inverses(_each(lambda m_, g: jnp.where(strict, m_ * g, 0.0), m, gam))
    e = _each(lambda v_, gc, ks_: v_ - gc * ks_, v, gam_col, ks)
    inv = _each(_split_bf16, inv)
    vn = _each(lambda inv_, b, e_: _dot3(inv_, _split_bf16(b * e_)), inv, b_col, e)
    p = _each(lambda qk_, g: jnp.where(tri, qk_ * g, 0.0), qk, gam)
    return [dict(tri=tri, strict=strict, ii=ii, jj=jj, gam=gam[g], g_last=g_last[g], gam_col=gam_col[g], del_col=del_col[g],
                 kb=kb[g], m=m[g], inv=inv[g], ks=ks[g], e=e[g], vn=vn[g], qk=qk[g], p=p[g]) for g in range(len(heads))]


GDN_HEADS_PER_STEP = HEADS


def _head_cols(ref, g):
    return ref[:, g * HEAD_DIM:(g + 1) * HEAD_DIM]


def _load_heads(q_ref, k_ref, v_ref, ar_ref, br_ref, states):
    return [(_head_cols(q_ref, g), _head_cols(k_ref, g), _head_cols(v_ref, g), ar_ref[g, 0].T, ar_ref[g, 0], br_ref[g, 0].T, states(g))
            for g in range(GDN_HEADS_PER_STEP)]


def gdn_fwd(qkv, a_row, b_row, *, name, exchange=None):
    T = qkv.shape[0]
    C = GDN_CHUNK
    N = T // C
    G = GDN_HEADS_PER_STEP

    def body(q_ref, k_ref, v_ref, ar_ref, br_ref, o_ref, s_ref, state):
        @pl.when(pl.program_id(1) == 0)
        def _():
            state[...] = jnp.zeros_like(state)
        loaded = _load_heads(q_ref, k_ref, v_ref, ar_ref, br_ref, lambda g: state[g])
        ws = _gdn_chunks(loaded)
        qs = _each(lambda h: _dot(_bf(h[0]), _bf(h[6])), loaded)
        pv = _each(lambda w: _dot(_bf(w["p"]), _bf(w["vn"])), ws)
        kv = _each(lambda h, w: _dot_tn(_bf(w["del_col"] * h[1]), _bf(w["vn"])), loaded, ws)
        for g, w in enumerate(ws):
            s0 = loaded[g][6]
            s_ref[g, 0] = s0
            o_ref[:, g * HEAD_DIM:(g + 1) * HEAD_DIM] = w["gam_col"] * qs[g] + pv[g]
            state[g] = jnp.exp(w["g_last"]) * s0 + kv[g]

    per = HEADS // G
    blk = lambda off: pl.BlockSpec((C, G * HEAD_DIM), lambda h, n: (n, off * per + h))
    row = pl.BlockSpec((G, 1, 1, C), lambda h, n: (h, n, 0, 0))
    (o, states), brought = hosted_call(
        body, name=name, grid=(per, N),
        in_specs=[blk(0), blk(1), blk(2), row, row],
        out_specs=[blk(0), pl.BlockSpec((G, 1, HEAD_DIM, HEAD_DIM), lambda h, n: (h, n, 0, 0))],
        out_shape=[jax.ShapeDtypeStruct((T, D_MODEL), F32), jax.ShapeDtypeStruct((HEADS, N, HEAD_DIM, HEAD_DIM), F32)],
        scratch_shapes=[pltpu.VMEM((G, HEAD_DIM, HEAD_DIM), F32)],
        args=(qkv, qkv, qkv, a_row, b_row), exchange=exchange)
    return o, states, brought


def gdn_bwd(qkv, a_row, b_row, states, do, *, name, exchange=None):
    T = qkv.shape[0]
    C = GDN_CHUNK
    N = T // C
    G = GDN_HEADS_PER_STEP
    per = HEADS // G
    rev = lambda n: N - 1 - n
    row = pl.BlockSpec((G, 1, 1, C), lambda h, n: (h, rev(n), 0, 0))

    def body(q_ref, k_ref, v_ref, ar_ref, br_ref, s_ref, do_ref, dqkv_ref, da_ref, db_ref, dstate):
        @pl.when(pl.program_id(1) == 0)
        def _():
            dstate[...] = jnp.zeros_like(dstate)
        loaded = _load_heads(q_ref, k_ref, v_ref, ar_ref, br_ref, lambda g: s_ref[g, 0])
        hs = _gdn_chunks(loaded)
        for g, d in enumerate(hs):
            q, k, _, _, _, b, s0 = loaded[g]
            d.update(q=q, k=k, b=b, s0=s0, ds1=dstate[g], dout=_head_cols(do_ref, g))
        rows = lambda t: jnp.sum(t, axis=1, keepdims=True)
        ii_col = lax.broadcasted_iota(jnp.int32, (C, 1), 0)

        def stage(**fns):
            for key, fn in fns.items():
                for d in hs:
                    d[key] = fn(d)

        stage(s0b=lambda d: _bf(d["s0"]), ds1b=lambda d: _bf(d["ds1"]), doutb=lambda d: _bf(d["dout"]),
              kbf=lambda d: _bf(d["k"]), qbf=lambda d: _bf(d["q"]), vnb=lambda d: _bf(d["vn"]))
        stage(dvn=lambda d: _dot_tn(_bf(d["p"]), d["doutb"]) + _dot(_bf(d["del_col"] * d["k"]), d["ds1b"]),
              dqk=lambda d: jnp.where(d["tri"], _dot_nt(d["doutb"], d["vnb"]), 0.0) * d["gam"],
              qs=lambda d: _dot(d["qbf"], d["s0b"]),
              dkd=lambda d: _dot_nt(d["vnb"], d["ds1b"]))
        stage(dr=lambda d: _dot3(d["inv"], _split_bf16(d["dvn"]), _dot_tn),
              dq=lambda d: d["gam_col"] * _dot_nt(d["doutb"], d["s0b"]) + _dot(_bf(d["dqk"]), d["kbf"]),
              dk=lambda d: _dot_tn(_bf(d["dqk"]), d["qbf"]) + d["del_col"] * d["dkd"],
              ddel=lambda d: d["del_col"] * rows(d["dkd"] * d["k"]))
        stage(dg=lambda d: d["gam_col"] * rows(d["dout"] * d["qs"]) - d["ddel"],
              dg_last=lambda d: jnp.sum(d["ddel"], axis=0, keepdims=True)
              + jnp.exp(d["g_last"]) * jnp.sum(rows(d["ds1"] * d["s0"]), axis=0, keepdims=True),
              dm=lambda d: jnp.where(d["strict"], -_dot_nt(_bf(d["dr"]), d["vnb"]), 0.0) * d["gam"],
              de=lambda d: d["b"] * d["dr"])
        stage(dkb=lambda d: _dot(_bf(d["dm"]), d["kbf"]),
              dks=lambda d: -d["gam_col"] * d["de"])
        stage(dk=lambda d: d["dk"] + _dot_tn(_bf(d["dm"]), _bf(d["kb"])) + _dot_nt(_bf(d["dks"]), d["s0b"]) + d["b"] * d["dkb"],
              dbeta=lambda d: rows(d["dr"] * d["e"]) + rows(d["dkb"] * d["k"]),
              ds0=lambda d: jnp.exp(d["g_last"]) * d["ds1"] + _dot_tn(_bf(d["gam_col"] * d["q"]), d["doutb"])
              + _dot_tn(d["kbf"], _bf(d["dks"])),
              wg=lambda d: d["dqk"] * d["qk"] + d["dm"] * d["m"])
        stage(dg=lambda d: d["dg"] - d["gam_col"] * rows(d["de"] * d["ks"]) + rows(d["wg"])
              - jnp.sum(d["wg"], axis=0, keepdims=True).T + jnp.where(ii_col == C - 1, d["dg_last"], 0.0))
        stage(da=lambda d: jnp.sum(jnp.where(d["ii"] >= d["jj"], d["dg"], 0.0), axis=0, keepdims=True),
              db=lambda d: d["dbeta"].T)
        for g, d in enumerate(hs):
            dstate[g] = d["ds0"]
            da_ref[g, 0] = d["da"]
            db_ref[g, 0] = d["db"]
            for part, key in enumerate(("dq", "dk", "de")):
                start = part * D_MODEL + g * HEAD_DIM
                dqkv_ref[:, start:start + HEAD_DIM] = d[key]

    assert per == 1
    blk = lambda off: pl.BlockSpec((C, G * HEAD_DIM), lambda h, n: (rev(n), off * per + h))
    (dqkv, da, db), brought = hosted_call(
        body, name=name, grid=(per, N),
        in_specs=[blk(0), blk(1), blk(2), row, row,
                  pl.BlockSpec((G, 1, HEAD_DIM, HEAD_DIM), lambda h, n: (h, rev(n), 0, 0)), blk(0)],
        out_specs=[pl.BlockSpec((C, 3 * D_MODEL), lambda h, n: (rev(n), 0)), row, row],
        out_shape=[jax.ShapeDtypeStruct((T, 3 * D_MODEL), F32)] + [jax.ShapeDtypeStruct((HEADS, N, 1, C), F32)] * 2,
        scratch_shapes=[pltpu.VMEM((G, HEAD_DIM, HEAD_DIM), F32)],
        args=(qkv, qkv, qkv, a_row, b_row, states, do), exchange=exchange)
    return dqkv, da, db, brought


SB_BLOCK = 128


SB_QBLOCKS = 8


def _sb_rows(j, blk):
    return pl.ds(pl.multiple_of(j * blk, blk), blk)


def _sb_tile(qb, k_ref, i, j, blk, live):
    z = _dot_nt(qb, _bf(k_ref[_sb_rows(j, blk), :]))
    t_idx = i * blk + lax.broadcasted_iota(jnp.int32, (blk, blk), 0)
    s_idx = j * blk + lax.broadcasted_iota(jnp.int32, (blk, blk), 1)
    mask = jnp.logical_and(s_idx < t_idx, live)
    lf = jnp.where(mask, -_softplus(z), 0.0)
    return z, mask, lf


SB_DEAD = 105.0


def sb_fwd(q, k, v, *, k_col=0, v_col=0, name):
    T = q.shape[0]
    blk = _tile(T, SB_BLOCK)
    P = min(SB_QBLOCKS, T // blk)
    scale = HEAD_DIM ** -0.5

    def body(q_ref, k_ref, v_ref, o_ref, l_ref, n_ref):
        iq = [P * pl.program_id(1) + p for p in range(P)]
        qb = [_bf(q_ref[p * blk:(p + 1) * blk, :] * scale) for p in range(P)]
        r_idx = lax.broadcasted_iota(jnp.int32, (blk, blk), 0)
        c_idx = lax.broadcasted_iota(jnp.int32, (blk, blk), 1)
        later = _bf(jnp.where(r_idx > c_idx, 1.0, 0.0))

        def live_blocks(jj, runs):
            return [jnp.logical_and(jj <= i, jnp.max(run) > -SB_DEAD) for i, run in zip(iq, runs)]

        def alive(carry):
            jj, _, runs, _ = carry
            some = False
            for f in live_blocks(jj, runs):
                some = jnp.logical_or(some, f)
            return some

        def step(carry):
            jj, accs, runs, visited = carry
            live = live_blocks(jj, runs)
            js = [jnp.maximum(i - jj, 0) for i in iq]
            tiles = _each(lambda q_, i, j, f: _sb_tile(q_, k_ref, i, j, blk, f), qb, iq, js, live)
            parts = _each(lambda t: _split_bf16(t[2]), tiles)
            after = _each(lambda run, s: run + _dot(s[0], later) + _dot(s[1], later), runs, parts)
            a = _each(lambda t, af: jnp.where(t[1], jnp.exp(t[0] + t[2] + af), 0.0), tiles, after)
            vb = _each(lambda j: _bf(v_ref[_sb_rows(j, blk), :]), js)
            accs = _each(lambda acc, a_, v_: acc + _dot(_bf(a_), v_), accs, a, vb)
            runs = _each(lambda run, t: run + jnp.sum(t[2], axis=1, keepdims=True), runs, tiles)
            visited = _each(lambda n, f: n + f.astype(jnp.int32), visited, live)
            return jj + 1, accs, runs, visited

        start = (jnp.int32(0), [jnp.zeros((blk, HEAD_DIM), F32)] * P, [jnp.zeros((blk, 1), F32)] * P, [jnp.int32(0)] * P)
        _, accs, runs, visited = lax.while_loop(alive, step, start)
        for p in range(P):
            o_ref[p * blk:(p + 1) * blk, :] = accs[p]
            l_ref[0, p] = runs[p].T
            n_ref[0, p] = jnp.full((SUBLANES, LANES), visited[p].astype(F32))

    return pl.pallas_call(
        body, name=name, grid=(HEADS, T // (P * blk)),
        in_specs=[pl.BlockSpec((P * blk, HEAD_DIM), lambda h, i: (i, h)), pl.BlockSpec((T, HEAD_DIM), lambda h, i: (0, k_col + h)),
                  pl.BlockSpec((T, HEAD_DIM), lambda h, i: (0, v_col + h))],
        out_specs=[pl.BlockSpec((P * blk, HEAD_DIM), lambda h, i: (i, h)), pl.BlockSpec((1, P, 1, blk), lambda h, i: (h, i, 0, 0)),
                   pl.BlockSpec((1, P, SUBLANES, LANES), lambda h, i: (h, i, 0, 0))],
        out_shape=[jax.ShapeDtypeStruct((T, D_MODEL), F32), jax.ShapeDtypeStruct((HEADS, T // blk, 1, blk), F32),
                   jax.ShapeDtypeStruct((HEADS, T // blk, SUBLANES, LANES), F32)],
        compiler_params=_cp("parallel", "arbitrary"),
    )(q, k, v)


def sb_bwd(q, k, v, ltot, visited, do, *, k_col=0, v_col=0, name, exchange=None):
    T = q.shape[0]
    blk = _tile(T, SB_BLOCK)
    P = min(SB_QBLOCKS, T // blk)
    scale = HEAD_DIM ** -0.5

    def body(q_ref, k_ref, v_ref, l_ref, n_ref, do_ref, dq_ref, dk_ref, dv_ref):
        iq = [P * pl.program_id(1) + p for p in range(P)]
        count = [jnp.max(n_ref[0, p]).astype(jnp.int32) for p in range(P)]
        first = [i + 1 - n for i, n in zip(iq, count)]
        trips = count[0]
        for n in count[1:]:
            trips = jnp.maximum(trips, n)

        @pl.when(pl.program_id(1) == 0)
        def _():
            dk_ref[...] = jnp.zeros_like(dk_ref)
            dv_ref[...] = jnp.zeros_like(dv_ref)

        qb = [_bf(q_ref[p * blk:(p + 1) * blk, :] * scale) for p in range(P)]
        dob = [_bf(do_ref[p * blk:(p + 1) * blk, :]) for p in range(P)]
        ltot_ = [l_ref[0, p].T for p in range(P)]
        r_idx = lax.broadcasted_iota(jnp.int32, (blk, blk), 0)
        c_idx = lax.broadcasted_iota(jnp.int32, (blk, blk), 1)
        upto = _bf(jnp.where(r_idx <= c_idx, 1.0, 0.0))
        before = _bf(jnp.where(r_idx < c_idx, 1.0, 0.0))

        def step(t, carry):
            dqs, lpre, cpre = carry
            live = [f + t <= i for f, i in zip(first, iq)]
            js = [jnp.minimum(f + t, i) for f, i in zip(first, iq)]
            tiles = _each(lambda q_, i, j, f: _sb_tile(q_, k_ref, i, j, blk, f), qb, iq, js, live)
            parts = _each(lambda tl: _split_bf16(tl[2]), tiles)
            after = _each(lambda lt, lp, s: lt - (lp + _dot(s[0], upto) + _dot(s[1], upto)), ltot_, lpre, parts)
            ls = _each(lambda tl: tl[0] + tl[2], tiles)
            a = _each(lambda tl, ls_, af: jnp.where(tl[1], jnp.exp(ls_ + af), 0.0), tiles, ls, after)
            vb = _each(lambda j: _bf(v_ref[_sb_rows(j, blk), :]), js)
            p = _each(lambda a_, do_, v_: a_ * _dot_nt(do_, v_), a, dob, vb)
            pparts = _each(_split_bf16, p)
            left = _each(lambda cp, s: cp + _dot(s[0], before) + _dot(s[1], before), cpre, pparts)
            dzb = _each(lambda tl, p_, lf_, ls_: _bf(jnp.where(tl[1], p_ * jnp.exp(tl[2]) - lf_ * jnp.exp(ls_), 0.0)),
                        tiles, p, left, ls)
            dks = _each(lambda dz, q_: _dot_tn(dz, q_), dzb, qb)
            dvs = _each(lambda a_, do_: _dot_tn(_bf(a_), do_), a, dob)
            dqs = _each(lambda dq, dz, j: dq + _dot(dz, _bf(k_ref[_sb_rows(j, blk), :])), dqs, dzb, js)
            for j, dk, dv in zip(js, dks, dvs):
                dk_ref[_sb_rows(j, blk), :] += dk
                dv_ref[_sb_rows(j, blk), :] += dv
            lpre = _each(lambda lp, tl: lp + jnp.sum(tl[2], axis=1, keepdims=True), lpre, tiles)
            cpre = _each(lambda cp, p_: cp + jnp.sum(p_, axis=1, keepdims=True), cpre, p)
            return dqs, lpre, cpre

        zero = [jnp.zeros((blk, 1), F32)] * P
        dqs, _, _ = lax.fori_loop(0, trips, step, ([jnp.zeros((blk, HEAD_DIM), F32)] * P, zero, zero))
        for p in range(P):
            dq_ref[p * blk:(p + 1) * blk, :] = dqs[p] * scale

    full = lambda off: pl.BlockSpec((T, HEAD_DIM), lambda h, i: (0, off + h))
    tile = pl.BlockSpec((P * blk, HEAD_DIM), lambda h, i: (i, h))
    (dq, dk, dv), brought = hosted_call(
        body, name=name, grid=(HEADS, T // (P * blk)),
        in_specs=[tile, full(k_col), full(v_col), pl.BlockSpec((1, P, 1, blk), lambda h, i: (h, i, 0, 0)),
                  pl.BlockSpec((1, P, SUBLANES, LANES), lambda h, i: (h, i, 0, 0)), tile],
        out_specs=[tile, full(0), full(0)],
        out_shape=[jax.ShapeDtypeStruct((T, D_MODEL), F32)] * 3,
        scratch_shapes=[], args=(q, k, v, ltot, visited, do), exchange=exchange)
    return dq, dk, dv, brought


def sum_slots(slots, *, name):
    n, R, C = slots.shape
    tr = _tile(R, 256)

    def body(s_ref, o_ref):
        acc = s_ref[0].astype(F32)
        for k in range(1, n):
            acc = acc + s_ref[k].astype(F32)
        o_ref[...] = acc

    return pl.pallas_call(
        body, name=name, grid=(R // tr,),
        in_specs=[pl.BlockSpec((n, tr, C), lambda i: (0, i, 0))],
        out_specs=pl.BlockSpec((tr, C), lambda i: (i, 0)),
        out_shape=jax.ShapeDtypeStruct((R, C), F32),
        compiler_params=_cp("parallel"),
    )(slots)


def adamw(w, g_parts, m, v, *, name):
    R, C = w.shape
    tr = _tile(R, 256)
    n = len(g_parts)

    def body(*refs):
        w_ref, m_ref, v_ref = refs[0], refs[1 + n], refs[2 + n]
        g_ref, d_ref, nm_ref, nv_ref = refs[3 + n:]
        g = refs[1][...]
        for r in refs[2:1 + n]:
            g = g + r[...]
        m2 = ADAM_B1 * m_ref[...] + (1.0 - ADAM_B1) * g
        v2 = ADAM_B2 * v_ref[...] + (1.0 - ADAM_B2) * (g * g)
        m_hat = m2 / (1.0 - ADAM_B1 ** ADAM_STEP)
        v_hat = v2 / (1.0 - ADAM_B2 ** ADAM_STEP)
        g_ref[...] = g
        d_ref[...] = -ADAM_LR * (m_hat / (jnp.sqrt(v_hat) + ADAM_EPS) + ADAM_WD * w_ref[...])
        nm_ref[...] = m2
        nv_ref[...] = v2

    spec = pl.BlockSpec((tr, C), lambda i: (i, 0))
    return pl.pallas_call(
        body, name=name, grid=(R // tr,),
        in_specs=[spec] * (3 + n), out_specs=[spec] * 4,
        out_shape=[jax.ShapeDtypeStruct((R, C), F32)] * 4,
        compiler_params=_cp("parallel"),
    )(w, *g_parts, m, v)


CHIP_FLIPS = ((0, 1), (1, 0), (1, 1))


def _place():
    return lax.axis_index("x"), lax.axis_index("y"), lax.axis_index("c")


def _flip(v, f):
    return 1 - v if f else v


class ChipExchange:
    def __init__(self, arrays, scatter):
        self.arrays, self.scatter, self.n = list(arrays), scatter, len(arrays)
        lead = () if scatter else (N_CHIPS,)
        self.out_shape = [jax.ShapeDtypeStruct(lead + a.shape, a.dtype) for a in arrays]
        self.scratch = [pltpu.SemaphoreType.DMA((self.n, len(CHIP_FLIPS))), pltpu.SemaphoreType.DMA((self.n, len(CHIP_FLIPS))),
                        pltpu.SemaphoreType.DMA((self.n,))]

    def _copies(self, ins, outs, sems):
        send_sems, recv_sems, local_sems = sems
        x, y, c = _place()
        me = 2 * x + y
        local, sent, landing = [], [], []
        for k in range(self.n):
            local.append(pltpu.make_async_copy(ins[k].at[me] if self.scatter else ins[k], outs[k].at[me], local_sems.at[k]))
            for p, (fx, fy) in enumerate(CHIP_FLIPS):
                px, py = _flip(x, fx), _flip(y, fy)
                src = ins[k].at[2 * px + py] if self.scatter else ins[k]
                for dst, group in ((me, sent), (2 * px + py, landing)):
                    group.append(pltpu.make_async_remote_copy(src_ref=src, dst_ref=outs[k].at[dst], send_sem=send_sems.at[k, p],
                                                              recv_sem=recv_sems.at[k, p], device_id=(px, py, c), device_id_type=MESH))
        return local, sent, landing

    def start(self, ins, outs, sems):
        local, sent, _ = self._copies(ins, outs, sems)
        for cp in local + sent:
            cp.start()

    def finish(self, ins, outs, sems):
        local, _, landing = self._copies(ins, outs, sems)
        for cp in landing:
            cp.wait_send()
            cp.wait_recv()
        for cp in local:
            cp.wait()


def chip_exchange(arrays, *, scatter, name):
    ex = ChipExchange(arrays, scatter)
    n = ex.n

    def body(*refs):
        ins, outs, sems = refs[:n], refs[n:2 * n], refs[2 * n:]
        ex.start(ins, outs, sems)
        ex.finish(ins, outs, sems)

    return pl.pallas_call(body, name=name, in_specs=[ANY] * n, out_specs=[ANY] * n, out_shape=ex.out_shape,
                          scratch_shapes=ex.scratch)(*arrays)


def hosted_call(body, *, name, grid, in_specs, out_specs, out_shape, scratch_shapes, args, exchange=None):
    in_specs, out_specs, out_shape = list(in_specs), list(out_specs), list(out_shape)
    params = _cp(*["arbitrary"] * len(grid))
    if exchange is None:
        outs = pl.pallas_call(body, name=name, grid=grid, in_specs=in_specs, out_specs=out_specs, out_shape=out_shape,
                              scratch_shapes=list(scratch_shapes), compiler_params=params)(*args)
        return list(outs), []
    n, n_in, n_out, n_scr = exchange.n, len(in_specs), len(out_specs), len(scratch_shapes)

    def both(*refs):
        ins, t_ins = refs[:n_in], refs[n_in:n_in + n]
        outs, t_outs = refs[n_in + n:n_in + n + n_out], refs[n_in + n + n_out:n_in + 2 * n + n_out]
        scratch, sems = refs[n_in + 2 * n + n_out:n_in + 2 * n + n_out + n_scr], refs[n_in + 2 * n + n_out + n_scr:]
        first, last = True, True
        for axis, size in enumerate(grid):
            first = jnp.logical_and(first, pl.program_id(axis) == 0)
            last = jnp.logical_and(last, pl.program_id(axis) == size - 1)

        @pl.when(first)
        def _():
            exchange.start(t_ins, t_outs, sems)

        body(*ins, *outs, *scratch)

        @pl.when(last)
        def _():
            exchange.finish(t_ins, t_outs, sems)

    outs = pl.pallas_call(both, name=name, grid=grid, in_specs=in_specs + [ANY] * n, out_specs=out_specs + [ANY] * n,
                          out_shape=out_shape + exchange.out_shape, scratch_shapes=list(scratch_shapes) + exchange.scratch,
                          compiler_params=params)(*args, *exchange.arrays)
    return list(outs[:n_out]), list(outs[n_out:])


def sibling_swap(arrays, *, name):
    n = len(arrays)

    def body(*refs):
        ins, outs = refs[:n], refs[n:2 * n]
        send_sems, recv_sems = refs[2 * n:]
        x, y, c = _place()
        copies = [pltpu.make_async_remote_copy(src_ref=ins[k], dst_ref=outs[k], send_sem=send_sems.at[k], recv_sem=recv_sems.at[k],
                                               device_id=(x, y, 1 - c), device_id_type=MESH) for k in range(n)]
        for cp in copies:
            cp.start()
        for cp in copies:
            cp.wait_send()
            cp.wait_recv()

    return pl.pallas_call(
        body, name=name, in_specs=[ANY] * n, out_specs=[ANY] * n,
        out_shape=[jax.ShapeDtypeStruct(a.shape, a.dtype) for a in arrays],
        scratch_shapes=[pltpu.SemaphoreType.DMA((n,)), pltpu.SemaphoreType.DMA((n,))],
    )(*arrays)


DEVICE_FLIPS = tuple((fx, fy, fc) for fx in (0, 1) for fy in (0, 1) for fc in (0, 1) if fx or fy or fc)


def all_gather_devices(a, *, name):
    def body(a_ref, o_ref, send_sems, recv_sems, local_sem):
        x, y, c = _place()
        me = 4 * x + 2 * y + c
        local = pltpu.make_async_copy(a_ref, o_ref.at[me], local_sem)
        local.start()
        for p, (fx, fy, fc) in enumerate(DEVICE_FLIPS):
            peer = (_flip(x, fx), _flip(y, fy), _flip(c, fc))
            pltpu.make_async_remote_copy(src_ref=a_ref, dst_ref=o_ref.at[me], send_sem=send_sems.at[p], recv_sem=recv_sems.at[p],
                                         device_id=peer, device_id_type=MESH).start()
        for p, (fx, fy, fc) in enumerate(DEVICE_FLIPS):
            px, py, pc = _flip(x, fx), _flip(y, fy), _flip(c, fc)
            landing = pltpu.make_async_remote_copy(src_ref=a_ref, dst_ref=o_ref.at[4 * px + 2 * py + pc], send_sem=send_sems.at[p],
                                                   recv_sem=recv_sems.at[p], device_id=(px, py, pc), device_id_type=MESH)
            landing.wait_send()
            landing.wait_recv()
        local.wait()

    return pl.pallas_call(
        body, name=name, in_specs=[ANY], out_specs=ANY,
        out_shape=jax.ShapeDtypeStruct((N_DEV,) + a.shape, a.dtype),
        scratch_shapes=[pltpu.SemaphoreType.DMA((len(DEVICE_FLIPS),)), pltpu.SemaphoreType.DMA((len(DEVICE_FLIPS),)),
                        pltpu.SemaphoreType.DMA(())],
    )(a)


def _row(v):
    return v.reshape(1, -1)


def _pad_rows(w):
    return jnp.pad(w, ((0, SUBLANES - w.shape[0]), (0, 0)))


def _pad_lanes(v):
    return jnp.pad(v.reshape(1, -1), ((0, 0), (0, LANES - v.shape[-1])))


def _head_layouts(gates):
    T = gates.shape[0]
    rows = lambda cols: cols.T.reshape(HEADS, T // GDN_CHUNK, 1, GDN_CHUNK)
    return rows(gates[:, :HEADS]), rows(gates[:, HEADS:2 * HEADS])


def _ffn_fwd(x, W, l, plan):
    u, h = norm_matmul(x, _row(W["ffn_norm"][l]), W["ffn_w_up"][l], name=f"ffn{l}_up")
    conv = _pad_rows(W["ffn_conv"][l])
    act, brought = ffn_act_fwd(u, conv, name=f"ffn{l}_act", exchange=plan.fetch(f"ffn{l}_act"))
    plan.arrived(f"ffn{l}_act", brought, W)
    y = matmul_residual(act, W["ffn_w_down"][l], x, name=f"ffn{l}_down")
    return y, (x, h, u, conv, act)


def _ffn_bwd(dx, saved, W, l, G):
    x, h, u, conv, act = saved
    dact = matmul_nt(dx, W["ffn_w_down"][l], name=f"ffn{l}_down_dx", out_dtype=BF16)
    G["ffn_w_down"][l] = matmul_tn(act, dx, name=f"ffn{l}_down_dw")
    dug, duu, dwg, dwu = ffn_act_bwd(u, conv, dact, name=f"ffn{l}_act_bwd")
    G["ffn_conv"][l] = jnp.concatenate([dwg, dwu], axis=1)[:FFN_CONV]
    G["ffn_w_up"][l] = jnp.concatenate([matmul_tn(h, dug, name=f"ffn{l}_gate_dw"), matmul_tn(h, duu, name=f"ffn{l}_up_dw")], axis=1)
    dx, dgain = matmul_nt_normbwd([(dug, 0), (duu, 1)], W["ffn_w_up"][l], x, _row(W["ffn_norm"][l]), dx, name=f"ffn{l}_up_dx")
    G["ffn_norm"][l] = dgain.sum(0)
    return dx


class NoTraffic:
    def fetch(self, host):
        return None

    def arrived(self, host, brought, W):
        pass

    def flush(self, G):
        return None

    def landed(self, brought):
        pass


def _gdn_fwd(x, W, l, plan):
    proj, h = norm_matmul(x, _row(W["a_norm"][l]), W["a_w_in"][l], name=f"gdn{l}_in")
    conv = _pad_rows(W["a_conv"][l])
    a_log, dt_bias = _pad_lanes(W["a_log"][l]), _pad_lanes(W["a_dt_bias"][l])
    qkv, brought = gdn_conv_fwd(proj, conv, name=f"gdn{l}_conv", exchange=plan.fetch(f"gdn{l}_conv"))
    plan.arrived(f"gdn{l}_conv", brought, W)
    heads = _head_layouts(gates_fwd(proj, a_log, dt_bias, name=f"gdn{l}_gates"))
    o, states, brought = gdn_fwd(qkv, *heads, name=f"gdn{l}_rule", exchange=plan.fetch(f"gdn{l}_rule"))
    plan.arrived(f"gdn{l}_rule", brought, W)
    gain = _row(W["a_out_norm"][l])
    on = head_norm_fwd(o, gain, proj, z_col=Z_BLOCK, name=f"gdn{l}_outnorm", out_dtype=BF16)
    y = matmul_residual(on, W["a_w_out"][l], x, name=f"gdn{l}_out")
    return y, (x, h, proj, conv, a_log, dt_bias, qkv, heads, states, o, gain, on)


def _gdn_bwd(dx, saved, W, l, G, plan):
    x, h, proj, conv, a_log, dt_bias, qkv, heads, states, o, gain, on = saved
    T = x.shape[0]
    don = matmul_nt(dx, W["a_w_out"][l], name=f"gdn{l}_out_dx")
    G["a_w_out"][l] = matmul_tn(on, dx, name=f"gdn{l}_out_dw")
    do, dz, dgain = head_norm_bwd(o, gain, [don], proj, z_col=Z_BLOCK, name=f"gdn{l}_outnorm_bwd")
    G["a_out_norm"][l] = dgain.reshape(SUBLANES, HEADS, HEAD_DIM).sum((0, 1))
    dqkv, da, db, brought = gdn_bwd(qkv, *heads, states, do, name=f"gdn{l}_rule_bwd", exchange=plan.flush(G))
    plan.landed(brought)
    dqkv, dconv = gdn_conv_bwd(proj, conv, dqkv, name=f"gdn{l}_conv_bwd")
    G["a_conv"][l] = dconv[:GDN_CONV]
    dgate = jnp.concatenate([da.reshape(HEADS, T).T, db.reshape(HEADS, T).T, jnp.zeros((T, LANES - 2 * HEADS), F32)], axis=1)
    dab, dal, ddt = gates_bwd(proj, a_log, dt_bias, dgate, name=f"gdn{l}_gates_bwd")
    G["a_log"][l] = dal.sum(0)[:HEADS]
    G["a_dt_bias"][l] = ddt.sum(0)[:HEADS]
    parts = [(dqkv, 0), (dz, Z_BLOCK * LANES // D_MODEL), (dab, AB_BLOCK)]
    G["a_w_in"][l] = jnp.concatenate([matmul_tn(h, d, name=f"gdn{l}_in_dw{i}") for i, (d, _) in enumerate(parts)], axis=1)
    last = plan.flush(G) if l == 0 else None
    out = matmul_nt_normbwd(parts, W["a_w_in"][l], x, _row(W["a_norm"][l]), dx, name=f"gdn{l}_in_dx", exchange=last)
    dx, dgain = out[:2]
    if last is not None:
        plan.landed(out[2])
    G["a_norm"][l] = dgain.sum(0)
    return dx


def _sb_fwd(x, kn, kv, W, j):
    qp, h = norm_matmul(x, _row(W["b_norm"][j]), W["b_w_q"][j], name=f"sb{j}_q")
    gain = _row(W["q_norm"][j])
    q = head_norm_fwd(qp, gain, name=f"sb{j}_qnorm")
    o, ltot, visited = sb_fwd(q, kn, kv, v_col=HEADS, name=f"sb{j}_attn")
    y = matmul_residual(o, W["b_w_out"][j], x, name=f"sb{j}_out")
    return y, (x, h, qp, gain, q, o, ltot, visited)


def _sb_bwd(dx, saved, kn, kv, W, j, G, plan):
    x, h, qp, gain, q, o, ltot, visited = saved
    do = matmul_nt(dx, W["b_w_out"][j], name=f"sb{j}_out_dx")
    G["b_w_out"][j] = matmul_tn(o, dx, name=f"sb{j}_out_dw")
    dq, dk, dv, brought = sb_bwd(q, kn, kv, ltot, visited, do, v_col=HEADS, name=f"sb{j}_attn_bwd", exchange=plan.flush(G))
    plan.landed(brought)
    dqp, dgain = head_norm_bwd(qp, gain, [dq], name=f"sb{j}_qnorm_bwd", dx_dtype=BF16)
    G["q_norm"][j] = dgain.reshape(SUBLANES, HEADS, HEAD_DIM).sum((0, 1))
    G["b_w_q"][j] = matmul_tn(h, dqp, name=f"sb{j}_q_dw")
    dx, dgain = matmul_nt_normbwd([(dqp, 0)], W["b_w_q"][j], x, _row(W["b_norm"][j]), dx, name=f"sb{j}_q_dx")
    G["b_norm"][j] = dgain.sum(0)
    return dx, dk, dv


def local_step(x, target, W, plan=None):
    plan = plan or NoTraffic()
    G = {k: [None] * (N_A if k.startswith("a_") else N_B if k in ("b_norm", "b_w_q", "q_norm", "b_w_out") else DEPTH)
         for k in ("a_norm", "a_w_in", "a_conv", "a_log", "a_dt_bias", "a_out_norm", "a_w_out", "b_norm", "b_w_q", "q_norm",
                   "b_w_out", "ffn_norm", "ffn_w_up", "ffn_conv", "ffn_w_down")}
    G["w_kv"] = [None]
    tape = []
    for l in range(N_A):
        x, s_mix = _gdn_fwd(x, W, l, plan)
        x, s_ffn = _ffn_fwd(x, W, l, plan)
        tape.append((s_mix, s_ffn))
    x_kv = x
    kv, h_kv = norm_matmul(x, _row(W["kv_norm"]), W["w_kv"], name="kv_proj")
    k_gain = _row(W["k_norm"])
    kn = head_norm_fwd(kv, k_gain, name="k_norm")
    for j in range(N_B):
        x, s_mix = _sb_fwd(x, kn, kv, W, j)
        x, s_ffn = _ffn_fwd(x, W, N_A + j, plan)
        tape.append((s_mix, s_ffn))
    dx, loss = loss_fwd(x, target, name="loss")

    dks, dvs = [], []
    for j in reversed(range(N_B)):
        s_mix, s_ffn = tape[N_A + j]
        dx = _ffn_bwd(dx, s_ffn, W, N_A + j, G)
        dx, dk, dv = _sb_bwd(dx, s_mix, kn, kv, W, j, G, plan)
        dks.append(dk)
        dvs.append(dv)
    dkp, dgain = head_norm_bwd(kv, k_gain, dks, name="k_norm_bwd", dx_dtype=BF16)
    G["k_norm"] = dgain.reshape(SUBLANES, HEADS, HEAD_DIM).sum((0, 1))
    dv = dvs[0] + dvs[1]
    G["w_kv"][0] = jnp.concatenate([matmul_tn(h_kv, dkp, name="k_proj_dw"), matmul_tn(h_kv, dv, name="v_proj_dw")], axis=1)
    dx, dgain = matmul_nt_normbwd([(dkp, 0), (dv, 1)], W["w_kv"], x_kv, _row(W["kv_norm"]), dx, name="kv_proj_dx")
    G["kv_norm"] = dgain.sum(0)
    for l in reversed(range(N_A)):
        s_mix, s_ffn = tape[l]
        dx = _ffn_bwd(dx, s_ffn, W, l, G)
        dx = _gdn_bwd(dx, s_mix, W, l, G, plan)
    return loss, dx, G


MATRICES = {"a_w_in": 2, "a_w_out": 1, "w_kv": 1, "b_w_q": 1, "b_w_out": 1, "ffn_w_up": 2, "ffn_w_down": 1}
SMALL_SHARDED = {"a_norm": 1, "a_conv": 2, "ffn_conv": 2}
SMALL_REPLICATED = ("a_log", "a_dt_bias", "a_out_norm", "kv_norm", "k_norm", "b_norm", "q_norm", "ffn_norm")
WEIGHT_ORDER = ("a_norm", "a_w_in", "a_conv", "a_log", "a_dt_bias", "a_out_norm", "a_w_out", "kv_norm", "w_kv", "k_norm",
                "b_norm", "b_w_q", "q_norm", "b_w_out", "ffn_norm", "ffn_w_up", "ffn_conv", "ffn_w_down")
SMALL_ORDER = tuple(n for n in WEIGHT_ORDER if n not in MATRICES)
PACK_QUANTUM = SUBLANES * LANES


def _unshard(g, axis):
    g = jnp.moveaxis(g, 0, axis)
    return g.reshape(g.shape[:axis] + (g.shape[axis] * g.shape[axis + 1],) + g.shape[axis + 2:])


def _shards(full, axis):
    n = full.shape[axis] // N_CHIPS
    return jnp.moveaxis(full.reshape(full.shape[:axis] + (N_CHIPS, n) + full.shape[axis + 1:]), axis, 0)


def _pack(arrays):
    parts = []
    for a in arrays:
        flat = a.reshape(-1)
        parts.append(jnp.pad(flat, (0, -flat.shape[0] % PACK_QUANTUM)).reshape(-1, LANES))
    return jnp.concatenate(parts, axis=0)


def _unpack(buf, shapes):
    out, row = [], 0
    for s in shapes:
        size = math.prod(s)
        rows = -(-size // PACK_QUANTUM) * SUBLANES
        out.append(buf[row:row + rows].reshape(-1)[:size].reshape(s))
        row += rows
    return out


def _stack(per_layer):
    return jnp.stack(per_layer) if isinstance(per_layer, list) else per_layer


FETCH_BESIDE = {
    "gdn0_conv": (("ffn_w_up", 0, 2),),
    "gdn0_rule": (("ffn_w_down", 0, 2), ("w_kv", 0, None), ("b_w_q", 0, 2), ("b_w_out", 0, 2)),
    "ffn0_act": (("a_w_in", 1, 1), ("a_w_out", 1, 1)),
    "gdn1_conv": (("ffn_w_up", 2, 2),),
    "gdn1_rule": (("ffn_w_down", 2, 2),),
}
FETCH_FIRST = (("a_w_in", 0, 1), ("a_w_out", 0, 1))


class Traffic:
    def __init__(self, local):
        self.local = local
        self.shipped, self.received = [], {}

    @staticmethod
    def _assemble(W, name, first, layers, gathered):
        whole = _unshard(gathered, MATRICES[name])
        if name == "a_w_in":
            whole = jnp.pad(whole, ((0, 0), (0, 0), (0, W_IN_PAD - W_IN_COLS)))
        if layers is None:
            W[name] = whole
        else:
            W.setdefault(name, {}).update({first + i: whole[i] for i in range(layers)})

    def _shards_of(self, wanted):
        return [(self.local[name] if layers is None else self.local[name][first:first + layers]).astype(BF16)
                for name, first, layers in wanted]

    def fetch_first(self, extra):
        brought = chip_exchange(self._shards_of(FETCH_FIRST) + list(extra), scatter=False, name="gather_first")
        W = {}
        for (name, first, layers), g in zip(FETCH_FIRST, brought):
            self._assemble(W, name, first, layers, g)
        return W, brought[len(FETCH_FIRST):]

    def fetch(self, host):
        return ChipExchange(self._shards_of(FETCH_BESIDE[host]), scatter=False) if host in FETCH_BESIDE else None

    def arrived(self, host, brought, W):
        for (name, first, layers), g in zip(FETCH_BESIDE.get(host, ()), brought):
            self._assemble(W, name, first, layers, g)

    def _ready(self, G):
        out = []
        for name, axis in MATRICES.items():
            for layer, g in enumerate(G[name]):
                if g is None or (name, layer) in self.shipped:
                    continue
                g = g[:, :W_IN_COLS] if name == "a_w_in" else g
                piece = _shards(g, axis - (0 if name == "w_kv" else 1)).astype(BF16)
                out.append((name, layer, piece.reshape(N_CHIPS, -1, piece.shape[-1])))
        return out

    def flush(self, G):
        ready = self._ready(G)
        self.in_flight = [r[:2] for r in ready]
        self.shipped += self.in_flight
        return ChipExchange([r[2] for r in ready], scatter=True) if ready else None

    def landed(self, brought):
        self.received.update(zip(self.in_flight, brought))
        self.in_flight = []

    def flush_last(self, G):
        ready = self._ready(G)
        if ready:
            self.shipped += [r[:2] for r in ready]
            brought = chip_exchange([r[2] for r in ready], scatter=True, name="scatter_last")
            self.received.update(zip([r[:2] for r in ready], brought))

    def my_sums(self):
        sums = {}
        for name in MATRICES:
            layers = sorted(l for n, l in self.received if n == name)
            parts = [sum_slots(self.received[(name, l)], name=f"sum_{name}{l}") for l in layers]
            sums[name] = parts[0] if len(parts) == 1 else jnp.concatenate(parts, axis=0)
        return sums


def _as_2d(a):
    return a.reshape(-1, a.shape[-1])


def kernel(x, a_norm, a_w_in, a_conv, a_log, a_dt_bias, a_out_norm, a_w_out, kv_norm, w_kv, k_norm, b_norm, b_w_q, q_norm, b_w_out, ffn_norm, ffn_w_up, ffn_conv, ffn_w_down, loss_target, m_a_norm, m_a_w_in, m_a_conv, m_a_log, m_a_dt_bias, m_a_out_norm, m_a_w_out, m_kv_norm, m_w_kv, m_k_norm, m_b_norm, m_b_w_q, m_q_norm, m_b_w_out, m_ffn_norm, m_ffn_w_up, m_ffn_conv, m_ffn_w_down, v_a_norm, v_a_w_in, v_a_conv, v_a_log, v_a_dt_bias, v_a_out_norm, v_a_w_out, v_kv_norm, v_w_kv, v_k_norm, v_b_norm, v_b_w_q, v_q_norm, v_b_w_out, v_ffn_norm, v_ffn_w_up, v_ffn_conv, v_ffn_w_down):
    local = dict(a_norm=a_norm, a_w_in=a_w_in, a_conv=a_conv, a_log=a_log, a_dt_bias=a_dt_bias, a_out_norm=a_out_norm,
                 a_w_out=a_w_out, kv_norm=kv_norm, w_kv=w_kv, k_norm=k_norm, b_norm=b_norm, b_w_q=b_w_q, q_norm=q_norm,
                 b_w_out=b_w_out, ffn_norm=ffn_norm, ffn_w_up=ffn_w_up, ffn_conv=ffn_conv, ffn_w_down=ffn_w_down)
    mom = dict(a_norm=m_a_norm, a_w_in=m_a_w_in, a_conv=m_a_conv, a_log=m_a_log, a_dt_bias=m_a_dt_bias, a_out_norm=m_a_out_norm,
               a_w_out=m_a_w_out, kv_norm=m_kv_norm, w_kv=m_w_kv, k_norm=m_k_norm, b_norm=m_b_norm, b_w_q=m_b_w_q, q_norm=m_q_norm,
               b_w_out=m_b_w_out, ffn_norm=m_ffn_norm, ffn_w_up=m_ffn_w_up, ffn_conv=m_ffn_conv, ffn_w_down=m_ffn_w_down)
    var = dict(a_norm=v_a_norm, a_w_in=v_a_w_in, a_conv=v_a_conv, a_log=v_a_log, a_dt_bias=v_a_dt_bias, a_out_norm=v_a_out_norm,
               a_w_out=v_a_w_out, kv_norm=v_kv_norm, w_kv=v_w_kv, k_norm=v_k_norm, b_norm=v_b_norm, b_w_q=v_b_w_q, q_norm=v_q_norm,
               b_w_out=v_b_w_out, ffn_norm=v_ffn_norm, ffn_w_up=v_ffn_w_up, ffn_conv=v_ffn_conv, ffn_w_down=v_ffn_w_down)
    chip = 2 * lax.axis_index("x") + lax.axis_index("y")

    mats = list(MATRICES)
    small_sharded = list(SMALL_SHARDED)
    traffic = Traffic(local)
    W, (vectors,) = traffic.fetch_first([_pack([local[n] for n in small_sharded])])
    W.update({n: local[n] for n in SMALL_REPLICATED})
    shard_shapes = [local[n].shape for n in small_sharded]
    per_chip = [_unpack(vectors[j], shard_shapes) for j in range(N_CHIPS)]
    for i, n in enumerate(small_sharded):
        W[n] = _unshard(jnp.stack([per_chip[j][i] for j in range(N_CHIPS)]), SMALL_SHARDED[n])

    T = x.shape[1]
    loss_part, dx, G = local_step(x.reshape(T, D_MODEL), loss_target.reshape(T, D_MODEL), W, traffic)

    traffic.flush_last(G)
    sums = traffic.my_sums()
    mine = [sums[n] for n in mats]
    theirs = sibling_swap(mine, name="swap_grads")

    small_full = {n: _stack(G[n]) for n in SMALL_ORDER}
    packed = _pack([small_full[n] for n in SMALL_ORDER] + [loss_part])
    total = sum_slots(all_gather_devices(packed, name="gather_small"), name="sum_small")
    small_shapes = [small_full[n].shape for n in SMALL_ORDER] + [loss_part.shape]
    summed = dict(zip(SMALL_ORDER + ("loss",), _unpack(total, small_shapes)))
    loss = jnp.sum(summed.pop("loss"))
    for n, axis in SMALL_SHARDED.items():
        size = local[n].shape[axis]
        summed[n] = lax.dynamic_slice_in_dim(summed[n], chip * size, size, axis)

    grads, deltas, new_m, new_v = {}, {}, {}, {}
    for n, p_mine, p_theirs in zip(mats, mine, theirs):
        outs = adamw(_as_2d(local[n]), [p_mine, p_theirs], _as_2d(mom[n]), _as_2d(var[n]), name=f"adamw_{n}")
        grads[n], deltas[n], new_m[n], new_v[n] = [o.reshape(local[n].shape) for o in outs]
    small_local_shapes = [local[n].shape for n in SMALL_ORDER]
    outs = adamw(_pack([local[n] for n in SMALL_ORDER]), [_pack([summed[n] for n in SMALL_ORDER])],
                 _pack([mom[n] for n in SMALL_ORDER]), _pack([var[n] for n in SMALL_ORDER]), name="adamw_small")
    for d, o in zip((grads, deltas, new_m, new_v), outs):
        d.update(zip(SMALL_ORDER, _unpack(o, small_local_shapes)))

    return (loss, dx.reshape(x.shape), *[grads[n] for n in WEIGHT_ORDER], *[deltas[n] for n in WEIGHT_ORDER],
            *[new_m[n] for n in WEIGHT_ORDER], *[new_v[n] for n in WEIGHT_ORDER])
```

```python
import math

import jax
import jax.numpy as jnp
from jax import lax
from jax.experimental import pallas as pl
from jax.experimental.pallas import tpu as pltpu

F32 = jnp.float32
BF16 = jnp.bfloat16

D_MODEL = 1024
HEADS = 8
HEAD_DIM = 128
GDN_CONV = 4
GDN_CHUNK = 64
D_FF = 2816
FFN_CONV = 3
EPS = 1e-6
N_A = 2
N_B = 2
DEPTH = N_A + N_B
W_IN_COLS = 4 * D_MODEL + 2 * HEADS
W_IN_PAD = 4 * D_MODEL + 128
Z_BLOCK = 3 * D_MODEL // 128
AB_BLOCK = 4 * D_MODEL // 128

ADAM_LR = 0.001
ADAM_B1 = 0.9
ADAM_B2 = 0.999
ADAM_EPS = 1e-08
ADAM_WD = 0.01
ADAM_STEP = 10

LANES = 128
SUBLANES = 8
VMEM_LIMIT = 56 * 1024 * 1024
HALO = SUBLANES
N_CHIPS = 4
N_DEV = 8

HI = lax.Precision.HIGHEST
MESH = pl.DeviceIdType.MESH
ANY = pl.BlockSpec(memory_space=pl.ANY)


def _cp(*sem):
    return pltpu.CompilerParams(dimension_semantics=sem, vmem_limit_bytes=VMEM_LIMIT)


def _tile(n, want, align=SUBLANES):
    t = (min(n, want) // align) * align
    while t > 0 and n % t:
        t -= align
    return t if t > 0 else n


def _dot(a, b, precision=None):
    return jnp.dot(a, b, preferred_element_type=F32, precision=precision)


def _dot_nt(a, b, precision=None):
    return lax.dot_general(a, b, (((1,), (1,)), ((), ())), preferred_element_type=F32, precision=precision)


def _dot_tn(a, b, precision=None):
    return lax.dot_general(a, b, (((0,), (0,)), ((), ())), preferred_element_type=F32, precision=precision)


def _bf(x):
    return x.astype(BF16)


def _sigmoid(x):
    return 0.5 * jnp.tanh(0.5 * x) + 0.5


def _softplus(x):
    return jnp.maximum(x, 0.0) + jnp.log(1.0 + jnp.exp(-jnp.abs(x)))


def _silu(x):
    return x * _sigmoid(x)


def _silu_and_grad(x):
    s = _sigmoid(x)
    return x * s, s * (1.0 + x * (1.0 - s))


def _fold_rows(v):
    return jnp.sum(v.reshape(v.shape[0] // SUBLANES, SUBLANES, v.shape[1]), axis=0)


def _accumulate(ref, value, axis):
    @pl.when(pl.program_id(axis) == 0)
    def _():
        ref[...] = jnp.zeros_like(ref)
    ref[...] += value


TILE_BUDGET = 44 * 1024 * 1024


def _rows_that_fit(T, fixed_bytes, row_bytes, want=1024):
    tm = _tile(T, want)
    while tm > SUBLANES and 2 * (fixed_bytes + tm * row_bytes) > TILE_BUDGET:
        tm //= 2
    return tm


def norm_matmul(x, gain, w, *, name):
    T, D = x.shape
    N = w.shape[1]
    tn = N
    tm = _rows_that_fit(T, D * N * 2, D * 4 + D * 2 + N * 4)

    def body(x_ref, g_ref, w_ref, y_ref, h_ref):
        @pl.when(pl.program_id(1) == 0)
        def _():
            xf = x_ref[...]
            r = lax.rsqrt(jnp.mean(xf * xf, axis=-1, keepdims=True) + EPS)
            h_ref[...] = _bf(xf * r * g_ref[...])
        y_ref[...] = _dot(h_ref[...], w_ref[...])

    return pl.pallas_call(
        body, name=name, grid=(T // tm, N // tn),
        in_specs=[pl.BlockSpec((tm, D), lambda i, j: (i, 0)), pl.BlockSpec((1, D), lambda i, j: (0, 0)),
                  pl.BlockSpec((D, tn), lambda i, j: (0, j))],
        out_specs=[pl.BlockSpec((tm, tn), lambda i, j: (i, j)), pl.BlockSpec((tm, D), lambda i, j: (i, 0))],
        out_shape=[jax.ShapeDtypeStruct((T, N), F32), jax.ShapeDtypeStruct((T, D), BF16)],
        compiler_params=_cp("parallel", "arbitrary"),
    )(x, gain, w)


def matmul_residual(a, w, res, *, name):
    T, K = a.shape
    N = w.shape[1]
    tn = N
    tm = _rows_that_fit(T, K * N * 2, K * a.dtype.itemsize + 2 * N * 4)

    def body(a_ref, w_ref, r_ref, o_ref):
        o_ref[...] = r_ref[...] + _dot(_bf(a_ref[...]), w_ref[...])

    return pl.pallas_call(
        body, name=name, grid=(T // tm, N // tn),
        in_specs=[pl.BlockSpec((tm, K), lambda i, j: (i, 0)), pl.BlockSpec((K, tn), lambda i, j: (0, j)),
                  pl.BlockSpec((tm, tn), lambda i, j: (i, j))],
        out_specs=pl.BlockSpec((tm, tn), lambda i, j: (i, j)),
        out_shape=jax.ShapeDtypeStruct((T, N), F32),
        compiler_params=_cp("parallel", "parallel"),
    )(a, w, res)


def matmul_nt(dy, w, *, name, out_dtype=F32):
    T, N = dy.shape
    K = w.shape[0]
    tk = K
    tm = _rows_that_fit(T, K * N * 2, N * dy.dtype.itemsize + K * jnp.dtype(out_dtype).itemsize)

    def body(dy_ref, w_ref, o_ref):
        o_ref[...] = _dot_nt(_bf(dy_ref[...]), w_ref[...]).astype(out_dtype)

    return pl.pallas_call(
        body, name=name, grid=(T // tm, K // tk),
        in_specs=[pl.BlockSpec((tm, N), lambda i, j: (i, 0)), pl.BlockSpec((tk, N), lambda i, j: (j, 0))],
        out_specs=pl.BlockSpec((tm, tk), lambda i, j: (i, j)),
        out_shape=jax.ShapeDtypeStruct((T, K), out_dtype),
        compiler_params=_cp("parallel", "parallel"),
    )(dy, w)


def matmul_nt_normbwd(parts, w, x, gain, dres, *, name, exchange=None):
    T = x.shape[0]
    D = w.shape[0]
    tm = _tile(T, 256)
    n = len(parts)

    def body(*refs):
        x_ref, g_ref, dr_ref, dx_ref, dg_ref = refs[2 * n:]
        dh = _dot_nt(_bf(refs[0][...]), refs[n][...])
        for i in range(1, n):
            dh = dh + _dot_nt(_bf(refs[i][...]), refs[n + i][...])
        xf = x_ref[...]
        r = lax.rsqrt(jnp.mean(xf * xf, axis=-1, keepdims=True) + EPS)
        xh = xf * r
        dxh = dh * g_ref[...]
        dx_ref[...] = dr_ref[...] + r * (dxh - xh * jnp.mean(dxh * xh, axis=-1, keepdims=True))
        _accumulate(dg_ref, _fold_rows(dh * xh), 0)

    dy_specs = [pl.BlockSpec((tm, dy.shape[1]), lambda i: (i, 0)) for dy, _ in parts]
    w_specs = [pl.BlockSpec((D, dy.shape[1]), lambda i, b=b: (0, b)) for dy, b in parts]
    (dx, dgain), brought = hosted_call(
        body, name=name, grid=(T // tm,),
        in_specs=dy_specs + w_specs + [pl.BlockSpec((tm, D), lambda i: (i, 0)), pl.BlockSpec((1, D), lambda i: (0, 0)),
                                       pl.BlockSpec((tm, D), lambda i: (i, 0))],
        out_specs=[pl.BlockSpec((tm, D), lambda i: (i, 0)), pl.BlockSpec((SUBLANES, D), lambda i: (0, 0))],
        out_shape=[jax.ShapeDtypeStruct((T, D), F32), jax.ShapeDtypeStruct((SUBLANES, D), F32)],
        scratch_shapes=[], args=(*[dy for dy, _ in parts], *([w] * n), x, gain, dres), exchange=exchange)
    return (dx, dgain, brought) if exchange is not None else (dx, dgain)


def matmul_tn(a, dys, *, name):
    dys = dys if isinstance(dys, (list, tuple)) else [dys]
    T, K = a.shape
    N = dys[0].shape[1]
    tk = _tile(K, 1408, LANES)
    tn = _tile(N, 1408 if N <= 2816 else 512, LANES)
    tm = _tile(T, (4096 if tn <= 512 else 2048) // len(dys))

    def body(a_ref, *refs):
        dy = refs[0][...]
        for r in refs[1:-1]:
            dy = dy + r[...]
        _accumulate(refs[-1], _dot_tn(_bf(a_ref[...]), _bf(dy)), 2)

    return pl.pallas_call(
        body, name=name, grid=(K // tk, N // tn, T // tm),
        in_specs=[pl.BlockSpec((tm, tk), lambda i, j, t: (t, i))] + [pl.BlockSpec((tm, tn), lambda i, j, t: (t, j))] * len(dys),
        out_specs=pl.BlockSpec((tk, tn), lambda i, j, t: (i, j)),
        out_shape=jax.ShapeDtypeStruct((K, N), F32),
        compiler_params=_cp("parallel", "parallel", "arbitrary"),
    )(a, *dys)


def _halo_specs(T, tt, tc, col):
    per = tt // HALO
    last = T // HALO - 1
    return [pl.BlockSpec((HALO, tc), lambda j, i: (jnp.maximum(i * per - 1, 0), col(j))),
            pl.BlockSpec((tt, tc), lambda j, i: (i, col(j))),
            pl.BlockSpec((HALO, tc), lambda j, i: (jnp.minimum((i + 1) * per, last), col(j)))]


def _extend(prev_ref, cur_ref, next_ref, nt):
    i = pl.program_id(1)
    p = jnp.where(i > 0, prev_ref[...].astype(F32), 0.0)
    q = jnp.where(i < nt - 1, next_ref[...].astype(F32), 0.0)
    return jnp.concatenate([p, cur_ref[...].astype(F32), q], axis=0)


def _rows_before(e, s):
    return e if s == 0 else pltpu.roll(e, s, 0)


def _rows_after(e, s):
    return e if s == 0 else pltpu.roll(e, e.shape[0] - s, 0)


def _causal_conv(e, w, taps):
    y = w[taps - 1:taps, :] * e
    for s in range(1, taps):
        y = y + w[taps - 1 - s:taps - s, :] * _rows_before(e, s)
    return y


def _causal_conv_bwd(e, dc, w, taps, tt):
    lo, hi = HALO, HALO + tt
    dx = w[taps - 1:taps, :] * dc
    dws = [None] * taps
    dws[taps - 1] = jnp.sum((e * dc)[lo:hi], axis=0, keepdims=True)
    for s in range(1, taps):
        dx = dx + w[taps - 1 - s:taps - s, :] * _rows_after(dc, s)
        dws[taps - 1 - s] = jnp.sum((_rows_before(e, s) * dc)[lo:hi], axis=0, keepdims=True)
    dw = jnp.concatenate(dws + [jnp.zeros((SUBLANES - taps, e.shape[1]), F32)], axis=0)
    return dx[lo:hi], dw


def _qkv_kind(col_block):
    return (col_block >= HEADS).astype(jnp.int32) + (col_block >= 2 * HEADS).astype(jnp.int32)


def _l2norm_scale(kind):
    return jnp.where(kind == 0, HEAD_DIM ** -0.5, 1.0)


def gdn_conv_fwd(proj, conv_w, *, name, exchange=None):
    T = proj.shape[0]
    tt, tc = _tile(T, 512), HEAD_DIM
    nt = T // tt

    def body(p_ref, c_ref, n_ref, w_ref, o_ref):
        kind = _qkv_kind(pl.program_id(0))
        e = _extend(p_ref, c_ref, n_ref, nt)
        s = _silu(_causal_conv(e, w_ref[...], GDN_CONV))[HALO:HALO + tt]
        r = lax.rsqrt(jnp.sum(s * s, axis=-1, keepdims=True) + EPS) * _l2norm_scale(kind)
        o_ref[...] = jnp.where(kind == 2, s, s * r)

    (qkv,), brought = hosted_call(
        body, name=name, grid=(3 * HEADS, nt),
        in_specs=_halo_specs(T, tt, tc, lambda j: j) + [pl.BlockSpec((SUBLANES, tc), lambda j, i: (0, j))],
        out_specs=[pl.BlockSpec((tt, tc), lambda j, i: (i, j))],
        out_shape=[jax.ShapeDtypeStruct((T, 3 * D_MODEL), F32)],
        scratch_shapes=[], args=(proj, proj, proj, conv_w), exchange=exchange)
    return qkv, brought


def gdn_conv_bwd(proj, conv_w, dqkv, *, name):
    T = proj.shape[0]
    tt, tc = _tile(T, 512), HEAD_DIM
    nt = T // tt

    def body(p_ref, c_ref, n_ref, w_ref, dp_ref, dc_ref, dn_ref, dx_ref, dw_ref):
        kind = _qkv_kind(pl.program_id(0))
        w = w_ref[...]
        e = _extend(p_ref, c_ref, n_ref, nt)
        c = _causal_conv(e, w, GDN_CONV)
        s, s_grad = _silu_and_grad(c)
        dy = _extend(dp_ref, dc_ref, dn_ref, nt)
        r = lax.rsqrt(jnp.sum(s * s, axis=-1, keepdims=True) + EPS)
        y = s * r
        ds_norm = r * _l2norm_scale(kind) * (dy - y * jnp.sum(dy * y, axis=-1, keepdims=True))
        ds = jnp.where(kind == 2, dy, ds_norm)
        dx, dw = _causal_conv_bwd(e, ds * s_grad, w, GDN_CONV, tt)
        dx_ref[...] = _bf(dx)
        _accumulate(dw_ref, dw, 1)

    return pl.pallas_call(
        body, name=name, grid=(3 * HEADS, nt),
        in_specs=_halo_specs(T, tt, tc, lambda j: j) + [pl.BlockSpec((SUBLANES, tc), lambda j, i: (0, j))]
        + _halo_specs(T, tt, tc, lambda j: j),
        out_specs=[pl.BlockSpec((tt, tc), lambda j, i: (i, j)), pl.BlockSpec((SUBLANES, tc), lambda j, i: (0, j))],
        out_shape=[jax.ShapeDtypeStruct((T, 3 * D_MODEL), BF16), jax.ShapeDtypeStruct((SUBLANES, 3 * D_MODEL), F32)],
        compiler_params=_cp("parallel", "arbitrary"),
    )(proj, proj, proj, conv_w, dqkv, dqkv, dqkv)


FFN_COLS = 256


FFN_UP_ROWS = 256


def ffn_up_act(x, gain, w, conv_w, *, name, exchange=None):
    T, D = x.shape
    N = w.shape[1]
    F = N // 2
    tm, tc = _tile(T, FFN_UP_ROWS), FFN_COLS
    n = T // tm

    def body(x_ref, g_ref, w_ref, cw_ref, u_ref, h_ref, act_ref, held):
        i = pl.program_id(0)

        @pl.when(i == 0)
        def _():
            held[...] = jnp.zeros_like(held)

        for c in range(F // tc):
            gate_cols, up_cols = slice(c * tc, (c + 1) * tc), slice(F + c * tc, F + (c + 1) * tc)
            gate = _causal_conv(held[:, gate_cols], cw_ref[:, gate_cols], FFN_CONV)
            up_ = _causal_conv(held[:, up_cols], cw_ref[:, up_cols], FFN_CONV)
            act_ref[:, gate_cols] = _bf((_silu(gate) * up_)[HALO:])
        xf = x_ref[...]
        h = _bf(xf * lax.rsqrt(jnp.mean(xf * xf, axis=-1, keepdims=True) + EPS) * g_ref[...])
        h_ref[...] = h
        u = _dot(h, w_ref[...])
        u_ref[...] = u
        held[0:HALO, :] = held[tm:tm + HALO, :]
        held[HALO:HALO + tm, :] = u

    (u, h, act), brought = hosted_call(
        body, name=name, grid=(n + 1,),
        in_specs=[pl.BlockSpec((tm, D), lambda i: (jnp.minimum(i, n - 1), 0)), pl.BlockSpec((1, D), lambda i: (0, 0)),
                  pl.BlockSpec((D, N), lambda i: (0, 0)), pl.BlockSpec((SUBLANES, N), lambda i: (0, 0))],
        out_specs=[pl.BlockSpec((tm, N), lambda i: (jnp.minimum(i, n - 1), 0)),
                   pl.BlockSpec((tm, D), lambda i: (jnp.minimum(i, n - 1), 0)),
                   pl.BlockSpec((tm, F), lambda i: (jnp.maximum(i - 1, 0), 0))],
        out_shape=[jax.ShapeDtypeStruct((T, N), F32), jax.ShapeDtypeStruct((T, D), BF16), jax.ShapeDtypeStruct((T, F), BF16)],
        scratch_shapes=[pltpu.VMEM((HALO + tm, N), F32)],
        args=(x, gain, w, conv_w), exchange=exchange)
    return u, h, act, brought


def ffn_act_bwd(u, conv_w, dact, *, name):
    T = u.shape[0]
    tt, tc = _tile(T, 512), FFN_COLS
    half = D_FF // tc
    nt = T // tt

    def body(gp, gc, gn, up, uc, un, wg_ref, wu_ref, dp, dc_, dn, dug_ref, duu_ref, dwg_ref, dwu_ref):
        wg, wu = wg_ref[...], wu_ref[...]
        eg, eu = _extend(gp, gc, gn, nt), _extend(up, uc, un, nt)
        gate, up_ = _causal_conv(eg, wg, FFN_CONV), _causal_conv(eu, wu, FFN_CONV)
        da = _extend(dp, dc_, dn, nt)
        act, act_grad = _silu_and_grad(gate)
        dxg, dwg = _causal_conv_bwd(eg, da * up_ * act_grad, wg, FFN_CONV, tt)
        dxu, dwu = _causal_conv_bwd(eu, da * act, wu, FFN_CONV, tt)
        dug_ref[...] = _bf(dxg)
        duu_ref[...] = _bf(dxu)
        _accumulate(dwg_ref, dwg, 1)
        _accumulate(dwu_ref, dwu, 1)

    return pl.pallas_call(
        body, name=name, grid=(half, nt),
        in_specs=_halo_specs(T, tt, tc, lambda j: j) + _halo_specs(T, tt, tc, lambda j: j + half)
        + [pl.BlockSpec((SUBLANES, tc), lambda j, i: (0, j)), pl.BlockSpec((SUBLANES, tc), lambda j, i: (0, j + half))]
        + _halo_specs(T, tt, tc, lambda j: j),
        out_specs=[pl.BlockSpec((tt, tc), lambda j, i: (i, j)), pl.BlockSpec((tt, tc), lambda j, i: (i, j)),
                   pl.BlockSpec((SUBLANES, tc), lambda j, i: (0, j)), pl.BlockSpec((SUBLANES, tc), lambda j, i: (0, j))],
        out_shape=[jax.ShapeDtypeStruct((T, D_FF), BF16), jax.ShapeDtypeStruct((T, D_FF), BF16),
                   jax.ShapeDtypeStruct((SUBLANES, D_FF), F32), jax.ShapeDtypeStruct((SUBLANES, D_FF), F32)],
        compiler_params=_cp("parallel", "arbitrary"),
    )(u, u, u, u, u, u, conv_w, conv_w, dact, dact, dact)


def head_norm_fwd(x, gain, z=None, *, x_col=0, z_col=0, name, out_dtype=F32):
    T = x.shape[0]
    tt = _tile(T, 512)
    gated = z is not None

    def body(*refs):
        x_ref, g_ref = refs[0], refs[1]
        o_ref = refs[-1]
        for h in range(HEADS):
            cols = slice(h * HEAD_DIM, (h + 1) * HEAD_DIM)
            xf = x_ref[:, cols]
            y = xf * lax.rsqrt(jnp.mean(xf * xf, axis=-1, keepdims=True) + EPS) * g_ref[...]
            if gated:
                y = y * _silu(refs[2][:, cols])
            o_ref[:, cols] = y.astype(out_dtype)

    wide = lambda col: pl.BlockSpec((tt, D_MODEL), lambda i: (i, col // HEADS))
    ins = [wide(x_col), pl.BlockSpec((1, HEAD_DIM), lambda i: (0, 0))]
    args = [x, gain]
    if gated:
        ins.append(wide(z_col))
        args.append(z)
    return pl.pallas_call(
        body, name=name, grid=(T // tt,), in_specs=ins, out_specs=wide(0),
        out_shape=jax.ShapeDtypeStruct((T, D_MODEL), out_dtype),
        compiler_params=_cp("parallel"),
    )(*args)


def head_norm_bwd(x, gain, dys, z=None, *, x_col=0, z_col=0, name, dx_dtype=F32):
    T = x.shape[0]
    tt = _tile(T, 512)
    gated = z is not None
    nd = len(dys)

    def body(*refs):
        x_ref, g_ref = refs[0], refs[1]
        outs = refs[2 + nd + (1 if gated else 0):]
        dx_ref, dg_ref = outs[0], outs[-1]
        folds = []
        for h in range(HEADS):
            cols = slice(h * HEAD_DIM, (h + 1) * HEAD_DIM)
            xf = x_ref[:, cols]
            r = lax.rsqrt(jnp.mean(xf * xf, axis=-1, keepdims=True) + EPS)
            xh = xf * r
            dy = refs[2][:, cols].astype(F32)
            for d_ref in refs[3:2 + nd]:
                dy = dy + d_ref[:, cols].astype(F32)
            if gated:
                gate, gate_grad = _silu_and_grad(refs[2 + nd][:, cols])
                outs[1][:, cols] = _bf(dy * xh * g_ref[...] * gate_grad)
                dn = dy * gate
            else:
                dn = dy
            dxh = dn * g_ref[...]
            dx_ref[:, cols] = (r * (dxh - xh * jnp.mean(dxh * xh, axis=-1, keepdims=True))).astype(dx_dtype)
            folds.append(_fold_rows(dn * xh))
        _accumulate(dg_ref, jnp.concatenate(folds, axis=1), 0)

    wide = lambda col: pl.BlockSpec((tt, D_MODEL), lambda i: (i, col // HEADS))
    ins = [wide(x_col), pl.BlockSpec((1, HEAD_DIM), lambda i: (0, 0))] + [wide(0)] * nd
    args = [x, gain] + list(dys)
    outs = [wide(0)]
    shapes = [jax.ShapeDtypeStruct((T, D_MODEL), dx_dtype)]
    if gated:
        ins.append(wide(z_col))
        args.append(z)
        outs.append(wide(0))
        shapes.append(jax.ShapeDtypeStruct((T, D_MODEL), BF16))
    outs.append(pl.BlockSpec((SUBLANES, D_MODEL), lambda i: (0, 0)))
    shapes.append(jax.ShapeDtypeStruct((SUBLANES, D_MODEL), F32))
    return pl.pallas_call(
        body, name=name, grid=(T // tt,), in_specs=ins, out_specs=outs, out_shape=shapes,
        compiler_params=_cp("arbitrary"),
    )(*args)


def gates_fwd(proj, a_log, dt_bias, *, name):
    T = proj.shape[0]
    tt = _tile(T, 1024)

    def body(p_ref, al_ref, dt_ref, o_ref):
        p = p_ref[...]
        lane = lax.broadcasted_iota(jnp.int32, p.shape, 1)
        o_ref[...] = jnp.where(lane < HEADS, -jnp.exp(al_ref[...]) * _softplus(p + dt_ref[...]), _sigmoid(p))

    return pl.pallas_call(
        body, name=name, grid=(T // tt,),
        in_specs=[pl.BlockSpec((tt, LANES), lambda i: (i, AB_BLOCK)), pl.BlockSpec((1, LANES), lambda i: (0, 0)),
                  pl.BlockSpec((1, LANES), lambda i: (0, 0))],
        out_specs=pl.BlockSpec((tt, LANES), lambda i: (i, 0)),
        out_shape=jax.ShapeDtypeStruct((T, LANES), F32),
        compiler_params=_cp("parallel"),
    )(proj, a_log, dt_bias)


def gates_bwd(proj, a_log, dt_bias, dgate, *, name):
    T = proj.shape[0]
    tt = _tile(T, 1024)

    def body(p_ref, al_ref, dt_ref, d_ref, dp_ref, dal_ref, ddt_ref):
        p, d = p_ref[...], d_ref[...]
        lane = lax.broadcasted_iota(jnp.int32, p.shape, 1)
        ea = jnp.exp(al_ref[...])
        pa = p + dt_ref[...]
        da = -d * ea * _sigmoid(pa)
        b = _sigmoid(p)
        dp_ref[...] = _bf(jnp.where(lane < HEADS, da, jnp.where(lane < 2 * HEADS, d * b * (1.0 - b), 0.0)))
        _accumulate(dal_ref, _fold_rows(jnp.where(lane < HEADS, -d * ea * _softplus(pa), 0.0)), 0)
        _accumulate(ddt_ref, _fold_rows(jnp.where(lane < HEADS, da, 0.0)), 0)

    acc = pl.BlockSpec((SUBLANES, LANES), lambda i: (0, 0))
    return pl.pallas_call(
        body, name=name, grid=(T // tt,),
        in_specs=[pl.BlockSpec((tt, LANES), lambda i: (i, AB_BLOCK)), pl.BlockSpec((1, LANES), lambda i: (0, 0)),
                  pl.BlockSpec((1, LANES), lambda i: (0, 0)), pl.BlockSpec((tt, LANES), lambda i: (i, 0))],
        out_specs=[pl.BlockSpec((tt, LANES), lambda i: (i, 0)), acc, acc],
        out_shape=[jax.ShapeDtypeStruct((T, LANES), BF16), jax.ShapeDtypeStruct((SUBLANES, LANES), F32),
                   jax.ShapeDtypeStruct((SUBLANES, LANES), F32)],
        compiler_params=_cp("arbitrary"),
    )(proj, a_log, dt_bias, dgate)


def loss_fwd(y, target, *, name):
    T, D = y.shape
    tt = _tile(T, 512)

    def body(y_ref, t_ref, dy_ref, l_ref):
        d = y_ref[...] - t_ref[...]
        dy_ref[...] = d * (1.0 / D)
        sq = d * d
        lanes = sq[:, 0:LANES]
        for c in range(1, D // LANES):
            lanes = lanes + sq[:, c * LANES:(c + 1) * LANES]
        _accumulate(l_ref, _fold_rows(lanes) * (0.5 / D), 0)

    return pl.pallas_call(
        body, name=name, grid=(T // tt,),
        in_specs=[pl.BlockSpec((tt, D), lambda i: (i, 0)), pl.BlockSpec((tt, D), lambda i: (i, 0))],
        out_specs=[pl.BlockSpec((tt, D), lambda i: (i, 0)), pl.BlockSpec((SUBLANES, LANES), lambda i: (0, 0))],
        out_shape=[jax.ShapeDtypeStruct((T, D), F32), jax.ShapeDtypeStruct((SUBLANES, LANES), F32)],
        compiler_params=_cp("arbitrary"),
    )(y, target)


def _split_bf16(x):
    hi = _bf(x)
    return hi, _bf(x - hi.astype(F32))


def _dot3(a, b, dot=_dot):
    return dot(a[0], b[0]) + dot(a[0], b[1]) + dot(a[1], b[0])


def _each(fn, *lists):
    return [fn(*args) for args in zip(*lists)]


def _unit_lower_inverses(lows):
    c = lows[0].shape[0]
    ii = lax.broadcasted_iota(jnp.int32, (c, c), 0)
    jj = lax.broadcasted_iota(jnp.int32, (c, c), 1)
    eye = jnp.where(ii == jj, 1.0, 0.0)
    invs = _each(lambda low: eye - low, lows)
    powers = _each(_split_bf16, lows)
    for _ in range(int(math.log2(c)) - 1):
        powers = _each(lambda p: _split_bf16(_dot3(p, p)), powers)
        invs = _each(lambda inv, p: inv + _dot3(_split_bf16(inv), p), invs, powers)
    return invs


def _gdn_chunks(heads):
    q, k, v, a_col, a_row, b_col, s0 = (list(t) for t in zip(*heads))
    c = q[0].shape[0]
    ii = lax.broadcasted_iota(jnp.int32, (c, c), 0)
    jj = lax.broadcasted_iota(jnp.int32, (c, c), 1)
    tri, strict = ii >= jj, ii > jj
    g_col = _each(lambda ar: jnp.sum(jnp.where(tri, ar, 0.0), axis=1, keepdims=True), a_row)
    g_row = _each(lambda ac: jnp.sum(jnp.where(ii <= jj, ac, 0.0), axis=0, keepdims=True), a_col)
    gam = _each(lambda gc, gr: jnp.exp(jnp.where(tri, gc - gr, -jnp.inf)), g_col, g_row)
    g_last = _each(lambda ac: jnp.sum(ac, axis=0, keepdims=True), a_col)
    gam_col = _each(jnp.exp, g_col)
    del_col = _each(lambda gl, gc: jnp.exp(gl - gc), g_last, g_col)
    kb = _each(lambda k_, b: k_ * b, k, b_col)
    m = _each(lambda kb_, k_: _dot_nt(_bf(kb_), _bf(k_)), kb, k)
    ks = _each(lambda k_, s: _dot(_bf(k_), _bf(s)), k, s0)
    qk = _each(lambda q_, k_: _dot_nt(_bf(q_), _bf(k_)), q, k)
    inv = _unit_lower_inverses(_each(lambda m_, g: jnp.where(strict, m_ * g, 0.0), m, gam))
    e = _each(lambda v_, gc, ks_: v_ - gc * ks_, v, gam_col, ks)
    inv = _each(_split_bf16, inv)
    vn = _each(lambda inv_, b, e_: _dot3(inv_, _split_bf16(b * e_)), inv, b_col, e)
    p = _each(lambda qk_, g: jnp.where(tri, qk_ * g, 0.0), qk, gam)
    return [dict(tri=tri, strict=strict, ii=ii, jj=jj, gam=gam[g], g_last=g_last[g], gam_col=gam_col[g], del_col=del_col[g],
                 kb=kb[g], m=m[g], inv=inv[g], ks=ks[g], e=e[g], vn=vn[g], qk=qk[g], p=p[g]) for g in range(len(heads))]


GDN_HEADS_PER_STEP = HEADS


def _head_cols(ref, g):
    return ref[:, g * HEAD_DIM:(g + 1) * HEAD_DIM]


def _load_heads(q_ref, k_ref, v_ref, ar_ref, br_ref, states):
    return [(_head_cols(q_ref, g), _head_cols(k_ref, g), _head_cols(v_ref, g), ar_ref[g, 0].T, ar_ref[g, 0], br_ref[g, 0].T, states(g))
            for g in range(GDN_HEADS_PER_STEP)]


def gdn_fwd(qkv, a_row, b_row, *, name, exchange=None):
    T = qkv.shape[0]
    C = GDN_CHUNK
    N = T // C
    G = GDN_HEADS_PER_STEP

    def body(q_ref, k_ref, v_ref, ar_ref, br_ref, o_ref, s_ref, state):
        @pl.when(pl.program_id(1) == 0)
        def _():
            state[...] = jnp.zeros_like(state)
        loaded = _load_heads(q_ref, k_ref, v_ref, ar_ref, br_ref, lambda g: state[g])
        ws = _gdn_chunks(loaded)
        qs = _each(lambda h: _dot(_bf(h[0]), _bf(h[6])), loaded)
        pv = _each(lambda w: _dot(_bf(w["p"]), _bf(w["vn"])), ws)
        kv = _each(lambda h, w: _dot_tn(_bf(w["del_col"] * h[1]), _bf(w["vn"])), loaded, ws)
        for g, w in enumerate(ws):
            s0 = loaded[g][6]
            s_ref[g, 0] = s0
            o_ref[:, g * HEAD_DIM:(g + 1) * HEAD_DIM] = w["gam_col"] * qs[g] + pv[g]
            state[g] = jnp.exp(w["g_last"]) * s0 + kv[g]

    per = HEADS // G
    blk = lambda off: pl.BlockSpec((C, G * HEAD_DIM), lambda h, n: (n, off * per + h))
    row = pl.BlockSpec((G, 1, 1, C), lambda h, n: (h, n, 0, 0))
    (o, states), brought = hosted_call(
        body, name=name, grid=(per, N),
        in_specs=[blk(0), blk(1), blk(2), row, row],
        out_specs=[blk(0), pl.BlockSpec((G, 1, HEAD_DIM, HEAD_DIM), lambda h, n: (h, n, 0, 0))],
        out_shape=[jax.ShapeDtypeStruct((T, D_MODEL), F32), jax.ShapeDtypeStruct((HEADS, N, HEAD_DIM, HEAD_DIM), F32)],
        scratch_shapes=[pltpu.VMEM((G, HEAD_DIM, HEAD_DIM), F32)],
        args=(qkv, qkv, qkv, a_row, b_row), exchange=exchange)
    return o, states, brought


def gdn_bwd(qkv, a_row, b_row, states, do, *, name, exchange=None):
    T = qkv.shape[0]
    C = GDN_CHUNK
    N = T // C
    G = GDN_HEADS_PER_STEP
    per = HEADS // G
    rev = lambda n: N - 1 - n
    row = pl.BlockSpec((G, 1, 1, C), lambda h, n: (h, rev(n), 0, 0))

    def body(q_ref, k_ref, v_ref, ar_ref, br_ref, s_ref, do_ref, dqkv_ref, da_ref, db_ref, dstate):
        @pl.when(pl.program_id(1) == 0)
        def _():
            dstate[...] = jnp.zeros_like(dstate)
        loaded = _load_heads(q_ref, k_ref, v_ref, ar_ref, br_ref, lambda g: s_ref[g, 0])
        hs = _gdn_chunks(loaded)
        for g, d in enumerate(hs):
            q, k, _, _, _, b, s0 = loaded[g]
            d.update(q=q, k=k, b=b, s0=s0, ds1=dstate[g], dout=_head_cols(do_ref, g))
        rows = lambda t: jnp.sum(t, axis=1, keepdims=True)
        ii_col = lax.broadcasted_iota(jnp.int32, (C, 1), 0)

        def stage(**fns):
            for key, fn in fns.items():
                for d in hs:
                    d[key] = fn(d)

        stage(s0b=lambda d: _bf(d["s0"]), ds1b=lambda d: _bf(d["ds1"]), doutb=lambda d: _bf(d["dout"]),
              kbf=lambda d: _bf(d["k"]), qbf=lambda d: _bf(d["q"]), vnb=lambda d: _bf(d["vn"]))
        stage(dvn=lambda d: _dot_tn(_bf(d["p"]), d["doutb"]) + _dot(_bf(d["del_col"] * d["k"]), d["ds1b"]),
              dqk=lambda d: jnp.where(d["tri"], _dot_nt(d["doutb"], d["vnb"]), 0.0) * d["gam"],
              qs=lambda d: _dot(d["qbf"], d["s0b"]),
              dkd=lambda d: _dot_nt(d["vnb"], d["ds1b"]))
        stage(dr=lambda d: _dot3(d["inv"], _split_bf16(d["dvn"]), _dot_tn),
              dq=lambda d: d["gam_col"] * _dot_nt(d["doutb"], d["s0b"]) + _dot(_bf(d["dqk"]), d["kbf"]),
              dk=lambda d: _dot_tn(_bf(d["dqk"]), d["qbf"]) + d["del_col"] * d["dkd"],
              ddel=lambda d: d["del_col"] * rows(d["dkd"] * d["k"]))
        stage(dg=lambda d: d["gam_col"] * rows(d["dout"] * d["qs"]) - d["ddel"],
              dg_last=lambda d: jnp.sum(d["ddel"], axis=0, keepdims=True)
              + jnp.exp(d["g_last"]) * jnp.sum(rows(d["ds1"] * d["s0"]), axis=0, keepdims=True),
              dm=lambda d: jnp.where(d["strict"], -_dot_nt(_bf(d["dr"]), d["vnb"]), 0.0) * d["gam"],
              de=lambda d: d["b"] * d["dr"])
        stage(dkb=lambda d: _dot(_bf(d["dm"]), d["kbf"]),
              dks=lambda d: -d["gam_col"] * d["de"])
        stage(dk=lambda d: d["dk"] + _dot_tn(_bf(d["dm"]), _bf(d["kb"])) + _dot_nt(_bf(d["dks"]), d["s0b"]) + d["b"] * d["dkb"],
              dbeta=lambda d: rows(d["dr"] * d["e"]) + rows(d["dkb"] * d["k"]),
              ds0=lambda d: jnp.exp(d["g_last"]) * d["ds1"] + _dot_tn(_bf(d["gam_col"] * d["q"]), d["doutb"])
              + _dot_tn(d["kbf"], _bf(d["dks"])),
              wg=lambda d: d["dqk"] * d["qk"] + d["dm"] * d["m"])
        stage(dg=lambda d: d["dg"] - d["gam_col"] * rows(d["de"] * d["ks"]) + rows(d["wg"])
              - jnp.sum(d["wg"], axis=0, keepdims=True).T + jnp.where(ii_col == C - 1, d["dg_last"], 0.0))
        stage(da=lambda d: jnp.sum(jnp.where(d["ii"] >= d["jj"], d["dg"], 0.0), axis=0, keepdims=True),
              db=lambda d: d["dbeta"].T)
        for g, d in enumerate(hs):
            dstate[g] = d["ds0"]
            da_ref[g, 0] = d["da"]
            db_ref[g, 0] = d["db"]
            for part, key in enumerate(("dq", "dk", "de")):
                start = part * D_MODEL + g * HEAD_DIM
                dqkv_ref[:, start:start + HEAD_DIM] = d[key]

    assert per == 1
    blk = lambda off: pl.BlockSpec((C, G * HEAD_DIM), lambda h, n: (rev(n), off * per + h))
    (dqkv, da, db), brought = hosted_call(
        body, name=name, grid=(per, N),
        in_specs=[blk(0), blk(1), blk(2), row, row,
                  pl.BlockSpec((G, 1, HEAD_DIM, HEAD_DIM), lambda h, n: (h, rev(n), 0, 0)), blk(0)],
        out_specs=[pl.BlockSpec((C, 3 * D_MODEL), lambda h, n: (rev(n), 0)), row, row],
        out_shape=[jax.ShapeDtypeStruct((T, 3 * D_MODEL), F32)] + [jax.ShapeDtypeStruct((HEADS, N, 1, C), F32)] * 2,
        scratch_shapes=[pltpu.VMEM((G, HEAD_DIM, HEAD_DIM), F32)],
        args=(qkv, qkv, qkv, a_row, b_row, states, do), exchange=exchange)
    return dqkv, da, db, brought


SB_BLOCK = 128


SB_QBLOCKS = 8


def _sb_rows(j, blk):
    return pl.ds(pl.multiple_of(j * blk, blk), blk)


def _sb_tile(qb, k_ref, i, j, blk, live):
    z = _dot_nt(qb, _bf(k_ref[_sb_rows(j, blk), :]))
    t_idx = i * blk + lax.broadcasted_iota(jnp.int32, (blk, blk), 0)
    s_idx = j * blk + lax.broadcasted_iota(jnp.int32, (blk, blk), 1)
    mask = jnp.logical_and(s_idx < t_idx, live)
    lf = jnp.where(mask, -_softplus(z), 0.0)
    return z, mask, lf


SB_DEAD = 105.0


def sb_fwd(q, k, v, *, k_col=0, v_col=0, name):
    T = q.shape[0]
    blk = _tile(T, SB_BLOCK)
    P = min(SB_QBLOCKS, T // blk)
    scale = HEAD_DIM ** -0.5

    def body(q_ref, k_ref, v_ref, o_ref, l_ref, n_ref):
        iq = [P * pl.program_id(1) + p for p in range(P)]
        qb = [_bf(q_ref[p * blk:(p + 1) * blk, :] * scale) for p in range(P)]
        r_idx = lax.broadcasted_iota(jnp.int32, (blk, blk), 0)
        c_idx = lax.broadcasted_iota(jnp.int32, (blk, blk), 1)
        later = _bf(jnp.where(r_idx > c_idx, 1.0, 0.0))

        def live_blocks(jj, runs):
            return [jnp.logical_and(jj <= i, jnp.max(run) > -SB_DEAD) for i, run in zip(iq, runs)]

        def alive(carry):
            jj, _, runs, _ = carry
            some = False
            for f in live_blocks(jj, runs):
                some = jnp.logical_or(some, f)
            return some

        def step(carry):
            jj, accs, runs, visited = carry
            live = live_blocks(jj, runs)
            js = [jnp.maximum(i - jj, 0) for i in iq]
            tiles = _each(lambda q_, i, j, f: _sb_tile(q_, k_ref, i, j, blk, f), qb, iq, js, live)
            parts = _each(lambda t: _split_bf16(t[2]), tiles)
            after = _each(lambda run, s: run + _dot(s[0], later) + _dot(s[1], later), runs, parts)
            a = _each(lambda t, af: jnp.where(t[1], jnp.exp(t[0] + t[2] + af), 0.0), tiles, after)
            vb = _each(lambda j: _bf(v_ref[_sb_rows(j, blk), :]), js)
            accs = _each(lambda acc, a_, v_: acc + _dot(_bf(a_), v_), accs, a, vb)
            runs = _each(lambda run, t: run + jnp.sum(t[2], axis=1, keepdims=True), runs, tiles)
            visited = _each(lambda n, f: n + f.astype(jnp.int32), visited, live)
            return jj + 1, accs, runs, visited

        start = (jnp.int32(0), [jnp.zeros((blk, HEAD_DIM), F32)] * P, [jnp.zeros((blk, 1), F32)] * P, [jnp.int32(0)] * P)
        _, accs, runs, visited = lax.while_loop(alive, step, start)
        for p in range(P):
            o_ref[p * blk:(p + 1) * blk, :] = accs[p]
            l_ref[0, p] = runs[p].T
            n_ref[0, p] = jnp.full((SUBLANES, LANES), visited[p].astype(F32))

    return pl.pallas_call(
        body, name=name, grid=(HEADS, T // (P * blk)),
        in_specs=[pl.BlockSpec((P * blk, HEAD_DIM), lambda h, i: (i, h)), pl.BlockSpec((T, HEAD_DIM), lambda h, i: (0, k_col + h)),
                  pl.BlockSpec((T, HEAD_DIM), lambda h, i: (0, v_col + h))],
        out_specs=[pl.BlockSpec((P * blk, HEAD_DIM), lambda h, i: (i, h)), pl.BlockSpec((1, P, 1, blk), lambda h, i: (h, i, 0, 0)),
                   pl.BlockSpec((1, P, SUBLANES, LANES), lambda h, i: (h, i, 0, 0))],
        out_shape=[jax.ShapeDtypeStruct((T, D_MODEL), F32), jax.ShapeDtypeStruct((HEADS, T // blk, 1, blk), F32),
                   jax.ShapeDtypeStruct((HEADS, T // blk, SUBLANES, LANES), F32)],
        compiler_params=_cp("parallel", "arbitrary"),
    )(q, k, v)


def sb_bwd(q, k, v, ltot, visited, do, *, k_col=0, v_col=0, name, exchange=None):
    T = q.shape[0]
    blk = _tile(T, SB_BLOCK)
    P = min(SB_QBLOCKS, T // blk)
    scale = HEAD_DIM ** -0.5

    def body(q_ref, k_ref, v_ref, l_ref, n_ref, do_ref, dq_ref, dk_ref, dv_ref):
        iq = [P * pl.program_id(1) + p for p in range(P)]
        count = [jnp.max(n_ref[0, p]).astype(jnp.int32) for p in range(P)]
        first = [i + 1 - n for i, n in zip(iq, count)]
        trips = count[0]
        for n in count[1:]:
            trips = jnp.maximum(trips, n)

        @pl.when(pl.program_id(1) == 0)
        def _():
            dk_ref[...] = jnp.zeros_like(dk_ref)
            dv_ref[...] = jnp.zeros_like(dv_ref)

        qb = [_bf(q_ref[p * blk:(p + 1) * blk, :] * scale) for p in range(P)]
        dob = [_bf(do_ref[p * blk:(p + 1) * blk, :]) for p in range(P)]
        ltot_ = [l_ref[0, p].T for p in range(P)]
        r_idx = lax.broadcasted_iota(jnp.int32, (blk, blk), 0)
        c_idx = lax.broadcasted_iota(jnp.int32, (blk, blk), 1)
        upto = _bf(jnp.where(r_idx <= c_idx, 1.0, 0.0))
        before = _bf(jnp.where(r_idx < c_idx, 1.0, 0.0))

        def step(t, carry):
            dqs, lpre, cpre = carry
            live = [f + t <= i for f, i in zip(first, iq)]
            js = [jnp.minimum(f + t, i) for f, i in zip(first, iq)]
            tiles = _each(lambda q_, i, j, f: _sb_tile(q_, k_ref, i, j, blk, f), qb, iq, js, live)
            parts = _each(lambda tl: _split_bf16(tl[2]), tiles)
            after = _each(lambda lt, lp, s: lt - (lp + _dot(s[0], upto) + _dot(s[1], upto)), ltot_, lpre, parts)
            ls = _each(lambda tl: tl[0] + tl[2], tiles)
            a = _each(lambda tl, ls_, af: jnp.where(tl[1], jnp.exp(ls_ + af), 0.0), tiles, ls, after)
            vb = _each(lambda j: _bf(v_ref[_sb_rows(j, blk), :]), js)
            p = _each(lambda a_, do_, v_: a_ * _dot_nt(do_, v_), a, dob, vb)
            pparts = _each(_split_bf16, p)
            left = _each(lambda cp, s: cp + _dot(s[0], before) + _dot(s[1], before), cpre, pparts)
            dzb = _each(lambda tl, p_, lf_, ls_: _bf(jnp.where(tl[1], p_ * jnp.exp(tl[2]) - lf_ * jnp.exp(ls_), 0.0)),
                        tiles, p, left, ls)
            dks = _each(lambda dz, q_: _dot_tn(dz, q_), dzb, qb)
            dvs = _each(lambda a_, do_: _dot_tn(_bf(a_), do_), a, dob)
            dqs = _each(lambda dq, dz, j: dq + _dot(dz, _bf(k_ref[_sb_rows(j, blk), :])), dqs, dzb, js)
            for j, dk, dv in zip(js, dks, dvs):
                dk_ref[_sb_rows(j, blk), :] += dk
                dv_ref[_sb_rows(j, blk), :] += dv
            lpre = _each(lambda lp, tl: lp + jnp.sum(tl[2], axis=1, keepdims=True), lpre, tiles)
            cpre = _each(lambda cp, p_: cp + jnp.sum(p_, axis=1, keepdims=True), cpre, p)
            return dqs, lpre, cpre

        zero = [jnp.zeros((blk, 1), F32)] * P
        dqs, _, _ = lax.fori_loop(0, trips, step, ([jnp.zeros((blk, HEAD_DIM), F32)] * P, zero, zero))
        for p in range(P):
            dq_ref[p * blk:(p + 1) * blk, :] = dqs[p] * scale

    full = lambda off: pl.BlockSpec((T, HEAD_DIM), lambda h, i: (0, off + h))
    tile = pl.BlockSpec((P * blk, HEAD_DIM), lambda h, i: (i, h))
    (dq, dk, dv), brought = hosted_call(
        body, name=name, grid=(HEADS, T // (P * blk)),
        in_specs=[tile, full(k_col), full(v_col), pl.BlockSpec((1, P, 1, blk), lambda h, i: (h, i, 0, 0)),
                  pl.BlockSpec((1, P, SUBLANES, LANES), lambda h, i: (h, i, 0, 0)), tile],
        out_specs=[tile, full(0), full(0)],
        out_shape=[jax.ShapeDtypeStruct((T, D_MODEL), F32)] * 3,
        scratch_shapes=[], args=(q, k, v, ltot, visited, do), exchange=exchange)
    return dq, dk, dv, brought


def sum_slots(slots, *, name):
    n, R, C = slots.shape
    tr = _tile(R, 256)

    def body(s_ref, o_ref):
        acc = s_ref[0].astype(F32)
        for k in range(1, n):
            acc = acc + s_ref[k].astype(F32)
        o_ref[...] = acc

    return pl.pallas_call(
        body, name=name, grid=(R // tr,),
        in_specs=[pl.BlockSpec((n, tr, C), lambda i: (0, i, 0))],
        out_specs=pl.BlockSpec((tr, C), lambda i: (i, 0)),
        out_shape=jax.ShapeDtypeStruct((R, C), F32),
        compiler_params=_cp("parallel"),
    )(slots)


def adamw(w, g_parts, m, v, *, name):
    R, C = w.shape
    tr = _tile(R, 256)
    n = len(g_parts)

    def body(*refs):
        w_ref, m_ref, v_ref = refs[0], refs[1 + n], refs[2 + n]
        g_ref, d_ref, nm_ref, nv_ref = refs[3 + n:]
        g = refs[1][...]
        for r in refs[2:1 + n]:
            g = g + r[...]
        m2 = ADAM_B1 * m_ref[...] + (1.0 - ADAM_B1) * g
        v2 = ADAM_B2 * v_ref[...] + (1.0 - ADAM_B2) * (g * g)
        m_hat = m2 / (1.0 - ADAM_B1 ** ADAM_STEP)
        v_hat = v2 / (1.0 - ADAM_B2 ** ADAM_STEP)
        g_ref[...] = g
        d_ref[...] = -ADAM_LR * (m_hat / (jnp.sqrt(v_hat) + ADAM_EPS) + ADAM_WD * w_ref[...])
        nm_ref[...] = m2
        nv_ref[...] = v2

    spec = pl.BlockSpec((tr, C), lambda i: (i, 0))
    return pl.pallas_call(
        body, name=name, grid=(R // tr,),
        in_specs=[spec] * (3 + n), out_specs=[spec] * 4,
        out_shape=[jax.ShapeDtypeStruct((R, C), F32)] * 4,
        compiler_params=_cp("parallel"),
    )(w, *g_parts, m, v)


CHIP_FLIPS = ((0, 1), (1, 0), (1, 1))


def _place():
    return lax.axis_index("x"), lax.axis_index("y"), lax.axis_index("c")


def _flip(v, f):
    return 1 - v if f else v


class ChipExchange:
    def __init__(self, arrays, scatter):
        self.arrays, self.scatter, self.n = list(arrays), scatter, len(arrays)
        lead = () if scatter else (N_CHIPS,)
        self.out_shape = [jax.ShapeDtypeStruct(lead + a.shape, a.dtype) for a in arrays]
        self.scratch = [pltpu.SemaphoreType.DMA((self.n, len(CHIP_FLIPS))), pltpu.SemaphoreType.DMA((self.n, len(CHIP_FLIPS))),
                        pltpu.SemaphoreType.DMA((self.n,))]

    def _copies(self, ins, outs, sems):
        send_sems, recv_sems, local_sems = sems
        x, y, c = _place()
        me = 2 * x + y
        local, sent, landing = [], [], []
        for k in range(self.n):
            local.append(pltpu.make_async_copy(ins[k].at[me] if self.scatter else ins[k], outs[k].at[me], local_sems.at[k]))
            for p, (fx, fy) in enumerate(CHIP_FLIPS):
                px, py = _flip(x, fx), _flip(y, fy)
                src = ins[k].at[2 * px + py] if self.scatter else ins[k]
                for dst, group in ((me, sent), (2 * px + py, landing)):
                    group.append(pltpu.make_async_remote_copy(src_ref=src, dst_ref=outs[k].at[dst], send_sem=send_sems.at[k, p],
                                                              recv_sem=recv_sems.at[k, p], device_id=(px, py, c), device_id_type=MESH))
        return local, sent, landing

    def start(self, ins, outs, sems):
        local, sent, _ = self._copies(ins, outs, sems)
        for cp in local + sent:
            cp.start()

    def finish(self, ins, outs, sems):
        local, _, landing = self._copies(ins, outs, sems)
        for cp in landing:
            cp.wait_send()
            cp.wait_recv()
        for cp in local:
            cp.wait()


def chip_exchange(arrays, *, scatter, name):
    ex = ChipExchange(arrays, scatter)
    n = ex.n

    def body(*refs):
        ins, outs, sems = refs[:n], refs[n:2 * n], refs[2 * n:]
        ex.start(ins, outs, sems)
        ex.finish(ins, outs, sems)

    return pl.pallas_call(body, name=name, in_specs=[ANY] * n, out_specs=[ANY] * n, out_shape=ex.out_shape,
                          scratch_shapes=ex.scratch)(*arrays)


def hosted_call(body, *, name, grid, in_specs, out_specs, out_shape, scratch_shapes, args, exchange=None):
    in_specs, out_specs, out_shape = list(in_specs), list(out_specs), list(out_shape)
    params = _cp(*["arbitrary"] * len(grid))
    if exchange is None:
        outs = pl.pallas_call(body, name=name, grid=grid, in_specs=in_specs, out_specs=out_specs, out_shape=out_shape,
                              scratch_shapes=list(scratch_shapes), compiler_params=params)(*args)
        return list(outs), []
    n, n_in, n_out, n_scr = exchange.n, len(in_specs), len(out_specs), len(scratch_shapes)

    def both(*refs):
        ins, t_ins = refs[:n_in], refs[n_in:n_in + n]
        outs, t_outs = refs[n_in + n:n_in + n + n_out], refs[n_in + n + n_out:n_in + 2 * n + n_out]
        scratch, sems = refs[n_in + 2 * n + n_out:n_in + 2 * n + n_out + n_scr], refs[n_in + 2 * n + n_out + n_scr:]
        first, last = True, True
        for axis, size in enumerate(grid):
            first = jnp.logical_and(first, pl.program_id(axis) == 0)
            last = jnp.logical_and(last, pl.program_id(axis) == size - 1)

        @pl.when(first)
        def _():
            exchange.start(t_ins, t_outs, sems)

        body(*ins, *outs, *scratch)

        @pl.when(last)
        def _():
            exchange.finish(t_ins, t_outs, sems)

    outs = pl.pallas_call(both, name=name, grid=grid, in_specs=in_specs + [ANY] * n, out_specs=out_specs + [ANY] * n,
                          out_shape=out_shape + exchange.out_shape, scratch_shapes=list(scratch_shapes) + exchange.scratch,
                          compiler_params=params)(*args, *exchange.arrays)
    return list(outs[:n_out]), list(outs[n_out:])


def sibling_swap(arrays, *, name):
    n = len(arrays)

    def body(*refs):
        ins, outs = refs[:n], refs[n:2 * n]
        send_sems, recv_sems = refs[2 * n:]
        x, y, c = _place()
        copies = [pltpu.make_async_remote_copy(src_ref=ins[k], dst_ref=outs[k], send_sem=send_sems.at[k], recv_sem=recv_sems.at[k],
                                               device_id=(x, y, 1 - c), device_id_type=MESH) for k in range(n)]
        for cp in copies:
            cp.start()
        for cp in copies:
            cp.wait_send()
            cp.wait_recv()

    return pl.pallas_call(
        body, name=name, in_specs=[ANY] * n, out_specs=[ANY] * n,
        out_shape=[jax.ShapeDtypeStruct(a.shape, a.dtype) for a in arrays],
        scratch_shapes=[pltpu.SemaphoreType.DMA((n,)), pltpu.SemaphoreType.DMA((n,))],
    )(*arrays)


DEVICE_FLIPS = tuple((fx, fy, fc) for fx in (0, 1) for fy in (0, 1) for fc in (0, 1) if fx or fy or fc)


def all_gather_devices(a, *, name):
    def body(a_ref, o_ref, send_sems, recv_sems, local_sem):
        x, y, c = _place()
        me = 4 * x + 2 * y + c
        local = pltpu.make_async_copy(a_ref, o_ref.at[me], local_sem)
        local.start()
        for p, (fx, fy, fc) in enumerate(DEVICE_FLIPS):
            peer = (_flip(x, fx), _flip(y, fy), _flip(c, fc))
            pltpu.make_async_remote_copy(src_ref=a_ref, dst_ref=o_ref.at[me], send_sem=send_sems.at[p], recv_sem=recv_sems.at[p],
                                         device_id=peer, device_id_type=MESH).start()
        for p, (fx, fy, fc) in enumerate(DEVICE_FLIPS):
            px, py, pc = _flip(x, fx), _flip(y, fy), _flip(c, fc)
            landing = pltpu.make_async_remote_copy(src_ref=a_ref, dst_ref=o_ref.at[4 * px + 2 * py + pc], send_sem=send_sems.at[p],
                                                   recv_sem=recv_sems.at[p], device_id=(px, py, pc), device_id_type=MESH)
            landing.wait_send()
            landing.wait_recv()
        local.wait()

    return pl.pallas_call(
        body, name=name, in_specs=[ANY], out_specs=ANY,
        out_shape=jax.ShapeDtypeStruct((N_DEV,) + a.shape, a.dtype),
        scratch_shapes=[pltpu.SemaphoreType.DMA((len(DEVICE_FLIPS),)), pltpu.SemaphoreType.DMA((len(DEVICE_FLIPS),)),
                        pltpu.SemaphoreType.DMA(())],
    )(a)


def _row(v):
    return v.reshape(1, -1)


def _pad_rows(w):
    return jnp.pad(w, ((0, SUBLANES - w.shape[0]), (0, 0)))


def _pad_lanes(v):
    return jnp.pad(v.reshape(1, -1), ((0, 0), (0, LANES - v.shape[-1])))


def _head_layouts(gates):
    T = gates.shape[0]
    rows = lambda cols: cols.T.reshape(HEADS, T // GDN_CHUNK, 1, GDN_CHUNK)
    return rows(gates[:, :HEADS]), rows(gates[:, HEADS:2 * HEADS])


def _ffn_fwd(x, W, l, plan):
    conv = _pad_rows(W["ffn_conv"][l])
    u, h, act, brought = ffn_up_act(x, _row(W["ffn_norm"][l]), W["ffn_w_up"][l], conv, name=f"ffn{l}_up_act",
                                    exchange=plan.fetch(f"ffn{l}_act"))
    plan.arrived(f"ffn{l}_act", brought, W)
    y = matmul_residual(act, W["ffn_w_down"][l], x, name=f"ffn{l}_down")
    return y, (x, h, u, conv, act)


def _ffn_bwd(dx, saved, W, l, G):
    x, h, u, conv, act = saved
    dact = matmul_nt(dx, W["ffn_w_down"][l], name=f"ffn{l}_down_dx", out_dtype=BF16)
    G["ffn_w_down"][l] = matmul_tn(act, dx, name=f"ffn{l}_down_dw")
    dug, duu, dwg, dwu = ffn_act_bwd(u, conv, dact, name=f"ffn{l}_act_bwd")
    G["ffn_conv"][l] = jnp.concatenate([dwg, dwu], axis=1)[:FFN_CONV]
    G["ffn_w_up"][l] = jnp.concatenate([matmul_tn(h, dug, name=f"ffn{l}_gate_dw"), matmul_tn(h, duu, name=f"ffn{l}_up_dw")], axis=1)
    dx, dgain = matmul_nt_normbwd([(dug, 0), (duu, 1)], W["ffn_w_up"][l], x, _row(W["ffn_norm"][l]), dx, name=f"ffn{l}_up_dx")
    G["ffn_norm"][l] = dgain.sum(0)
    return dx


class NoTraffic:
    def fetch(self, host):
        return None

    def arrived(self, host, brought, W):
        pass

    def flush(self, G):
        return None

    def landed(self, brought):
        pass


def _gdn_fwd(x, W, l, plan):
    proj, h = norm_matmul(x, _row(W["a_norm"][l]), W["a_w_in"][l], name=f"gdn{l}_in")
    conv = _pad_rows(W["a_conv"][l])
    a_log, dt_bias = _pad_lanes(W["a_log"][l]), _pad_lanes(W["a_dt_bias"][l])
    qkv, brought = gdn_conv_fwd(proj, conv, name=f"gdn{l}_conv", exchange=plan.fetch(f"gdn{l}_conv"))
    plan.arrived(f"gdn{l}_conv", brought, W)
    heads = _head_layouts(gates_fwd(proj, a_log, dt_bias, name=f"gdn{l}_gates"))
    o, states, brought = gdn_fwd(qkv, *heads, name=f"gdn{l}_rule", exchange=plan.fetch(f"gdn{l}_rule"))
    plan.arrived(f"gdn{l}_rule", brought, W)
    gain = _row(W["a_out_norm"][l])
    on = head_norm_fwd(o, gain, proj, z_col=Z_BLOCK, name=f"gdn{l}_outnorm", out_dtype=BF16)
    y = matmul_residual(on, W["a_w_out"][l], x, name=f"gdn{l}_out")
    return y, (x, h, proj, conv, a_log, dt_bias, qkv, heads, states, o, gain, on)


def _gdn_bwd(dx, saved, W, l, G, plan):
    x, h, proj, conv, a_log, dt_bias, qkv, heads, states, o, gain, on = saved
    T = x.shape[0]
    don = matmul_nt(dx, W["a_w_out"][l], name=f"gdn{l}_out_dx")
    G["a_w_out"][l] = matmul_tn(on, dx, name=f"gdn{l}_out_dw")
    do, dz, dgain = head_norm_bwd(o, gain, [don], proj, z_col=Z_BLOCK, name=f"gdn{l}_outnorm_bwd")
    G["a_out_norm"][l] = dgain.reshape(SUBLANES, HEADS, HEAD_DIM).sum((0, 1))
    dqkv, da, db, brought = gdn_bwd(qkv, *heads, states, do, name=f"gdn{l}_rule_bwd", exchange=plan.flush(G))
    plan.landed(brought)
    dqkv, dconv = gdn_conv_bwd(proj, conv, dqkv, name=f"gdn{l}_conv_bwd")
    G["a_conv"][l] = dconv[:GDN_CONV]
    dgate = jnp.concatenate([da.reshape(HEADS, T).T, db.reshape(HEADS, T).T, jnp.zeros((T, LANES - 2 * HEADS), F32)], axis=1)
    dab, dal, ddt = gates_bwd(proj, a_log, dt_bias, dgate, name=f"gdn{l}_gates_bwd")
    G["a_log"][l] = dal.sum(0)[:HEADS]
    G["a_dt_bias"][l] = ddt.sum(0)[:HEADS]
    parts = [(dqkv, 0), (dz, Z_BLOCK * LANES // D_MODEL), (dab, AB_BLOCK)]
    G["a_w_in"][l] = jnp.concatenate([matmul_tn(h, d, name=f"gdn{l}_in_dw{i}") for i, (d, _) in enumerate(parts)], axis=1)
    last = plan.flush(G) if l == 0 else None
    out = matmul_nt_normbwd(parts, W["a_w_in"][l], x, _row(W["a_norm"][l]), dx, name=f"gdn{l}_in_dx", exchange=last)
    dx, dgain = out[:2]
    if last is not None:
        plan.landed(out[2])
    G["a_norm"][l] = dgain.sum(0)
    return dx


def _sb_fwd(x, kn, kv, W, j):
    qp, h = norm_matmul(x, _row(W["b_norm"][j]), W["b_w_q"][j], name=f"sb{j}_q")
    gain = _row(W["q_norm"][j])
    q = head_norm_fwd(qp, gain, name=f"sb{j}_qnorm")
    o, ltot, visited = sb_fwd(q, kn, kv, v_col=HEADS, name=f"sb{j}_attn")
    y = matmul_residual(o, W["b_w_out"][j], x, name=f"sb{j}_out")
    return y, (x, h, qp, gain, q, o, ltot, visited)


def _sb_bwd(dx, saved, kn, kv, W, j, G, plan):
    x, h, qp, gain, q, o, ltot, visited = saved
    do = matmul_nt(dx, W["b_w_out"][j], name=f"sb{j}_out_dx")
    G["b_w_out"][j] = matmul_tn(o, dx, name=f"sb{j}_out_dw")
    dq, dk, dv, brought = sb_bwd(q, kn, kv, ltot, visited, do, v_col=HEADS, name=f"sb{j}_attn_bwd", exchange=plan.flush(G))
    plan.landed(brought)
    dqp, dgain = head_norm_bwd(qp, gain, [dq], name=f"sb{j}_qnorm_bwd", dx_dtype=BF16)
    G["q_norm"][j] = dgain.reshape(SUBLANES, HEADS, HEAD_DIM).sum((0, 1))
    G["b_w_q"][j] = matmul_tn(h, dqp, name=f"sb{j}_q_dw")
    dx, dgain = matmul_nt_normbwd([(dqp, 0)], W["b_w_q"][j], x, _row(W["b_norm"][j]), dx, name=f"sb{j}_q_dx")
    G["b_norm"][j] = dgain.sum(0)
    return dx, dk, dv


def local_step(x, target, W, plan=None):
    plan = plan or NoTraffic()
    G = {k: [None] * (N_A if k.startswith("a_") else N_B if k in ("b_norm", "b_w_q", "q_norm", "b_w_out") else DEPTH)
         for k in ("a_norm", "a_w_in", "a_conv", "a_log", "a_dt_bias", "a_out_norm", "a_w_out", "b_norm", "b_w_q", "q_norm",
                   "b_w_out", "ffn_norm", "ffn_w_up", "ffn_conv", "ffn_w_down")}
    G["w_kv"] = [None]
    tape = []
    for l in range(N_A):
        x, s_mix = _gdn_fwd(x, W, l, plan)
        x, s_ffn = _ffn_fwd(x, W, l, plan)
        tape.append((s_mix, s_ffn))
    x_kv = x
    kv, h_kv = norm_matmul(x, _row(W["kv_norm"]), W["w_kv"], name="kv_proj")
    k_gain = _row(W["k_norm"])
    kn = head_norm_fwd(kv, k_gain, name="k_norm")
    for j in range(N_B):
        x, s_mix = _sb_fwd(x, kn, kv, W, j)
        x, s_ffn = _ffn_fwd(x, W, N_A + j, plan)
        tape.append((s_mix, s_ffn))
    dx, loss = loss_fwd(x, target, name="loss")

    dks, dvs = [], []
    for j in reversed(range(N_B)):
        s_mix, s_ffn = tape[N_A + j]
        dx = _ffn_bwd(dx, s_ffn, W, N_A + j, G)
        dx, dk, dv = _sb_bwd(dx, s_mix, kn, kv, W, j, G, plan)
        dks.append(dk)
        dvs.append(dv)
    dkp, dgain = head_norm_bwd(kv, k_gain, dks, name="k_norm_bwd", dx_dtype=BF16)
    G["k_norm"] = dgain.reshape(SUBLANES, HEADS, HEAD_DIM).sum((0, 1))
    G["w_kv"][0] = jnp.concatenate([matmul_tn(h_kv, dkp, name="k_proj_dw"), matmul_tn(h_kv, dvs, name="v_proj_dw")], axis=1)
    dx, dgain = matmul_nt_normbwd([(dkp, 0)] + [(dv, 1) for dv in dvs], W["w_kv"], x_kv, _row(W["kv_norm"]), dx,
                                  name="kv_proj_dx")
    G["kv_norm"] = dgain.sum(0)
    for l in reversed(range(N_A)):
        s_mix, s_ffn = tape[l]
        dx = _ffn_bwd(dx, s_ffn, W, l, G)
        dx = _gdn_bwd(dx, s_mix, W, l, G, plan)
    return loss, dx, G


MATRICES = {"a_w_in": 2, "a_w_out": 1, "w_kv": 1, "b_w_q": 1, "b_w_out": 1, "ffn_w_up": 2, "ffn_w_down": 1}
SMALL_SHARDED = {"a_norm": 1, "a_conv": 2, "ffn_conv": 2}
SMALL_REPLICATED = ("a_log", "a_dt_bias", "a_out_norm", "kv_norm", "k_norm", "b_norm", "q_norm", "ffn_norm")
WEIGHT_ORDER = ("a_norm", "a_w_in", "a_conv", "a_log", "a_dt_bias", "a_out_norm", "a_w_out", "kv_norm", "w_kv", "k_norm",
                "b_norm", "b_w_q", "q_norm", "b_w_out", "ffn_norm", "ffn_w_up", "ffn_conv", "ffn_w_down")
SMALL_ORDER = tuple(n for n in WEIGHT_ORDER if n not in MATRICES)
PACK_QUANTUM = SUBLANES * LANES


def _unshard(g, axis):
    g = jnp.moveaxis(g, 0, axis)
    return g.reshape(g.shape[:axis] + (g.shape[axis] * g.shape[axis + 1],) + g.shape[axis + 2:])


def _shards(full, axis):
    n = full.shape[axis] // N_CHIPS
    return jnp.moveaxis(full.reshape(full.shape[:axis] + (N_CHIPS, n) + full.shape[axis + 1:]), axis, 0)


def _pack(arrays):
    parts = []
    for a in arrays:
        flat = a.reshape(-1)
        parts.append(jnp.pad(flat, (0, -flat.shape[0] % PACK_QUANTUM)).reshape(-1, LANES))
    return jnp.concatenate(parts, axis=0)


def _unpack(buf, shapes):
    out, row = [], 0
    for s in shapes:
        size = math.prod(s)
        rows = -(-size // PACK_QUANTUM) * SUBLANES
        out.append(buf[row:row + rows].reshape(-1)[:size].reshape(s))
        row += rows
    return out


def _stack(per_layer):
    return jnp.stack(per_layer) if isinstance(per_layer, list) else per_layer


FETCH_BESIDE = {
    "gdn0_conv": (("ffn_w_up", 0, 2),),
    "gdn0_rule": (("ffn_w_down", 0, 2), ("w_kv", 0, None), ("b_w_q", 0, 2), ("b_w_out", 0, 2)),
    "ffn0_act": (("a_w_in", 1, 1), ("a_w_out", 1, 1)),
    "gdn1_conv": (("ffn_w_up", 2, 2),),
    "gdn1_rule": (("ffn_w_down", 2, 2),),
}
FETCH_FIRST = (("a_w_in", 0, 1), ("a_w_out", 0, 1))


class Traffic:
    def __init__(self, local):
        self.local = local
        self.shipped, self.received = [], {}

    @staticmethod
    def _assemble(W, name, first, layers, gathered):
        whole = _unshard(gathered, MATRICES[name])
        if name == "a_w_in":
            whole = jnp.pad(whole, ((0, 0), (0, 0), (0, W_IN_PAD - W_IN_COLS)))
        if layers is None:
            W[name] = whole
        else:
            W.setdefault(name, {}).update({first + i: whole[i] for i in range(layers)})

    def _shards_of(self, wanted):
        return [(self.local[name] if layers is None else self.local[name][first:first + layers]).astype(BF16)
                for name, first, layers in wanted]

    def fetch_first(self, extra):
        brought = chip_exchange(self._shards_of(FETCH_FIRST) + list(extra), scatter=False, name="gather_first")
        W = {}
        for (name, first, layers), g in zip(FETCH_FIRST, brought):
            self._assemble(W, name, first, layers, g)
        return W, brought[len(FETCH_FIRST):]

    def fetch(self, host):
        return ChipExchange(self._shards_of(FETCH_BESIDE[host]), scatter=False) if host in FETCH_BESIDE else None

    def arrived(self, host, brought, W):
        for (name, first, layers), g in zip(FETCH_BESIDE.get(host, ()), brought):
            self._assemble(W, name, first, layers, g)

    def _ready(self, G):
        out = []
        for name, axis in MATRICES.items():
            for layer, g in enumerate(G[name]):
                if g is None or (name, layer) in self.shipped:
                    continue
                g = g[:, :W_IN_COLS] if name == "a_w_in" else g
                piece = _shards(g, axis - (0 if name == "w_kv" else 1)).astype(BF16)
                out.append((name, layer, piece.reshape(N_CHIPS, -1, piece.shape[-1])))
        return out

    def flush(self, G):
        ready = self._ready(G)
        self.in_flight = [r[:2] for r in ready]
        self.shipped += self.in_flight
        return ChipExchange([r[2] for r in ready], scatter=True) if ready else None

    def landed(self, brought):
        self.received.update(zip(self.in_flight, brought))
        self.in_flight = []

    def flush_last(self, G):
        ready = self._ready(G)
        if ready:
            self.shipped += [r[:2] for r in ready]
            brought = chip_exchange([r[2] for r in ready], scatter=True, name="scatter_last")
            self.received.update(zip([r[:2] for r in ready], brought))

    def my_sums(self):
        sums = {}
        for name in MATRICES:
            layers = sorted(l for n, l in self.received if n == name)
            parts = [sum_slots(self.received[(name, l)], name=f"sum_{name}{l}") for l in layers]
            sums[name] = parts[0] if len(parts) == 1 else jnp.concatenate(parts, axis=0)
        return sums


def _as_2d(a):
    return a.reshape(-1, a.shape[-1])


def kernel(x, a_norm, a_w_in, a_conv, a_log, a_dt_bias, a_out_norm, a_w_out, kv_norm, w_kv, k_norm, b_norm, b_w_q, q_norm, b_w_out, ffn_norm, ffn_w_up, ffn_conv, ffn_w_down, loss_target, m_a_norm, m_a_w_in, m_a_conv, m_a_log, m_a_dt_bias, m_a_out_norm, m_a_w_out, m_kv_norm, m_w_kv, m_k_norm, m_b_norm, m_b_w_q, m_q_norm, m_b_w_out, m_ffn_norm, m_ffn_w_up, m_ffn_conv, m_ffn_w_down, v_a_norm, v_a_w_in, v_a_conv, v_a_log, v_a_dt_bias, v_a_out_norm, v_a_w_out, v_kv_norm, v_w_kv, v_k_norm, v_b_norm, v_b_w_q, v_q_norm, v_b_w_out, v_ffn_norm, v_ffn_w_up, v_ffn_conv, v_ffn_w_down):
    local = dict(a_norm=a_norm, a_w_in=a_w_in, a_conv=a_conv, a_log=a_log, a_dt_bias=a_dt_bias, a_out_norm=a_out_norm,
                 a_w_out=a_w_out, kv_norm=kv_norm, w_kv=w_kv, k_norm=k_norm, b_norm=b_norm, b_w_q=b_w_q, q_norm=q_norm,
                 b_w_out=b_w_out, ffn_norm=ffn_norm, ffn_w_up=ffn_w_up, ffn_conv=ffn_conv, ffn_w_down=ffn_w_down)
    mom = dict(a_norm=m_a_norm, a_w_in=m_a_w_in, a_conv=m_a_conv, a_log=m_a_log, a_dt_bias=m_a_dt_bias, a_out_norm=m_a_out_norm,
               a_w_out=m_a_w_out, kv_norm=m_kv_norm, w_kv=m_w_kv, k_norm=m_k_norm, b_norm=m_b_norm, b_w_q=m_b_w_q, q_norm=m_q_norm,
               b_w_out=m_b_w_out, ffn_norm=m_ffn_norm, ffn_w_up=m_ffn_w_up, ffn_conv=m_ffn_conv, ffn_w_down=m_ffn_w_down)
    var = dict(a_norm=v_a_norm, a_w_in=v_a_w_in, a_conv=v_a_conv, a_log=v_a_log, a_dt_bias=v_a_dt_bias, a_out_norm=v_a_out_norm,
               a_w_out=v_a_w_out, kv_norm=v_kv_norm, w_kv=v_w_kv, k_norm=v_k_norm, b_norm=v_b_norm, b_w_q=v_b_w_q, q_norm=v_q_norm,
               b_w_out=v_b_w_out, ffn_norm=v_ffn_norm, ffn_w_up=v_ffn_w_up, ffn_conv=v_ffn_conv, ffn_w_down=v_ffn_w_down)
    chip = 2 * lax.axis_index("x") + lax.axis_index("y")

    mats = list(MATRICES)
    small_sharded = list(SMALL_SHARDED)
    traffic = Traffic(local)
    W, (vectors,) = traffic.fetch_first([_pack([local[n] for n in small_sharded])])
    W.update({n: local[n] for n in SMALL_REPLICATED})
    shard_shapes = [local[n].shape for n in small_sharded]
    per_chip = [_unpack(vectors[j], shard_shapes) for j in range(N_CHIPS)]
    for i, n in enumerate(small_sharded):
        W[n] = _unshard(jnp.stack([per_chip[j][i] for j in range(N_CHIPS)]), SMALL_SHARDED[n])

    T = x.shape[1]
    loss_part, dx, G = local_step(x.reshape(T, D_MODEL), loss_target.reshape(T, D_MODEL), W, traffic)

    traffic.flush_last(G)
    sums = traffic.my_sums()
    mine = [sums[n] for n in mats]
    theirs = sibling_swap(mine, name="swap_grads")

    small_full = {n: _stack(G[n]) for n in SMALL_ORDER}
    packed = _pack([small_full[n] for n in SMALL_ORDER] + [loss_part])
    total = sum_slots(all_gather_devices(packed, name="gather_small"), name="sum_small")
    small_shapes = [small_full[n].shape for n in SMALL_ORDER] + [loss_part.shape]
    summed = dict(zip(SMALL_ORDER + ("loss",), _unpack(total, small_shapes)))
    loss = jnp.sum(summed.pop("loss"))
    for n, axis in SMALL_SHARDED.items():
        size = local[n].shape[axis]
        summed[n] = lax.dynamic_slice_in_dim(summed[n], chip * size, size, axis)

    grads, deltas, new_m, new_v = {}, {}, {}, {}
    for n, p_mine, p_theirs in zip(mats, mine, theirs):
        outs = adamw(_as_2d(local[n]), [p_mine, p_theirs], _as_2d(mom[n]), _as_2d(var[n]), name=f"adamw_{n}")
        grads[n], deltas[n], new_m[n], new_v[n] = [o.reshape(local[n].shape) for o in outs]
    small_local_shapes = [local[n].shape for n in SMALL_ORDER]
    outs = adamw(_pack([local[n] for n in SMALL_ORDER]), [_pack([summed[n] for n in SMALL_ORDER])],
                 _pack([mom[n] for n in SMALL_ORDER]), _pack([var[n] for n in SMALL_ORDER]), name="adamw_small")
    for d, o in zip((grads, deltas, new_m, new_v), outs):
        d.update(zip(SMALL_ORDER, _unpack(o, small_local_shapes)))

    return (loss, dx.reshape(x.shape), *[grads[n] for n in WEIGHT_ORDER], *[deltas[n] for n in WEIGHT_ORDER],
            *[new_m[n] for n in WEIGHT_ORDER], *[new_v[n] for n in WEIGHT_ORDER])
```

```python
import math

import jax
import jax.numpy as jnp
from jax import lax
from jax.experimental import pallas as pl
from jax.experimental.pallas import tpu as pltpu

F32 = jnp.float32
BF16 = jnp.bfloat16

D_MODEL = 1024
HEADS = 8
HEAD_DIM = 128
GDN_CONV = 4
GDN_CHUNK = 64
D_FF = 2816
FFN_CONV = 3
EPS = 1e-6
N_A = 2
N_B = 2
DEPTH = N_A + N_B
W_IN_COLS = 4 * D_MODEL + 2 * HEADS
W_IN_PAD = 4 * D_MODEL + 128
Z_BLOCK = 3 * D_MODEL // 128
AB_BLOCK = 4 * D_MODEL // 128

ADAM_LR = 0.001
ADAM_B1 = 0.9
ADAM_B2 = 0.999
ADAM_EPS = 1e-08
ADAM_WD = 0.01
ADAM_STEP = 10

LANES = 128
SUBLANES = 8
VMEM_LIMIT = 56 * 1024 * 1024
HALO = SUBLANES
N_CHIPS = 4
N_DEV = 8

HI = lax.Precision.HIGHEST
MESH = pl.DeviceIdType.MESH
ANY = pl.BlockSpec(memory_space=pl.ANY)


def _cp(*sem):
    return pltpu.CompilerParams(dimension_semantics=sem, vmem_limit_bytes=VMEM_LIMIT)


def _tile(n, want, align=SUBLANES):
    t = (min(n, want) // align) * align
    while t > 0 and n % t:
        t -= align
    return t if t > 0 else n


def _dot(a, b, precision=None):
    return jnp.dot(a, b, preferred_element_type=F32, precision=precision)


def _dot_nt(a, b, precision=None):
    return lax.dot_general(a, b, (((1,), (1,)), ((), ())), preferred_element_type=F32, precision=precision)


def _dot_tn(a, b, precision=None):
    return lax.dot_general(a, b, (((0,), (0,)), ((), ())), preferred_element_type=F32, precision=precision)


def _bf(x):
    return x.astype(BF16)


def _sigmoid(x):
    return 0.5 * jnp.tanh(0.5 * x) + 0.5


def _softplus(x):
    return jnp.maximum(x, 0.0) + jnp.log(1.0 + jnp.exp(-jnp.abs(x)))


def _silu(x):
    return x * _sigmoid(x)


def _silu_and_grad(x):
    s = _sigmoid(x)
    return x * s, s * (1.0 + x * (1.0 - s))


def _fold_rows(v):
    return jnp.sum(v.reshape(v.shape[0] // SUBLANES, SUBLANES, v.shape[1]), axis=0)


def _accumulate(ref, value, axis):
    @pl.when(pl.program_id(axis) == 0)
    def _():
        ref[...] = jnp.zeros_like(ref)
    ref[...] += value


TILE_BUDGET = 44 * 1024 * 1024


def _rows_that_fit(T, fixed_bytes, row_bytes, want=1024):
    tm = _tile(T, want)
    while tm > SUBLANES and 2 * (fixed_bytes + tm * row_bytes) > TILE_BUDGET:
        tm //= 2
    return tm


def norm_matmul(x, gain, w, *, name):
    T, D = x.shape
    N = w.shape[1]
    tn = N
    tm = _rows_that_fit(T, D * N * 2, D * 4 + D * 2 + N * 4)

    def body(x_ref, g_ref, w_ref, y_ref, h_ref):
        @pl.when(pl.program_id(1) == 0)
        def _():
            xf = x_ref[...]
            r = lax.rsqrt(jnp.mean(xf * xf, axis=-1, keepdims=True) + EPS)
            h_ref[...] = _bf(xf * r * g_ref[...])
        y_ref[...] = _dot(h_ref[...], w_ref[...])

    return pl.pallas_call(
        body, name=name, grid=(T // tm, N // tn),
        in_specs=[pl.BlockSpec((tm, D), lambda i, j: (i, 0)), pl.BlockSpec((1, D), lambda i, j: (0, 0)),
                  pl.BlockSpec((D, tn), lambda i, j: (0, j))],
        out_specs=[pl.BlockSpec((tm, tn), lambda i, j: (i, j)), pl.BlockSpec((tm, D), lambda i, j: (i, 0))],
        out_shape=[jax.ShapeDtypeStruct((T, N), F32), jax.ShapeDtypeStruct((T, D), BF16)],
        compiler_params=_cp("parallel", "arbitrary"),
    )(x, gain, w)


def matmul_residual(a, w, res, *, name):
    T, K = a.shape
    N = w.shape[1]
    tn = N
    tm = _rows_that_fit(T, K * N * 2, K * a.dtype.itemsize + 2 * N * 4)

    def body(a_ref, w_ref, r_ref, o_ref):
        o_ref[...] = r_ref[...] + _dot(_bf(a_ref[...]), w_ref[...])

    return pl.pallas_call(
        body, name=name, grid=(T // tm, N // tn),
        in_specs=[pl.BlockSpec((tm, K), lambda i, j: (i, 0)), pl.BlockSpec((K, tn), lambda i, j: (0, j)),
                  pl.BlockSpec((tm, tn), lambda i, j: (i, j))],
        out_specs=pl.BlockSpec((tm, tn), lambda i, j: (i, j)),
        out_shape=jax.ShapeDtypeStruct((T, N), F32),
        compiler_params=_cp("parallel", "parallel"),
    )(a, w, res)


def matmul_nt(dy, w, *, name, out_dtype=F32):
    T, N = dy.shape
    K = w.shape[0]
    tk = K
    tm = _rows_that_fit(T, K * N * 2, N * dy.dtype.itemsize + K * jnp.dtype(out_dtype).itemsize)

    def body(dy_ref, w_ref, o_ref):
        o_ref[...] = _dot_nt(_bf(dy_ref[...]), w_ref[...]).astype(out_dtype)

    return pl.pallas_call(
        body, name=name, grid=(T // tm, K // tk),
        in_specs=[pl.BlockSpec((tm, N), lambda i, j: (i, 0)), pl.BlockSpec((tk, N), lambda i, j: (j, 0))],
        out_specs=pl.BlockSpec((tm, tk), lambda i, j: (i, j)),
        out_shape=jax.ShapeDtypeStruct((T, K), out_dtype),
        compiler_params=_cp("parallel", "parallel"),
    )(dy, w)


def matmul_nt_normbwd(parts, w, x, gain, dres, *, name, exchange=None):
    T = x.shape[0]
    D = w.shape[0]
    tm = _tile(T, 256)
    n = len(parts)

    def body(*refs):
        x_ref, g_ref, dr_ref, dx_ref, dg_ref = refs[2 * n:]
        dh = _dot_nt(_bf(refs[0][...]), refs[n][...])
        for i in range(1, n):
            dh = dh + _dot_nt(_bf(refs[i][...]), refs[n + i][...])
        xf = x_ref[...]
        r = lax.rsqrt(jnp.mean(xf * xf, axis=-1, keepdims=True) + EPS)
        xh = xf * r
        dxh = dh * g_ref[...]
        dx_ref[...] = dr_ref[...] + r * (dxh - xh * jnp.mean(dxh * xh, axis=-1, keepdims=True))
        _accumulate(dg_ref, _fold_rows(dh * xh), 0)

    dy_specs = [pl.BlockSpec((tm, dy.shape[1]), lambda i: (i, 0)) for dy, _ in parts]
    w_specs = [pl.BlockSpec((D, dy.shape[1]), lambda i, b=b: (0, b)) for dy, b in parts]
    (dx, dgain), brought = hosted_call(
        body, name=name, grid=(T // tm,),
        in_specs=dy_specs + w_specs + [pl.BlockSpec((tm, D), lambda i: (i, 0)), pl.BlockSpec((1, D), lambda i: (0, 0)),
                                       pl.BlockSpec((tm, D), lambda i: (i, 0))],
        out_specs=[pl.BlockSpec((tm, D), lambda i: (i, 0)), pl.BlockSpec((SUBLANES, D), lambda i: (0, 0))],
        out_shape=[jax.ShapeDtypeStruct((T, D), F32), jax.ShapeDtypeStruct((SUBLANES, D), F32)],
        scratch_shapes=[], args=(*[dy for dy, _ in parts], *([w] * n), x, gain, dres), exchange=exchange)
    return (dx, dgain, brought) if exchange is not None else (dx, dgain)


def matmul_tn(a, dys, *, name):
    dys = dys if isinstance(dys, (list, tuple)) else [dys]
    T, K = a.shape
    N = dys[0].shape[1]
    tk = _tile(K, 1408, LANES)
    tn = _tile(N, 1408 if N <= 2816 else 512, LANES)
    tm = _tile(T, (4096 if tn <= 512 else 2048) // len(dys))

    def body(a_ref, *refs):
        dy = refs[0][...]
        for r in refs[1:-1]:
            dy = dy + r[...]
        _accumulate(refs[-1], _dot_tn(_bf(a_ref[...]), _bf(dy)), 2)

    return pl.pallas_call(
        body, name=name, grid=(K // tk, N // tn, T // tm),
        in_specs=[pl.BlockSpec((tm, tk), lambda i, j, t: (t, i))] + [pl.BlockSpec((tm, tn), lambda i, j, t: (t, j))] * len(dys),
        out_specs=pl.BlockSpec((tk, tn), lambda i, j, t: (i, j)),
        out_shape=jax.ShapeDtypeStruct((K, N), F32),
        compiler_params=_cp("parallel", "parallel", "arbitrary"),
    )(a, *dys)


def _halo_specs(T, tt, tc, col):
    per = tt // HALO
    last = T // HALO - 1
    return [pl.BlockSpec((HALO, tc), lambda j, i: (jnp.maximum(i * per - 1, 0), col(j))),
            pl.BlockSpec((tt, tc), lambda j, i: (i, col(j))),
            pl.BlockSpec((HALO, tc), lambda j, i: (jnp.minimum((i + 1) * per, last), col(j)))]


def _extend(prev_ref, cur_ref, next_ref, nt):
    i = pl.program_id(1)
    p = jnp.where(i > 0, prev_ref[...].astype(F32), 0.0)
    q = jnp.where(i < nt - 1, next_ref[...].astype(F32), 0.0)
    return jnp.concatenate([p, cur_ref[...].astype(F32), q], axis=0)


def _rows_before(e, s):
    return e if s == 0 else pltpu.roll(e, s, 0)


def _rows_after(e, s):
    return e if s == 0 else pltpu.roll(e, e.shape[0] - s, 0)


def _causal_conv(e, w, taps):
    y = w[taps - 1:taps, :] * e
    for s in range(1, taps):
        y = y + w[taps - 1 - s:taps - s, :] * _rows_before(e, s)
    return y


def _causal_conv_bwd(e, dc, w, taps, tt):
    lo, hi = HALO, HALO + tt
    dx = w[taps - 1:taps, :] * dc
    dws = [None] * taps
    dws[taps - 1] = jnp.sum((e * dc)[lo:hi], axis=0, keepdims=True)
    for s in range(1, taps):
        dx = dx + w[taps - 1 - s:taps - s, :] * _rows_after(dc, s)
        dws[taps - 1 - s] = jnp.sum((_rows_before(e, s) * dc)[lo:hi], axis=0, keepdims=True)
    dw = jnp.concatenate(dws + [jnp.zeros((SUBLANES - taps, e.shape[1]), F32)], axis=0)
    return dx[lo:hi], dw


def _qkv_kind(col_block):
    return (col_block >= HEADS).astype(jnp.int32) + (col_block >= 2 * HEADS).astype(jnp.int32)


def _l2norm_scale(kind):
    return jnp.where(kind == 0, HEAD_DIM ** -0.5, 1.0)


GDN_IN_ROWS = 256


def gdn_in_conv(x, gain, w, conv_w, *, name, exchange=None):
    T, D = x.shape
    N = w.shape[1]
    C = 3 * D_MODEL
    tm = _tile(T, GDN_IN_ROWS)
    n = T // tm

    def body(x_ref, g_ref, w_ref, cw_ref, p_ref, h_ref, o_ref, held):
        @pl.when(pl.program_id(0) == 0)
        def _():
            held[...] = jnp.zeros_like(held)

        for block in range(3 * HEADS):
            cols = slice(block * HEAD_DIM, (block + 1) * HEAD_DIM)
            s = _silu(_causal_conv(held[:, cols], cw_ref[:, cols], GDN_CONV))[HALO:]
            if block < 2 * HEADS:
                scale = HEAD_DIM ** -0.5 if block < HEADS else 1.0
                s = s * (lax.rsqrt(jnp.sum(s * s, axis=-1, keepdims=True) + EPS) * scale)
            o_ref[:, cols] = s
        xf = x_ref[...]
        h = _bf(xf * lax.rsqrt(jnp.mean(xf * xf, axis=-1, keepdims=True) + EPS) * g_ref[...])
        h_ref[...] = h
        proj = _dot(h, w_ref[...])
        p_ref[...] = proj
        held[0:HALO, :] = held[tm:tm + HALO, :]
        held[HALO:HALO + tm, :] = proj[:, :C]

    (proj, h, qkv), brought = hosted_call(
        body, name=name, grid=(n + 1,),
        in_specs=[pl.BlockSpec((tm, D), lambda i: (jnp.minimum(i, n - 1), 0)), pl.BlockSpec((1, D), lambda i: (0, 0)),
                  pl.BlockSpec((D, N), lambda i: (0, 0)), pl.BlockSpec((SUBLANES, C), lambda i: (0, 0))],
        out_specs=[pl.BlockSpec((tm, N), lambda i: (jnp.minimum(i, n - 1), 0)),
                   pl.BlockSpec((tm, D), lambda i: (jnp.minimum(i, n - 1), 0)),
                   pl.BlockSpec((tm, C), lambda i: (jnp.maximum(i - 1, 0), 0))],
        out_shape=[jax.ShapeDtypeStruct((T, N), F32), jax.ShapeDtypeStruct((T, D), BF16), jax.ShapeDtypeStruct((T, C), F32)],
        scratch_shapes=[pltpu.VMEM((HALO + tm, C), F32)],
        args=(x, gain, w, conv_w), exchange=exchange)
    return proj, h, qkv, brought


def gdn_conv_bwd(proj, conv_w, dqkv, *, name):
    T = proj.shape[0]
    tt, tc = _tile(T, 1024), HEAD_DIM
    nt = T // tt

    def body(p_ref, c_ref, n_ref, w_ref, dp_ref, dc_ref, dn_ref, dx_ref, dw_ref):
        kind = _qkv_kind(pl.program_id(0))
        w = w_ref[...]
        e = _extend(p_ref, c_ref, n_ref, nt)
        c = _causal_conv(e, w, GDN_CONV)
        s, s_grad = _silu_and_grad(c)
        dy = _extend(dp_ref, dc_ref, dn_ref, nt)
        r = lax.rsqrt(jnp.sum(s * s, axis=-1, keepdims=True) + EPS)
        y = s * r
        ds_norm = r * _l2norm_scale(kind) * (dy - y * jnp.sum(dy * y, axis=-1, keepdims=True))
        ds = jnp.where(kind == 2, dy, ds_norm)
        dx, dw = _causal_conv_bwd(e, ds * s_grad, w, GDN_CONV, tt)
        dx_ref[...] = _bf(dx)
        _accumulate(dw_ref, dw, 1)

    return pl.pallas_call(
        body, name=name, grid=(3 * HEADS, nt),
        in_specs=_halo_specs(T, tt, tc, lambda j: j) + [pl.BlockSpec((SUBLANES, tc), lambda j, i: (0, j))]
        + _halo_specs(T, tt, tc, lambda j: j),
        out_specs=[pl.BlockSpec((tt, tc), lambda j, i: (i, j)), pl.BlockSpec((SUBLANES, tc), lambda j, i: (0, j))],
        out_shape=[jax.ShapeDtypeStruct((T, 3 * D_MODEL), BF16), jax.ShapeDtypeStruct((SUBLANES, 3 * D_MODEL), F32)],
        compiler_params=_cp("parallel", "arbitrary"),
    )(proj, proj, proj, conv_w, dqkv, dqkv, dqkv)


FFN_COLS = 256


FFN_UP_ROWS = 256


def ffn_up_act(x, gain, w, conv_w, *, name, exchange=None):
    T, D = x.shape
    N = w.shape[1]
    F = N // 2
    tm, tc = _tile(T, FFN_UP_ROWS), FFN_COLS
    n = T // tm

    def body(x_ref, g_ref, w_ref, cw_ref, u_ref, h_ref, act_ref, held):
        i = pl.program_id(0)

        @pl.when(i == 0)
        def _():
            held[...] = jnp.zeros_like(held)

        for c in range(F // tc):
            gate_cols, up_cols = slice(c * tc, (c + 1) * tc), slice(F + c * tc, F + (c + 1) * tc)
            gate = _causal_conv(held[:, gate_cols], cw_ref[:, gate_cols], FFN_CONV)
            up_ = _causal_conv(held[:, up_cols], cw_ref[:, up_cols], FFN_CONV)
            act_ref[:, gate_cols] = _bf((_silu(gate) * up_)[HALO:])
        xf = x_ref[...]
        h = _bf(xf * lax.rsqrt(jnp.mean(xf * xf, axis=-1, keepdims=True) + EPS) * g_ref[...])
        h_ref[...] = h
        u = _dot(h, w_ref[...])
        u_ref[...] = u
        held[0:HALO, :] = held[tm:tm + HALO, :]
        held[HALO:HALO + tm, :] = u

    (u, h, act), brought = hosted_call(
        body, name=name, grid=(n + 1,),
        in_specs=[pl.BlockSpec((tm, D), lambda i: (jnp.minimum(i, n - 1), 0)), pl.BlockSpec((1, D), lambda i: (0, 0)),
                  pl.BlockSpec((D, N), lambda i: (0, 0)), pl.BlockSpec((SUBLANES, N), lambda i: (0, 0))],
        out_specs=[pl.BlockSpec((tm, N), lambda i: (jnp.minimum(i, n - 1), 0)),
                   pl.BlockSpec((tm, D), lambda i: (jnp.minimum(i, n - 1), 0)),
                   pl.BlockSpec((tm, F), lambda i: (jnp.maximum(i - 1, 0), 0))],
        out_shape=[jax.ShapeDtypeStruct((T, N), F32), jax.ShapeDtypeStruct((T, D), BF16), jax.ShapeDtypeStruct((T, F), BF16)],
        scratch_shapes=[pltpu.VMEM((HALO + tm, N), F32)],
        args=(x, gain, w, conv_w), exchange=exchange)
    return u, h, act, brought


def ffn_act_bwd(u, conv_w, dact, *, name):
    T = u.shape[0]
    tt, tc = _tile(T, 512), FFN_COLS
    half = D_FF // tc
    nt = T // tt

    def body(gp, gc, gn, up, uc, un, wg_ref, wu_ref, dp, dc_, dn, dug_ref, duu_ref, dwg_ref, dwu_ref):
        wg, wu = wg_ref[...], wu_ref[...]
        eg, eu = _extend(gp, gc, gn, nt), _extend(up, uc, un, nt)
        gate, up_ = _causal_conv(eg, wg, FFN_CONV), _causal_conv(eu, wu, FFN_CONV)
        da = _extend(dp, dc_, dn, nt)
        act, act_grad = _silu_and_grad(gate)
        dxg, dwg = _causal_conv_bwd(eg, da * up_ * act_grad, wg, FFN_CONV, tt)
        dxu, dwu = _causal_conv_bwd(eu, da * act, wu, FFN_CONV, tt)
        dug_ref[...] = _bf(dxg)
        duu_ref[...] = _bf(dxu)
        _accumulate(dwg_ref, dwg, 1)
        _accumulate(dwu_ref, dwu, 1)

    return pl.pallas_call(
        body, name=name, grid=(half, nt),
        in_specs=_halo_specs(T, tt, tc, lambda j: j) + _halo_specs(T, tt, tc, lambda j: j + half)
        + [pl.BlockSpec((SUBLANES, tc), lambda j, i: (0, j)), pl.BlockSpec((SUBLANES, tc), lambda j, i: (0, j + half))]
        + _halo_specs(T, tt, tc, lambda j: j),
        out_specs=[pl.BlockSpec((tt, tc), lambda j, i: (i, j)), pl.BlockSpec((tt, tc), lambda j, i: (i, j)),
                   pl.BlockSpec((SUBLANES, tc), lambda j, i: (0, j)), pl.BlockSpec((SUBLANES, tc), lambda j, i: (0, j))],
        out_shape=[jax.ShapeDtypeStruct((T, D_FF), BF16), jax.ShapeDtypeStruct((T, D_FF), BF16),
                   jax.ShapeDtypeStruct((SUBLANES, D_FF), F32), jax.ShapeDtypeStruct((SUBLANES, D_FF), F32)],
        compiler_params=_cp("parallel", "arbitrary"),
    )(u, u, u, u, u, u, conv_w, conv_w, dact, dact, dact)


def head_norm_fwd(x, gain, z=None, *, x_col=0, z_col=0, name, out_dtype=F32):
    T = x.shape[0]
    tt = _tile(T, 512)
    gated = z is not None

    def body(*refs):
        x_ref, g_ref = refs[0], refs[1]
        o_ref = refs[-1]
        for h in range(HEADS):
            cols = slice(h * HEAD_DIM, (h + 1) * HEAD_DIM)
            xf = x_ref[:, cols]
            y = xf * lax.rsqrt(jnp.mean(xf * xf, axis=-1, keepdims=True) + EPS) * g_ref[...]
            if gated:
                y = y * _silu(refs[2][:, cols])
            o_ref[:, cols] = y.astype(out_dtype)

    wide = lambda col: pl.BlockSpec((tt, D_MODEL), lambda i: (i, col // HEADS))
    ins = [wide(x_col), pl.BlockSpec((1, HEAD_DIM), lambda i: (0, 0))]
    args = [x, gain]
    if gated:
        ins.append(wide(z_col))
        args.append(z)
    return pl.pallas_call(
        body, name=name, grid=(T // tt,), in_specs=ins, out_specs=wide(0),
        out_shape=jax.ShapeDtypeStruct((T, D_MODEL), out_dtype),
        compiler_params=_cp("parallel"),
    )(*args)


def head_norm_bwd(x, gain, dys, z=None, *, x_col=0, z_col=0, name, dx_dtype=F32):
    T = x.shape[0]
    tt = _tile(T, 512)
    gated = z is not None
    nd = len(dys)

    def body(*refs):
        x_ref, g_ref = refs[0], refs[1]
        outs = refs[2 + nd + (1 if gated else 0):]
        dx_ref, dg_ref = outs[0], outs[-1]
        folds = []
        for h in range(HEADS):
            cols = slice(h * HEAD_DIM, (h + 1) * HEAD_DIM)
            xf = x_ref[:, cols]
            r = lax.rsqrt(jnp.mean(xf * xf, axis=-1, keepdims=True) + EPS)
            xh = xf * r
            dy = refs[2][:, cols].astype(F32)
            for d_ref in refs[3:2 + nd]:
                dy = dy + d_ref[:, cols].astype(F32)
            if gated:
                gate, gate_grad = _silu_and_grad(refs[2 + nd][:, cols])
                outs[1][:, cols] = _bf(dy * xh * g_ref[...] * gate_grad)
                dn = dy * gate
            else:
                dn = dy
            dxh = dn * g_ref[...]
            dx_ref[:, cols] = (r * (dxh - xh * jnp.mean(dxh * xh, axis=-1, keepdims=True))).astype(dx_dtype)
            folds.append(_fold_rows(dn * xh))
        _accumulate(dg_ref, jnp.concatenate(folds, axis=1), 0)

    wide = lambda col: pl.BlockSpec((tt, D_MODEL), lambda i: (i, col // HEADS))
    ins = [wide(x_col), pl.BlockSpec((1, HEAD_DIM), lambda i: (0, 0))] + [wide(0)] * nd
    args = [x, gain] + list(dys)
    outs = [wide(0)]
    shapes = [jax.ShapeDtypeStruct((T, D_MODEL), dx_dtype)]
    if gated:
        ins.append(wide(z_col))
        args.append(z)
        outs.append(wide(0))
        shapes.append(jax.ShapeDtypeStruct((T, D_MODEL), BF16))
    outs.append(pl.BlockSpec((SUBLANES, D_MODEL), lambda i: (0, 0)))
    shapes.append(jax.ShapeDtypeStruct((SUBLANES, D_MODEL), F32))
    return pl.pallas_call(
        body, name=name, grid=(T // tt,), in_specs=ins, out_specs=outs, out_shape=shapes,
        compiler_params=_cp("arbitrary"),
    )(*args)


def gates_fwd(proj, a_log, dt_bias, *, name):
    T = proj.shape[0]
    tt = _tile(T, 1024)

    def body(p_ref, al_ref, dt_ref, o_ref):
        p = p_ref[...]
        lane = lax.broadcasted_iota(jnp.int32, p.shape, 1)
        o_ref[...] = jnp.where(lane < HEADS, -jnp.exp(al_ref[...]) * _softplus(p + dt_ref[...]), _sigmoid(p))

    return pl.pallas_call(
        body, name=name, grid=(T // tt,),
        in_specs=[pl.BlockSpec((tt, LANES), lambda i: (i, AB_BLOCK)), pl.BlockSpec((1, LANES), lambda i: (0, 0)),
                  pl.BlockSpec((1, LANES), lambda i: (0, 0))],
        out_specs=pl.BlockSpec((tt, LANES), lambda i: (i, 0)),
        out_shape=jax.ShapeDtypeStruct((T, LANES), F32),
        compiler_params=_cp("parallel"),
    )(proj, a_log, dt_bias)


def gates_bwd(proj, a_log, dt_bias, dgate, *, name):
    T = proj.shape[0]
    tt = _tile(T, 1024)

    def body(p_ref, al_ref, dt_ref, d_ref, dp_ref, dal_ref, ddt_ref):
        p, d = p_ref[...], d_ref[...]
        lane = lax.broadcasted_iota(jnp.int32, p.shape, 1)
        ea = jnp.exp(al_ref[...])
        pa = p + dt_ref[...]
        da = -d * ea * _sigmoid(pa)
        b = _sigmoid(p)
        dp_ref[...] = _bf(jnp.where(lane < HEADS, da, jnp.where(lane < 2 * HEADS, d * b * (1.0 - b), 0.0)))
        _accumulate(dal_ref, _fold_rows(jnp.where(lane < HEADS, -d * ea * _softplus(pa), 0.0)), 0)
        _accumulate(ddt_ref, _fold_rows(jnp.where(lane < HEADS, da, 0.0)), 0)

    acc = pl.BlockSpec((SUBLANES, LANES), lambda i: (0, 0))
    return pl.pallas_call(
        body, name=name, grid=(T // tt,),
        in_specs=[pl.BlockSpec((tt, LANES), lambda i: (i, AB_BLOCK)), pl.BlockSpec((1, LANES), lambda i: (0, 0)),
                  pl.BlockSpec((1, LANES), lambda i: (0, 0)), pl.BlockSpec((tt, LANES), lambda i: (i, 0))],
        out_specs=[pl.BlockSpec((tt, LANES), lambda i: (i, 0)), acc, acc],
        out_shape=[jax.ShapeDtypeStruct((T, LANES), BF16), jax.ShapeDtypeStruct((SUBLANES, LANES), F32),
                   jax.ShapeDtypeStruct((SUBLANES, LANES), F32)],
        compiler_params=_cp("arbitrary"),
    )(proj, a_log, dt_bias, dgate)


def loss_fwd(y, target, *, name):
    T, D = y.shape
    tt = _tile(T, 512)

    def body(y_ref, t_ref, dy_ref, l_ref):
        d = y_ref[...] - t_ref[...]
        dy_ref[...] = d * (1.0 / D)
        sq = d * d
        lanes = sq[:, 0:LANES]
        for c in range(1, D // LANES):
            lanes = lanes + sq[:, c * LANES:(c + 1) * LANES]
        _accumulate(l_ref, _fold_rows(lanes) * (0.5 / D), 0)

    return pl.pallas_call(
        body, name=name, grid=(T // tt,),
        in_specs=[pl.BlockSpec((tt, D), lambda i: (i, 0)), pl.BlockSpec((tt, D), lambda i: (i, 0))],
        out_specs=[pl.BlockSpec((tt, D), lambda i: (i, 0)), pl.BlockSpec((SUBLANES, LANES), lambda i: (0, 0))],
        out_shape=[jax.ShapeDtypeStruct((T, D), F32), jax.ShapeDtypeStruct((SUBLANES, LANES), F32)],
        compiler_params=_cp("arbitrary"),
    )(y, target)


def _split_bf16(x):
    hi = _bf(x)
    return hi, _bf(x - hi.astype(F32))


def _dot3(a, b, dot=_dot):
    return dot(a[0], b[0]) + dot(a[0], b[1]) + dot(a[1], b[0])


def _each(fn, *lists):
    return [fn(*args) for args in zip(*lists)]


def _unit_lower_inverses(lows):
    c = lows[0].shape[0]
    ii = lax.broadcasted_iota(jnp.int32, (c, c), 0)
    jj = lax.broadcasted_iota(jnp.int32, (c, c), 1)
    eye = jnp.where(ii == jj, 1.0, 0.0)
    invs = _each(lambda low: eye - low, lows)
    powers = _each(_split_bf16, lows)
    for _ in range(int(math.log2(c)) - 1):
        powers = _each(lambda p: _split_bf16(_dot3(p, p)), powers)
        invs = _each(lambda inv, p: inv + _dot3(_split_bf16(inv), p), invs, powers)
    return invs


def _gdn_chunks(heads):
    q, k, v, a_col, a_row, b_col, s0 = (list(t) for t in zip(*heads))
    c = q[0].shape[0]
    ii = lax.broadcasted_iota(jnp.int32, (c, c), 0)
    jj = lax.broadcasted_iota(jnp.int32, (c, c), 1)
    tri, strict = ii >= jj, ii > jj
    g_col = _each(lambda ar: jnp.sum(jnp.where(tri, ar, 0.0), axis=1, keepdims=True), a_row)
    g_row = _each(lambda ac: jnp.sum(jnp.where(ii <= jj, ac, 0.0), axis=0, keepdims=True), a_col)
    gam = _each(lambda gc, gr: jnp.exp(jnp.where(tri, gc - gr, -jnp.inf)), g_col, g_row)
    g_last = _each(lambda ac: jnp.sum(ac, axis=0, keepdims=True), a_col)
    gam_col = _each(jnp.exp, g_col)
    del_col = _each(lambda gl, gc: jnp.exp(gl - gc), g_last, g_col)
    kb = _each(lambda k_, b: k_ * b, k, b_col)
    m = _each(lambda kb_, k_: _dot_nt(_bf(kb_), _bf(k_)), kb, k)
    ks = _each(lambda k_, s: _dot(_bf(k_), _bf(s)), k, s0)
    qk = _each(lambda q_, k_: _dot_nt(_bf(q_), _bf(k_)), q, k)
    inv = _unit_lower_inverses(_each(lambda m_, g: jnp.where(strict, m_ * g, 0.0), m, gam))
    e = _each(lambda v_, gc, ks_: v_ - gc * ks_, v, gam_col, ks)
    inv = _each(_split_bf16, inv)
    vn = _each(lambda inv_, b, e_: _dot3(inv_, _split_bf16(b * e_)), inv, b_col, e)
    p = _each(lambda qk_, g: jnp.where(tri, qk_ * g, 0.0), qk, gam)
    return [dict(tri=tri, strict=strict, ii=ii, jj=jj, gam=gam[g], g_last=g_last[g], gam_col=gam_col[g], del_col=del_col[g],
                 kb=kb[g], m=m[g], inv=inv[g], ks=ks[g], e=e[g], vn=vn[g], qk=qk[g], p=p[g]) for g in range(len(heads))]


GDN_HEADS_PER_STEP = HEADS


def _head_cols(ref, g):
    return ref[:, g * HEAD_DIM:(g + 1) * HEAD_DIM]


def _load_heads(q_ref, k_ref, v_ref, ar_ref, br_ref, states):
    return [(_head_cols(q_ref, g), _head_cols(k_ref, g), _head_cols(v_ref, g), ar_ref[g, 0].T, ar_ref[g, 0], br_ref[g, 0].T, states(g))
            for g in range(GDN_HEADS_PER_STEP)]


def gdn_fwd(qkv, a_row, b_row, *, name, exchange=None):
    T = qkv.shape[0]
    C = GDN_CHUNK
    N = T // C
    G = GDN_HEADS_PER_STEP

    def body(q_ref, k_ref, v_ref, ar_ref, br_ref, o_ref, s_ref, state):
        @pl.when(pl.program_id(1) == 0)
        def _():
            state[...] = jnp.zeros_like(state)
        loaded = _load_heads(q_ref, k_ref, v_ref, ar_ref, br_ref, lambda g: state[g])
        ws = _gdn_chunks(loaded)
        qs = _each(lambda h: _dot(_bf(h[0]), _bf(h[6])), loaded)
        pv = _each(lambda w: _dot(_bf(w["p"]), _bf(w["vn"])), ws)
        kv = _each(lambda h, w: _dot_tn(_bf(w["del_col"] * h[1]), _bf(w["vn"])), loaded, ws)
        for g, w in enumerate(ws):
            s0 = loaded[g][6]
            s_ref[g, 0] = s0
            o_ref[:, g * HEAD_DIM:(g + 1) * HEAD_DIM] = w["gam_col"] * qs[g] + pv[g]
            state[g] = jnp.exp(w["g_last"]) * s0 + kv[g]

    per = HEADS // G
    blk = lambda off: pl.BlockSpec((C, G * HEAD_DIM), lambda h, n: (n, off * per + h))
    row = pl.BlockSpec((G, 1, 1, C), lambda h, n: (h, n, 0, 0))
    (o, states), brought = hosted_call(
        body, name=name, grid=(per, N),
        in_specs=[blk(0), blk(1), blk(2), row, row],
        out_specs=[blk(0), pl.BlockSpec((G, 1, HEAD_DIM, HEAD_DIM), lambda h, n: (h, n, 0, 0))],
        out_shape=[jax.ShapeDtypeStruct((T, D_MODEL), F32), jax.ShapeDtypeStruct((HEADS, N, HEAD_DIM, HEAD_DIM), F32)],
        scratch_shapes=[pltpu.VMEM((G, HEAD_DIM, HEAD_DIM), F32)],
        args=(qkv, qkv, qkv, a_row, b_row), exchange=exchange)
    return o, states, brought


def gdn_bwd(qkv, a_row, b_row, states, do, *, name, exchange=None):
    T = qkv.shape[0]
    C = GDN_CHUNK
    N = T // C
    G = GDN_HEADS_PER_STEP
    per = HEADS // G
    rev = lambda n: N - 1 - n
    row = pl.BlockSpec((G, 1, 1, C), lambda h, n: (h, rev(n), 0, 0))

    def body(q_ref, k_ref, v_ref, ar_ref, br_ref, s_ref, do_ref, dqkv_ref, da_ref, db_ref, dstate):
        @pl.when(pl.program_id(1) == 0)
        def _():
            dstate[...] = jnp.zeros_like(dstate)
        loaded = _load_heads(q_ref, k_ref, v_ref, ar_ref, br_ref, lambda g: s_ref[g, 0])
        hs = _gdn_chunks(loaded)
        for g, d in enumerate(hs):
            q, k, _, _, _, b, s0 = loaded[g]
            d.update(q=q, k=k, b=b, s0=s0, ds1=dstate[g], dout=_head_cols(do_ref, g))
        rows = lambda t: jnp.sum(t, axis=1, keepdims=True)
        ii_col = lax.broadcasted_iota(jnp.int32, (C, 1), 0)

        def stage(**fns):
            for key, fn in fns.items():
                for d in hs:
                    d[key] = fn(d)

        stage(s0b=lambda d: _bf(d["s0"]), ds1b=lambda d: _bf(d["ds1"]), doutb=lambda d: _bf(d["dout"]),
              kbf=lambda d: _bf(d["k"]), qbf=lambda d: _bf(d["q"]), vnb=lambda d: _bf(d["vn"]))
        stage(dvn=lambda d: _dot_tn(_bf(d["p"]), d["doutb"]) + _dot(_bf(d["del_col"] * d["k"]), d["ds1b"]),
              dqk=lambda d: jnp.where(d["tri"], _dot_nt(d["doutb"], d["vnb"]), 0.0) * d["gam"],
              qs=lambda d: _dot(d["qbf"], d["s0b"]),
              dkd=lambda d: _dot_nt(d["vnb"], d["ds1b"]))
        stage(dr=lambda d: _dot3(d["inv"], _split_bf16(d["dvn"]), _dot_tn),
              dq=lambda d: d["gam_col"] * _dot_nt(d["doutb"], d["s0b"]) + _dot(_bf(d["dqk"]), d["kbf"]),
              dk=lambda d: _dot_tn(_bf(d["dqk"]), d["qbf"]) + d["del_col"] * d["dkd"],
              ddel=lambda d: d["del_col"] * rows(d["dkd"] * d["k"]))
        stage(dg=lambda d: d["gam_col"] * rows(d["dout"] * d["qs"]) - d["ddel"],
              dg_last=lambda d: jnp.sum(d["ddel"], axis=0, keepdims=True)
              + jnp.exp(d["g_last"]) * jnp.sum(rows(d["ds1"] * d["s0"]), axis=0, keepdims=True),
              dm=lambda d: jnp.where(d["strict"], -_dot_nt(_bf(d["dr"]), d["vnb"]), 0.0) * d["gam"],
              de=lambda d: d["b"] * d["dr"])
        stage(dkb=lambda d: _dot(_bf(d["dm"]), d["kbf"]),
              dks=lambda d: -d["gam_col"] * d["de"])
        stage(dk=lambda d: d["dk"] + _dot_tn(_bf(d["dm"]), _bf(d["kb"])) + _dot_nt(_bf(d["dks"]), d["s0b"]) + d["b"] * d["dkb"],
              dbeta=lambda d: rows(d["dr"] * d["e"]) + rows(d["dkb"] * d["k"]),
              ds0=lambda d: jnp.exp(d["g_last"]) * d["ds1"] + _dot_tn(_bf(d["gam_col"] * d["q"]), d["doutb"])
              + _dot_tn(d["kbf"], _bf(d["dks"])),
              wg=lambda d: d["dqk"] * d["qk"] + d["dm"] * d["m"])
        stage(dg=lambda d: d["dg"] - d["gam_col"] * rows(d["de"] * d["ks"]) + rows(d["wg"])
              - jnp.sum(d["wg"], axis=0, keepdims=True).T + jnp.where(ii_col == C - 1, d["dg_last"], 0.0))
        stage(da=lambda d: jnp.sum(jnp.where(d["ii"] >= d["jj"], d["dg"], 0.0), axis=0, keepdims=True),
              db=lambda d: d["dbeta"].T)
        for g, d in enumerate(hs):
            dstate[g] = d["ds0"]
            da_ref[g, 0] = d["da"]
            db_ref[g, 0] = d["db"]
            for part, key in enumerate(("dq", "dk", "de")):
                start = part * D_MODEL + g * HEAD_DIM
                dqkv_ref[:, start:start + HEAD_DIM] = d[key]

    assert per == 1
    blk = lambda off: pl.BlockSpec((C, G * HEAD_DIM), lambda h, n: (rev(n), off * per + h))
    (dqkv, da, db), brought = hosted_call(
        body, name=name, grid=(per, N),
        in_specs=[blk(0), blk(1), blk(2), row, row,
                  pl.BlockSpec((G, 1, HEAD_DIM, HEAD_DIM), lambda h, n: (h, rev(n), 0, 0)), blk(0)],
        out_specs=[pl.BlockSpec((C, 3 * D_MODEL), lambda h, n: (rev(n), 0)), row, row],
        out_shape=[jax.ShapeDtypeStruct((T, 3 * D_MODEL), F32)] + [jax.ShapeDtypeStruct((HEADS, N, 1, C), F32)] * 2,
        scratch_shapes=[pltpu.VMEM((G, HEAD_DIM, HEAD_DIM), F32)],
        args=(qkv, qkv, qkv, a_row, b_row, states, do), exchange=exchange)
    return dqkv, da, db, brought


SB_BLOCK = 128


SB_QBLOCKS = 8


def _sb_rows(j, blk):
    return pl.ds(pl.multiple_of(j * blk, blk), blk)


def _sb_tile(qb, k_ref, i, j, blk, live):
    z = _dot_nt(qb, _bf(k_ref[_sb_rows(j, blk), :]))
    t_idx = i * blk + lax.broadcasted_iota(jnp.int32, (blk, blk), 0)
    s_idx = j * blk + lax.broadcasted_iota(jnp.int32, (blk, blk), 1)
    mask = jnp.logical_and(s_idx < t_idx, live)
    lf = jnp.where(mask, -_softplus(z), 0.0)
    return z, mask, lf


SB_DEAD = 105.0


def sb_fwd(q, k, v, *, k_col=0, v_col=0, name):
    T = q.shape[0]
    blk = _tile(T, SB_BLOCK)
    P = min(SB_QBLOCKS, T // blk)
    scale = HEAD_DIM ** -0.5

    def body(q_ref, k_ref, v_ref, o_ref, l_ref, n_ref):
        iq = [P * pl.program_id(1) + p for p in range(P)]
        qb = [_bf(q_ref[p * blk:(p + 1) * blk, :] * scale) for p in range(P)]
        r_idx = lax.broadcasted_iota(jnp.int32, (blk, blk), 0)
        c_idx = lax.broadcasted_iota(jnp.int32, (blk, blk), 1)
        later = _bf(jnp.where(r_idx > c_idx, 1.0, 0.0))

        def live_blocks(jj, runs):
            return [jnp.logical_and(jj <= i, jnp.max(run) > -SB_DEAD) for i, run in zip(iq, runs)]

        def alive(carry):
            jj, _, runs, _ = carry
            some = False
            for f in live_blocks(jj, runs):
                some = jnp.logical_or(some, f)
            return some

        def step(carry):
            jj, accs, runs, visited = carry
            live = live_blocks(jj, runs)
            js = [jnp.maximum(i - jj, 0) for i in iq]
            tiles = _each(lambda q_, i, j, f: _sb_tile(q_, k_ref, i, j, blk, f), qb, iq, js, live)
            parts = _each(lambda t: _split_bf16(t[2]), tiles)
            after = _each(lambda run, s: run + _dot(s[0], later) + _dot(s[1], later), runs, parts)
            a = _each(lambda t, af: jnp.where(t[1], jnp.exp(t[0] + t[2] + af), 0.0), tiles, after)
            vb = _each(lambda j: _bf(v_ref[_sb_rows(j, blk), :]), js)
            accs = _each(lambda acc, a_, v_: acc + _dot(_bf(a_), v_), accs, a, vb)
            runs = _each(lambda run, t: run + jnp.sum(t[2], axis=1, keepdims=True), runs, tiles)
            visited = _each(lambda n, f: n + f.astype(jnp.int32), visited, live)
            return jj + 1, accs, runs, visited

        start = (jnp.int32(0), [jnp.zeros((blk, HEAD_DIM), F32)] * P, [jnp.zeros((blk, 1), F32)] * P, [jnp.int32(0)] * P)
        _, accs, runs, visited = lax.while_loop(alive, step, start)
        for p in range(P):
            o_ref[p * blk:(p + 1) * blk, :] = accs[p]
            l_ref[0, p] = runs[p].T
            n_ref[0, p] = jnp.full((SUBLANES, LANES), visited[p].astype(F32))

    return pl.pallas_call(
        body, name=name, grid=(HEADS, T // (P * blk)),
        in_specs=[pl.BlockSpec((P * blk, HEAD_DIM), lambda h, i: (i, h)), pl.BlockSpec((T, HEAD_DIM), lambda h, i: (0, k_col + h)),
                  pl.BlockSpec((T, HEAD_DIM), lambda h, i: (0, v_col + h))],
        out_specs=[pl.BlockSpec((P * blk, HEAD_DIM), lambda h, i: (i, h)), pl.BlockSpec((1, P, 1, blk), lambda h, i: (h, i, 0, 0)),
                   pl.BlockSpec((1, P, SUBLANES, LANES), lambda h, i: (h, i, 0, 0))],
        out_shape=[jax.ShapeDtypeStruct((T, D_MODEL), F32), jax.ShapeDtypeStruct((HEADS, T // blk, 1, blk), F32),
                   jax.ShapeDtypeStruct((HEADS, T // blk, SUBLANES, LANES), F32)],
        compiler_params=_cp("parallel", "arbitrary"),
    )(q, k, v)


def sb_bwd(q, k, v, ltot, visited, do, *, k_col=0, v_col=0, name, exchange=None):
    T = q.shape[0]
    blk = _tile(T, SB_BLOCK)
    P = min(SB_QBLOCKS, T // blk)
    scale = HEAD_DIM ** -0.5

    def body(q_ref, k_ref, v_ref, l_ref, n_ref, do_ref, dq_ref, dk_ref, dv_ref):
        iq = [P * pl.program_id(1) + p for p in range(P)]
        count = [jnp.max(n_ref[0, p]).astype(jnp.int32) for p in range(P)]
        first = [i + 1 - n for i, n in zip(iq, count)]
        trips = count[0]
        for n in count[1:]:
            trips = jnp.maximum(trips, n)

        @pl.when(pl.program_id(1) == 0)
        def _():
            dk_ref[...] = jnp.zeros_like(dk_ref)
            dv_ref[...] = jnp.zeros_like(dv_ref)

        qb = [_bf(q_ref[p * blk:(p + 1) * blk, :] * scale) for p in range(P)]
        dob = [_bf(do_ref[p * blk:(p + 1) * blk, :]) for p in range(P)]
        ltot_ = [l_ref[0, p].T for p in range(P)]
        r_idx = lax.broadcasted_iota(jnp.int32, (blk, blk), 0)
        c_idx = lax.broadcasted_iota(jnp.int32, (blk, blk), 1)
        upto = _bf(jnp.where(r_idx <= c_idx, 1.0, 0.0))
        before = _bf(jnp.where(r_idx < c_idx, 1.0, 0.0))

        def step(t, carry):
            dqs, lpre, cpre = carry
            live = [f + t <= i for f, i in zip(first, iq)]
            js = [jnp.minimum(f + t, i) for f, i in zip(first, iq)]
            tiles = _each(lambda q_, i, j, f: _sb_tile(q_, k_ref, i, j, blk, f), qb, iq, js, live)
            parts = _each(lambda tl: _split_bf16(tl[2]), tiles)
            after = _each(lambda lt, lp, s: lt - (lp + _dot(s[0], upto) + _dot(s[1], upto)), ltot_, lpre, parts)
            ls = _each(lambda tl: tl[0] + tl[2], tiles)
            a = _each(lambda tl, ls_, af: jnp.where(tl[1], jnp.exp(ls_ + af), 0.0), tiles, ls, after)
            vb = _each(lambda j: _bf(v_ref[_sb_rows(j, blk), :]), js)
            p = _each(lambda a_, do_, v_: a_ * _dot_nt(do_, v_), a, dob, vb)
            pparts = _each(_split_bf16, p)
            left = _each(lambda cp, s: cp + _dot(s[0], before) + _dot(s[1], before), cpre, pparts)
            dzb = _each(lambda tl, p_, lf_, ls_: _bf(jnp.where(tl[1], p_ * jnp.exp(tl[2]) - lf_ * jnp.exp(ls_), 0.0)),
                        tiles, p, left, ls)
            dks = _each(lambda dz, q_: _dot_tn(dz, q_), dzb, qb)
            dvs = _each(lambda a_, do_: _dot_tn(_bf(a_), do_), a, dob)
            dqs = _each(lambda dq, dz, j: dq + _dot(dz, _bf(k_ref[_sb_rows(j, blk), :])), dqs, dzb, js)
            for j, dk, dv in zip(js, dks, dvs):
                dk_ref[_sb_rows(j, blk), :] += dk
                dv_ref[_sb_rows(j, blk), :] += dv
            lpre = _each(lambda lp, tl: lp + jnp.sum(tl[2], axis=1, keepdims=True), lpre, tiles)
            cpre = _each(lambda cp, p_: cp + jnp.sum(p_, axis=1, keepdims=True), cpre, p)
            return dqs, lpre, cpre

        zero = [jnp.zeros((blk, 1), F32)] * P
        dqs, _, _ = lax.fori_loop(0, trips, step, ([jnp.zeros((blk, HEAD_DIM), F32)] * P, zero, zero))
        for p in range(P):
            dq_ref[p * blk:(p + 1) * blk, :] = dqs[p] * scale

    full = lambda off: pl.BlockSpec((T, HEAD_DIM), lambda h, i: (0, off + h))
    tile = pl.BlockSpec((P * blk, HEAD_DIM), lambda h, i: (i, h))
    (dq, dk, dv), brought = hosted_call(
        body, name=name, grid=(HEADS, T // (P * blk)),
        in_specs=[tile, full(k_col), full(v_col), pl.BlockSpec((1, P, 1, blk), lambda h, i: (h, i, 0, 0)),
                  pl.BlockSpec((1, P, SUBLANES, LANES), lambda h, i: (h, i, 0, 0)), tile],
        out_specs=[tile, full(0), full(0)],
        out_shape=[jax.ShapeDtypeStruct((T, D_MODEL), F32)] * 3,
        scratch_shapes=[], args=(q, k, v, ltot, visited, do), exchange=exchange)
    return dq, dk, dv, brought


def sum_slots(slots, *, name):
    n, R, C = slots.shape
    tr = _tile(R, 256)

    def body(s_ref, o_ref):
        acc = s_ref[0].astype(F32)
        for k in range(1, n):
            acc = acc + s_ref[k].astype(F32)
        o_ref[...] = acc

    return pl.pallas_call(
        body, name=name, grid=(R // tr,),
        in_specs=[pl.BlockSpec((n, tr, C), lambda i: (0, i, 0))],
        out_specs=pl.BlockSpec((tr, C), lambda i: (i, 0)),
        out_shape=jax.ShapeDtypeStruct((R, C), F32),
        compiler_params=_cp("parallel"),
    )(slots)


def adamw(w, g_parts, m, v, *, name):
    R, C = w.shape
    tr = _tile(R, 256)
    n = len(g_parts)

    def body(*refs):
        w_ref, m_ref, v_ref = refs[0], refs[1 + n], refs[2 + n]
        g_ref, d_ref, nm_ref, nv_ref = refs[3 + n:]
        g = refs[1][...]
        for r in refs[2:1 + n]:
            g = g + r[...]
        m2 = ADAM_B1 * m_ref[...] + (1.0 - ADAM_B1) * g
        v2 = ADAM_B2 * v_ref[...] + (1.0 - ADAM_B2) * (g * g)
        m_hat = m2 / (1.0 - ADAM_B1 ** ADAM_STEP)
        v_hat = v2 / (1.0 - ADAM_B2 ** ADAM_STEP)
        g_ref[...] = g
        d_ref[...] = -ADAM_LR * (m_hat / (jnp.sqrt(v_hat) + ADAM_EPS) + ADAM_WD * w_ref[...])
        nm_ref[...] = m2
        nv_ref[...] = v2

    spec = pl.BlockSpec((tr, C), lambda i: (i, 0))
    return pl.pallas_call(
        body, name=name, grid=(R // tr,),
        in_specs=[spec] * (3 + n), out_specs=[spec] * 4,
        out_shape=[jax.ShapeDtypeStruct((R, C), F32)] * 4,
        compiler_params=_cp("parallel"),
    )(w, *g_parts, m, v)


CHIP_FLIPS = ((0, 1), (1, 0), (1, 1))


def _place():
    return lax.axis_index("x"), lax.axis_index("y"), lax.axis_index("c")


def _flip(v, f):
    return 1 - v if f else v


class ChipExchange:
    def __init__(self, arrays, scatter):
        self.arrays, self.scatter, self.n = list(arrays), scatter, len(arrays)
        lead = () if scatter else (N_CHIPS,)
        self.out_shape = [jax.ShapeDtypeStruct(lead + a.shape, a.dtype) for a in arrays]
        self.scratch = [pltpu.SemaphoreType.DMA((self.n, len(CHIP_FLIPS))), pltpu.SemaphoreType.DMA((self.n, len(CHIP_FLIPS))),
                        pltpu.SemaphoreType.DMA((self.n,))]

    def _copies(self, ins, outs, sems):
        send_sems, recv_sems, local_sems = sems
        x, y, c = _place()
        me = 2 * x + y
        local, sent, landing = [], [], []
        for k in range(self.n):
            local.append(pltpu.make_async_copy(ins[k].at[me] if self.scatter else ins[k], outs[k].at[me], local_sems.at[k]))
            for p, (fx, fy) in enumerate(CHIP_FLIPS):
                px, py = _flip(x, fx), _flip(y, fy)
                src = ins[k].at[2 * px + py] if self.scatter else ins[k]
                for dst, group in ((me, sent), (2 * px + py, landing)):
                    group.append(pltpu.make_async_remote_copy(src_ref=src, dst_ref=outs[k].at[dst], send_sem=send_sems.at[k, p],
                                                              recv_sem=recv_sems.at[k, p], device_id=(px, py, c), device_id_type=MESH))
        return local, sent, landing

    def start(self, ins, outs, sems):
        local, sent, _ = self._copies(ins, outs, sems)
        for cp in local + sent:
            cp.start()

    def finish(self, ins, outs, sems):
        local, _, landing = self._copies(ins, outs, sems)
        for cp in landing:
            cp.wait_send()
            cp.wait_recv()
        for cp in local:
            cp.wait()


def chip_exchange(arrays, *, scatter, name):
    ex = ChipExchange(arrays, scatter)
    n = ex.n

    def body(*refs):
        ins, outs, sems = refs[:n], refs[n:2 * n], refs[2 * n:]
        ex.start(ins, outs, sems)
        ex.finish(ins, outs, sems)

    return pl.pallas_call(body, name=name, in_specs=[ANY] * n, out_specs=[ANY] * n, out_shape=ex.out_shape,
                          scratch_shapes=ex.scratch)(*arrays)


def hosted_call(body, *, name, grid, in_specs, out_specs, out_shape, scratch_shapes, args, exchange=None):
    in_specs, out_specs, out_shape = list(in_specs), list(out_specs), list(out_shape)
    params = _cp(*["arbitrary"] * len(grid))
    if exchange is None:
        outs = pl.pallas_call(body, name=name, grid=grid, in_specs=in_specs, out_specs=out_specs, out_shape=out_shape,
                              scratch_shapes=list(scratch_shapes), compiler_params=params)(*args)
        return list(outs), []
    n, n_in, n_out, n_scr = exchange.n, len(in_specs), len(out_specs), len(scratch_shapes)

    def both(*refs):
        ins, t_ins = refs[:n_in], refs[n_in:n_in + n]
        outs, t_outs = refs[n_in + n:n_in + n + n_out], refs[n_in + n + n_out:n_in + 2 * n + n_out]
        scratch, sems = refs[n_in + 2 * n + n_out:n_in + 2 * n + n_out + n_scr], refs[n_in + 2 * n + n_out + n_scr:]
        first, last = True, True
        for axis, size in enumerate(grid):
            first = jnp.logical_and(first, pl.program_id(axis) == 0)
            last = jnp.logical_and(last, pl.program_id(axis) == size - 1)

        @pl.when(first)
        def _():
            exchange.start(t_ins, t_outs, sems)

        body(*ins, *outs, *scratch)

        @pl.when(last)
        def _():
            exchange.finish(t_ins, t_outs, sems)

    outs = pl.pallas_call(both, name=name, grid=grid, in_specs=in_specs + [ANY] * n, out_specs=out_specs + [ANY] * n,
                          out_shape=out_shape + exchange.out_shape, scratch_shapes=list(scratch_shapes) + exchange.scratch,
                          compiler_params=params)(*args, *exchange.arrays)
    return list(outs[:n_out]), list(outs[n_out:])


def sibling_swap(arrays, *, name):
    n = len(arrays)

    def body(*refs):
        ins, outs = refs[:n], refs[n:2 * n]
        send_sems, recv_sems = refs[2 * n:]
        x, y, c = _place()
        copies = [pltpu.make_async_remote_copy(src_ref=ins[k], dst_ref=outs[k], send_sem=send_sems.at[k], recv_sem=recv_sems.at[k],
                                               device_id=(x, y, 1 - c), device_id_type=MESH) for k in range(n)]
        for cp in copies:
            cp.start()
        for cp in copies:
            cp.wait_send()
            cp.wait_recv()

    return pl.pallas_call(
        body, name=name, in_specs=[ANY] * n, out_specs=[ANY] * n,
        out_shape=[jax.ShapeDtypeStruct(a.shape, a.dtype) for a in arrays],
        scratch_shapes=[pltpu.SemaphoreType.DMA((n,)), pltpu.SemaphoreType.DMA((n,))],
    )(*arrays)


DEVICE_FLIPS = tuple((fx, fy, fc) for fx in (0, 1) for fy in (0, 1) for fc in (0, 1) if fx or fy or fc)


def all_gather_devices(a, *, name):
    def body(a_ref, o_ref, send_sems, recv_sems, local_sem):
        x, y, c = _place()
        me = 4 * x + 2 * y + c
        local = pltpu.make_async_copy(a_ref, o_ref.at[me], local_sem)
        local.start()
        for p, (fx, fy, fc) in enumerate(DEVICE_FLIPS):
            peer = (_flip(x, fx), _flip(y, fy), _flip(c, fc))
            pltpu.make_async_remote_copy(src_ref=a_ref, dst_ref=o_ref.at[me], send_sem=send_sems.at[p], recv_sem=recv_sems.at[p],
                                         device_id=peer, device_id_type=MESH).start()
        for p, (fx, fy, fc) in enumerate(DEVICE_FLIPS):
            px, py, pc = _flip(x, fx), _flip(y, fy), _flip(c, fc)
            landing = pltpu.make_async_remote_copy(src_ref=a_ref, dst_ref=o_ref.at[4 * px + 2 * py + pc], send_sem=send_sems.at[p],
                                                   recv_sem=recv_sems.at[p], device_id=(px, py, pc), device_id_type=MESH)
            landing.wait_send()
            landing.wait_recv()
        local.wait()

    return pl.pallas_call(
        body, name=name, in_specs=[ANY], out_specs=ANY,
        out_shape=jax.ShapeDtypeStruct((N_DEV,) + a.shape, a.dtype),
        scratch_shapes=[pltpu.SemaphoreType.DMA((len(DEVICE_FLIPS),)), pltpu.SemaphoreType.DMA((len(DEVICE_FLIPS),)),
                        pltpu.SemaphoreType.DMA(())],
    )(a)


def _row(v):
    return v.reshape(1, -1)


def _pad_rows(w):
    return jnp.pad(w, ((0, SUBLANES - w.shape[0]), (0, 0)))


def _pad_lanes(v):
    return jnp.pad(v.reshape(1, -1), ((0, 0), (0, LANES - v.shape[-1])))


def _head_layouts(gates):
    T = gates.shape[0]
    rows = lambda cols: cols.T.reshape(HEADS, T // GDN_CHUNK, 1, GDN_CHUNK)
    return rows(gates[:, :HEADS]), rows(gates[:, HEADS:2 * HEADS])


def _ffn_fwd(x, W, l, plan):
    conv = _pad_rows(W["ffn_conv"][l])
    u, h, act, brought = ffn_up_act(x, _row(W["ffn_norm"][l]), W["ffn_w_up"][l], conv, name=f"ffn{l}_up_act",
                                    exchange=plan.fetch(f"ffn{l}_act"))
    plan.arrived(f"ffn{l}_act", brought, W)
    y = matmul_residual(act, W["ffn_w_down"][l], x, name=f"ffn{l}_down")
    return y, (x, h, u, conv, act)


def _ffn_bwd(dx, saved, W, l, G):
    x, h, u, conv, act = saved
    dact = matmul_nt(dx, W["ffn_w_down"][l], name=f"ffn{l}_down_dx", out_dtype=BF16)
    G["ffn_w_down"][l] = matmul_tn(act, dx, name=f"ffn{l}_down_dw")
    dug, duu, dwg, dwu = ffn_act_bwd(u, conv, dact, name=f"ffn{l}_act_bwd")
    G["ffn_conv"][l] = jnp.concatenate([dwg, dwu], axis=1)[:FFN_CONV]
    G["ffn_w_up"][l] = jnp.concatenate([matmul_tn(h, dug, name=f"ffn{l}_gate_dw"), matmul_tn(h, duu, name=f"ffn{l}_up_dw")], axis=1)
    dx, dgain = matmul_nt_normbwd([(dug, 0), (duu, 1)], W["ffn_w_up"][l], x, _row(W["ffn_norm"][l]), dx, name=f"ffn{l}_up_dx")
    G["ffn_norm"][l] = dgain.sum(0)
    return dx


class NoTraffic:
    def fetch(self, host):
        return None

    def arrived(self, host, brought, W):
        pass

    def flush(self, G):
        return None

    def landed(self, brought):
        pass


def _gdn_fwd(x, W, l, plan):
    conv = _pad_rows(W["a_conv"][l])
    a_log, dt_bias = _pad_lanes(W["a_log"][l]), _pad_lanes(W["a_dt_bias"][l])
    proj, h, qkv, brought = gdn_in_conv(x, _row(W["a_norm"][l]), W["a_w_in"][l], conv, name=f"gdn{l}_in_conv",
                                        exchange=plan.fetch(f"gdn{l}_conv"))
    plan.arrived(f"gdn{l}_conv", brought, W)
    heads = _head_layouts(gates_fwd(proj, a_log, dt_bias, name=f"gdn{l}_gates"))
    o, states, brought = gdn_fwd(qkv, *heads, name=f"gdn{l}_rule", exchange=plan.fetch(f"gdn{l}_rule"))
    plan.arrived(f"gdn{l}_rule", brought, W)
    gain = _row(W["a_out_norm"][l])
    on = head_norm_fwd(o, gain, proj, z_col=Z_BLOCK, name=f"gdn{l}_outnorm", out_dtype=BF16)
    y = matmul_residual(on, W["a_w_out"][l], x, name=f"gdn{l}_out")
    return y, (x, h, proj, conv, a_log, dt_bias, qkv, heads, states, o, gain, on)


def _gdn_bwd(dx, saved, W, l, G, plan):
    x, h, proj, conv, a_log, dt_bias, qkv, heads, states, o, gain, on = saved
    T = x.shape[0]
    don = matmul_nt(dx, W["a_w_out"][l], name=f"gdn{l}_out_dx")
    G["a_w_out"][l] = matmul_tn(on, dx, name=f"gdn{l}_out_dw")
    do, dz, dgain = head_norm_bwd(o, gain, [don], proj, z_col=Z_BLOCK, name=f"gdn{l}_outnorm_bwd")
    G["a_out_norm"][l] = dgain.reshape(SUBLANES, HEADS, HEAD_DIM).sum((0, 1))
    dqkv, da, db, brought = gdn_bwd(qkv, *heads, states, do, name=f"gdn{l}_rule_bwd", exchange=plan.flush(G))
    plan.landed(brought)
    dqkv, dconv = gdn_conv_bwd(proj, conv, dqkv, name=f"gdn{l}_conv_bwd")
    G["a_conv"][l] = dconv[:GDN_CONV]
    dgate = jnp.concatenate([da.reshape(HEADS, T).T, db.reshape(HEADS, T).T, jnp.zeros((T, LANES - 2 * HEADS), F32)], axis=1)
    dab, dal, ddt = gates_bwd(proj, a_log, dt_bias, dgate, name=f"gdn{l}_gates_bwd")
    G["a_log"][l] = dal.sum(0)[:HEADS]
    G["a_dt_bias"][l] = ddt.sum(0)[:HEADS]
    parts = [(dqkv, 0), (dz, Z_BLOCK * LANES // D_MODEL), (dab, AB_BLOCK)]
    G["a_w_in"][l] = jnp.concatenate([matmul_tn(h, d, name=f"gdn{l}_in_dw{i}") for i, (d, _) in enumerate(parts)], axis=1)
    last = plan.flush(G) if l == 0 else None
    out = matmul_nt_normbwd(parts, W["a_w_in"][l], x, _row(W["a_norm"][l]), dx, name=f"gdn{l}_in_dx", exchange=last)
    dx, dgain = out[:2]
    if last is not None:
        plan.landed(out[2])
    G["a_norm"][l] = dgain.sum(0)
    return dx


def _sb_fwd(x, kn, kv, W, j):
    qp, h = norm_matmul(x, _row(W["b_norm"][j]), W["b_w_q"][j], name=f"sb{j}_q")
    gain = _row(W["q_norm"][j])
    q = head_norm_fwd(qp, gain, name=f"sb{j}_qnorm")
    o, ltot, visited = sb_fwd(q, kn, kv, v_col=HEADS, name=f"sb{j}_attn")
    y = matmul_residual(o, W["b_w_out"][j], x, name=f"sb{j}_out")
    return y, (x, h, qp, gain, q, o, ltot, visited)


def _sb_bwd(dx, saved, kn, kv, W, j, G, plan):
    x, h, qp, gain, q, o, ltot, visited = saved
    do = matmul_nt(dx, W["b_w_out"][j], name=f"sb{j}_out_dx")
    G["b_w_out"][j] = matmul_tn(o, dx, name=f"sb{j}_out_dw")
    dq, dk, dv, brought = sb_bwd(q, kn, kv, ltot, visited, do, v_col=HEADS, name=f"sb{j}_attn_bwd", exchange=plan.flush(G))
    plan.landed(brought)
    dqp, dgain = head_norm_bwd(qp, gain, [dq], name=f"sb{j}_qnorm_bwd", dx_dtype=BF16)
    G["q_norm"][j] = dgain.reshape(SUBLANES, HEADS, HEAD_DIM).sum((0, 1))
    G["b_w_q"][j] = matmul_tn(h, dqp, name=f"sb{j}_q_dw")
    dx, dgain = matmul_nt_normbwd([(dqp, 0)], W["b_w_q"][j], x, _row(W["b_norm"][j]), dx, name=f"sb{j}_q_dx")
    G["b_norm"][j] = dgain.sum(0)
    return dx, dk, dv


def local_step(x, target, W, plan=None):
    plan = plan or NoTraffic()
    G = {k: [None] * (N_A if k.startswith("a_") else N_B if k in ("b_norm", "b_w_q", "q_norm", "b_w_out") else DEPTH)
         for k in ("a_norm", "a_w_in", "a_conv", "a_log", "a_dt_bias", "a_out_norm", "a_w_out", "b_norm", "b_w_q", "q_norm",
                   "b_w_out", "ffn_norm", "ffn_w_up", "ffn_conv", "ffn_w_down")}
    G["w_kv"] = [None]
    tape = []
    for l in range(N_A):
        x, s_mix = _gdn_fwd(x, W, l, plan)
        x, s_ffn = _ffn_fwd(x, W, l, plan)
        tape.append((s_mix, s_ffn))
    x_kv = x
    kv, h_kv = norm_matmul(x, _row(W["kv_norm"]), W["w_kv"], name="kv_proj")
    k_gain = _row(W["k_norm"])
    kn = head_norm_fwd(kv, k_gain, name="k_norm")
    for j in range(N_B):
        x, s_mix = _sb_fwd(x, kn, kv, W, j)
        x, s_ffn = _ffn_fwd(x, W, N_A + j, plan)
        tape.append((s_mix, s_ffn))
    dx, loss = loss_fwd(x, target, name="loss")

    dks, dvs = [], []
    for j in reversed(range(N_B)):
        s_mix, s_ffn = tape[N_A + j]
        dx = _ffn_bwd(dx, s_ffn, W, N_A + j, G)
        dx, dk, dv = _sb_bwd(dx, s_mix, kn, kv, W, j, G, plan)
        dks.append(dk)
        dvs.append(dv)
    dkp, dgain = head_norm_bwd(kv, k_gain, dks, name="k_norm_bwd", dx_dtype=BF16)
    G["k_norm"] = dgain.reshape(SUBLANES, HEADS, HEAD_DIM).sum((0, 1))
    G["w_kv"][0] = jnp.concatenate([matmul_tn(h_kv, dkp, name="k_proj_dw"), matmul_tn(h_kv, dvs, name="v_proj_dw")], axis=1)
    dx, dgain = matmul_nt_normbwd([(dkp, 0)] + [(dv, 1) for dv in dvs], W["w_kv"], x_kv, _row(W["kv_norm"]), dx,
                                  name="kv_proj_dx")
    G["kv_norm"] = dgain.sum(0)
    for l in reversed(range(N_A)):
        s_mix, s_ffn = tape[l]
        dx = _ffn_bwd(dx, s_ffn, W, l, G)
        dx = _gdn_bwd(dx, s_mix, W, l, G, plan)
    return loss, dx, G


MATRICES = {"a_w_in": 2, "a_w_out": 1, "w_kv": 1, "b_w_q": 1, "b_w_out": 1, "ffn_w_up": 2, "ffn_w_down": 1}
SMALL_SHARDED = {"a_norm": 1, "a_conv": 2, "ffn_conv": 2}
SMALL_REPLICATED = ("a_log", "a_dt_bias", "a_out_norm", "kv_norm", "k_norm", "b_norm", "q_norm", "ffn_norm")
WEIGHT_ORDER = ("a_norm", "a_w_in", "a_conv", "a_log", "a_dt_bias", "a_out_norm", "a_w_out", "kv_norm", "w_kv", "k_norm",
                "b_norm", "b_w_q", "q_norm", "b_w_out", "ffn_norm", "ffn_w_up", "ffn_conv", "ffn_w_down")
SMALL_ORDER = tuple(n for n in WEIGHT_ORDER if n not in MATRICES)
PACK_QUANTUM = SUBLANES * LANES


def _unshard(g, axis):
    g = jnp.moveaxis(g, 0, axis)
    return g.reshape(g.shape[:axis] + (g.shape[axis] * g.shape[axis + 1],) + g.shape[axis + 2:])


def _shards(full, axis):
    n = full.shape[axis] // N_CHIPS
    return jnp.moveaxis(full.reshape(full.shape[:axis] + (N_CHIPS, n) + full.shape[axis + 1:]), axis, 0)


def _pack(arrays):
    parts = []
    for a in arrays:
        flat = a.reshape(-1)
        parts.append(jnp.pad(flat, (0, -flat.shape[0] % PACK_QUANTUM)).reshape(-1, LANES))
    return jnp.concatenate(parts, axis=0)


def _unpack(buf, shapes):
    out, row = [], 0
    for s in shapes:
        size = math.prod(s)
        rows = -(-size // PACK_QUANTUM) * SUBLANES
        out.append(buf[row:row + rows].reshape(-1)[:size].reshape(s))
        row += rows
    return out


def _stack(per_layer):
    return jnp.stack(per_layer) if isinstance(per_layer, list) else per_layer


FETCH_BESIDE = {
    "gdn0_conv": (("ffn_w_up", 0, 2),),
    "gdn0_rule": (("ffn_w_down", 0, 2), ("w_kv", 0, None), ("b_w_q", 0, 2), ("b_w_out", 0, 2)),
    "ffn0_act": (("a_w_in", 1, 1), ("a_w_out", 1, 1)),
    "gdn1_conv": (("ffn_w_up", 2, 2),),
    "gdn1_rule": (("ffn_w_down", 2, 2),),
}
FETCH_FIRST = (("a_w_in", 0, 1), ("a_w_out", 0, 1))


class Traffic:
    def __init__(self, local):
        self.local = local
        self.shipped, self.received = [], {}

    @staticmethod
    def _assemble(W, name, first, layers, gathered):
        whole = _unshard(gathered, MATRICES[name])
        if name == "a_w_in":
            whole = jnp.pad(whole, ((0, 0), (0, 0), (0, W_IN_PAD - W_IN_COLS)))
        if layers is None:
            W[name] = whole
        else:
            W.setdefault(name, {}).update({first + i: whole[i] for i in range(layers)})

    def _shards_of(self, wanted):
        return [(self.local[name] if layers is None else self.local[name][first:first + layers]).astype(BF16)
                for name, first, layers in wanted]

    def fetch_first(self, extra):
        brought = chip_exchange(self._shards_of(FETCH_FIRST) + list(extra), scatter=False, name="gather_first")
        W = {}
        for (name, first, layers), g in zip(FETCH_FIRST, brought):
            self._assemble(W, name, first, layers, g)
        return W, brought[len(FETCH_FIRST):]

    def fetch(self, host):
        return ChipExchange(self._shards_of(FETCH_BESIDE[host]), scatter=False) if host in FETCH_BESIDE else None

    def arrived(self, host, brought, W):
        for (name, first, layers), g in zip(FETCH_BESIDE.get(host, ()), brought):
            self._assemble(W, name, first, layers, g)

    def _ready(self, G):
        out = []
        for name, axis in MATRICES.items():
            for layer, g in enumerate(G[name]):
                if g is None or (name, layer) in self.shipped:
                    continue
                g = g[:, :W_IN_COLS] if name == "a_w_in" else g
                piece = _shards(g, axis - (0 if name == "w_kv" else 1)).astype(BF16)
                out.append((name, layer, piece.reshape(N_CHIPS, -1, piece.shape[-1])))
        return out

    def flush(self, G):
        ready = self._ready(G)
        self.in_flight = [r[:2] for r in ready]
        self.shipped += self.in_flight
        return ChipExchange([r[2] for r in ready], scatter=True) if ready else None

    def landed(self, brought):
        self.received.update(zip(self.in_flight, brought))
        self.in_flight = []

    def flush_last(self, G):
        ready = self._ready(G)
        if ready:
            self.shipped += [r[:2] for r in ready]
            brought = chip_exchange([r[2] for r in ready], scatter=True, name="scatter_last")
            self.received.update(zip([r[:2] for r in ready], brought))

    def my_sums(self):
        sums = {}
        for name in MATRICES:
            layers = sorted(l for n, l in self.received if n == name)
            parts = [sum_slots(self.received[(name, l)], name=f"sum_{name}{l}") for l in layers]
            sums[name] = parts[0] if len(parts) == 1 else jnp.concatenate(parts, axis=0)
        return sums


def _as_2d(a):
    return a.reshape(-1, a.shape[-1])


def kernel(x, a_norm, a_w_in, a_conv, a_log, a_dt_bias, a_out_norm, a_w_out, kv_norm, w_kv, k_norm, b_norm, b_w_q, q_norm, b_w_out, ffn_norm, ffn_w_up, ffn_conv, ffn_w_down, loss_target, m_a_norm, m_a_w_in, m_a_conv, m_a_log, m_a_dt_bias, m_a_out_norm, m_a_w_out, m_kv_norm, m_w_kv, m_k_norm, m_b_norm, m_b_w_q, m_q_norm, m_b_w_out, m_ffn_norm, m_ffn_w_up, m_ffn_conv, m_ffn_w_down, v_a_norm, v_a_w_in, v_a_conv, v_a_log, v_a_dt_bias, v_a_out_norm, v_a_w_out, v_kv_norm, v_w_kv, v_k_norm, v_b_norm, v_b_w_q, v_q_norm, v_b_w_out, v_ffn_norm, v_ffn_w_up, v_ffn_conv, v_ffn_w_down):
    local = dict(a_norm=a_norm, a_w_in=a_w_in, a_conv=a_conv, a_log=a_log, a_dt_bias=a_dt_bias, a_out_norm=a_out_norm,
                 a_w_out=a_w_out, kv_norm=kv_norm, w_kv=w_kv, k_norm=k_norm, b_norm=b_norm, b_w_q=b_w_q, q_norm=q_norm,
                 b_w_out=b_w_out, ffn_norm=ffn_norm, ffn_w_up=ffn_w_up, ffn_conv=ffn_conv, ffn_w_down=ffn_w_down)
    mom = dict(a_norm=m_a_norm, a_w_in=m_a_w_in, a_conv=m_a_conv, a_log=m_a_log, a_dt_bias=m_a_dt_bias, a_out_norm=m_a_out_norm,
               a_w_out=m_a_w_out, kv_norm=m_kv_norm, w_kv=m_w_kv, k_norm=m_k_norm, b_norm=m_b_norm, b_w_q=m_b_w_q, q_norm=m_q_norm,
               b_w_out=m_b_w_out, ffn_norm=m_ffn_norm, ffn_w_up=m_ffn_w_up, ffn_conv=m_ffn_conv, ffn_w_down=m_ffn_w_down)
    var = dict(a_norm=v_a_norm, a_w_in=v_a_w_in, a_conv=v_a_conv, a_log=v_a_log, a_dt_bias=v_a_dt_bias, a_out_norm=v_a_out_norm,
               a_w_out=v_a_w_out, kv_norm=v_kv_norm, w_kv=v_w_kv, k_norm=v_k_norm, b_norm=v_b_norm, b_w_q=v_b_w_q, q_norm=v_q_norm,
               b_w_out=v_b_w_out, ffn_norm=v_ffn_norm, ffn_w_up=v_ffn_w_up, ffn_conv=v_ffn_conv, ffn_w_down=v_ffn_w_down)
    chip = 2 * lax.axis_index("x") + lax.axis_index("y")

    mats = list(MATRICES)
    small_sharded = list(SMALL_SHARDED)
    traffic = Traffic(local)
    W, (vectors,) = traffic.fetch_first([_pack([local[n] for n in small_sharded])])
    W.update({n: local[n] for n in SMALL_REPLICATED})
    shard_shapes = [local[n].shape for n in small_sharded]
    per_chip = [_unpack(vectors[j], shard_shapes) for j in range(N_CHIPS)]
    for i, n in enumerate(small_sharded):
        W[n] = _unshard(jnp.stack([per_chip[j][i] for j in range(N_CHIPS)]), SMALL_SHARDED[n])

    T = x.shape[1]
    loss_part, dx, G = local_step(x.reshape(T, D_MODEL), loss_target.reshape(T, D_MODEL), W, traffic)

    traffic.flush_last(G)
    sums = traffic.my_sums()
    mine = [sums[n] for n in mats]
    theirs = sibling_swap(mine, name="swap_grads")

    small_full = {n: _stack(G[n]) for n in SMALL_ORDER}
    packed = _pack([small_full[n] for n in SMALL_ORDER] + [loss_part])
    total = sum_slots(all_gather_devices(packed, name="gather_small"), name="sum_small")
    small_shapes = [small_full[n].shape for n in SMALL_ORDER] + [loss_part.shape]
    summed = dict(zip(SMALL_ORDER + ("loss",), _unpack(total, small_shapes)))
    loss = jnp.sum(summed.pop("loss"))
    for n, axis in SMALL_SHARDED.items():
        size = local[n].shape[axis]
        summed[n] = lax.dynamic_slice_in_dim(summed[n], chip * size, size, axis)

    grads, deltas, new_m, new_v = {}, {}, {}, {}
    for n, p_mine, p_theirs in zip(mats, mine, theirs):
        outs = adamw(_as_2d(local[n]), [p_mine, p_theirs], _as_2d(mom[n]), _as_2d(var[n]), name=f"adamw_{n}")
        grads[n], deltas[n], new_m[n], new_v[n] = [o.reshape(local[n].shape) for o in outs]
    small_local_shapes = [local[n].shape for n in SMALL_ORDER]
    outs = adamw(_pack([local[n] for n in SMALL_ORDER]), [_pack([summed[n] for n in SMALL_ORDER])],
                 _pack([mom[n] for n in SMALL_ORDER]), _pack([var[n] for n in SMALL_ORDER]), name="adamw_small")
    for d, o in zip((grads, deltas, new_m, new_v), outs):
        d.update(zip(SMALL_ORDER, _unpack(o, small_local_shapes)))

    return (loss, dx.reshape(x.shape), *[grads[n] for n in WEIGHT_ORDER], *[deltas[n] for n in WEIGHT_ORDER],
            *[new_m[n] for n in WEIGHT_ORDER], *[new_v[n] for n in WEIGHT_ORDER])
```

```python
import math

import jax
import jax.numpy as jnp
from jax import lax
from jax.experimental import pallas as pl
from jax.experimental.pallas import tpu as pltpu

F32 = jnp.float32
BF16 = jnp.bfloat16

D_MODEL = 1024
HEADS = 8
HEAD_DIM = 128
GDN_CONV = 4
GDN_CHUNK = 64
D_FF = 2816
FFN_CONV = 3
EPS = 1e-6
N_A = 2
N_B = 2
DEPTH = N_A + N_B
W_IN_COLS = 4 * D_MODEL + 2 * HEADS
W_IN_PAD = 4 * D_MODEL + 128
Z_BLOCK = 3 * D_MODEL // 128
AB_BLOCK = 4 * D_MODEL // 128

ADAM_LR = 0.001
ADAM_B1 = 0.9
ADAM_B2 = 0.999
ADAM_EPS = 1e-08
ADAM_WD = 0.01
ADAM_STEP = 10

LANES = 128
SUBLANES = 8
VMEM_LIMIT = 56 * 1024 * 1024
HALO = SUBLANES
N_CHIPS = 4
N_DEV = 8

HI = lax.Precision.HIGHEST
MESH = pl.DeviceIdType.MESH
ANY = pl.BlockSpec(memory_space=pl.ANY)


def _cp(*sem):
    return pltpu.CompilerParams(dimension_semantics=sem, vmem_limit_bytes=VMEM_LIMIT)


def _tile(n, want, align=SUBLANES):
    t = (min(n, want) // align) * align
    while t > 0 and n % t:
        t -= align
    return t if t > 0 else n


def _dot(a, b, precision=None):
    return jnp.dot(a, b, preferred_element_type=F32, precision=precision)


def _dot_nt(a, b, precision=None):
    return lax.dot_general(a, b, (((1,), (1,)), ((), ())), preferred_element_type=F32, precision=precision)


def _dot_tn(a, b, precision=None):
    return lax.dot_general(a, b, (((0,), (0,)), ((), ())), preferred_element_type=F32, precision=precision)


def _bf(x):
    return x.astype(BF16)


def _sigmoid(x):
    return 0.5 * jnp.tanh(0.5 * x) + 0.5


def _softplus(x):
    return jnp.maximum(x, 0.0) + jnp.log(1.0 + jnp.exp(-jnp.abs(x)))


def _silu(x):
    return x * _sigmoid(x)


def _silu_and_grad(x):
    s = _sigmoid(x)
    return x * s, s * (1.0 + x * (1.0 - s))


def _fold_rows(v):
    return jnp.sum(v.reshape(v.shape[0] // SUBLANES, SUBLANES, v.shape[1]), axis=0)


def _accumulate(ref, value, axis):
    @pl.when(pl.program_id(axis) == 0)
    def _():
        ref[...] = jnp.zeros_like(ref)
    ref[...] += value


TILE_BUDGET = 44 * 1024 * 1024


def _rows_that_fit(T, fixed_bytes, row_bytes, want=1024):
    tm = _tile(T, want)
    while tm > SUBLANES and 2 * (fixed_bytes + tm * row_bytes) > TILE_BUDGET:
        tm //= 2
    return tm


def norm_matmul(x, gain, w, *, name):
    T, D = x.shape
    N = w.shape[1]
    tn = N
    tm = _rows_that_fit(T, D * N * 2, D * 4 + D * 2 + N * 4)

    def body(x_ref, g_ref, w_ref, y_ref, h_ref):
        @pl.when(pl.program_id(1) == 0)
        def _():
            xf = x_ref[...]
            r = lax.rsqrt(jnp.mean(xf * xf, axis=-1, keepdims=True) + EPS)
            h_ref[...] = _bf(xf * r * g_ref[...])
        y_ref[...] = _dot(h_ref[...], w_ref[...])

    return pl.pallas_call(
        body, name=name, grid=(T // tm, N // tn),
        in_specs=[pl.BlockSpec((tm, D), lambda i, j: (i, 0)), pl.BlockSpec((1, D), lambda i, j: (0, 0)),
                  pl.BlockSpec((D, tn), lambda i, j: (0, j))],
        out_specs=[pl.BlockSpec((tm, tn), lambda i, j: (i, j)), pl.BlockSpec((tm, D), lambda i, j: (i, 0))],
        out_shape=[jax.ShapeDtypeStruct((T, N), F32), jax.ShapeDtypeStruct((T, D), BF16)],
        compiler_params=_cp("parallel", "arbitrary"),
    )(x, gain, w)


def matmul_residual(a, w, res, *, name):
    T, K = a.shape
    N = w.shape[1]
    tn = N
    tm = _rows_that_fit(T, K * N * 2, K * a.dtype.itemsize + 2 * N * 4)

    def body(a_ref, w_ref, r_ref, o_ref):
        o_ref[...] = r_ref[...] + _dot(_bf(a_ref[...]), w_ref[...])

    return pl.pallas_call(
        body, name=name, grid=(T // tm, N // tn),
        in_specs=[pl.BlockSpec((tm, K), lambda i, j: (i, 0)), pl.BlockSpec((K, tn), lambda i, j: (0, j)),
                  pl.BlockSpec((tm, tn), lambda i, j: (i, j))],
        out_specs=pl.BlockSpec((tm, tn), lambda i, j: (i, j)),
        out_shape=jax.ShapeDtypeStruct((T, N), F32),
        compiler_params=_cp("parallel", "parallel"),
    )(a, w, res)


def matmul_nt(dy, w, *, name, out_dtype=F32):
    T, N = dy.shape
    K = w.shape[0]
    tk = K
    tm = _rows_that_fit(T, K * N * 2, N * dy.dtype.itemsize + K * jnp.dtype(out_dtype).itemsize)

    def body(dy_ref, w_ref, o_ref):
        o_ref[...] = _dot_nt(_bf(dy_ref[...]), w_ref[...]).astype(out_dtype)

    return pl.pallas_call(
        body, name=name, grid=(T // tm, K // tk),
        in_specs=[pl.BlockSpec((tm, N), lambda i, j: (i, 0)), pl.BlockSpec((tk, N), lambda i, j: (j, 0))],
        out_specs=pl.BlockSpec((tm, tk), lambda i, j: (i, j)),
        out_shape=jax.ShapeDtypeStruct((T, K), out_dtype),
        compiler_params=_cp("parallel", "parallel"),
    )(dy, w)


def matmul_nt_normbwd(parts, w, x, gain, dres, *, name, exchange=None):
    T = x.shape[0]
    D = w.shape[0]
    tm = _tile(T, 256)
    n = len(parts)

    def body(*refs):
        x_ref, g_ref, dr_ref, dx_ref, dg_ref = refs[2 * n:]
        dh = _dot_nt(_bf(refs[0][...]), refs[n][...])
        for i in range(1, n):
            dh = dh + _dot_nt(_bf(refs[i][...]), refs[n + i][...])
        xf = x_ref[...]
        r = lax.rsqrt(jnp.mean(xf * xf, axis=-1, keepdims=True) + EPS)
        xh = xf * r
        dxh = dh * g_ref[...]
        dx_ref[...] = dr_ref[...] + r * (dxh - xh * jnp.mean(dxh * xh, axis=-1, keepdims=True))
        _accumulate(dg_ref, _fold_rows(dh * xh), 0)

    dy_specs = [pl.BlockSpec((tm, dy.shape[1]), lambda i: (i, 0)) for dy, _ in parts]
    w_specs = [pl.BlockSpec((D, dy.shape[1]), lambda i, b=b: (0, b)) for dy, b in parts]
    (dx, dgain), brought = hosted_call(
        body, name=name, grid=(T // tm,),
        in_specs=dy_specs + w_specs + [pl.BlockSpec((tm, D), lambda i: (i, 0)), pl.BlockSpec((1, D), lambda i: (0, 0)),
                                       pl.BlockSpec((tm, D), lambda i: (i, 0))],
        out_specs=[pl.BlockSpec((tm, D), lambda i: (i, 0)), pl.BlockSpec((SUBLANES, D), lambda i: (0, 0))],
        out_shape=[jax.ShapeDtypeStruct((T, D), F32), jax.ShapeDtypeStruct((SUBLANES, D), F32)],
        scratch_shapes=[], args=(*[dy for dy, _ in parts], *([w] * n), x, gain, dres), exchange=exchange)
    return (dx, dgain, brought) if exchange is not None else (dx, dgain)


def matmul_tn(a, dys, *, name):
    dys = dys if isinstance(dys, (list, tuple)) else [dys]
    T, K = a.shape
    N = dys[0].shape[1]
    tk = _tile(K, 1408, LANES)
    tn = _tile(N, 1408 if N <= 2816 else 512, LANES)
    tm = _tile(T, (4096 if tn <= 512 else 2048) // len(dys))

    def body(a_ref, *refs):
        dy = refs[0][...]
        for r in refs[1:-1]:
            dy = dy + r[...]
        _accumulate(refs[-1], _dot_tn(_bf(a_ref[...]), _bf(dy)), 2)

    return pl.pallas_call(
        body, name=name, grid=(K // tk, N // tn, T // tm),
        in_specs=[pl.BlockSpec((tm, tk), lambda i, j, t: (t, i))] + [pl.BlockSpec((tm, tn), lambda i, j, t: (t, j))] * len(dys),
        out_specs=pl.BlockSpec((tk, tn), lambda i, j, t: (i, j)),
        out_shape=jax.ShapeDtypeStruct((K, N), F32),
        compiler_params=_cp("parallel", "parallel", "arbitrary"),
    )(a, *dys)


def _halo_specs(T, tt, tc, col):
    per = tt // HALO
    last = T // HALO - 1
    return [pl.BlockSpec((HALO, tc), lambda j, i: (jnp.maximum(i * per - 1, 0), col(j))),
            pl.BlockSpec((tt, tc), lambda j, i: (i, col(j))),
            pl.BlockSpec((HALO, tc), lambda j, i: (jnp.minimum((i + 1) * per, last), col(j)))]


def _extend(prev_ref, cur_ref, next_ref, nt):
    i = pl.program_id(1)
    p = jnp.where(i > 0, prev_ref[...].astype(F32), 0.0)
    q = jnp.where(i < nt - 1, next_ref[...].astype(F32), 0.0)
    return jnp.concatenate([p, cur_ref[...].astype(F32), q], axis=0)


def _rows_before(e, s):
    return e if s == 0 else pltpu.roll(e, s, 0)


def _rows_after(e, s):
    return e if s == 0 else pltpu.roll(e, e.shape[0] - s, 0)


def _causal_conv(e, w, taps):
    y = w[taps - 1:taps, :] * e
    for s in range(1, taps):
        y = y + w[taps - 1 - s:taps - s, :] * _rows_before(e, s)
    return y


def _causal_conv_bwd(e, dc, w, taps, tt):
    lo, hi = HALO, HALO + tt
    dx = w[taps - 1:taps, :] * dc
    dws = [None] * taps
    dws[taps - 1] = jnp.sum((e * dc)[lo:hi], axis=0, keepdims=True)
    for s in range(1, taps):
        dx = dx + w[taps - 1 - s:taps - s, :] * _rows_after(dc, s)
        dws[taps - 1 - s] = jnp.sum((_rows_before(e, s) * dc)[lo:hi], axis=0, keepdims=True)
    dw = jnp.concatenate(dws + [jnp.zeros((SUBLANES - taps, e.shape[1]), F32)], axis=0)
    return dx[lo:hi], dw


def _qkv_kind(col_block):
    return (col_block >= HEADS).astype(jnp.int32) + (col_block >= 2 * HEADS).astype(jnp.int32)


def _l2norm_scale(kind):
    return jnp.where(kind == 0, HEAD_DIM ** -0.5, 1.0)


GDN_IN_ROWS = 256


def gdn_in_conv(x, gain, w, conv_w, *, name, exchange=None):
    T, D = x.shape
    N = w.shape[1]
    C = 3 * D_MODEL
    tm = _tile(T, GDN_IN_ROWS)
    n = T // tm

    def body(x_ref, g_ref, w_ref, cw_ref, p_ref, h_ref, o_ref, held):
        @pl.when(pl.program_id(0) == 0)
        def _():
            held[...] = jnp.zeros_like(held)

        for block in range(3 * HEADS):
            cols = slice(block * HEAD_DIM, (block + 1) * HEAD_DIM)
            s = _silu(_causal_conv(held[:, cols], cw_ref[:, cols], GDN_CONV))[HALO:]
            if block < 2 * HEADS:
                scale = HEAD_DIM ** -0.5 if block < HEADS else 1.0
                s = s * (lax.rsqrt(jnp.sum(s * s, axis=-1, keepdims=True) + EPS) * scale)
            o_ref[:, cols] = s
        xf = x_ref[...]
        h = _bf(xf * lax.rsqrt(jnp.mean(xf * xf, axis=-1, keepdims=True) + EPS) * g_ref[...])
        h_ref[...] = h
        proj = _dot(h, w_ref[...])
        p_ref[...] = proj
        held[0:HALO, :] = held[tm:tm + HALO, :]
        held[HALO:HALO + tm, :] = proj[:, :C]

    (proj, h, qkv), brought = hosted_call(
        body, name=name, grid=(n + 1,),
        in_specs=[pl.BlockSpec((tm, D), lambda i: (jnp.minimum(i, n - 1), 0)), pl.BlockSpec((1, D), lambda i: (0, 0)),
                  pl.BlockSpec((D, N), lambda i: (0, 0)), pl.BlockSpec((SUBLANES, C), lambda i: (0, 0))],
        out_specs=[pl.BlockSpec((tm, N), lambda i: (jnp.minimum(i, n - 1), 0)),
                   pl.BlockSpec((tm, D), lambda i: (jnp.minimum(i, n - 1), 0)),
                   pl.BlockSpec((tm, C), lambda i: (jnp.maximum(i - 1, 0), 0))],
        out_shape=[jax.ShapeDtypeStruct((T, N), F32), jax.ShapeDtypeStruct((T, D), BF16), jax.ShapeDtypeStruct((T, C), F32)],
        scratch_shapes=[pltpu.VMEM((HALO + tm, C), F32)],
        args=(x, gain, w, conv_w), exchange=exchange)
    return proj, h, qkv, brought


def gdn_conv_bwd(proj, conv_w, dqkv, *, name):
    T = proj.shape[0]
    tt, tc = _tile(T, 1024), HEAD_DIM
    nt = T // tt

    def body(p_ref, c_ref, n_ref, w_ref, dp_ref, dc_ref, dn_ref, dx_ref, dw_ref):
        kind = _qkv_kind(pl.program_id(0))
        w = w_ref[...]
        e = _extend(p_ref, c_ref, n_ref, nt)
        c = _causal_conv(e, w, GDN_CONV)
        s, s_grad = _silu_and_grad(c)
        dy = _extend(dp_ref, dc_ref, dn_ref, nt)
        r = lax.rsqrt(jnp.sum(s * s, axis=-1, keepdims=True) + EPS)
        y = s * r
        ds_norm = r * _l2norm_scale(kind) * (dy - y * jnp.sum(dy * y, axis=-1, keepdims=True))
        ds = jnp.where(kind == 2, dy, ds_norm)
        dx, dw = _causal_conv_bwd(e, ds * s_grad, w, GDN_CONV, tt)
        dx_ref[...] = _bf(dx)
        _accumulate(dw_ref, dw, 1)

    return pl.pallas_call(
        body, name=name, grid=(3 * HEADS, nt),
        in_specs=_halo_specs(T, tt, tc, lambda j: j) + [pl.BlockSpec((SUBLANES, tc), lambda j, i: (0, j))]
        + _halo_specs(T, tt, tc, lambda j: j),
        out_specs=[pl.BlockSpec((tt, tc), lambda j, i: (i, j)), pl.BlockSpec((SUBLANES, tc), lambda j, i: (0, j))],
        out_shape=[jax.ShapeDtypeStruct((T, 3 * D_MODEL), BF16), jax.ShapeDtypeStruct((SUBLANES, 3 * D_MODEL), F32)],
        compiler_params=_cp("parallel", "arbitrary"),
    )(proj, proj, proj, conv_w, dqkv, dqkv, dqkv)


FFN_COLS = 256


FFN_UP_ROWS = 256


def ffn_up_act(x, gain, w, conv_w, *, name, exchange=None):
    T, D = x.shape
    N = w.shape[1]
    F = N // 2
    tm, tc = _tile(T, FFN_UP_ROWS), FFN_COLS
    n = T // tm

    def body(x_ref, g_ref, w_ref, cw_ref, u_ref, h_ref, act_ref, held):
        i = pl.program_id(0)

        @pl.when(i == 0)
        def _():
            held[...] = jnp.zeros_like(held)

        for c in range(F // tc):
            gate_cols, up_cols = slice(c * tc, (c + 1) * tc), slice(F + c * tc, F + (c + 1) * tc)
            gate = _causal_conv(held[:, gate_cols], cw_ref[:, gate_cols], FFN_CONV)
            up_ = _causal_conv(held[:, up_cols], cw_ref[:, up_cols], FFN_CONV)
            act_ref[:, gate_cols] = _bf((_silu(gate) * up_)[HALO:])
        xf = x_ref[...]
        h = _bf(xf * lax.rsqrt(jnp.mean(xf * xf, axis=-1, keepdims=True) + EPS) * g_ref[...])
        h_ref[...] = h
        u = _dot(h, w_ref[...])
        u_ref[...] = u
        held[0:HALO, :] = held[tm:tm + HALO, :]
        held[HALO:HALO + tm, :] = u

    (u, h, act), brought = hosted_call(
        body, name=name, grid=(n + 1,),
        in_specs=[pl.BlockSpec((tm, D), lambda i: (jnp.minimum(i, n - 1), 0)), pl.BlockSpec((1, D), lambda i: (0, 0)),
                  pl.BlockSpec((D, N), lambda i: (0, 0)), pl.BlockSpec((SUBLANES, N), lambda i: (0, 0))],
        out_specs=[pl.BlockSpec((tm, N), lambda i: (jnp.minimum(i, n - 1), 0)),
                   pl.BlockSpec((tm, D), lambda i: (jnp.minimum(i, n - 1), 0)),
                   pl.BlockSpec((tm, F), lambda i: (jnp.maximum(i - 1, 0), 0))],
        out_shape=[jax.ShapeDtypeStruct((T, N), F32), jax.ShapeDtypeStruct((T, D), BF16), jax.ShapeDtypeStruct((T, F), BF16)],
        scratch_shapes=[pltpu.VMEM((HALO + tm, N), F32)],
        args=(x, gain, w, conv_w), exchange=exchange)
    return u, h, act, brought


def ffn_act_bwd(u, conv_w, dact, *, name):
    T = u.shape[0]
    tt, tc = _tile(T, 1024), FFN_COLS
    half = D_FF // tc
    nt = T // tt

    def body(gp, gc, gn, up, uc, un, wg_ref, wu_ref, dp, dc_, dn, dug_ref, duu_ref, dwg_ref, dwu_ref):
        wg, wu = wg_ref[...], wu_ref[...]
        eg, eu = _extend(gp, gc, gn, nt), _extend(up, uc, un, nt)
        gate, up_ = _causal_conv(eg, wg, FFN_CONV), _causal_conv(eu, wu, FFN_CONV)
        da = _extend(dp, dc_, dn, nt)
        act, act_grad = _silu_and_grad(gate)
        dxg, dwg = _causal_conv_bwd(eg, da * up_ * act_grad, wg, FFN_CONV, tt)
        dxu, dwu = _causal_conv_bwd(eu, da * act, wu, FFN_CONV, tt)
        dug_ref[...] = _bf(dxg)
        duu_ref[...] = _bf(dxu)
        _accumulate(dwg_ref, dwg, 1)
        _accumulate(dwu_ref, dwu, 1)

    return pl.pallas_call(
        body, name=name, grid=(half, nt),
        in_specs=_halo_specs(T, tt, tc, lambda j: j) + _halo_specs(T, tt, tc, lambda j: j + half)
        + [pl.BlockSpec((SUBLANES, tc), lambda j, i: (0, j)), pl.BlockSpec((SUBLANES, tc), lambda j, i: (0, j + half))]
        + _halo_specs(T, tt, tc, lambda j: j),
        out_specs=[pl.BlockSpec((tt, tc), lambda j, i: (i, j)), pl.BlockSpec((tt, tc), lambda j, i: (i, j)),
                   pl.BlockSpec((SUBLANES, tc), lambda j, i: (0, j)), pl.BlockSpec((SUBLANES, tc), lambda j, i: (0, j))],
        out_shape=[jax.ShapeDtypeStruct((T, D_FF), BF16), jax.ShapeDtypeStruct((T, D_FF), BF16),
                   jax.ShapeDtypeStruct((SUBLANES, D_FF), F32), jax.ShapeDtypeStruct((SUBLANES, D_FF), F32)],
        compiler_params=_cp("parallel", "arbitrary"),
    )(u, u, u, u, u, u, conv_w, conv_w, dact, dact, dact)


def head_norm_fwd(x, gain, z=None, *, x_col=0, z_col=0, name, out_dtype=F32):
    T = x.shape[0]
    tt = _tile(T, 512)
    gated = z is not None

    def body(*refs):
        x_ref, g_ref = refs[0], refs[1]
        o_ref = refs[-1]
        for h in range(HEADS):
            cols = slice(h * HEAD_DIM, (h + 1) * HEAD_DIM)
            xf = x_ref[:, cols]
            y = xf * lax.rsqrt(jnp.mean(xf * xf, axis=-1, keepdims=True) + EPS) * g_ref[...]
            if gated:
                y = y * _silu(refs[2][:, cols])
            o_ref[:, cols] = y.astype(out_dtype)

    wide = lambda col: pl.BlockSpec((tt, D_MODEL), lambda i: (i, col // HEADS))
    ins = [wide(x_col), pl.BlockSpec((1, HEAD_DIM), lambda i: (0, 0))]
    args = [x, gain]
    if gated:
        ins.append(wide(z_col))
        args.append(z)
    return pl.pallas_call(
        body, name=name, grid=(T // tt,), in_specs=ins, out_specs=wide(0),
        out_shape=jax.ShapeDtypeStruct((T, D_MODEL), out_dtype),
        compiler_params=_cp("parallel"),
    )(*args)


def head_norm_bwd(x, gain, dys, z=None, *, x_col=0, z_col=0, name, dx_dtype=F32):
    T = x.shape[0]
    tt = _tile(T, 512)
    gated = z is not None
    nd = len(dys)

    def body(*refs):
        x_ref, g_ref = refs[0], refs[1]
        outs = refs[2 + nd + (1 if gated else 0):]
        dx_ref, dg_ref = outs[0], outs[-1]
        folds = []
        for h in range(HEADS):
            cols = slice(h * HEAD_DIM, (h + 1) * HEAD_DIM)
            xf = x_ref[:, cols]
            r = lax.rsqrt(jnp.mean(xf * xf, axis=-1, keepdims=True) + EPS)
            xh = xf * r
            dy = refs[2][:, cols].astype(F32)
            for d_ref in refs[3:2 + nd]:
                dy = dy + d_ref[:, cols].astype(F32)
            if gated:
                gate, gate_grad = _silu_and_grad(refs[2 + nd][:, cols])
                outs[1][:, cols] = _bf(dy * xh * g_ref[...] * gate_grad)
                dn = dy * gate
            else:
                dn = dy
            dxh = dn * g_ref[...]
            dx_ref[:, cols] = (r * (dxh - xh * jnp.mean(dxh * xh, axis=-1, keepdims=True))).astype(dx_dtype)
            folds.append(_fold_rows(dn * xh))
        _accumulate(dg_ref, jnp.concatenate(folds, axis=1), 0)

    wide = lambda col: pl.BlockSpec((tt, D_MODEL), lambda i: (i, col // HEADS))
    ins = [wide(x_col), pl.BlockSpec((1, HEAD_DIM), lambda i: (0, 0))] + [wide(0)] * nd
    args = [x, gain] + list(dys)
    outs = [wide(0)]
    shapes = [jax.ShapeDtypeStruct((T, D_MODEL), dx_dtype)]
    if gated:
        ins.append(wide(z_col))
        args.append(z)
        outs.append(wide(0))
        shapes.append(jax.ShapeDtypeStruct((T, D_MODEL), BF16))
    outs.append(pl.BlockSpec((SUBLANES, D_MODEL), lambda i: (0, 0)))
    shapes.append(jax.ShapeDtypeStruct((SUBLANES, D_MODEL), F32))
    return pl.pallas_call(
        body, name=name, grid=(T // tt,), in_specs=ins, out_specs=outs, out_shape=shapes,
        compiler_params=_cp("arbitrary"),
    )(*args)


def gates_fwd(proj, a_log, dt_bias, *, name):
    T = proj.shape[0]
    tt = _tile(T, 1024)

    def body(p_ref, al_ref, dt_ref, o_ref):
        p = p_ref[...]
        lane = lax.broadcasted_iota(jnp.int32, p.shape, 1)
        o_ref[...] = jnp.where(lane < HEADS, -jnp.exp(al_ref[...]) * _softplus(p + dt_ref[...]), _sigmoid(p))

    return pl.pallas_call(
        body, name=name, grid=(T // tt,),
        in_specs=[pl.BlockSpec((tt, LANES), lambda i: (i, AB_BLOCK)), pl.BlockSpec((1, LANES), lambda i: (0, 0)),
                  pl.BlockSpec((1, LANES), lambda i: (0, 0))],
        out_specs=pl.BlockSpec((tt, LANES), lambda i: (i, 0)),
        out_shape=jax.ShapeDtypeStruct((T, LANES), F32),
        compiler_params=_cp("parallel"),
    )(proj, a_log, dt_bias)


def gates_bwd(proj, a_log, dt_bias, dgate, *, name):
    T = proj.shape[0]
    tt = _tile(T, 1024)

    def body(p_ref, al_ref, dt_ref, d_ref, dp_ref, dal_ref, ddt_ref):
        p, d = p_ref[...], d_ref[...]
        lane = lax.broadcasted_iota(jnp.int32, p.shape, 1)
        ea = jnp.exp(al_ref[...])
        pa = p + dt_ref[...]
        da = -d * ea * _sigmoid(pa)
        b = _sigmoid(p)
        dp_ref[...] = _bf(jnp.where(lane < HEADS, da, jnp.where(lane < 2 * HEADS, d * b * (1.0 - b), 0.0)))
        _accumulate(dal_ref, _fold_rows(jnp.where(lane < HEADS, -d * ea * _softplus(pa), 0.0)), 0)
        _accumulate(ddt_ref, _fold_rows(jnp.where(lane < HEADS, da, 0.0)), 0)

    acc = pl.BlockSpec((SUBLANES, LANES), lambda i: (0, 0))
    return pl.pallas_call(
        body, name=name, grid=(T // tt,),
        in_specs=[pl.BlockSpec((tt, LANES), lambda i: (i, AB_BLOCK)), pl.BlockSpec((1, LANES), lambda i: (0, 0)),
                  pl.BlockSpec((1, LANES), lambda i: (0, 0)), pl.BlockSpec((tt, LANES), lambda i: (i, 0))],
        out_specs=[pl.BlockSpec((tt, LANES), lambda i: (i, 0)), acc, acc],
        out_shape=[jax.ShapeDtypeStruct((T, LANES), BF16), jax.ShapeDtypeStruct((SUBLANES, LANES), F32),
                   jax.ShapeDtypeStruct((SUBLANES, LANES), F32)],
        compiler_params=_cp("arbitrary"),
    )(proj, a_log, dt_bias, dgate)


def loss_fwd(y, target, *, name):
    T, D = y.shape
    tt = _tile(T, 512)

    def body(y_ref, t_ref, dy_ref, l_ref):
        d = y_ref[...] - t_ref[...]
        dy_ref[...] = d * (1.0 / D)
        sq = d * d
        lanes = sq[:, 0:LANES]
        for c in range(1, D // LANES):
            lanes = lanes + sq[:, c * LANES:(c + 1) * LANES]
        _accumulate(l_ref, _fold_rows(lanes) * (0.5 / D), 0)

    return pl.pallas_call(
        body, name=name, grid=(T // tt,),
        in_specs=[pl.BlockSpec((tt, D), lambda i: (i, 0)), pl.BlockSpec((tt, D), lambda i: (i, 0))],
        out_specs=[pl.BlockSpec((tt, D), lambda i: (i, 0)), pl.BlockSpec((SUBLANES, LANES), lambda i: (0, 0))],
        out_shape=[jax.ShapeDtypeStruct((T, D), F32), jax.ShapeDtypeStruct((SUBLANES, LANES), F32)],
        compiler_params=_cp("arbitrary"),
    )(y, target)


def _split_bf16(x):
    hi = _bf(x)
    return hi, _bf(x - hi.astype(F32))


def _dot3(a, b, dot=_dot):
    return dot(a[0], b[0]) + dot(a[0], b[1]) + dot(a[1], b[0])


def _each(fn, *lists):
    return [fn(*args) for args in zip(*lists)]


def _unit_lower_inverses(lows):
    c = lows[0].shape[0]
    ii = lax.broadcasted_iota(jnp.int32, (c, c), 0)
    jj = lax.broadcasted_iota(jnp.int32, (c, c), 1)
    eye = jnp.where(ii == jj, 1.0, 0.0)
    invs = _each(lambda low: eye - low, lows)
    powers = _each(_split_bf16, lows)
    for _ in range(int(math.log2(c)) - 1):
        powers = _each(lambda p: _split_bf16(_dot3(p, p)), powers)
        invs = _each(lambda inv, p: inv + _dot3(_split_bf16(inv), p), invs, powers)
    return invs


def _gdn_chunks(heads):
    q, k, v, a_col, a_row, b_col, s0 = (list(t) for t in zip(*heads))
    c = q[0].shape[0]
    ii = lax.broadcasted_iota(jnp.int32, (c, c), 0)
    jj = lax.broadcasted_iota(jnp.int32, (c, c), 1)
    tri, strict = ii >= jj, ii > jj
    g_col = _each(lambda ar: jnp.sum(jnp.where(tri, ar, 0.0), axis=1, keepdims=True), a_row)
    g_row = _each(lambda ac: jnp.sum(jnp.where(ii <= jj, ac, 0.0), axis=0, keepdims=True), a_col)
    gam = _each(lambda gc, gr: jnp.exp(jnp.where(tri, gc - gr, -jnp.inf)), g_col, g_row)
    g_last = _each(lambda ac: jnp.sum(ac, axis=0, keepdims=True), a_col)
    gam_col = _each(jnp.exp, g_col)
    del_col = _each(lambda gl, gc: jnp.exp(gl - gc), g_last, g_col)
    kb = _each(lambda k_, b: k_ * b, k, b_col)
    m = _each(lambda kb_, k_: _dot_nt(_bf(kb_), _bf(k_)), kb, k)
    ks = _each(lambda k_, s: _dot(_bf(k_), _bf(s)), k, s0)
    qk = _each(lambda q_, k_: _dot_nt(_bf(q_), _bf(k_)), q, k)
    inv = _unit_lower_inverses(_each(lambda m_, g: jnp.where(strict, m_ * g, 0.0), m, gam))
    e = _each(lambda v_, gc, ks_: v_ - gc * ks_, v, gam_col, ks)
    inv = _each(_split_bf16, inv)
    vn = _each(lambda inv_, b, e_: _dot3(inv_, _split_bf16(b * e_)), inv, b_col, e)
    p = _each(lambda qk_, g: jnp.where(tri, qk_ * g, 0.0), qk, gam)
    return [dict(tri=tri, strict=strict, ii=ii, jj=jj, gam=gam[g], g_last=g_last[g], gam_col=gam_col[g], del_col=del_col[g],
                 kb=kb[g], m=m[g], inv=inv[g], ks=ks[g], e=e[g], vn=vn[g], qk=qk[g], p=p[g]) for g in range(len(heads))]


GDN_HEADS_PER_STEP = HEADS


def _head_cols(ref, g):
    return ref[:, g * HEAD_DIM:(g + 1) * HEAD_DIM]


def _load_heads(q_ref, k_ref, v_ref, ar_ref, br_ref, states):
    return [(_head_cols(q_ref, g), _head_cols(k_ref, g), _head_cols(v_ref, g), ar_ref[g, 0].T, ar_ref[g, 0], br_ref[g, 0].T, states(g))
            for g in range(GDN_HEADS_PER_STEP)]


def gdn_fwd(qkv, a_row, b_row, *, name, exchange=None):
    T = qkv.shape[0]
    C = GDN_CHUNK
    N = T // C
    G = GDN_HEADS_PER_STEP

    def body(q_ref, k_ref, v_ref, ar_ref, br_ref, o_ref, s_ref, state):
        @pl.when(pl.program_id(1) == 0)
        def _():
            state[...] = jnp.zeros_like(state)
        loaded = _load_heads(q_ref, k_ref, v_ref, ar_ref, br_ref, lambda g: state[g])
        ws = _gdn_chunks(loaded)
        qs = _each(lambda h: _dot(_bf(h[0]), _bf(h[6])), loaded)
        pv = _each(lambda w: _dot(_bf(w["p"]), _bf(w["vn"])), ws)
        kv = _each(lambda h, w: _dot_tn(_bf(w["del_col"] * h[1]), _bf(w["vn"])), loaded, ws)
        for g, w in enumerate(ws):
            s0 = loaded[g][6]
            s_ref[g, 0] = s0
            o_ref[:, g * HEAD_DIM:(g + 1) * HEAD_DIM] = w["gam_col"] * qs[g] + pv[g]
            state[g] = jnp.exp(w["g_last"]) * s0 + kv[g]

    per = HEADS // G
    blk = lambda off: pl.BlockSpec((C, G * HEAD_DIM), lambda h, n: (n, off * per + h))
    row = pl.BlockSpec((G, 1, 1, C), lambda h, n: (h, n, 0, 0))
    (o, states), brought = hosted_call(
        body, name=name, grid=(per, N),
        in_specs=[blk(0), blk(1), blk(2), row, row],
        out_specs=[blk(0), pl.BlockSpec((G, 1, HEAD_DIM, HEAD_DIM), lambda h, n: (h, n, 0, 0))],
        out_shape=[jax.ShapeDtypeStruct((T, D_MODEL), F32), jax.ShapeDtypeStruct((HEADS, N, HEAD_DIM, HEAD_DIM), F32)],
        scratch_shapes=[pltpu.VMEM((G, HEAD_DIM, HEAD_DIM), F32)],
        args=(qkv, qkv, qkv, a_row, b_row), exchange=exchange)
    return o, states, brought


def gdn_bwd(qkv, a_row, b_row, states, do, *, name, exchange=None):
    T = qkv.shape[0]
    C = GDN_CHUNK
    N = T // C
    G = GDN_HEADS_PER_STEP
    per = HEADS // G
    rev = lambda n: N - 1 - n
    row = pl.BlockSpec((G, 1, 1, C), lambda h, n: (h, rev(n), 0, 0))

    def body(q_ref, k_ref, v_ref, ar_ref, br_ref, s_ref, do_ref, dqkv_ref, da_ref, db_ref, dstate):
        @pl.when(pl.program_id(1) == 0)
        def _():
            dstate[...] = jnp.zeros_like(dstate)
        loaded = _load_heads(q_ref, k_ref, v_ref, ar_ref, br_ref, lambda g: s_ref[g, 0])
        hs = _gdn_chunks(loaded)
        for g, d in enumerate(hs):
            q, k, _, _, _, b, s0 = loaded[g]
            d.update(q=q, k=k, b=b, s0=s0, ds1=dstate[g], dout=_head_cols(do_ref, g))
        rows = lambda t: jnp.sum(t, axis=1, keepdims=True)
        ii_col = lax.broadcasted_iota(jnp.int32, (C, 1), 0)

        def stage(**fns):
            for key, fn in fns.items():
                for d in hs:
                    d[key] = fn(d)

        stage(s0b=lambda d: _bf(d["s0"]), ds1b=lambda d: _bf(d["ds1"]), doutb=lambda d: _bf(d["dout"]),
              kbf=lambda d: _bf(d["k"]), qbf=lambda d: _bf(d["q"]), vnb=lambda d: _bf(d["vn"]))
        stage(dvn=lambda d: _dot_tn(_bf(d["p"]), d["doutb"]) + _dot(_bf(d["del_col"] * d["k"]), d["ds1b"]),
              dqk=lambda d: jnp.where(d["tri"], _dot_nt(d["doutb"], d["vnb"]), 0.0) * d["gam"],
              qs=lambda d: _dot(d["qbf"], d["s0b"]),
              dkd=lambda d: _dot_nt(d["vnb"], d["ds1b"]))
        stage(dr=lambda d: _dot3(d["inv"], _split_bf16(d["dvn"]), _dot_tn),
              dq=lambda d: d["gam_col"] * _dot_nt(d["doutb"], d["s0b"]) + _dot(_bf(d["dqk"]), d["kbf"]),
              dk=lambda d: _dot_tn(_bf(d["dqk"]), d["qbf"]) + d["del_col"] * d["dkd"],
              ddel=lambda d: d["del_col"] * rows(d["dkd"] * d["k"]))
        stage(dg=lambda d: d["gam_col"] * rows(d["dout"] * d["qs"]) - d["ddel"],
              dg_last=lambda d: jnp.sum(d["ddel"], axis=0, keepdims=True)
              + jnp.exp(d["g_last"]) * jnp.sum(rows(d["ds1"] * d["s0"]), axis=0, keepdims=True),
              dm=lambda d: jnp.where(d["strict"], -_dot_nt(_bf(d["dr"]), d["vnb"]), 0.0) * d["gam"],
              de=lambda d: d["b"] * d["dr"])
        stage(dkb=lambda d: _dot(_bf(d["dm"]), d["kbf"]),
              dks=lambda d: -d["gam_col"] * d["de"])
        stage(dk=lambda d: d["dk"] + _dot_tn(_bf(d["dm"]), _bf(d["kb"])) + _dot_nt(_bf(d["dks"]), d["s0b"]) + d["b"] * d["dkb"],
              dbeta=lambda d: rows(d["dr"] * d["e"]) + rows(d["dkb"] * d["k"]),
              ds0=lambda d: jnp.exp(d["g_last"]) * d["ds1"] + _dot_tn(_bf(d["gam_col"] * d["q"]), d["doutb"])
              + _dot_tn(d["kbf"], _bf(d["dks"])),
              wg=lambda d: d["dqk"] * d["qk"] + d["dm"] * d["m"])
        stage(dg=lambda d: d["dg"] - d["gam_col"] * rows(d["de"] * d["ks"]) + rows(d["wg"])
              - jnp.sum(d["wg"], axis=0, keepdims=True).T + jnp.where(ii_col == C - 1, d["dg_last"], 0.0))
        stage(da=lambda d: jnp.sum(jnp.where(d["ii"] >= d["jj"], d["dg"], 0.0), axis=0, keepdims=True),
              db=lambda d: d["dbeta"].T)
        for g, d in enumerate(hs):
            dstate[g] = d["ds0"]
            da_ref[g, 0] = d["da"]
            db_ref[g, 0] = d["db"]
            for part, key in enumerate(("dq", "dk", "de")):
                start = part * D_MODEL + g * HEAD_DIM
                dqkv_ref[:, start:start + HEAD_DIM] = d[key]

    assert per == 1
    blk = lambda off: pl.BlockSpec((C, G * HEAD_DIM), lambda h, n: (rev(n), off * per + h))
    (dqkv, da, db), brought = hosted_call(
        body, name=name, grid=(per, N),
        in_specs=[blk(0), blk(1), blk(2), row, row,
                  pl.BlockSpec((G, 1, HEAD_DIM, HEAD_DIM), lambda h, n: (h, rev(n), 0, 0)), blk(0)],
        out_specs=[pl.BlockSpec((C, 3 * D_MODEL), lambda h, n: (rev(n), 0)), row, row],
        out_shape=[jax.ShapeDtypeStruct((T, 3 * D_MODEL), F32)] + [jax.ShapeDtypeStruct((HEADS, N, 1, C), F32)] * 2,
        scratch_shapes=[pltpu.VMEM((G, HEAD_DIM, HEAD_DIM), F32)],
        args=(qkv, qkv, qkv, a_row, b_row, states, do), exchange=exchange)
    return dqkv, da, db, brought


SB_BLOCK = 128


SB_QBLOCKS = 8


def _sb_rows(j, blk):
    return pl.ds(pl.multiple_of(j * blk, blk), blk)


def _sb_tile(qb, k_ref, i, j, blk, live):
    z = _dot_nt(qb, _bf(k_ref[_sb_rows(j, blk), :]))
    t_idx = i * blk + lax.broadcasted_iota(jnp.int32, (blk, blk), 0)
    s_idx = j * blk + lax.broadcasted_iota(jnp.int32, (blk, blk), 1)
    mask = jnp.logical_and(s_idx < t_idx, live)
    lf = jnp.where(mask, -_softplus(z), 0.0)
    return z, mask, lf


SB_DEAD = 105.0


def sb_fwd(q, k, v, *, k_col=0, v_col=0, name):
    T = q.shape[0]
    blk = _tile(T, SB_BLOCK)
    P = min(SB_QBLOCKS, T // blk)
    scale = HEAD_DIM ** -0.5

    def body(q_ref, k_ref, v_ref, o_ref, l_ref, n_ref):
        iq = [P * pl.program_id(1) + p for p in range(P)]
        qb = [_bf(q_ref[p * blk:(p + 1) * blk, :] * scale) for p in range(P)]
        r_idx = lax.broadcasted_iota(jnp.int32, (blk, blk), 0)
        c_idx = lax.broadcasted_iota(jnp.int32, (blk, blk), 1)
        later = _bf(jnp.where(r_idx > c_idx, 1.0, 0.0))

        def live_blocks(jj, runs):
            return [jnp.logical_and(jj <= i, jnp.max(run) > -SB_DEAD) for i, run in zip(iq, runs)]

        def alive(carry):
            jj, _, runs, _ = carry
            some = False
            for f in live_blocks(jj, runs):
                some = jnp.logical_or(some, f)
            return some

        def step(carry):
            jj, accs, runs, visited = carry
            live = live_blocks(jj, runs)
            js = [jnp.maximum(i - jj, 0) for i in iq]
            tiles = _each(lambda q_, i, j, f: _sb_tile(q_, k_ref, i, j, blk, f), qb, iq, js, live)
            parts = _each(lambda t: _split_bf16(t[2]), tiles)
            after = _each(lambda run, s: run + _dot(s[0], later) + _dot(s[1], later), runs, parts)
            a = _each(lambda t, af: jnp.where(t[1], jnp.exp(t[0] + t[2] + af), 0.0), tiles, after)
            vb = _each(lambda j: _bf(v_ref[_sb_rows(j, blk), :]), js)
            accs = _each(lambda acc, a_, v_: acc + _dot(_bf(a_), v_), accs, a, vb)
            runs = _each(lambda run, t: run + jnp.sum(t[2], axis=1, keepdims=True), runs, tiles)
            visited = _each(lambda n, f: n + f.astype(jnp.int32), visited, live)
            return jj + 1, accs, runs, visited

        start = (jnp.int32(0), [jnp.zeros((blk, HEAD_DIM), F32)] * P, [jnp.zeros((blk, 1), F32)] * P, [jnp.int32(0)] * P)
        _, accs, runs, visited = lax.while_loop(alive, step, start)
        for p in range(P):
            o_ref[p * blk:(p + 1) * blk, :] = accs[p]
            l_ref[0, p] = runs[p].T
            n_ref[0, p] = jnp.full((SUBLANES, LANES), visited[p].astype(F32))

    return pl.pallas_call(
        body, name=name, grid=(HEADS, T // (P * blk)),
        in_specs=[pl.BlockSpec((P * blk, HEAD_DIM), lambda h, i: (i, h)), pl.BlockSpec((T, HEAD_DIM), lambda h, i: (0, k_col + h)),
                  pl.BlockSpec((T, HEAD_DIM), lambda h, i: (0, v_col + h))],
        out_specs=[pl.BlockSpec((P * blk, HEAD_DIM), lambda h, i: (i, h)), pl.BlockSpec((1, P, 1, blk), lambda h, i: (h, i, 0, 0)),
                   pl.BlockSpec((1, P, SUBLANES, LANES), lambda h, i: (h, i, 0, 0))],
        out_shape=[jax.ShapeDtypeStruct((T, D_MODEL), F32), jax.ShapeDtypeStruct((HEADS, T // blk, 1, blk), F32),
                   jax.ShapeDtypeStruct((HEADS, T // blk, SUBLANES, LANES), F32)],
        compiler_params=_cp("parallel", "arbitrary"),
    )(q, k, v)


def sb_bwd(q, k, v, ltot, visited, do, *, k_col=0, v_col=0, name, exchange=None):
    T = q.shape[0]
    blk = _tile(T, SB_BLOCK)
    P = min(SB_QBLOCKS, T // blk)
    scale = HEAD_DIM ** -0.5

    def body(q_ref, k_ref, v_ref, l_ref, n_ref, do_ref, dq_ref, dk_ref, dv_ref):
        iq = [P * pl.program_id(1) + p for p in range(P)]
        count = [jnp.max(n_ref[0, p]).astype(jnp.int32) for p in range(P)]
        first = [i + 1 - n for i, n in zip(iq, count)]
        trips = count[0]
        for n in count[1:]:
            trips = jnp.maximum(trips, n)

        @pl.when(pl.program_id(1) == 0)
        def _():
            dk_ref[...] = jnp.zeros_like(dk_ref)
            dv_ref[...] = jnp.zeros_like(dv_ref)

        qb = [_bf(q_ref[p * blk:(p + 1) * blk, :] * scale) for p in range(P)]
        dob = [_bf(do_ref[p * blk:(p + 1) * blk, :]) for p in range(P)]
        ltot_ = [l_ref[0, p].T for p in range(P)]
        r_idx = lax.broadcasted_iota(jnp.int32, (blk, blk), 0)
        c_idx = lax.broadcasted_iota(jnp.int32, (blk, blk), 1)
        upto = _bf(jnp.where(r_idx <= c_idx, 1.0, 0.0))
        before = _bf(jnp.where(r_idx < c_idx, 1.0, 0.0))

        def step(t, carry):
            dqs, lpre, cpre = carry
            live = [f + t <= i for f, i in zip(first, iq)]
            js = [jnp.minimum(f + t, i) for f, i in zip(first, iq)]
            tiles = _each(lambda q_, i, j, f: _sb_tile(q_, k_ref, i, j, blk, f), qb, iq, js, live)
            parts = _each(lambda tl: _split_bf16(tl[2]), tiles)
            after = _each(lambda lt, lp, s: lt - (lp + _dot(s[0], upto) + _dot(s[1], upto)), ltot_, lpre, parts)
            ls = _each(lambda tl: tl[0] + tl[2], tiles)
            a = _each(lambda tl, ls_, af: jnp.where(tl[1], jnp.exp(ls_ + af), 0.0), tiles, ls, after)
            vb = _each(lambda j: _bf(v_ref[_sb_rows(j, blk), :]), js)
            p = _each(lambda a_, do_, v_: a_ * _dot_nt(do_, v_), a, dob, vb)
            pparts = _each(_split_bf16, p)
            left = _each(lambda cp, s: cp + _dot(s[0], before) + _dot(s[1], before), cpre, pparts)
            dzb = _each(lambda tl, p_, lf_, ls_: _bf(jnp.where(tl[1], p_ * jnp.exp(tl[2]) - lf_ * jnp.exp(ls_), 0.0)),
                        tiles, p, left, ls)
            dks = _each(lambda dz, q_: _dot_tn(dz, q_), dzb, qb)
            dvs = _each(lambda a_, do_: _dot_tn(_bf(a_), do_), a, dob)
            dqs = _each(lambda dq, dz, j: dq + _dot(dz, _bf(k_ref[_sb_rows(j, blk), :])), dqs, dzb, js)
            for j, dk, dv in zip(js, dks, dvs):
                dk_ref[_sb_rows(j, blk), :] += dk
                dv_ref[_sb_rows(j, blk), :] += dv
            lpre = _each(lambda lp, tl: lp + jnp.sum(tl[2], axis=1, keepdims=True), lpre, tiles)
            cpre = _each(lambda cp, p_: cp + jnp.sum(p_, axis=1, keepdims=True), cpre, p)
            return dqs, lpre, cpre

        zero = [jnp.zeros((blk, 1), F32)] * P
        dqs, _, _ = lax.fori_loop(0, trips, step, ([jnp.zeros((blk, HEAD_DIM), F32)] * P, zero, zero))
        for p in range(P):
            dq_ref[p * blk:(p + 1) * blk, :] = dqs[p] * scale

    full = lambda off: pl.BlockSpec((T, HEAD_DIM), lambda h, i: (0, off + h))
    tile = pl.BlockSpec((P * blk, HEAD_DIM), lambda h, i: (i, h))
    (dq, dk, dv), brought = hosted_call(
        body, name=name, grid=(HEADS, T // (P * blk)),
        in_specs=[tile, full(k_col), full(v_col), pl.BlockSpec((1, P, 1, blk), lambda h, i: (h, i, 0, 0)),
                  pl.BlockSpec((1, P, SUBLANES, LANES), lambda h, i: (h, i, 0, 0)), tile],
        out_specs=[tile, full(0), full(0)],
        out_shape=[jax.ShapeDtypeStruct((T, D_MODEL), F32)] * 3,
        scratch_shapes=[], args=(q, k, v, ltot, visited, do), exchange=exchange)
    return dq, dk, dv, brought


def sum_slots(slots, *, name):
    n, R, C = slots.shape
    tr = _tile(R, 256)

    def body(s_ref, o_ref):
        acc = s_ref[0].astype(F32)
        for k in range(1, n):
            acc = acc + s_ref[k].astype(F32)
        o_ref[...] = acc

    return pl.pallas_call(
        body, name=name, grid=(R // tr,),
        in_specs=[pl.BlockSpec((n, tr, C), lambda i: (0, i, 0))],
        out_specs=pl.BlockSpec((tr, C), lambda i: (i, 0)),
        out_shape=jax.ShapeDtypeStruct((R, C), F32),
        compiler_params=_cp("parallel"),
    )(slots)


def adamw(w, g_parts, m, v, *, name):
    R, C = w.shape
    tr = _tile(R, 256)
    n = len(g_parts)

    def body(*refs):
        w_ref, m_ref, v_ref = refs[0], refs[1 + n], refs[2 + n]
        g_ref, d_ref, nm_ref, nv_ref = refs[3 + n:]
        g = refs[1][...]
        for r in refs[2:1 + n]:
            g = g + r[...]
        m2 = ADAM_B1 * m_ref[...] + (1.0 - ADAM_B1) * g
        v2 = ADAM_B2 * v_ref[...] + (1.0 - ADAM_B2) * (g * g)
        m_hat = m2 / (1.0 - ADAM_B1 ** ADAM_STEP)
        v_hat = v2 / (1.0 - ADAM_B2 ** ADAM_STEP)
        g_ref[...] = g
        d_ref[...] = -ADAM_LR * (m_hat / (jnp.sqrt(v_hat) + ADAM_EPS) + ADAM_WD * w_ref[...])
        nm_ref[...] = m2
        nv_ref[...] = v2

    spec = pl.BlockSpec((tr, C), lambda i: (i, 0))
    return pl.pallas_call(
        body, name=name, grid=(R // tr,),
        in_specs=[spec] * (3 + n), out_specs=[spec] * 4,
        out_shape=[jax.ShapeDtypeStruct((R, C), F32)] * 4,
        compiler_params=_cp("parallel"),
    )(w, *g_parts, m, v)


CHIP_FLIPS = ((0, 1), (1, 0), (1, 1))


def _place():
    return lax.axis_index("x"), lax.axis_index("y"), lax.axis_index("c")


def _flip(v, f):
    return 1 - v if f else v


class ChipExchange:
    def __init__(self, arrays, scatter):
        self.arrays, self.scatter, self.n = list(arrays), scatter, len(arrays)
        lead = () if scatter else (N_CHIPS,)
        self.out_shape = [jax.ShapeDtypeStruct(lead + a.shape, a.dtype) for a in arrays]
        self.scratch = [pltpu.SemaphoreType.DMA((self.n, len(CHIP_FLIPS))), pltpu.SemaphoreType.DMA((self.n, len(CHIP_FLIPS))),
                        pltpu.SemaphoreType.DMA((self.n,))]

    def _copies(self, ins, outs, sems):
        send_sems, recv_sems, local_sems = sems
        x, y, c = _place()
        me = 2 * x + y
        local, sent, landing = [], [], []
        for k in range(self.n):
            local.append(pltpu.make_async_copy(ins[k].at[me] if self.scatter else ins[k], outs[k].at[me], local_sems.at[k]))
            for p, (fx, fy) in enumerate(CHIP_FLIPS):
                px, py = _flip(x, fx), _flip(y, fy)
                src = ins[k].at[2 * px + py] if self.scatter else ins[k]
                for dst, group in ((me, sent), (2 * px + py, landing)):
                    group.append(pltpu.make_async_remote_copy(src_ref=src, dst_ref=outs[k].at[dst], send_sem=send_sems.at[k, p],
                                                              recv_sem=recv_sems.at[k, p], device_id=(px, py, c), device_id_type=MESH))
        return local, sent, landing

    def start(self, ins, outs, sems):
        local, sent, _ = self._copies(ins, outs, sems)
        for cp in local + sent:
            cp.start()

    def finish(self, ins, outs, sems):
        local, _, landing = self._copies(ins, outs, sems)
        for cp in landing:
            cp.wait_send()
            cp.wait_recv()
        for cp in local:
            cp.wait()


def chip_exchange(arrays, *, scatter, name):
    ex = ChipExchange(arrays, scatter)
    n = ex.n

    def body(*refs):
        ins, outs, sems = refs[:n], refs[n:2 * n], refs[2 * n:]
        ex.start(ins, outs, sems)
        ex.finish(ins, outs, sems)

    return pl.pallas_call(body, name=name, in_specs=[ANY] * n, out_specs=[ANY] * n, out_shape=ex.out_shape,
                          scratch_shapes=ex.scratch)(*arrays)


def hosted_call(body, *, name, grid, in_specs, out_specs, out_shape, scratch_shapes, args, exchange=None):
    in_specs, out_specs, out_shape = list(in_specs), list(out_specs), list(out_shape)
    params = _cp(*["arbitrary"] * len(grid))
    if exchange is None:
        outs = pl.pallas_call(body, name=name, grid=grid, in_specs=in_specs, out_specs=out_specs, out_shape=out_shape,
                              scratch_shapes=list(scratch_shapes), compiler_params=params)(*args)
        return list(outs), []
    n, n_in, n_out, n_scr = exchange.n, len(in_specs), len(out_specs), len(scratch_shapes)

    def both(*refs):
        ins, t_ins = refs[:n_in], refs[n_in:n_in + n]
        outs, t_outs = refs[n_in + n:n_in + n + n_out], refs[n_in + n + n_out:n_in + 2 * n + n_out]
        scratch, sems = refs[n_in + 2 * n + n_out:n_in + 2 * n + n_out + n_scr], refs[n_in + 2 * n + n_out + n_scr:]
        first, last = True, True
        for axis, size in enumerate(grid):
            first = jnp.logical_and(first, pl.program_id(axis) == 0)
            last = jnp.logical_and(last, pl.program_id(axis) == size - 1)

        @pl.when(first)
        def _():
            exchange.start(t_ins, t_outs, sems)

        body(*ins, *outs, *scratch)

        @pl.when(last)
        def _():
            exchange.finish(t_ins, t_outs, sems)

    outs = pl.pallas_call(both, name=name, grid=grid, in_specs=in_specs + [ANY] * n, out_specs=out_specs + [ANY] * n,
                          out_shape=out_shape + exchange.out_shape, scratch_shapes=list(scratch_shapes) + exchange.scratch,
                          compiler_params=params)(*args, *exchange.arrays)
    return list(outs[:n_out]), list(outs[n_out:])


def sibling_swap(arrays, *, name):
    n = len(arrays)

    def body(*refs):
        ins, outs = refs[:n], refs[n:2 * n]
        send_sems, recv_sems = refs[2 * n:]
        x, y, c = _place()
        copies = [pltpu.make_async_remote_copy(src_ref=ins[k], dst_ref=outs[k], send_sem=send_sems.at[k], recv_sem=recv_sems.at[k],
                                               device_id=(x, y, 1 - c), device_id_type=MESH) for k in range(n)]
        for cp in copies:
            cp.start()
        for cp in copies:
            cp.wait_send()
            cp.wait_recv()

    return pl.pallas_call(
        body, name=name, in_specs=[ANY] * n, out_specs=[ANY] * n,
        out_shape=[jax.ShapeDtypeStruct(a.shape, a.dtype) for a in arrays],
        scratch_shapes=[pltpu.SemaphoreType.DMA((n,)), pltpu.SemaphoreType.DMA((n,))],
    )(*arrays)


DEVICE_FLIPS = tuple((fx, fy, fc) for fx in (0, 1) for fy in (0, 1) for fc in (0, 1) if fx or fy or fc)


def all_gather_devices(a, *, name):
    def body(a_ref, o_ref, send_sems, recv_sems, local_sem):
        x, y, c = _place()
        me = 4 * x + 2 * y + c
        local = pltpu.make_async_copy(a_ref, o_ref.at[me], local_sem)
        local.start()
        for p, (fx, fy, fc) in enumerate(DEVICE_FLIPS):
            peer = (_flip(x, fx), _flip(y, fy), _flip(c, fc))
            pltpu.make_async_remote_copy(src_ref=a_ref, dst_ref=o_ref.at[me], send_sem=send_sems.at[p], recv_sem=recv_sems.at[p],
                                         device_id=peer, device_id_type=MESH).start()
        for p, (fx, fy, fc) in enumerate(DEVICE_FLIPS):
            px, py, pc = _flip(x, fx), _flip(y, fy), _flip(c, fc)
            landing = pltpu.make_async_remote_copy(src_ref=a_ref, dst_ref=o_ref.at[4 * px + 2 * py + pc], send_sem=send_sems.at[p],
                                                   recv_sem=recv_sems.at[p], device_id=(px, py, pc), device_id_type=MESH)
            landing.wait_send()
            landing.wait_recv()
        local.wait()

    return pl.pallas_call(
        body, name=name, in_specs=[ANY], out_specs=ANY,
        out_shape=jax.ShapeDtypeStruct((N_DEV,) + a.shape, a.dtype),
        scratch_shapes=[pltpu.SemaphoreType.DMA((len(DEVICE_FLIPS),)), pltpu.SemaphoreType.DMA((len(DEVICE_FLIPS),)),
                        pltpu.SemaphoreType.DMA(())],
    )(a)


def _row(v):
    return v.reshape(1, -1)


def _pad_rows(w):
    return jnp.pad(w, ((0, SUBLANES - w.shape[0]), (0, 0)))


def _pad_lanes(v):
    return jnp.pad(v.reshape(1, -1), ((0, 0), (0, LANES - v.shape[-1])))


def _head_layouts(gates):
    T = gates.shape[0]
    rows = lambda cols: cols.T.reshape(HEADS, T // GDN_CHUNK, 1, GDN_CHUNK)
    return rows(gates[:, :HEADS]), rows(gates[:, HEADS:2 * HEADS])


def _ffn_fwd(x, W, l, plan):
    conv = _pad_rows(W["ffn_conv"][l])
    u, h, act, brought = ffn_up_act(x, _row(W["ffn_norm"][l]), W["ffn_w_up"][l], conv, name=f"ffn{l}_up_act",
                                    exchange=plan.fetch(f"ffn{l}_act"))
    plan.arrived(f"ffn{l}_act", brought, W)
    y = matmul_residual(act, W["ffn_w_down"][l], x, name=f"ffn{l}_down")
    return y, (x, h, u, conv, act)


def _ffn_bwd(dx, saved, W, l, G):
    x, h, u, conv, act = saved
    dact = matmul_nt(dx, W["ffn_w_down"][l], name=f"ffn{l}_down_dx", out_dtype=BF16)
    G["ffn_w_down"][l] = matmul_tn(act, dx, name=f"ffn{l}_down_dw")
    dug, duu, dwg, dwu = ffn_act_bwd(u, conv, dact, name=f"ffn{l}_act_bwd")
    G["ffn_conv"][l] = jnp.concatenate([dwg, dwu], axis=1)[:FFN_CONV]
    G["ffn_w_up"][l] = jnp.concatenate([matmul_tn(h, dug, name=f"ffn{l}_gate_dw"), matmul_tn(h, duu, name=f"ffn{l}_up_dw")], axis=1)
    dx, dgain = matmul_nt_normbwd([(dug, 0), (duu, 1)], W["ffn_w_up"][l], x, _row(W["ffn_norm"][l]), dx, name=f"ffn{l}_up_dx")
    G["ffn_norm"][l] = dgain.sum(0)
    return dx


class NoTraffic:
    def fetch(self, host):
        return None

    def arrived(self, host, brought, W):
        pass

    def flush(self, G):
        return None

    def landed(self, brought):
        pass


def _gdn_fwd(x, W, l, plan):
    conv = _pad_rows(W["a_conv"][l])
    a_log, dt_bias = _pad_lanes(W["a_log"][l]), _pad_lanes(W["a_dt_bias"][l])
    proj, h, qkv, brought = gdn_in_conv(x, _row(W["a_norm"][l]), W["a_w_in"][l], conv, name=f"gdn{l}_in_conv",
                                        exchange=plan.fetch(f"gdn{l}_conv"))
    plan.arrived(f"gdn{l}_conv", brought, W)
    heads = _head_layouts(gates_fwd(proj, a_log, dt_bias, name=f"gdn{l}_gates"))
    o, states, brought = gdn_fwd(qkv, *heads, name=f"gdn{l}_rule", exchange=plan.fetch(f"gdn{l}_rule"))
    plan.arrived(f"gdn{l}_rule", brought, W)
    gain = _row(W["a_out_norm"][l])
    on = head_norm_fwd(o, gain, proj, z_col=Z_BLOCK, name=f"gdn{l}_outnorm", out_dtype=BF16)
    y = matmul_residual(on, W["a_w_out"][l], x, name=f"gdn{l}_out")
    return y, (x, h, proj, conv, a_log, dt_bias, qkv, heads, states, o, gain, on)


def _gdn_bwd(dx, saved, W, l, G, plan):
    x, h, proj, conv, a_log, dt_bias, qkv, heads, states, o, gain, on = saved
    T = x.shape[0]
    don = matmul_nt(dx, W["a_w_out"][l], name=f"gdn{l}_out_dx")
    G["a_w_out"][l] = matmul_tn(on, dx, name=f"gdn{l}_out_dw")
    do, dz, dgain = head_norm_bwd(o, gain, [don], proj, z_col=Z_BLOCK, name=f"gdn{l}_outnorm_bwd")
    G["a_out_norm"][l] = dgain.reshape(SUBLANES, HEADS, HEAD_DIM).sum((0, 1))
    dqkv, da, db, brought = gdn_bwd(qkv, *heads, states, do, name=f"gdn{l}_rule_bwd", exchange=plan.flush(G))
    plan.landed(brought)
    dqkv, dconv = gdn_conv_bwd(proj, conv, dqkv, name=f"gdn{l}_conv_bwd")
    G["a_conv"][l] = dconv[:GDN_CONV]
    dgate = jnp.concatenate([da.reshape(HEADS, T).T, db.reshape(HEADS, T).T, jnp.zeros((T, LANES - 2 * HEADS), F32)], axis=1)
    dab, dal, ddt = gates_bwd(proj, a_log, dt_bias, dgate, name=f"gdn{l}_gates_bwd")
    G["a_log"][l] = dal.sum(0)[:HEADS]
    G["a_dt_bias"][l] = ddt.sum(0)[:HEADS]
    parts = [(dqkv, 0), (dz, Z_BLOCK * LANES // D_MODEL), (dab, AB_BLOCK)]
    G["a_w_in"][l] = jnp.concatenate([matmul_tn(h, d, name=f"gdn{l}_in_dw{i}") for i, (d, _) in enumerate(parts)], axis=1)
    last = plan.flush(G) if l == 0 else None
    out = matmul_nt_normbwd(parts, W["a_w_in"][l], x, _row(W["a_norm"][l]), dx, name=f"gdn{l}_in_dx", exchange=last)
    dx, dgain = out[:2]
    if last is not None:
        plan.landed(out[2])
    G["a_norm"][l] = dgain.sum(0)
    return dx


def _sb_fwd(x, kn, kv, W, j):
    qp, h = norm_matmul(x, _row(W["b_norm"][j]), W["b_w_q"][j], name=f"sb{j}_q")
    gain = _row(W["q_norm"][j])
    q = head_norm_fwd(qp, gain, name=f"sb{j}_qnorm")
    o, ltot, visited = sb_fwd(q, kn, kv, v_col=HEADS, name=f"sb{j}_attn")
    y = matmul_residual(o, W["b_w_out"][j], x, name=f"sb{j}_out")
    return y, (x, h, qp, gain, q, o, ltot, visited)


def _sb_bwd(dx, saved, kn, kv, W, j, G, plan):
    x, h, qp, gain, q, o, ltot, visited = saved
    do = matmul_nt(dx, W["b_w_out"][j], name=f"sb{j}_out_dx")
    G["b_w_out"][j] = matmul_tn(o, dx, name=f"sb{j}_out_dw")
    dq, dk, dv, brought = sb_bwd(q, kn, kv, ltot, visited, do, v_col=HEADS, name=f"sb{j}_attn_bwd", exchange=plan.flush(G))
    plan.landed(brought)
    dqp, dgain = head_norm_bwd(qp, gain, [dq], name=f"sb{j}_qnorm_bwd", dx_dtype=BF16)
    G["q_norm"][j] = dgain.reshape(SUBLANES, HEADS, HEAD_DIM).sum((0, 1))
    G["b_w_q"][j] = matmul_tn(h, dqp, name=f"sb{j}_q_dw")
    dx, dgain = matmul_nt_normbwd([(dqp, 0)], W["b_w_q"][j], x, _row(W["b_norm"][j]), dx, name=f"sb{j}_q_dx")
    G["b_norm"][j] = dgain.sum(0)
    return dx, dk, dv


def local_step(x, target, W, plan=None):
    plan = plan or NoTraffic()
    G = {k: [None] * (N_A if k.startswith("a_") else N_B if k in ("b_norm", "b_w_q", "q_norm", "b_w_out") else DEPTH)
         for k in ("a_norm", "a_w_in", "a_conv", "a_log", "a_dt_bias", "a_out_norm", "a_w_out", "b_norm", "b_w_q", "q_norm",
                   "b_w_out", "ffn_norm", "ffn_w_up", "ffn_conv", "ffn_w_down")}
    G["w_kv"] = [None]
    tape = []
    for l in range(N_A):
        x, s_mix = _gdn_fwd(x, W, l, plan)
        x, s_ffn = _ffn_fwd(x, W, l, plan)
        tape.append((s_mix, s_ffn))
    x_kv = x
    kv, h_kv = norm_matmul(x, _row(W["kv_norm"]), W["w_kv"], name="kv_proj")
    k_gain = _row(W["k_norm"])
    kn = head_norm_fwd(kv, k_gain, name="k_norm")
    for j in range(N_B):
        x, s_mix = _sb_fwd(x, kn, kv, W, j)
        x, s_ffn = _ffn_fwd(x, W, N_A + j, plan)
        tape.append((s_mix, s_ffn))
    dx, loss = loss_fwd(x, target, name="loss")

    dks, dvs = [], []
    for j in reversed(range(N_B)):
        s_mix, s_ffn = tape[N_A + j]
        dx = _ffn_bwd(dx, s_ffn, W, N_A + j, G)
        dx, dk, dv = _sb_bwd(dx, s_mix, kn, kv, W, j, G, plan)
        dks.append(dk)
        dvs.append(dv)
    dkp, dgain = head_norm_bwd(kv, k_gain, dks, name="k_norm_bwd", dx_dtype=BF16)
    G["k_norm"] = dgain.reshape(SUBLANES, HEADS, HEAD_DIM).sum((0, 1))
    G["w_kv"][0] = jnp.concatenate([matmul_tn(h_kv, dkp, name="k_proj_dw"), matmul_tn(h_kv, dvs, name="v_proj_dw")], axis=1)
    dx, dgain = matmul_nt_normbwd([(dkp, 0)] + [(dv, 1) for dv in dvs], W["w_kv"], x_kv, _row(W["kv_norm"]), dx,
                                  name="kv_proj_dx")
    G["kv_norm"] = dgain.sum(0)
    for l in reversed(range(N_A)):
        s_mix, s_ffn = tape[l]
        dx = _ffn_bwd(dx, s_ffn, W, l, G)
        dx = _gdn_bwd(dx, s_mix, W, l, G, plan)
    return loss, dx, G


MATRICES = {"a_w_in": 2, "a_w_out": 1, "w_kv": 1, "b_w_q": 1, "b_w_out": 1, "ffn_w_up": 2, "ffn_w_down": 1}
SMALL_SHARDED = {"a_norm": 1, "a_conv": 2, "ffn_conv": 2}
SMALL_REPLICATED = ("a_log", "a_dt_bias", "a_out_norm", "kv_norm", "k_norm", "b_norm", "q_norm", "ffn_norm")
WEIGHT_ORDER = ("a_norm", "a_w_in", "a_conv", "a_log", "a_dt_bias", "a_out_norm", "a_w_out", "kv_norm", "w_kv", "k_norm",
                "b_norm", "b_w_q", "q_norm", "b_w_out", "ffn_norm", "ffn_w_up", "ffn_conv", "ffn_w_down")
SMALL_ORDER = tuple(n for n in WEIGHT_ORDER if n not in MATRICES)
PACK_QUANTUM = SUBLANES * LANES


def _unshard(g, axis):
    g = jnp.moveaxis(g, 0, axis)
    return g.reshape(g.shape[:axis] + (g.shape[axis] * g.shape[axis + 1],) + g.shape[axis + 2:])


def _shards(full, axis):
    n = full.shape[axis] // N_CHIPS
    return jnp.moveaxis(full.reshape(full.shape[:axis] + (N_CHIPS, n) + full.shape[axis + 1:]), axis, 0)


def _pack(arrays):
    parts = []
    for a in arrays:
        flat = a.reshape(-1)
        parts.append(jnp.pad(flat, (0, -flat.shape[0] % PACK_QUANTUM)).reshape(-1, LANES))
    return jnp.concatenate(parts, axis=0)


def _unpack(buf, shapes):
    out, row = [], 0
    for s in shapes:
        size = math.prod(s)
        rows = -(-size // PACK_QUANTUM) * SUBLANES
        out.append(buf[row:row + rows].reshape(-1)[:size].reshape(s))
        row += rows
    return out


def _stack(per_layer):
    return jnp.stack(per_layer) if isinstance(per_layer, list) else per_layer


FETCH_BESIDE = {
    "gdn0_conv": (("ffn_w_up", 0, 1),),
    "gdn0_rule": (("ffn_w_down", 0, 2), ("w_kv", 0, None), ("b_w_q", 0, 2), ("b_w_out", 0, 2)),
    "ffn0_act": (("a_w_in", 1, 1), ("a_w_out", 1, 1)),
    "gdn1_conv": (("ffn_w_up", 2, 1),),
    "gdn1_rule": (("ffn_w_down", 2, 2), ("ffn_w_up", 1, 1)),
    "ffn1_act": (("ffn_w_up", 3, 1),),
}
FETCH_FIRST = (("a_w_in", 0, 1), ("a_w_out", 0, 1))


class Traffic:
    def __init__(self, local):
        self.local = local
        self.shipped, self.received = [], {}

    @staticmethod
    def _assemble(W, name, first, layers, gathered):
        whole = _unshard(gathered, MATRICES[name])
        if name == "a_w_in":
            whole = jnp.pad(whole, ((0, 0), (0, 0), (0, W_IN_PAD - W_IN_COLS)))
        if layers is None:
            W[name] = whole
        else:
            W.setdefault(name, {}).update({first + i: whole[i] for i in range(layers)})

    def _shards_of(self, wanted):
        return [(self.local[name] if layers is None else self.local[name][first:first + layers]).astype(BF16)
                for name, first, layers in wanted]

    def fetch_first(self, extra):
        brought = chip_exchange(self._shards_of(FETCH_FIRST) + list(extra), scatter=False, name="gather_first")
        W = {}
        for (name, first, layers), g in zip(FETCH_FIRST, brought):
            self._assemble(W, name, first, layers, g)
        return W, brought[len(FETCH_FIRST):]

    def fetch(self, host):
        return ChipExchange(self._shards_of(FETCH_BESIDE[host]), scatter=False) if host in FETCH_BESIDE else None

    def arrived(self, host, brought, W):
        for (name, first, layers), g in zip(FETCH_BESIDE.get(host, ()), brought):
            self._assemble(W, name, first, layers, g)

    def _ready(self, G):
        out = []
        for name, axis in MATRICES.items():
            for layer, g in enumerate(G[name]):
                if g is None or (name, layer) in self.shipped:
                    continue
                g = g[:, :W_IN_COLS] if name == "a_w_in" else g
                piece = _shards(g, axis - (0 if name == "w_kv" else 1)).astype(BF16)
                out.append((name, layer, piece.reshape(N_CHIPS, -1, piece.shape[-1])))
        return out

    def flush(self, G):
        ready = self._ready(G)
        self.in_flight = [r[:2] for r in ready]
        self.shipped += self.in_flight
        return ChipExchange([r[2] for r in ready], scatter=True) if ready else None

    def landed(self, brought):
        self.received.update(zip(self.in_flight, brought))
        self.in_flight = []

    def flush_last(self, G):
        ready = self._ready(G)
        if ready:
            self.shipped += [r[:2] for r in ready]
            brought = chip_exchange([r[2] for r in ready], scatter=True, name="scatter_last")
            self.received.update(zip([r[:2] for r in ready], brought))

    def my_sums(self):
        sums = {}
        for name in MATRICES:
            layers = sorted(l for n, l in self.received if n == name)
            parts = [sum_slots(self.received[(name, l)], name=f"sum_{name}{l}") for l in layers]
            sums[name] = parts[0] if len(parts) == 1 else jnp.concatenate(parts, axis=0)
        return sums


def _as_2d(a):
    return a.reshape(-1, a.shape[-1])


def kernel(x, a_norm, a_w_in, a_conv, a_log, a_dt_bias, a_out_norm, a_w_out, kv_norm, w_kv, k_norm, b_norm, b_w_q, q_norm, b_w_out, ffn_norm, ffn_w_up, ffn_conv, ffn_w_down, loss_target, m_a_norm, m_a_w_in, m_a_conv, m_a_log, m_a_dt_bias, m_a_out_norm, m_a_w_out, m_kv_norm, m_w_kv, m_k_norm, m_b_norm, m_b_w_q, m_q_norm, m_b_w_out, m_ffn_norm, m_ffn_w_up, m_ffn_conv, m_ffn_w_down, v_a_norm, v_a_w_in, v_a_conv, v_a_log, v_a_dt_bias, v_a_out_norm, v_a_w_out, v_kv_norm, v_w_kv, v_k_norm, v_b_norm, v_b_w_q, v_q_norm, v_b_w_out, v_ffn_norm, v_ffn_w_up, v_ffn_conv, v_ffn_w_down):
    local = dict(a_norm=a_norm, a_w_in=a_w_in, a_conv=a_conv, a_log=a_log, a_dt_bias=a_dt_bias, a_out_norm=a_out_norm,
                 a_w_out=a_w_out, kv_norm=kv_norm, w_kv=w_kv, k_norm=k_norm, b_norm=b_norm, b_w_q=b_w_q, q_norm=q_norm,
                 b_w_out=b_w_out, ffn_norm=ffn_norm, ffn_w_up=ffn_w_up, ffn_conv=ffn_conv, ffn_w_down=ffn_w_down)
    mom = dict(a_norm=m_a_norm, a_w_in=m_a_w_in, a_conv=m_a_conv, a_log=m_a_log, a_dt_bias=m_a_dt_bias, a_out_norm=m_a_out_norm,
               a_w_out=m_a_w_out, kv_norm=m_kv_norm, w_kv=m_w_kv, k_norm=m_k_norm, b_norm=m_b_norm, b_w_q=m_b_w_q, q_norm=m_q_norm,
               b_w_out=m_b_w_out, ffn_norm=m_ffn_norm, ffn_w_up=m_ffn_w_up, ffn_conv=m_ffn_conv, ffn_w_down=m_ffn_w_down)
    var = dict(a_norm=v_a_norm, a_w_in=v_a_w_in, a_conv=v_a_conv, a_log=v_a_log, a_dt_bias=v_a_dt_bias, a_out_norm=v_a_out_norm,
               a_w_out=v_a_w_out, kv_norm=v_kv_norm, w_kv=v_w_kv, k_norm=v_k_norm, b_norm=v_b_norm, b_w_q=v_b_w_q, q_norm=v_q_norm,
               b_w_out=v_b_w_out, ffn_norm=v_ffn_norm, ffn_w_up=v_ffn_w_up, ffn_conv=v_ffn_conv, ffn_w_down=v_ffn_w_down)
    chip = 2 * lax.axis_index("x") + lax.axis_index("y")

    mats = list(MATRICES)
    small_sharded = list(SMALL_SHARDED)
    traffic = Traffic(local)
    W, (vectors,) = traffic.fetch_first([_pack([local[n] for n in small_sharded])])
    W.update({n: local[n] for n in SMALL_REPLICATED})
    shard_shapes = [local[n].shape for n in small_sharded]
    per_chip = [_unpack(vectors[j], shard_shapes) for j in range(N_CHIPS)]
    for i, n in enumerate(small_sharded):
        W[n] = _unshard(jnp.stack([per_chip[j][i] for j in range(N_CHIPS)]), SMALL_SHARDED[n])

    T = x.shape[1]
    loss_part, dx, G = local_step(x.reshape(T, D_MODEL), loss_target.reshape(T, D_MODEL), W, traffic)

    traffic.flush_last(G)
    sums = traffic.my_sums()
    mine = [sums[n] for n in mats]
    theirs = sibling_swap(mine, name="swap_grads")

    small_full = {n: _stack(G[n]) for n in SMALL_ORDER}
    packed = _pack([small_full[n] for n in SMALL_ORDER] + [loss_part])
    total = sum_slots(all_gather_devices(packed, name="gather_small"), name="sum_small")
    small_shapes = [small_full[n].shape for n in SMALL_ORDER] + [loss_part.shape]
    summed = dict(zip(SMALL_ORDER + ("loss",), _unpack(total, small_shapes)))
    loss = jnp.sum(summed.pop("loss"))
    for n, axis in SMALL_SHARDED.items():
        size = local[n].shape[axis]
        summed[n] = lax.dynamic_slice_in_dim(summed[n], chip * size, size, axis)

    grads, deltas, new_m, new_v = {}, {}, {}, {}
    for n, p_mine, p_theirs in zip(mats, mine, theirs):
        outs = adamw(_as_2d(local[n]), [p_mine, p_theirs], _as_2d(mom[n]), _as_2d(var[n]), name=f"adamw_{n}")
        grads[n], deltas[n], new_m[n], new_v[n] = [o.reshape(local[n].shape) for o in outs]
    small_local_shapes = [local[n].shape for n in SMALL_ORDER]
    outs = adamw(_pack([local[n] for n in SMALL_ORDER]), [_pack([summed[n] for n in SMALL_ORDER])],
                 _pack([mom[n] for n in SMALL_ORDER]), _pack([var[n] for n in SMALL_ORDER]), name="adamw_small")
    for d, o in zip((grads, deltas, new_m, new_v), outs):
        d.update(zip(SMALL_ORDER, _unpack(o, small_local_shapes)))

    return (loss, dx.reshape(x.shape), *[grads[n] for n in WEIGHT_ORDER], *[deltas[n] for n in WEIGHT_ORDER],
            *[new_m[n] for n in WEIGHT_ORDER], *[new_v[n] for n in WEIGHT_ORDER])
```

```python
import math

import jax
import jax.numpy as jnp
from jax import lax
from jax.experimental import pallas as pl
from jax.experimental.pallas import tpu as pltpu

F32 = jnp.float32
BF16 = jnp.bfloat16

D_MODEL = 1024
HEADS = 8
HEAD_DIM = 128
GDN_CONV = 4
GDN_CHUNK = 64
D_FF = 2816
FFN_CONV = 3
EPS = 1e-6
N_A = 2
N_B = 2
DEPTH = N_A + N_B
W_IN_COLS = 4 * D_MODEL + 2 * HEADS
W_IN_PAD = 4 * D_MODEL + 128
Z_BLOCK = 3 * D_MODEL // 128
AB_BLOCK = 4 * D_MODEL // 128

ADAM_LR = 0.001
ADAM_B1 = 0.9
ADAM_B2 = 0.999
ADAM_EPS = 1e-08
ADAM_WD = 0.01
ADAM_STEP = 10

LANES = 128
SUBLANES = 8
VMEM_LIMIT = 56 * 1024 * 1024
HALO = SUBLANES
N_CHIPS = 4
N_DEV = 8

HI = lax.Precision.HIGHEST
MESH = pl.DeviceIdType.MESH
ANY = pl.BlockSpec(memory_space=pl.ANY)


def _cp(*sem):
    return pltpu.CompilerParams(dimension_semantics=sem, vmem_limit_bytes=VMEM_LIMIT)


def _tile(n, want, align=SUBLANES):
    t = (min(n, want) // align) * align
    while t > 0 and n % t:
        t -= align
    return t if t > 0 else n


def _dot(a, b, precision=None):
    return jnp.dot(a, b, preferred_element_type=F32, precision=precision)


def _dot_nt(a, b, precision=None):
    return lax.dot_general(a, b, (((1,), (1,)), ((), ())), preferred_element_type=F32, precision=precision)


def _dot_tn(a, b, precision=None):
    return lax.dot_general(a, b, (((0,), (0,)), ((), ())), preferred_element_type=F32, precision=precision)


def _bf(x):
    return x.astype(BF16)


def _sigmoid(x):
    return 0.5 * jnp.tanh(0.5 * x) + 0.5


def _softplus(x):
    return jnp.maximum(x, 0.0) + jnp.log(1.0 + jnp.exp(-jnp.abs(x)))


def _silu(x):
    return x * _sigmoid(x)


def _silu_and_grad(x):
    s = _sigmoid(x)
    return x * s, s * (1.0 + x * (1.0 - s))


def _fold_rows(v):
    return jnp.sum(v.reshape(v.shape[0] // SUBLANES, SUBLANES, v.shape[1]), axis=0)


def _accumulate(ref, value, axis):
    @pl.when(pl.program_id(axis) == 0)
    def _():
        ref[...] = jnp.zeros_like(ref)
    ref[...] += value


TILE_BUDGET = 44 * 1024 * 1024


def _rows_that_fit(T, fixed_bytes, row_bytes, want=1024):
    tm = _tile(T, want)
    while tm > SUBLANES and 2 * (fixed_bytes + tm * row_bytes) > TILE_BUDGET:
        tm //= 2
    return tm


def norm_matmul(x, gain, w, *, name):
    T, D = x.shape
    N = w.shape[1]
    tn = N
    tm = _rows_that_fit(T, D * N * 2, D * 4 + D * 2 + N * 4)

    def body(x_ref, g_ref, w_ref, y_ref, h_ref):
        @pl.when(pl.program_id(1) == 0)
        def _():
            xf = x_ref[...]
            r = lax.rsqrt(jnp.mean(xf * xf, axis=-1, keepdims=True) + EPS)
            h_ref[...] = _bf(xf * r * g_ref[...])
        y_ref[...] = _dot(h_ref[...], w_ref[...])

    return pl.pallas_call(
        body, name=name, grid=(T // tm, N // tn),
        in_specs=[pl.BlockSpec((tm, D), lambda i, j: (i, 0)), pl.BlockSpec((1, D), lambda i, j: (0, 0)),
                  pl.BlockSpec((D, tn), lambda i, j: (0, j))],
        out_specs=[pl.BlockSpec((tm, tn), lambda i, j: (i, j)), pl.BlockSpec((tm, D), lambda i, j: (i, 0))],
        out_shape=[jax.ShapeDtypeStruct((T, N), F32), jax.ShapeDtypeStruct((T, D), BF16)],
        compiler_params=_cp("parallel", "arbitrary"),
    )(x, gain, w)


def matmul_residual(a, w, res, *, name):
    T, K = a.shape
    N = w.shape[1]
    tn = N
    tm = _rows_that_fit(T, K * N * 2, K * a.dtype.itemsize + 2 * N * 4)

    def body(a_ref, w_ref, r_ref, o_ref):
        o_ref[...] = r_ref[...] + _dot(_bf(a_ref[...]), w_ref[...])

    return pl.pallas_call(
        body, name=name, grid=(T // tm, N // tn),
        in_specs=[pl.BlockSpec((tm, K), lambda i, j: (i, 0)), pl.BlockSpec((K, tn), lambda i, j: (0, j)),
                  pl.BlockSpec((tm, tn), lambda i, j: (i, j))],
        out_specs=pl.BlockSpec((tm, tn), lambda i, j: (i, j)),
        out_shape=jax.ShapeDtypeStruct((T, N), F32),
        compiler_params=_cp("parallel", "parallel"),
    )(a, w, res)


def matmul_nt(dy, w, *, name, out_dtype=F32):
    T, N = dy.shape
    K = w.shape[0]
    tk = K
    tm = _rows_that_fit(T, K * N * 2, N * dy.dtype.itemsize + K * jnp.dtype(out_dtype).itemsize)

    def body(dy_ref, w_ref, o_ref):
        o_ref[...] = _dot_nt(_bf(dy_ref[...]), w_ref[...]).astype(out_dtype)

    return pl.pallas_call(
        body, name=name, grid=(T // tm, K // tk),
        in_specs=[pl.BlockSpec((tm, N), lambda i, j: (i, 0)), pl.BlockSpec((tk, N), lambda i, j: (j, 0))],
        out_specs=pl.BlockSpec((tm, tk), lambda i, j: (i, j)),
        out_shape=jax.ShapeDtypeStruct((T, K), out_dtype),
        compiler_params=_cp("parallel", "parallel"),
    )(dy, w)


def matmul_nt_normbwd(parts, w, x, gain, dres, *, name, exchange=None):
    T = x.shape[0]
    D = w.shape[0]
    tm = _tile(T, 256)
    n = len(parts)

    def body(*refs):
        x_ref, g_ref, dr_ref, dx_ref, dg_ref = refs[2 * n:]
        dh = _dot_nt(_bf(refs[0][...]), refs[n][...])
        for i in range(1, n):
            dh = dh + _dot_nt(_bf(refs[i][...]), refs[n + i][...])
        xf = x_ref[...]
        r = lax.rsqrt(jnp.mean(xf * xf, axis=-1, keepdims=True) + EPS)
        xh = xf * r
        dxh = dh * g_ref[...]
        dx_ref[...] = dr_ref[...] + r * (dxh - xh * jnp.mean(dxh * xh, axis=-1, keepdims=True))
        _accumulate(dg_ref, _fold_rows(dh * xh), 0)

    dy_specs = [pl.BlockSpec((tm, dy.shape[1]), lambda i: (i, 0)) for dy, _ in parts]
    w_specs = [pl.BlockSpec((D, dy.shape[1]), lambda i, b=b: (0, b)) for dy, b in parts]
    (dx, dgain), brought = hosted_call(
        body, name=name, grid=(T // tm,),
        in_specs=dy_specs + w_specs + [pl.BlockSpec((tm, D), lambda i: (i, 0)), pl.BlockSpec((1, D), lambda i: (0, 0)),
                                       pl.BlockSpec((tm, D), lambda i: (i, 0))],
        out_specs=[pl.BlockSpec((tm, D), lambda i: (i, 0)), pl.BlockSpec((SUBLANES, D), lambda i: (0, 0))],
        out_shape=[jax.ShapeDtypeStruct((T, D), F32), jax.ShapeDtypeStruct((SUBLANES, D), F32)],
        scratch_shapes=[], args=(*[dy for dy, _ in parts], *([w] * n), x, gain, dres), exchange=exchange)
    return (dx, dgain, brought) if exchange is not None else (dx, dgain)


def matmul_tn(a, dys, *, name):
    dys = dys if isinstance(dys, (list, tuple)) else [dys]
    T, K = a.shape
    N = dys[0].shape[1]
    tk = _tile(K, 1408, LANES)
    tn = _tile(N, 1408 if N <= 2816 else 512, LANES)
    tm = _tile(T, (4096 if tn <= 512 else 2048) // len(dys))

    def body(a_ref, *refs):
        dy = refs[0][...]
        for r in refs[1:-1]:
            dy = dy + r[...]
        _accumulate(refs[-1], _dot_tn(_bf(a_ref[...]), _bf(dy)), 2)

    return pl.pallas_call(
        body, name=name, grid=(K // tk, N // tn, T // tm),
        in_specs=[pl.BlockSpec((tm, tk), lambda i, j, t: (t, i))] + [pl.BlockSpec((tm, tn), lambda i, j, t: (t, j))] * len(dys),
        out_specs=pl.BlockSpec((tk, tn), lambda i, j, t: (i, j)),
        out_shape=jax.ShapeDtypeStruct((K, N), F32),
        compiler_params=_cp("parallel", "parallel", "arbitrary"),
    )(a, *dys)


def _halo_specs(T, tt, tc, col):
    per = tt // HALO
    last = T // HALO - 1
    return [pl.BlockSpec((HALO, tc), lambda j, i: (jnp.maximum(i * per - 1, 0), col(j))),
            pl.BlockSpec((tt, tc), lambda j, i: (i, col(j))),
            pl.BlockSpec((HALO, tc), lambda j, i: (jnp.minimum((i + 1) * per, last), col(j)))]


def _extend(prev_ref, cur_ref, next_ref, nt):
    i = pl.program_id(1)
    p = jnp.where(i > 0, prev_ref[...].astype(F32), 0.0)
    q = jnp.where(i < nt - 1, next_ref[...].astype(F32), 0.0)
    return jnp.concatenate([p, cur_ref[...].astype(F32), q], axis=0)


def _rows_before(e, s):
    return e if s == 0 else pltpu.roll(e, s, 0)


def _rows_after(e, s):
    return e if s == 0 else pltpu.roll(e, e.shape[0] - s, 0)


def _causal_conv(e, w, taps):
    y = w[taps - 1:taps, :] * e
    for s in range(1, taps):
        y = y + w[taps - 1 - s:taps - s, :] * _rows_before(e, s)
    return y


def _causal_conv_bwd(e, dc, w, taps, tt):
    lo, hi = HALO, HALO + tt
    dx = w[taps - 1:taps, :] * dc
    dws = [None] * taps
    dws[taps - 1] = jnp.sum((e * dc)[lo:hi], axis=0, keepdims=True)
    for s in range(1, taps):
        dx = dx + w[taps - 1 - s:taps - s, :] * _rows_after(dc, s)
        dws[taps - 1 - s] = jnp.sum((_rows_before(e, s) * dc)[lo:hi], axis=0, keepdims=True)
    dw = jnp.concatenate(dws + [jnp.zeros((SUBLANES - taps, e.shape[1]), F32)], axis=0)
    return dx[lo:hi], dw


def _qkv_kind(col_block):
    return (col_block >= HEADS).astype(jnp.int32) + (col_block >= 2 * HEADS).astype(jnp.int32)


def _l2norm_scale(kind):
    return jnp.where(kind == 0, HEAD_DIM ** -0.5, 1.0)


GDN_IN_ROWS = 256


def gdn_in_conv(x, gain, w, conv_w, *, name, exchange=None):
    T, D = x.shape
    N = w.shape[1]
    C = 3 * D_MODEL
    tm = _tile(T, GDN_IN_ROWS)
    n = T // tm

    def body(x_ref, g_ref, w_ref, cw_ref, p_ref, h_ref, o_ref, held):
        @pl.when(pl.program_id(0) == 0)
        def _():
            held[...] = jnp.zeros_like(held)

        for block in range(3 * HEADS):
            cols = slice(block * HEAD_DIM, (block + 1) * HEAD_DIM)
            s = _silu(_causal_conv(held[:, cols], cw_ref[:, cols], GDN_CONV))[HALO:]
            if block < 2 * HEADS:
                scale = HEAD_DIM ** -0.5 if block < HEADS else 1.0
                s = s * (lax.rsqrt(jnp.sum(s * s, axis=-1, keepdims=True) + EPS) * scale)
            o_ref[:, cols] = s
        xf = x_ref[...]
        h = _bf(xf * lax.rsqrt(jnp.mean(xf * xf, axis=-1, keepdims=True) + EPS) * g_ref[...])
        h_ref[...] = h
        proj = _dot(h, w_ref[...])
        p_ref[...] = proj
        held[0:HALO, :] = held[tm:tm + HALO, :]
        held[HALO:HALO + tm, :] = proj[:, :C]

    (proj, h, qkv), brought = hosted_call(
        body, name=name, grid=(n + 1,),
        in_specs=[pl.BlockSpec((tm, D), lambda i: (jnp.minimum(i, n - 1), 0)), pl.BlockSpec((1, D), lambda i: (0, 0)),
                  pl.BlockSpec((D, N), lambda i: (0, 0)), pl.BlockSpec((SUBLANES, C), lambda i: (0, 0))],
        out_specs=[pl.BlockSpec((tm, N), lambda i: (jnp.minimum(i, n - 1), 0)),
                   pl.BlockSpec((tm, D), lambda i: (jnp.minimum(i, n - 1), 0)),
                   pl.BlockSpec((tm, C), lambda i: (jnp.maximum(i - 1, 0), 0))],
        out_shape=[jax.ShapeDtypeStruct((T, N), F32), jax.ShapeDtypeStruct((T, D), BF16), jax.ShapeDtypeStruct((T, C), F32)],
        scratch_shapes=[pltpu.VMEM((HALO + tm, C), F32)],
        args=(x, gain, w, conv_w), exchange=exchange)
    return proj, h, qkv, brought


def gdn_conv_bwd(proj, conv_w, dqkv, *, name):
    T = proj.shape[0]
    tt, tc = _tile(T, 2048), HEAD_DIM
    nt = T // tt

    def body(p_ref, c_ref, n_ref, w_ref, dp_ref, dc_ref, dn_ref, dx_ref, dw_ref):
        kind = _qkv_kind(pl.program_id(0))
        w = w_ref[...]
        e = _extend(p_ref, c_ref, n_ref, nt)
        c = _causal_conv(e, w, GDN_CONV)
        s, s_grad = _silu_and_grad(c)
        dy = _extend(dp_ref, dc_ref, dn_ref, nt)
        r = lax.rsqrt(jnp.sum(s * s, axis=-1, keepdims=True) + EPS)
        y = s * r
        ds_norm = r * _l2norm_scale(kind) * (dy - y * jnp.sum(dy * y, axis=-1, keepdims=True))
        ds = jnp.where(kind == 2, dy, ds_norm)
        dx, dw = _causal_conv_bwd(e, ds * s_grad, w, GDN_CONV, tt)
        dx_ref[...] = _bf(dx)
        _accumulate(dw_ref, dw, 1)

    return pl.pallas_call(
        body, name=name, grid=(3 * HEADS, nt),
        in_specs=_halo_specs(T, tt, tc, lambda j: j) + [pl.BlockSpec((SUBLANES, tc), lambda j, i: (0, j))]
        + _halo_specs(T, tt, tc, lambda j: j),
        out_specs=[pl.BlockSpec((tt, tc), lambda j, i: (i, j)), pl.BlockSpec((SUBLANES, tc), lambda j, i: (0, j))],
        out_shape=[jax.ShapeDtypeStruct((T, 3 * D_MODEL), BF16), jax.ShapeDtypeStruct((SUBLANES, 3 * D_MODEL), F32)],
        compiler_params=_cp("parallel", "arbitrary"),
    )(proj, proj, proj, conv_w, dqkv, dqkv, dqkv)


FFN_COLS = 256


FFN_UP_ROWS = 256


def ffn_up_act(x, gain, w, conv_w, *, name, exchange=None):
    T, D = x.shape
    N = w.shape[1]
    F = N // 2
    tm, tc = _tile(T, FFN_UP_ROWS), FFN_COLS
    n = T // tm

    def body(x_ref, g_ref, w_ref, cw_ref, u_ref, h_ref, act_ref, held):
        i = pl.program_id(0)

        @pl.when(i == 0)
        def _():
            held[...] = jnp.zeros_like(held)

        for c in range(F // tc):
            gate_cols, up_cols = slice(c * tc, (c + 1) * tc), slice(F + c * tc, F + (c + 1) * tc)
            gate = _causal_conv(held[:, gate_cols], cw_ref[:, gate_cols], FFN_CONV)
            up_ = _causal_conv(held[:, up_cols], cw_ref[:, up_cols], FFN_CONV)
            act_ref[:, gate_cols] = _bf((_silu(gate) * up_)[HALO:])
        xf = x_ref[...]
        h = _bf(xf * lax.rsqrt(jnp.mean(xf * xf, axis=-1, keepdims=True) + EPS) * g_ref[...])
        h_ref[...] = h
        u = _dot(h, w_ref[...])
        u_ref[...] = u
        held[0:HALO, :] = held[tm:tm + HALO, :]
        held[HALO:HALO + tm, :] = u

    (u, h, act), brought = hosted_call(
        body, name=name, grid=(n + 1,),
        in_specs=[pl.BlockSpec((tm, D), lambda i: (jnp.minimum(i, n - 1), 0)), pl.BlockSpec((1, D), lambda i: (0, 0)),
                  pl.BlockSpec((D, N), lambda i: (0, 0)), pl.BlockSpec((SUBLANES, N), lambda i: (0, 0))],
        out_specs=[pl.BlockSpec((tm, N), lambda i: (jnp.minimum(i, n - 1), 0)),
                   pl.BlockSpec((tm, D), lambda i: (jnp.minimum(i, n - 1), 0)),
                   pl.BlockSpec((tm, F), lambda i: (jnp.maximum(i - 1, 0), 0))],
        out_shape=[jax.ShapeDtypeStruct((T, N), F32), jax.ShapeDtypeStruct((T, D), BF16), jax.ShapeDtypeStruct((T, F), BF16)],
        scratch_shapes=[pltpu.VMEM((HALO + tm, N), F32)],
        args=(x, gain, w, conv_w), exchange=exchange)
    return u, h, act, brought


def ffn_act_bwd(u, conv_w, dact, *, name):
    T = u.shape[0]
    tt, tc = _tile(T, 1024), FFN_COLS
    half = D_FF // tc
    nt = T // tt

    def body(gp, gc, gn, up, uc, un, wg_ref, wu_ref, dp, dc_, dn, dug_ref, duu_ref, dwg_ref, dwu_ref):
        wg, wu = wg_ref[...], wu_ref[...]
        eg, eu = _extend(gp, gc, gn, nt), _extend(up, uc, un, nt)
        gate, up_ = _causal_conv(eg, wg, FFN_CONV), _causal_conv(eu, wu, FFN_CONV)
        da = _extend(dp, dc_, dn, nt)
        act, act_grad = _silu_and_grad(gate)
        dxg, dwg = _causal_conv_bwd(eg, da * up_ * act_grad, wg, FFN_CONV, tt)
        dxu, dwu = _causal_conv_bwd(eu, da * act, wu, FFN_CONV, tt)
        dug_ref[...] = _bf(dxg)
        duu_ref[...] = _bf(dxu)
        _accumulate(dwg_ref, dwg, 1)
        _accumulate(dwu_ref, dwu, 1)

    return pl.pallas_call(
        body, name=name, grid=(half, nt),
        in_specs=_halo_specs(T, tt, tc, lambda j: j) + _halo_specs(T, tt, tc, lambda j: j + half)
        + [pl.BlockSpec((SUBLANES, tc), lambda j, i: (0, j)), pl.BlockSpec((SUBLANES, tc), lambda j, i: (0, j + half))]
        + _halo_specs(T, tt, tc, lambda j: j),
        out_specs=[pl.BlockSpec((tt, tc), lambda j, i: (i, j)), pl.BlockSpec((tt, tc), lambda j, i: (i, j)),
                   pl.BlockSpec((SUBLANES, tc), lambda j, i: (0, j)), pl.BlockSpec((SUBLANES, tc), lambda j, i: (0, j))],
        out_shape=[jax.ShapeDtypeStruct((T, D_FF), BF16), jax.ShapeDtypeStruct((T, D_FF), BF16),
                   jax.ShapeDtypeStruct((SUBLANES, D_FF), F32), jax.ShapeDtypeStruct((SUBLANES, D_FF), F32)],
        compiler_params=_cp("parallel", "arbitrary"),
    )(u, u, u, u, u, u, conv_w, conv_w, dact, dact, dact)


def head_norm_fwd(x, gain, z=None, *, x_col=0, z_col=0, name, out_dtype=F32):
    T = x.shape[0]
    tt = _tile(T, 512)
    gated = z is not None

    def body(*refs):
        x_ref, g_ref = refs[0], refs[1]
        o_ref = refs[-1]
        for h in range(HEADS):
            cols = slice(h * HEAD_DIM, (h + 1) * HEAD_DIM)
            xf = x_ref[:, cols]
            y = xf * lax.rsqrt(jnp.mean(xf * xf, axis=-1, keepdims=True) + EPS) * g_ref[...]
            if gated:
                y = y * _silu(refs[2][:, cols])
            o_ref[:, cols] = y.astype(out_dtype)

    wide = lambda col: pl.BlockSpec((tt, D_MODEL), lambda i: (i, col // HEADS))
    ins = [wide(x_col), pl.BlockSpec((1, HEAD_DIM), lambda i: (0, 0))]
    args = [x, gain]
    if gated:
        ins.append(wide(z_col))
        args.append(z)
    return pl.pallas_call(
        body, name=name, grid=(T // tt,), in_specs=ins, out_specs=wide(0),
        out_shape=jax.ShapeDtypeStruct((T, D_MODEL), out_dtype),
        compiler_params=_cp("parallel"),
    )(*args)


def head_norm_bwd(x, gain, dys, z=None, *, x_col=0, z_col=0, name, dx_dtype=F32):
    T = x.shape[0]
    tt = _tile(T, 512)
    gated = z is not None
    nd = len(dys)

    def body(*refs):
        x_ref, g_ref = refs[0], refs[1]
        outs = refs[2 + nd + (1 if gated else 0):]
        dx_ref, dg_ref = outs[0], outs[-1]
        folds = []
        for h in range(HEADS):
            cols = slice(h * HEAD_DIM, (h + 1) * HEAD_DIM)
            xf = x_ref[:, cols]
            r = lax.rsqrt(jnp.mean(xf * xf, axis=-1, keepdims=True) + EPS)
            xh = xf * r
            dy = refs[2][:, cols].astype(F32)
            for d_ref in refs[3:2 + nd]:
                dy = dy + d_ref[:, cols].astype(F32)
            if gated:
                gate, gate_grad = _silu_and_grad(refs[2 + nd][:, cols])
                outs[1][:, cols] = _bf(dy * xh * g_ref[...] * gate_grad)
                dn = dy * gate
            else:
                dn = dy
            dxh = dn * g_ref[...]
            dx_ref[:, cols] = (r * (dxh - xh * jnp.mean(dxh * xh, axis=-1, keepdims=True))).astype(dx_dtype)
            folds.append(_fold_rows(dn * xh))
        _accumulate(dg_ref, jnp.concatenate(folds, axis=1), 0)

    wide = lambda col: pl.BlockSpec((tt, D_MODEL), lambda i: (i, col // HEADS))
    ins = [wide(x_col), pl.BlockSpec((1, HEAD_DIM), lambda i: (0, 0))] + [wide(0)] * nd
    args = [x, gain] + list(dys)
    outs = [wide(0)]
    shapes = [jax.ShapeDtypeStruct((T, D_MODEL), dx_dtype)]
    if gated:
        ins.append(wide(z_col))
        args.append(z)
        outs.append(wide(0))
        shapes.append(jax.ShapeDtypeStruct((T, D_MODEL), BF16))
    outs.append(pl.BlockSpec((SUBLANES, D_MODEL), lambda i: (0, 0)))
    shapes.append(jax.ShapeDtypeStruct((SUBLANES, D_MODEL), F32))
    return pl.pallas_call(
        body, name=name, grid=(T // tt,), in_specs=ins, out_specs=outs, out_shape=shapes,
        compiler_params=_cp("arbitrary"),
    )(*args)


def gates_fwd(proj, a_log, dt_bias, *, name):
    T = proj.shape[0]
    tt = _tile(T, 1024)

    def body(p_ref, al_ref, dt_ref, o_ref):
        p = p_ref[...]
        lane = lax.broadcasted_iota(jnp.int32, p.shape, 1)
        o_ref[...] = jnp.where(lane < HEADS, -jnp.exp(al_ref[...]) * _softplus(p + dt_ref[...]), _sigmoid(p))

    return pl.pallas_call(
        body, name=name, grid=(T // tt,),
        in_specs=[pl.BlockSpec((tt, LANES), lambda i: (i, AB_BLOCK)), pl.BlockSpec((1, LANES), lambda i: (0, 0)),
                  pl.BlockSpec((1, LANES), lambda i: (0, 0))],
        out_specs=pl.BlockSpec((tt, LANES), lambda i: (i, 0)),
        out_shape=jax.ShapeDtypeStruct((T, LANES), F32),
        compiler_params=_cp("parallel"),
    )(proj, a_log, dt_bias)


def gates_bwd(proj, a_log, dt_bias, dgate, *, name):
    T = proj.shape[0]
    tt = _tile(T, 1024)

    def body(p_ref, al_ref, dt_ref, d_ref, dp_ref, dal_ref, ddt_ref):
        p, d = p_ref[...], d_ref[...]
        lane = lax.broadcasted_iota(jnp.int32, p.shape, 1)
        ea = jnp.exp(al_ref[...])
        pa = p + dt_ref[...]
        da = -d * ea * _sigmoid(pa)
        b = _sigmoid(p)
        dp_ref[...] = _bf(jnp.where(lane < HEADS, da, jnp.where(lane < 2 * HEADS, d * b * (1.0 - b), 0.0)))
        _accumulate(dal_ref, _fold_rows(jnp.where(lane < HEADS, -d * ea * _softplus(pa), 0.0)), 0)
        _accumulate(ddt_ref, _fold_rows(jnp.where(lane < HEADS, da, 0.0)), 0)

    acc = pl.BlockSpec((SUBLANES, LANES), lambda i: (0, 0))
    return pl.pallas_call(
        body, name=name, grid=(T // tt,),
        in_specs=[pl.BlockSpec((tt, LANES), lambda i: (i, AB_BLOCK)), pl.BlockSpec((1, LANES), lambda i: (0, 0)),
                  pl.BlockSpec((1, LANES), lambda i: (0, 0)), pl.BlockSpec((tt, LANES), lambda i: (i, 0))],
        out_specs=[pl.BlockSpec((tt, LANES), lambda i: (i, 0)), acc, acc],
        out_shape=[jax.ShapeDtypeStruct((T, LANES), BF16), jax.ShapeDtypeStruct((SUBLANES, LANES), F32),
                   jax.ShapeDtypeStruct((SUBLANES, LANES), F32)],
        compiler_params=_cp("arbitrary"),
    )(proj, a_log, dt_bias, dgate)


def loss_fwd(y, target, *, name):
    T, D = y.shape
    tt = _tile(T, 512)

    def body(y_ref, t_ref, dy_ref, l_ref):
        d = y_ref[...] - t_ref[...]
        dy_ref[...] = d * (1.0 / D)
        sq = d * d
        lanes = sq[:, 0:LANES]
        for c in range(1, D // LANES):
            lanes = lanes + sq[:, c * LANES:(c + 1) * LANES]
        _accumulate(l_ref, _fold_rows(lanes) * (0.5 / D), 0)

    return pl.pallas_call(
        body, name=name, grid=(T // tt,),
        in_specs=[pl.BlockSpec((tt, D), lambda i: (i, 0)), pl.BlockSpec((tt, D), lambda i: (i, 0))],
        out_specs=[pl.BlockSpec((tt, D), lambda i: (i, 0)), pl.BlockSpec((SUBLANES, LANES), lambda i: (0, 0))],
        out_shape=[jax.ShapeDtypeStruct((T, D), F32), jax.ShapeDtypeStruct((SUBLANES, LANES), F32)],
        compiler_params=_cp("arbitrary"),
    )(y, target)


def _split_bf16(x):
    hi = _bf(x)
    return hi, _bf(x - hi.astype(F32))


def _dot3(a, b, dot=_dot):
    return dot(a[0], b[0]) + dot(a[0], b[1]) + dot(a[1], b[0])


def _each(fn, *lists):
    return [fn(*args) for args in zip(*lists)]


def _unit_lower_inverses(lows):
    c = lows[0].shape[0]
    ii = lax.broadcasted_iota(jnp.int32, (c, c), 0)
    jj = lax.broadcasted_iota(jnp.int32, (c, c), 1)
    eye = jnp.where(ii == jj, 1.0, 0.0)
    invs = _each(lambda low: eye - low, lows)
    powers = _each(_split_bf16, lows)
    for _ in range(int(math.log2(c)) - 1):
        powers = _each(lambda p: _split_bf16(_dot3(p, p)), powers)
        invs = _each(lambda inv, p: inv + _dot3(_split_bf16(inv), p), invs, powers)
    return invs


def _gdn_chunks(heads):
    q, k, v, a_col, a_row, b_col, s0 = (list(t) for t in zip(*heads))
    c = q[0].shape[0]
    ii = lax.broadcasted_iota(jnp.int32, (c, c), 0)
    jj = lax.broadcasted_iota(jnp.int32, (c, c), 1)
    tri, strict = ii >= jj, ii > jj
    g_col = _each(lambda ar: jnp.sum(jnp.where(tri, ar, 0.0), axis=1, keepdims=True), a_row)
    g_row = _each(lambda ac: jnp.sum(jnp.where(ii <= jj, ac, 0.0), axis=0, keepdims=True), a_col)
    gam = _each(lambda gc, gr: jnp.exp(jnp.where(tri, gc - gr, -jnp.inf)), g_col, g_row)
    g_last = _each(lambda ac: jnp.sum(ac, axis=0, keepdims=True), a_col)
    gam_col = _each(jnp.exp, g_col)
    del_col = _each(lambda gl, gc: jnp.exp(gl - gc), g_last, g_col)
    kb = _each(lambda k_, b: k_ * b, k, b_col)
    m = _each(lambda kb_, k_: _dot_nt(_bf(kb_), _bf(k_)), kb, k)
    ks = _each(lambda k_, s: _dot(_bf(k_), _bf(s)), k, s0)
    qk = _each(lambda q_, k_: _dot_nt(_bf(q_), _bf(k_)), q, k)
    inv = _unit_lower_inverses(_each(lambda m_, g: jnp.where(strict, m_ * g, 0.0), m, gam))
    e = _each(lambda v_, gc, ks_: v_ - gc * ks_, v, gam_col, ks)
    inv = _each(_split_bf16, inv)
    vn = _each(lambda inv_, b, e_: _dot3(inv_, _split_bf16(b * e_)), inv, b_col, e)
    p = _each(lambda qk_, g: jnp.where(tri, qk_ * g, 0.0), qk, gam)
    return [dict(tri=tri, strict=strict, ii=ii, jj=jj, gam=gam[g], g_last=g_last[g], gam_col=gam_col[g], del_col=del_col[g],
                 kb=kb[g], m=m[g], inv=inv[g], ks=ks[g], e=e[g], vn=vn[g], qk=qk[g], p=p[g]) for g in range(len(heads))]


GDN_HEADS_PER_STEP = HEADS


def _head_cols(ref, g):
    return ref[:, g * HEAD_DIM:(g + 1) * HEAD_DIM]


def _load_heads(q_ref, k_ref, v_ref, ar_ref, br_ref, states):
    return [(_head_cols(q_ref, g), _head_cols(k_ref, g), _head_cols(v_ref, g), ar_ref[g, 0].T, ar_ref[g, 0], br_ref[g, 0].T, states(g))
            for g in range(GDN_HEADS_PER_STEP)]


def gdn_fwd(qkv, a_row, b_row, *, name, exchange=None):
    T = qkv.shape[0]
    C = GDN_CHUNK
    N = T // C
    G = GDN_HEADS_PER_STEP

    def body(q_ref, k_ref, v_ref, ar_ref, br_ref, o_ref, s_ref, state):
        @pl.when(pl.program_id(1) == 0)
        def _():
            state[...] = jnp.zeros_like(state)
        loaded = _load_heads(q_ref, k_ref, v_ref, ar_ref, br_ref, lambda g: state[g])
        ws = _gdn_chunks(loaded)
        qs = _each(lambda h: _dot(_bf(h[0]), _bf(h[6])), loaded)
        pv = _each(lambda w: _dot(_bf(w["p"]), _bf(w["vn"])), ws)
        kv = _each(lambda h, w: _dot_tn(_bf(w["del_col"] * h[1]), _bf(w["vn"])), loaded, ws)
        for g, w in enumerate(ws):
            s0 = loaded[g][6]
            s_ref[g, 0] = s0
            o_ref[:, g * HEAD_DIM:(g + 1) * HEAD_DIM] = w["gam_col"] * qs[g] + pv[g]
            state[g] = jnp.exp(w["g_last"]) * s0 + kv[g]

    per = HEADS // G
    blk = lambda off: pl.BlockSpec((C, G * HEAD_DIM), lambda h, n: (n, off * per + h))
    row = pl.BlockSpec((G, 1, 1, C), lambda h, n: (h, n, 0, 0))
    (o, states), brought = hosted_call(
        body, name=name, grid=(per, N),
        in_specs=[blk(0), blk(1), blk(2), row, row],
        out_specs=[blk(0), pl.BlockSpec((G, 1, HEAD_DIM, HEAD_DIM), lambda h, n: (h, n, 0, 0))],
        out_shape=[jax.ShapeDtypeStruct((T, D_MODEL), F32), jax.ShapeDtypeStruct((HEADS, N, HEAD_DIM, HEAD_DIM), F32)],
        scratch_shapes=[pltpu.VMEM((G, HEAD_DIM, HEAD_DIM), F32)],
        args=(qkv, qkv, qkv, a_row, b_row), exchange=exchange)
    return o, states, brought


def gdn_bwd(qkv, a_row, b_row, states, do, *, name, exchange=None):
    T = qkv.shape[0]
    C = GDN_CHUNK
    N = T // C
    G = GDN_HEADS_PER_STEP
    per = HEADS // G
    rev = lambda n: N - 1 - n
    row = pl.BlockSpec((G, 1, 1, C), lambda h, n: (h, rev(n), 0, 0))

    def body(q_ref, k_ref, v_ref, ar_ref, br_ref, s_ref, do_ref, dqkv_ref, da_ref, db_ref, dstate):
        @pl.when(pl.program_id(1) == 0)
        def _():
            dstate[...] = jnp.zeros_like(dstate)
        loaded = _load_heads(q_ref, k_ref, v_ref, ar_ref, br_ref, lambda g: s_ref[g, 0])
        hs = _gdn_chunks(loaded)
        for g, d in enumerate(hs):
            q, k, _, _, _, b, s0 = loaded[g]
            d.update(q=q, k=k, b=b, s0=s0, ds1=dstate[g], dout=_head_cols(do_ref, g))
        rows = lambda t: jnp.sum(t, axis=1, keepdims=True)
        ii_col = lax.broadcasted_iota(jnp.int32, (C, 1), 0)

        def stage(**fns):
            for key, fn in fns.items():
                for d in hs:
                    d[key] = fn(d)

        stage(s0b=lambda d: _bf(d["s0"]), ds1b=lambda d: _bf(d["ds1"]), doutb=lambda d: _bf(d["dout"]),
              kbf=lambda d: _bf(d["k"]), qbf=lambda d: _bf(d["q"]), vnb=lambda d: _bf(d["vn"]))
        stage(dvn=lambda d: _dot_tn(_bf(d["p"]), d["doutb"]) + _dot(_bf(d["del_col"] * d["k"]), d["ds1b"]),
              dqk=lambda d: jnp.where(d["tri"], _dot_nt(d["doutb"], d["vnb"]), 0.0) * d["gam"],
              qs=lambda d: _dot(d["qbf"], d["s0b"]),
              dkd=lambda d: _dot_nt(d["vnb"], d["ds1b"]))
        stage(dr=lambda d: _dot3(d["inv"], _split_bf16(d["dvn"]), _dot_tn),
              dq=lambda d: d["gam_col"] * _dot_nt(d["doutb"], d["s0b"]) + _dot(_bf(d["dqk"]), d["kbf"]),
              dk=lambda d: _dot_tn(_bf(d["dqk"]), d["qbf"]) + d["del_col"] * d["dkd"],
              ddel=lambda d: d["del_col"] * rows(d["dkd"] * d["k"]))
        stage(dg=lambda d: d["gam_col"] * rows(d["dout"] * d["qs"]) - d["ddel"],
              dg_last=lambda d: jnp.sum(d["ddel"], axis=0, keepdims=True)
              + jnp.exp(d["g_last"]) * jnp.sum(rows(d["ds1"] * d["s0"]), axis=0, keepdims=True),
              dm=lambda d: jnp.where(d["strict"], -_dot_nt(_bf(d["dr"]), d["vnb"]), 0.0) * d["gam"],
              de=lambda d: d["b"] * d["dr"])
        stage(dkb=lambda d: _dot(_bf(d["dm"]), d["kbf"]),
              dks=lambda d: -d["gam_col"] * d["de"])
        stage(dk=lambda d: d["dk"] + _dot_tn(_bf(d["dm"]), _bf(d["kb"])) + _dot_nt(_bf(d["dks"]), d["s0b"]) + d["b"] * d["dkb"],
              dbeta=lambda d: rows(d["dr"] * d["e"]) + rows(d["dkb"] * d["k"]),
              ds0=lambda d: jnp.exp(d["g_last"]) * d["ds1"] + _dot_tn(_bf(d["gam_col"] * d["q"]), d["doutb"])
              + _dot_tn(d["kbf"], _bf(d["dks"])),
              wg=lambda d: d["dqk"] * d["qk"] + d["dm"] * d["m"])
        stage(dg=lambda d: d["dg"] - d["gam_col"] * rows(d["de"] * d["ks"]) + rows(d["wg"])
              - jnp.sum(d["wg"], axis=0, keepdims=True).T + jnp.where(ii_col == C - 1, d["dg_last"], 0.0))
        stage(da=lambda d: jnp.sum(jnp.where(d["ii"] >= d["jj"], d["dg"], 0.0), axis=0, keepdims=True),
              db=lambda d: d["dbeta"].T)
        for g, d in enumerate(hs):
            dstate[g] = d["ds0"]
            da_ref[g, 0] = d["da"]
            db_ref[g, 0] = d["db"]
            for part, key in enumerate(("dq", "dk", "de")):
                start = part * D_MODEL + g * HEAD_DIM
                dqkv_ref[:, start:start + HEAD_DIM] = d[key]

    assert per == 1
    blk = lambda off: pl.BlockSpec((C, G * HEAD_DIM), lambda h, n: (rev(n), off * per + h))
    (dqkv, da, db), brought = hosted_call(
        body, name=name, grid=(per, N),
        in_specs=[blk(0), blk(1), blk(2), row, row,
                  pl.BlockSpec((G, 1, HEAD_DIM, HEAD_DIM), lambda h, n: (h, rev(n), 0, 0)), blk(0)],
        out_specs=[pl.BlockSpec((C, 3 * D_MODEL), lambda h, n: (rev(n), 0)), row, row],
        out_shape=[jax.ShapeDtypeStruct((T, 3 * D_MODEL), F32)] + [jax.ShapeDtypeStruct((HEADS, N, 1, C), F32)] * 2,
        scratch_shapes=[pltpu.VMEM((G, HEAD_DIM, HEAD_DIM), F32)],
        args=(qkv, qkv, qkv, a_row, b_row, states, do), exchange=exchange)
    return dqkv, da, db, brought


SB_BLOCK = 256


SB_QBLOCKS = 4


def _sb_rows(j, blk):
    return pl.ds(pl.multiple_of(j * blk, blk), blk)


def _sb_tile(qb, k_ref, i, j, blk, live):
    z = _dot_nt(qb, _bf(k_ref[_sb_rows(j, blk), :]))
    t_idx = i * blk + lax.broadcasted_iota(jnp.int32, (blk, blk), 0)
    s_idx = j * blk + lax.broadcasted_iota(jnp.int32, (blk, blk), 1)
    mask = jnp.logical_and(s_idx < t_idx, live)
    lf = jnp.where(mask, -_softplus(z), 0.0)
    return z, mask, lf


SB_DEAD = 105.0


def sb_fwd(q, k, v, *, k_col=0, v_col=0, name):
    T = q.shape[0]
    blk = _tile(T, SB_BLOCK)
    P = min(SB_QBLOCKS, T // blk)
    scale = HEAD_DIM ** -0.5

    def body(q_ref, k_ref, v_ref, o_ref, l_ref, n_ref):
        iq = [P * pl.program_id(1) + p for p in range(P)]
        qb = [_bf(q_ref[p * blk:(p + 1) * blk, :] * scale) for p in range(P)]
        r_idx = lax.broadcasted_iota(jnp.int32, (blk, blk), 0)
        c_idx = lax.broadcasted_iota(jnp.int32, (blk, blk), 1)
        later = _bf(jnp.where(r_idx > c_idx, 1.0, 0.0))

        def live_blocks(jj, runs):
            return [jnp.logical_and(jj <= i, jnp.max(run) > -SB_DEAD) for i, run in zip(iq, runs)]

        def alive(carry):
            jj, _, runs, _ = carry
            some = False
            for f in live_blocks(jj, runs):
                some = jnp.logical_or(some, f)
            return some

        def step(carry):
            jj, accs, runs, visited = carry
            live = live_blocks(jj, runs)
            js = [jnp.maximum(i - jj, 0) for i in iq]
            tiles = _each(lambda q_, i, j, f: _sb_tile(q_, k_ref, i, j, blk, f), qb, iq, js, live)
            parts = _each(lambda t: _split_bf16(t[2]), tiles)
            after = _each(lambda run, s: run + _dot(s[0], later) + _dot(s[1], later), runs, parts)
            a = _each(lambda t, af: jnp.where(t[1], jnp.exp(t[0] + t[2] + af), 0.0), tiles, after)
            vb = _each(lambda j: _bf(v_ref[_sb_rows(j, blk), :]), js)
            accs = _each(lambda acc, a_, v_: acc + _dot(_bf(a_), v_), accs, a, vb)
            runs = _each(lambda run, t: run + jnp.sum(t[2], axis=1, keepdims=True), runs, tiles)
            visited = _each(lambda n, f: n + f.astype(jnp.int32), visited, live)
            return jj + 1, accs, runs, visited

        start = (jnp.int32(0), [jnp.zeros((blk, HEAD_DIM), F32)] * P, [jnp.zeros((blk, 1), F32)] * P, [jnp.int32(0)] * P)
        _, accs, runs, visited = lax.while_loop(alive, step, start)
        for p in range(P):
            o_ref[p * blk:(p + 1) * blk, :] = accs[p]
            l_ref[0, p] = runs[p].T
            n_ref[0, p] = jnp.full((SUBLANES, LANES), visited[p].astype(F32))

    return pl.pallas_call(
        body, name=name, grid=(HEADS, T // (P * blk)),
        in_specs=[pl.BlockSpec((P * blk, HEAD_DIM), lambda h, i: (i, h)), pl.BlockSpec((T, HEAD_DIM), lambda h, i: (0, k_col + h)),
                  pl.BlockSpec((T, HEAD_DIM), lambda h, i: (0, v_col + h))],
        out_specs=[pl.BlockSpec((P * blk, HEAD_DIM), lambda h, i: (i, h)), pl.BlockSpec((1, P, 1, blk), lambda h, i: (h, i, 0, 0)),
                   pl.BlockSpec((1, P, SUBLANES, LANES), lambda h, i: (h, i, 0, 0))],
        out_shape=[jax.ShapeDtypeStruct((T, D_MODEL), F32), jax.ShapeDtypeStruct((HEADS, T // blk, 1, blk), F32),
                   jax.ShapeDtypeStruct((HEADS, T // blk, SUBLANES, LANES), F32)],
        compiler_params=_cp("parallel", "arbitrary"),
    )(q, k, v)


def sb_bwd(q, k, v, ltot, visited, do, *, k_col=0, v_col=0, name, exchange=None):
    T = q.shape[0]
    blk = _tile(T, SB_BLOCK)
    P = min(SB_QBLOCKS, T // blk)
    scale = HEAD_DIM ** -0.5

    def body(q_ref, k_ref, v_ref, l_ref, n_ref, do_ref, dq_ref, dk_ref, dv_ref):
        iq = [P * pl.program_id(1) + p for p in range(P)]
        count = [jnp.max(n_ref[0, p]).astype(jnp.int32) for p in range(P)]
        first = [i + 1 - n for i, n in zip(iq, count)]
        trips = count[0]
        for n in count[1:]:
            trips = jnp.maximum(trips, n)

        @pl.when(pl.program_id(1) == 0)
        def _():
            dk_ref[...] = jnp.zeros_like(dk_ref)
            dv_ref[...] = jnp.zeros_like(dv_ref)

        qb = [_bf(q_ref[p * blk:(p + 1) * blk, :] * scale) for p in range(P)]
        dob = [_bf(do_ref[p * blk:(p + 1) * blk, :]) for p in range(P)]
        ltot_ = [l_ref[0, p].T for p in range(P)]
        r_idx = lax.broadcasted_iota(jnp.int32, (blk, blk), 0)
        c_idx = lax.broadcasted_iota(jnp.int32, (blk, blk), 1)
        upto = _bf(jnp.where(r_idx <= c_idx, 1.0, 0.0))
        before = _bf(jnp.where(r_idx < c_idx, 1.0, 0.0))

        def step(t, carry):
            dqs, lpre, cpre = carry
            live = [f + t <= i for f, i in zip(first, iq)]
            js = [jnp.minimum(f + t, i) for f, i in zip(first, iq)]
            tiles = _each(lambda q_, i, j, f: _sb_tile(q_, k_ref, i, j, blk, f), qb, iq, js, live)
            parts = _each(lambda tl: _split_bf16(tl[2]), tiles)
            after = _each(lambda lt, lp, s: lt - (lp + _dot(s[0], upto) + _dot(s[1], upto)), ltot_, lpre, parts)
            ls = _each(lambda tl: tl[0] + tl[2], tiles)
            a = _each(lambda tl, ls_, af: jnp.where(tl[1], jnp.exp(ls_ + af), 0.0), tiles, ls, after)
            vb = _each(lambda j: _bf(v_ref[_sb_rows(j, blk), :]), js)
            p = _each(lambda a_, do_, v_: a_ * _dot_nt(do_, v_), a, dob, vb)
            pparts = _each(_split_bf16, p)
            left = _each(lambda cp, s: cp + _dot(s[0], before) + _dot(s[1], before), cpre, pparts)
            dzb = _each(lambda tl, p_, lf_, ls_: _bf(jnp.where(tl[1], p_ * jnp.exp(tl[2]) - lf_ * jnp.exp(ls_), 0.0)),
                        tiles, p, left, ls)
            dks = _each(lambda dz, q_: _dot_tn(dz, q_), dzb, qb)
            dvs = _each(lambda a_, do_: _dot_tn(_bf(a_), do_), a, dob)
            dqs = _each(lambda dq, dz, j: dq + _dot(dz, _bf(k_ref[_sb_rows(j, blk), :])), dqs, dzb, js)
            for j, dk, dv in zip(js, dks, dvs):
                dk_ref[_sb_rows(j, blk), :] += dk
                dv_ref[_sb_rows(j, blk), :] += dv
            lpre = _each(lambda lp, tl: lp + jnp.sum(tl[2], axis=1, keepdims=True), lpre, tiles)
            cpre = _each(lambda cp, p_: cp + jnp.sum(p_, axis=1, keepdims=True), cpre, p)
            return dqs, lpre, cpre

        zero = [jnp.zeros((blk, 1), F32)] * P
        dqs, _, _ = lax.fori_loop(0, trips, step, ([jnp.zeros((blk, HEAD_DIM), F32)] * P, zero, zero))
        for p in range(P):
            dq_ref[p * blk:(p + 1) * blk, :] = dqs[p] * scale

    full = lambda off: pl.BlockSpec((T, HEAD_DIM), lambda h, i: (0, off + h))
    tile = pl.BlockSpec((P * blk, HEAD_DIM), lambda h, i: (i, h))
    (dq, dk, dv), brought = hosted_call(
        body, name=name, grid=(HEADS, T // (P * blk)),
        in_specs=[tile, full(k_col), full(v_col), pl.BlockSpec((1, P, 1, blk), lambda h, i: (h, i, 0, 0)),
                  pl.BlockSpec((1, P, SUBLANES, LANES), lambda h, i: (h, i, 0, 0)), tile],
        out_specs=[tile, full(0), full(0)],
        out_shape=[jax.ShapeDtypeStruct((T, D_MODEL), F32)] * 3,
        scratch_shapes=[], args=(q, k, v, ltot, visited, do), exchange=exchange)
    return dq, dk, dv, brought


def sum_slots(slots, *, name):
    n, R, C = slots.shape
    tr = _tile(R, 256)

    def body(s_ref, o_ref):
        acc = s_ref[0].astype(F32)
        for k in range(1, n):
            acc = acc + s_ref[k].astype(F32)
        o_ref[...] = acc

    return pl.pallas_call(
        body, name=name, grid=(R // tr,),
        in_specs=[pl.BlockSpec((n, tr, C), lambda i: (0, i, 0))],
        out_specs=pl.BlockSpec((tr, C), lambda i: (i, 0)),
        out_shape=jax.ShapeDtypeStruct((R, C), F32),
        compiler_params=_cp("parallel"),
    )(slots)


def adamw(w, g_parts, m, v, *, name):
    R, C = w.shape
    tr = _tile(R, 256)
    n = len(g_parts)

    def body(*refs):
        w_ref, m_ref, v_ref = refs[0], refs[1 + n], refs[2 + n]
        g_ref, d_ref, nm_ref, nv_ref = refs[3 + n:]
        g = refs[1][...]
        for r in refs[2:1 + n]:
            g = g + r[...]
        m2 = ADAM_B1 * m_ref[...] + (1.0 - ADAM_B1) * g
        v2 = ADAM_B2 * v_ref[...] + (1.0 - ADAM_B2) * (g * g)
        m_hat = m2 / (1.0 - ADAM_B1 ** ADAM_STEP)
        v_hat = v2 / (1.0 - ADAM_B2 ** ADAM_STEP)
        g_ref[...] = g
        d_ref[...] = -ADAM_LR * (m_hat / (jnp.sqrt(v_hat) + ADAM_EPS) + ADAM_WD * w_ref[...])
        nm_ref[...] = m2
        nv_ref[...] = v2

    spec = pl.BlockSpec((tr, C), lambda i: (i, 0))
    return pl.pallas_call(
        body, name=name, grid=(R // tr,),
        in_specs=[spec] * (3 + n), out_specs=[spec] * 4,
        out_shape=[jax.ShapeDtypeStruct((R, C), F32)] * 4,
        compiler_params=_cp("parallel"),
    )(w, *g_parts, m, v)


CHIP_FLIPS = ((0, 1), (1, 0), (1, 1))


def _place():
    return lax.axis_index("x"), lax.axis_index("y"), lax.axis_index("c")


def _flip(v, f):
    return 1 - v if f else v


class ChipExchange:
    def __init__(self, arrays, scatter):
        self.arrays, self.scatter, self.n = list(arrays), scatter, len(arrays)
        lead = () if scatter else (N_CHIPS,)
        self.out_shape = [jax.ShapeDtypeStruct(lead + a.shape, a.dtype) for a in arrays]
        self.scratch = [pltpu.SemaphoreType.DMA((self.n, len(CHIP_FLIPS))), pltpu.SemaphoreType.DMA((self.n, len(CHIP_FLIPS))),
                        pltpu.SemaphoreType.DMA((self.n,))]

    def _copies(self, ins, outs, sems):
        send_sems, recv_sems, local_sems = sems
        x, y, c = _place()
        me = 2 * x + y
        local, sent, landing = [], [], []
        for k in range(self.n):
            local.append(pltpu.make_async_copy(ins[k].at[me] if self.scatter else ins[k], outs[k].at[me], local_sems.at[k]))
            for p, (fx, fy) in enumerate(CHIP_FLIPS):
                px, py = _flip(x, fx), _flip(y, fy)
                src = ins[k].at[2 * px + py] if self.scatter else ins[k]
                for dst, group in ((me, sent), (2 * px + py, landing)):
                    group.append(pltpu.make_async_remote_copy(src_ref=src, dst_ref=outs[k].at[dst], send_sem=send_sems.at[k, p],
                                                              recv_sem=recv_sems.at[k, p], device_id=(px, py, c), device_id_type=MESH))
        return local, sent, landing

    def start(self, ins, outs, sems):
        local, sent, _ = self._copies(ins, outs, sems)
        for cp in local + sent:
            cp.start()

    def finish(self, ins, outs, sems):
        local, _, landing = self._copies(ins, outs, sems)
        for cp in landing:
            cp.wait_send()
            cp.wait_recv()
        for cp in local:
            cp.wait()


def chip_exchange(arrays, *, scatter, name):
    ex = ChipExchange(arrays, scatter)
    n = ex.n

    def body(*refs):
        ins, outs, sems = refs[:n], refs[n:2 * n], refs[2 * n:]
        ex.start(ins, outs, sems)
        ex.finish(ins, outs, sems)

    return pl.pallas_call(body, name=name, in_specs=[ANY] * n, out_specs=[ANY] * n, out_shape=ex.out_shape,
                          scratch_shapes=ex.scratch)(*arrays)


def hosted_call(body, *, name, grid, in_specs, out_specs, out_shape, scratch_shapes, args, exchange=None):
    in_specs, out_specs, out_shape = list(in_specs), list(out_specs), list(out_shape)
    params = _cp(*["arbitrary"] * len(grid))
    if exchange is None:
        outs = pl.pallas_call(body, name=name, grid=grid, in_specs=in_specs, out_specs=out_specs, out_shape=out_shape,
                              scratch_shapes=list(scratch_shapes), compiler_params=params)(*args)
        return list(outs), []
    n, n_in, n_out, n_scr = exchange.n, len(in_specs), len(out_specs), len(scratch_shapes)

    def both(*refs):
        ins, t_ins = refs[:n_in], refs[n_in:n_in + n]
        outs, t_outs = refs[n_in + n:n_in + n + n_out], refs[n_in + n + n_out:n_in + 2 * n + n_out]
        scratch, sems = refs[n_in + 2 * n + n_out:n_in + 2 * n + n_out + n_scr], refs[n_in + 2 * n + n_out + n_scr:]
        first, last = True, True
        for axis, size in enumerate(grid):
            first = jnp.logical_and(first, pl.program_id(axis) == 0)
            last = jnp.logical_and(last, pl.program_id(axis) == size - 1)

        @pl.when(first)
        def _():
            exchange.start(t_ins, t_outs, sems)

        body(*ins, *outs, *scratch)

        @pl.when(last)
        def _():
            exchange.finish(t_ins, t_outs, sems)

    outs = pl.pallas_call(both, name=name, grid=grid, in_specs=in_specs + [ANY] * n, out_specs=out_specs + [ANY] * n,
                          out_shape=out_shape + exchange.out_shape, scratch_shapes=list(scratch_shapes) + exchange.scratch,
                          compiler_params=params)(*args, *exchange.arrays)
    return list(outs[:n_out]), list(outs[n_out:])


def sibling_swap(arrays, *, name):
    n = len(arrays)

    def body(*refs):
        ins, outs = refs[:n], refs[n:2 * n]
        send_sems, recv_sems = refs[2 * n:]
        x, y, c = _place()
        copies = [pltpu.make_async_remote_copy(src_ref=ins[k], dst_ref=outs[k], send_sem=send_sems.at[k], recv_sem=recv_sems.at[k],
                                               device_id=(x, y, 1 - c), device_id_type=MESH) for k in range(n)]
        for cp in copies:
            cp.start()
        for cp in copies:
            cp.wait_send()
            cp.wait_recv()

    return pl.pallas_call(
        body, name=name, in_specs=[ANY] * n, out_specs=[ANY] * n,
        out_shape=[jax.ShapeDtypeStruct(a.shape, a.dtype) for a in arrays],
        scratch_shapes=[pltpu.SemaphoreType.DMA((n,)), pltpu.SemaphoreType.DMA((n,))],
    )(*arrays)


DEVICE_FLIPS = tuple((fx, fy, fc) for fx in (0, 1) for fy in (0, 1) for fc in (0, 1) if fx or fy or fc)


def all_gather_devices(a, *, name):
    def body(a_ref, o_ref, send_sems, recv_sems, local_sem):
        x, y, c = _place()
        me = 4 * x + 2 * y + c
        local = pltpu.make_async_copy(a_ref, o_ref.at[me], local_sem)
        local.start()
        for p, (fx, fy, fc) in enumerate(DEVICE_FLIPS):
            peer = (_flip(x, fx), _flip(y, fy), _flip(c, fc))
            pltpu.make_async_remote_copy(src_ref=a_ref, dst_ref=o_ref.at[me], send_sem=send_sems.at[p], recv_sem=recv_sems.at[p],
                                         device_id=peer, device_id_type=MESH).start()
        for p, (fx, fy, fc) in enumerate(DEVICE_FLIPS):
            px, py, pc = _flip(x, fx), _flip(y, fy), _flip(c, fc)
            landing = pltpu.make_async_remote_copy(src_ref=a_ref, dst_ref=o_ref.at[4 * px + 2 * py + pc], send_sem=send_sems.at[p],
                                                   recv_sem=recv_sems.at[p], device_id=(px, py, pc), device_id_type=MESH)
            landing.wait_send()
            landing.wait_recv()
        local.wait()

    return pl.pallas_call(
        body, name=name, in_specs=[ANY], out_specs=ANY,
        out_shape=jax.ShapeDtypeStruct((N_DEV,) + a.shape, a.dtype),
        scratch_shapes=[pltpu.SemaphoreType.DMA((len(DEVICE_FLIPS),)), pltpu.SemaphoreType.DMA((len(DEVICE_FLIPS),)),
                        pltpu.SemaphoreType.DMA(())],
    )(a)


def _row(v):
    return v.reshape(1, -1)


def _pad_rows(w):
    return jnp.pad(w, ((0, SUBLANES - w.shape[0]), (0, 0)))


def _pad_lanes(v):
    return jnp.pad(v.reshape(1, -1), ((0, 0), (0, LANES - v.shape[-1])))


def _head_layouts(gates):
    T = gates.shape[0]
    rows = lambda cols: cols.T.reshape(HEADS, T // GDN_CHUNK, 1, GDN_CHUNK)
    return rows(gates[:, :HEADS]), rows(gates[:, HEADS:2 * HEADS])


def _ffn_fwd(x, W, l, plan):
    conv = _pad_rows(W["ffn_conv"][l])
    u, h, act, brought = ffn_up_act(x, _row(W["ffn_norm"][l]), W["ffn_w_up"][l], conv, name=f"ffn{l}_up_act",
                                    exchange=plan.fetch(f"ffn{l}_act"))
    plan.arrived(f"ffn{l}_act", brought, W)
    y = matmul_residual(act, W["ffn_w_down"][l], x, name=f"ffn{l}_down")
    return y, (x, h, u, conv, act)


def _ffn_bwd(dx, saved, W, l, G):
    x, h, u, conv, act = saved
    dact = matmul_nt(dx, W["ffn_w_down"][l], name=f"ffn{l}_down_dx", out_dtype=BF16)
    G["ffn_w_down"][l] = matmul_tn(act, dx, name=f"ffn{l}_down_dw")
    dug, duu, dwg, dwu = ffn_act_bwd(u, conv, dact, name=f"ffn{l}_act_bwd")
    G["ffn_conv"][l] = jnp.concatenate([dwg, dwu], axis=1)[:FFN_CONV]
    G["ffn_w_up"][l] = jnp.concatenate([matmul_tn(h, dug, name=f"ffn{l}_gate_dw"), matmul_tn(h, duu, name=f"ffn{l}_up_dw")], axis=1)
    dx, dgain = matmul_nt_normbwd([(dug, 0), (duu, 1)], W["ffn_w_up"][l], x, _row(W["ffn_norm"][l]), dx, name=f"ffn{l}_up_dx")
    G["ffn_norm"][l] = dgain.sum(0)
    return dx


class NoTraffic:
    def fetch(self, host):
        return None

    def arrived(self, host, brought, W):
        pass

    def flush(self, G):
        return None

    def landed(self, brought):
        pass


def _gdn_fwd(x, W, l, plan):
    conv = _pad_rows(W["a_conv"][l])
    a_log, dt_bias = _pad_lanes(W["a_log"][l]), _pad_lanes(W["a_dt_bias"][l])
    proj, h, qkv, brought = gdn_in_conv(x, _row(W["a_norm"][l]), W["a_w_in"][l], conv, name=f"gdn{l}_in_conv",
                                        exchange=plan.fetch(f"gdn{l}_conv"))
    plan.arrived(f"gdn{l}_conv", brought, W)
    heads = _head_layouts(gates_fwd(proj, a_log, dt_bias, name=f"gdn{l}_gates"))
    o, states, brought = gdn_fwd(qkv, *heads, name=f"gdn{l}_rule", exchange=plan.fetch(f"gdn{l}_rule"))
    plan.arrived(f"gdn{l}_rule", brought, W)
    gain = _row(W["a_out_norm"][l])
    on = head_norm_fwd(o, gain, proj, z_col=Z_BLOCK, name=f"gdn{l}_outnorm", out_dtype=BF16)
    y = matmul_residual(on, W["a_w_out"][l], x, name=f"gdn{l}_out")
    return y, (x, h, proj, conv, a_log, dt_bias, qkv, heads, states, o, gain, on)


def _gdn_bwd(dx, saved, W, l, G, plan):
    x, h, proj, conv, a_log, dt_bias, qkv, heads, states, o, gain, on = saved
    T = x.shape[0]
    don = matmul_nt(dx, W["a_w_out"][l], name=f"gdn{l}_out_dx")
    G["a_w_out"][l] = matmul_tn(on, dx, name=f"gdn{l}_out_dw")
    do, dz, dgain = head_norm_bwd(o, gain, [don], proj, z_col=Z_BLOCK, name=f"gdn{l}_outnorm_bwd")
    G["a_out_norm"][l] = dgain.reshape(SUBLANES, HEADS, HEAD_DIM).sum((0, 1))
    dqkv, da, db, brought = gdn_bwd(qkv, *heads, states, do, name=f"gdn{l}_rule_bwd", exchange=plan.flush(G))
    plan.landed(brought)
    dqkv, dconv = gdn_conv_bwd(proj, conv, dqkv, name=f"gdn{l}_conv_bwd")
    G["a_conv"][l] = dconv[:GDN_CONV]
    dgate = jnp.concatenate([da.reshape(HEADS, T).T, db.reshape(HEADS, T).T, jnp.zeros((T, LANES - 2 * HEADS), F32)], axis=1)
    dab, dal, ddt = gates_bwd(proj, a_log, dt_bias, dgate, name=f"gdn{l}_gates_bwd")
    G["a_log"][l] = dal.sum(0)[:HEADS]
    G["a_dt_bias"][l] = ddt.sum(0)[:HEADS]
    parts = [(dqkv, 0), (dz, Z_BLOCK * LANES // D_MODEL), (dab, AB_BLOCK)]
    G["a_w_in"][l] = jnp.concatenate([matmul_tn(h, d, name=f"gdn{l}_in_dw{i}") for i, (d, _) in enumerate(parts)], axis=1)
    last = plan.flush(G) if l == 0 else None
    out = matmul_nt_normbwd(parts, W["a_w_in"][l], x, _row(W["a_norm"][l]), dx, name=f"gdn{l}_in_dx", exchange=last)
    dx, dgain = out[:2]
    if last is not None:
        plan.landed(out[2])
    G["a_norm"][l] = dgain.sum(0)
    return dx


def _sb_fwd(x, kn, kv, W, j):
    qp, h = norm_matmul(x, _row(W["b_norm"][j]), W["b_w_q"][j], name=f"sb{j}_q")
    gain = _row(W["q_norm"][j])
    q = head_norm_fwd(qp, gain, name=f"sb{j}_qnorm")
    o, ltot, visited = sb_fwd(q, kn, kv, v_col=HEADS, name=f"sb{j}_attn")
    y = matmul_residual(o, W["b_w_out"][j], x, name=f"sb{j}_out")
    return y, (x, h, qp, gain, q, o, ltot, visited)


def _sb_bwd(dx, saved, kn, kv, W, j, G, plan):
    x, h, qp, gain, q, o, ltot, visited = saved
    do = matmul_nt(dx, W["b_w_out"][j], name=f"sb{j}_out_dx")
    G["b_w_out"][j] = matmul_tn(o, dx, name=f"sb{j}_out_dw")
    dq, dk, dv, brought = sb_bwd(q, kn, kv, ltot, visited, do, v_col=HEADS, name=f"sb{j}_attn_bwd", exchange=plan.flush(G))
    plan.landed(brought)
    dqp, dgain = head_norm_bwd(qp, gain, [dq], name=f"sb{j}_qnorm_bwd", dx_dtype=BF16)
    G["q_norm"][j] = dgain.reshape(SUBLANES, HEADS, HEAD_DIM).sum((0, 1))
    G["b_w_q"][j] = matmul_tn(h, dqp, name=f"sb{j}_q_dw")
    dx, dgain = matmul_nt_normbwd([(dqp, 0)], W["b_w_q"][j], x, _row(W["b_norm"][j]), dx, name=f"sb{j}_q_dx")
    G["b_norm"][j] = dgain.sum(0)
    return dx, dk, dv


def local_step(x, target, W, plan=None):
    plan = plan or NoTraffic()
    G = {k: [None] * (N_A if k.startswith("a_") else N_B if k in ("b_norm", "b_w_q", "q_norm", "b_w_out") else DEPTH)
         for k in ("a_norm", "a_w_in", "a_conv", "a_log", "a_dt_bias", "a_out_norm", "a_w_out", "b_norm", "b_w_q", "q_norm",
                   "b_w_out", "ffn_norm", "ffn_w_up", "ffn_conv", "ffn_w_down")}
    G["w_kv"] = [None]
    tape = []
    for l in range(N_A):
        x, s_mix = _gdn_fwd(x, W, l, plan)
        x, s_ffn = _ffn_fwd(x, W, l, plan)
        tape.append((s_mix, s_ffn))
    x_kv = x
    kv, h_kv = norm_matmul(x, _row(W["kv_norm"]), W["w_kv"], name="kv_proj")
    k_gain = _row(W["k_norm"])
    kn = head_norm_fwd(kv, k_gain, name="k_norm")
    for j in range(N_B):
        x, s_mix = _sb_fwd(x, kn, kv, W, j)
        x, s_ffn = _ffn_fwd(x, W, N_A + j, plan)
        tape.append((s_mix, s_ffn))
    dx, loss = loss_fwd(x, target, name="loss")

    dks, dvs = [], []
    for j in reversed(range(N_B)):
        s_mix, s_ffn = tape[N_A + j]
        dx = _ffn_bwd(dx, s_ffn, W, N_A + j, G)
        dx, dk, dv = _sb_bwd(dx, s_mix, kn, kv, W, j, G, plan)
        dks.append(dk)
        dvs.append(dv)
    dkp, dgain = head_norm_bwd(kv, k_gain, dks, name="k_norm_bwd", dx_dtype=BF16)
    G["k_norm"] = dgain.reshape(SUBLANES, HEADS, HEAD_DIM).sum((0, 1))
    G["w_kv"][0] = jnp.concatenate([matmul_tn(h_kv, dkp, name="k_proj_dw"), matmul_tn(h_kv, dvs, name="v_proj_dw")], axis=1)
    dx, dgain = matmul_nt_normbwd([(dkp, 0)] + [(dv, 1) for dv in dvs], W["w_kv"], x_kv, _row(W["kv_norm"]), dx,
                                  name="kv_proj_dx")
    G["kv_norm"] = dgain.sum(0)
    for l in reversed(range(N_A)):
        s_mix, s_ffn = tape[l]
        dx = _ffn_bwd(dx, s_ffn, W, l, G)
        dx = _gdn_bwd(dx, s_mix, W, l, G, plan)
    return loss, dx, G


MATRICES = {"a_w_in": 2, "a_w_out": 1, "w_kv": 1, "b_w_q": 1, "b_w_out": 1, "ffn_w_up": 2, "ffn_w_down": 1}
SMALL_SHARDED = {"a_norm": 1, "a_conv": 2, "ffn_conv": 2}
SMALL_REPLICATED = ("a_log", "a_dt_bias", "a_out_norm", "kv_norm", "k_norm", "b_norm", "q_norm", "ffn_norm")
WEIGHT_ORDER = ("a_norm", "a_w_in", "a_conv", "a_log", "a_dt_bias", "a_out_norm", "a_w_out", "kv_norm", "w_kv", "k_norm",
                "b_norm", "b_w_q", "q_norm", "b_w_out", "ffn_norm", "ffn_w_up", "ffn_conv", "ffn_w_down")
SMALL_ORDER = tuple(n for n in WEIGHT_ORDER if n not in MATRICES)
PACK_QUANTUM = SUBLANES * LANES


def _unshard(g, axis):
    g = jnp.moveaxis(g, 0, axis)
    return g.reshape(g.shape[:axis] + (g.shape[axis] * g.shape[axis + 1],) + g.shape[axis + 2:])


def _shards(full, axis):
    n = full.shape[axis] // N_CHIPS
    return jnp.moveaxis(full.reshape(full.shape[:axis] + (N_CHIPS, n) + full.shape[axis + 1:]), axis, 0)


def _pack(arrays):
    parts = []
    for a in arrays:
        flat = a.reshape(-1)
        parts.append(jnp.pad(flat, (0, -flat.shape[0] % PACK_QUANTUM)).reshape(-1, LANES))
    return jnp.concatenate(parts, axis=0)


def _unpack(buf, shapes):
    out, row = [], 0
    for s in shapes:
        size = math.prod(s)
        rows = -(-size // PACK_QUANTUM) * SUBLANES
        out.append(buf[row:row + rows].reshape(-1)[:size].reshape(s))
        row += rows
    return out


def _stack(per_layer):
    return jnp.stack(per_layer) if isinstance(per_layer, list) else per_layer


FETCH_BESIDE = {
    "gdn0_conv": (("ffn_w_up", 0, 1),),
    "gdn0_rule": (("a_w_out", 0, 1), ("ffn_w_down", 0, 2), ("w_kv", 0, None), ("b_w_q", 0, 2), ("b_w_out", 0, 2)),
    "ffn0_act": (("a_w_in", 1, 1), ("a_w_out", 1, 1)),
    "gdn1_conv": (("ffn_w_up", 2, 1),),
    "gdn1_rule": (("ffn_w_down", 2, 2), ("ffn_w_up", 1, 1)),
    "ffn1_act": (("ffn_w_up", 3, 1),),
}
FETCH_FIRST = (("a_w_in", 0, 1),)


class Traffic:
    def __init__(self, local):
        self.local = local
        self.shipped, self.received = [], {}

    @staticmethod
    def _assemble(W, name, first, layers, gathered):
        whole = _unshard(gathered, MATRICES[name])
        if name == "a_w_in":
            whole = jnp.pad(whole, ((0, 0), (0, 0), (0, W_IN_PAD - W_IN_COLS)))
        if layers is None:
            W[name] = whole
        else:
            W.setdefault(name, {}).update({first + i: whole[i] for i in range(layers)})

    def _shards_of(self, wanted):
        return [(self.local[name] if layers is None else self.local[name][first:first + layers]).astype(BF16)
                for name, first, layers in wanted]

    def fetch_first(self, extra):
        brought = chip_exchange(self._shards_of(FETCH_FIRST) + list(extra), scatter=False, name="gather_first")
        W = {}
        for (name, first, layers), g in zip(FETCH_FIRST, brought):
            self._assemble(W, name, first, layers, g)
        return W, brought[len(FETCH_FIRST):]

    def fetch(self, host):
        return ChipExchange(self._shards_of(FETCH_BESIDE[host]), scatter=False) if host in FETCH_BESIDE else None

    def arrived(self, host, brought, W):
        for (name, first, layers), g in zip(FETCH_BESIDE.get(host, ()), brought):
            self._assemble(W, name, first, layers, g)

    def _ready(self, G):
        out = []
        for name, axis in MATRICES.items():
            for layer, g in enumerate(G[name]):
                if g is None or (name, layer) in self.shipped:
                    continue
                g = g[:, :W_IN_COLS] if name == "a_w_in" else g
                piece = _shards(g, axis - (0 if name == "w_kv" else 1)).astype(BF16)
                out.append((name, layer, piece.reshape(N_CHIPS, -1, piece.shape[-1])))
        return out

    def flush(self, G):
        ready = self._ready(G)
        self.in_flight = [r[:2] for r in ready]
        self.shipped += self.in_flight
        return ChipExchange([r[2] for r in ready], scatter=True) if ready else None

    def landed(self, brought):
        self.received.update(zip(self.in_flight, brought))
        self.in_flight = []

    def flush_last(self, G):
        ready = self._ready(G)
        if ready:
            self.shipped += [r[:2] for r in ready]
            brought = chip_exchange([r[2] for r in ready], scatter=True, name="scatter_last")
            self.received.update(zip([r[:2] for r in ready], brought))

    def my_sums(self):
        sums = {}
        for name in MATRICES:
            layers = sorted(l for n, l in self.received if n == name)
            parts = [sum_slots(self.received[(name, l)], name=f"sum_{name}{l}") for l in layers]
            sums[name] = parts[0] if len(parts) == 1 else jnp.concatenate(parts, axis=0)
        return sums


def _as_2d(a):
    return a.reshape(-1, a.shape[-1])


def kernel(x, a_norm, a_w_in, a_conv, a_log, a_dt_bias, a_out_norm, a_w_out, kv_norm, w_kv, k_norm, b_norm, b_w_q, q_norm, b_w_out, ffn_norm, ffn_w_up, ffn_conv, ffn_w_down, loss_target, m_a_norm, m_a_w_in, m_a_conv, m_a_log, m_a_dt_bias, m_a_out_norm, m_a_w_out, m_kv_norm, m_w_kv, m_k_norm, m_b_norm, m_b_w_q, m_q_norm, m_b_w_out, m_ffn_norm, m_ffn_w_up, m_ffn_conv, m_ffn_w_down, v_a_norm, v_a_w_in, v_a_conv, v_a_log, v_a_dt_bias, v_a_out_norm, v_a_w_out, v_kv_norm, v_w_kv, v_k_norm, v_b_norm, v_b_w_q, v_q_norm, v_b_w_out, v_ffn_norm, v_ffn_w_up, v_ffn_conv, v_ffn_w_down):
    local = dict(a_norm=a_norm, a_w_in=a_w_in, a_conv=a_conv, a_log=a_log, a_dt_bias=a_dt_bias, a_out_norm=a_out_norm,
                 a_w_out=a_w_out, kv_norm=kv_norm, w_kv=w_kv, k_norm=k_norm, b_norm=b_norm, b_w_q=b_w_q, q_norm=q_norm,
                 b_w_out=b_w_out, ffn_norm=ffn_norm, ffn_w_up=ffn_w_up, ffn_conv=ffn_conv, ffn_w_down=ffn_w_down)
    mom = dict(a_norm=m_a_norm, a_w_in=m_a_w_in, a_conv=m_a_conv, a_log=m_a_log, a_dt_bias=m_a_dt_bias, a_out_norm=m_a_out_norm,
               a_w_out=m_a_w_out, kv_norm=m_kv_norm, w_kv=m_w_kv, k_norm=m_k_norm, b_norm=m_b_norm, b_w_q=m_b_w_q, q_norm=m_q_norm,
               b_w_out=m_b_w_out, ffn_norm=m_ffn_norm, ffn_w_up=m_ffn_w_up, ffn_conv=m_ffn_conv, ffn_w_down=m_ffn_w_down)
    var = dict(a_norm=v_a_norm, a_w_in=v_a_w_in, a_conv=v_a_conv, a_log=v_a_log, a_dt_bias=v_a_dt_bias, a_out_norm=v_a_out_norm,
               a_w_out=v_a_w_out, kv_norm=v_kv_norm, w_kv=v_w_kv, k_norm=v_k_norm, b_norm=v_b_norm, b_w_q=v_b_w_q, q_norm=v_q_norm,
               b_w_out=v_b_w_out, ffn_norm=v_ffn_norm, ffn_w_up=v_ffn_w_up, ffn_conv=v_ffn_conv, ffn_w_down=v_ffn_w_down)
    chip = 2 * lax.axis_index("x") + lax.axis_index("y")

    mats = list(MATRICES)
    small_sharded = list(SMALL_SHARDED)
    traffic = Traffic(local)
    W, (vectors,) = traffic.fetch_first([_pack([local[n] for n in small_sharded])])
    W.update({n: local[n] for n in SMALL_REPLICATED})
    shard_shapes = [local[n].shape for n in small_sharded]
    per_chip = [_unpack(vectors[j], shard_shapes) for j in range(N_CHIPS)]
    for i, n in enumerate(small_sharded):
        W[n] = _unshard(jnp.stack([per_chip[j][i] for j in range(N_CHIPS)]), SMALL_SHARDED[n])

    T = x.shape[1]
    loss_part, dx, G = local_step(x.reshape(T, D_MODEL), loss_target.reshape(T, D_MODEL), W, traffic)

    traffic.flush_last(G)
    sums = traffic.my_sums()
    mine = [sums[n] for n in mats]
    theirs = sibling_swap(mine, name="swap_grads")

    small_full = {n: _stack(G[n]) for n in SMALL_ORDER}
    packed = _pack([small_full[n] for n in SMALL_ORDER] + [loss_part])
    total = sum_slots(all_gather_devices(packed, name="gather_small"), name="sum_small")
    small_shapes = [small_full[n].shape for n in SMALL_ORDER] + [loss_part.shape]
    summed = dict(zip(SMALL_ORDER + ("loss",), _unpack(total, small_shapes)))
    loss = jnp.sum(summed.pop("loss"))
    for n, axis in SMALL_SHARDED.items():
        size = local[n].shape[axis]
        summed[n] = lax.dynamic_slice_in_dim(summed[n], chip * size, size, axis)

    grads, deltas, new_m, new_v = {}, {}, {}, {}
    for n, p_mine, p_theirs in zip(mats, mine, theirs):
        outs = adamw(_as_2d(local[n]), [p_mine, p_theirs], _as_2d(mom[n]), _as_2d(var[n]), name=f"adamw_{n}")
        grads[n], deltas[n], new_m[n], new_v[n] = [o.reshape(local[n].shape) for o in outs]
    small_local_shapes = [local[n].shape for n in SMALL_ORDER]
    outs = adamw(_pack([local[n] for n in SMALL_ORDER]), [_pack([summed[n] for n in SMALL_ORDER])],
                 _pack([mom[n] for n in SMALL_ORDER]), _pack([var[n] for n in SMALL_ORDER]), name="adamw_small")
    for d, o in zip((grads, deltas, new_m, new_v), outs):
        d.update(zip(SMALL_ORDER, _unpack(o, small_local_shapes)))

    return (loss, dx.reshape(x.shape), *[grads[n] for n in WEIGHT_ORDER], *[deltas[n] for n in WEIGHT_ORDER],
            *[new_m[n] for n in WEIGHT_ORDER], *[new_v[n] for n in WEIGHT_ORDER])
```
